```python
import math
import jax, jax.numpy as jnp
from jax import lax
import numpy as np

D_MODEL = 1024
BATCH = 16
SEQ = 256
DEPTH = 2
DEC_BATCH = 4
DEC_SEQ = 2048
PAST_LEN = 512

GRID_W = 64
Q_BLOCK = 128
ROPE_THETA = 10000.0
EPS = 1e-6
FORGET_BIAS = 4.0

HD_A = 64
H_A = D_MODEL // (2 * HD_A)
KV_A = 2
DK_B = 64
DV_B = 64
H_B = D_MODEL // (4 * DV_B)
CHUNK_B = 64
D_C = 32
DV_C = 2 * D_C
H_C = D_MODEL // (4 * DV_C)

D_MIX = H_A * HD_A + H_B * DV_B + H_C * DV_C
D_FF = 2816
N_A = (H_A + 2 * KV_A) * HD_A
N_B = 2 * H_B * DK_B + 2 * H_B * DV_B + 4 * H_B
N_C = 4 * H_C * D_C + H_C * DV_C
N_IN = N_A + N_B + N_C

kernel_name = "hybrid_diffusion_prefix_step"


def rmsnorm(x, g):
    xf = x.astype(jnp.float32)
    y = xf * lax.rsqrt(jnp.mean(xf * xf, axis=-1, keepdims=True) + EPS)
    return (y * g.astype(jnp.float32)).astype(x.dtype)


def swiglu(x, w_in, w_out):
    g, u = jnp.split(x @ w_in, 2, axis=-1)
    return (jax.nn.silu(g) * u) @ w_out


def axial_rope(length, dim):
    rows = length // GRID_W
    t = jnp.arange(rows * GRID_W)
    row = (t // GRID_W).astype(jnp.float32)
    col = (t % GRID_W).astype(jnp.float32)
    axis_dim = dim // 2
    freqs = ROPE_THETA ** (-jnp.arange(0, axis_dim, 2, dtype=jnp.float32) / axis_dim)
    ang = jnp.concatenate([row[:, None] * freqs, col[:, None] * freqs], axis=-1)
    return jnp.cos(ang), jnp.sin(ang)


def apply_rope(x, cos, sin):
    half = x.shape[-1] // 2
    bshape = (cos.shape[0],) + (1,) * (x.ndim - 3) + (half,)
    cos = cos.reshape(bshape)
    sin = sin.reshape(bshape)
    xf = x.astype(jnp.float32)
    x1, x2 = xf[..., :half], xf[..., half:]
    return jnp.concatenate([x1 * cos - x2 * sin, x1 * sin + x2 * cos], axis=-1).astype(x.dtype)


def sweep_query_blocks(fn, q):
    B, Lq = q.shape[:2]
    nb = Lq // Q_BLOCK
    qb = jnp.moveaxis(q.reshape(B, nb, Q_BLOCK, *q.shape[2:]), 1, 0)
    out = jnp.moveaxis(lax.map(fn, qb), 0, 1)
    return out.reshape(B, Lq, *out.shape[3:])


def gqa_attention(q, k, v):
    scale = HD_A ** -0.5
    def block(qb):
        s = jnp.einsum("bqhgd,bkhd->bhgqk", qb, k).astype(jnp.float32) * scale
        p = jax.nn.softmax(s, axis=-1).astype(v.dtype)
        return jnp.einsum("bhgqk,bkhd->bqhgd", p, v)
    return sweep_query_blocks(block, q)


def diff_attention(q, k, v, lam):
    scale = D_C ** -0.5
    def block(qb):
        s = jnp.einsum("bqhnd,bkhnd->bhnqk", qb, k).astype(jnp.float32) * scale
        p = jax.nn.softmax(s, axis=-1)
        a = (p[:, :, 0] - lam * p[:, :, 1]).astype(v.dtype)
        return jnp.einsum("bhqk,bkhd->bqhd", a, v)
    return sweep_query_blocks(block, q)


def mlstm_scan(q, k, v, ig, logf, C0, n0, m0):
    f32 = jnp.float32
    B, L, H, _ = q.shape
    nc = L // CHUNK_B
    def to_chunks(x):
        x = x.astype(f32).reshape(B, nc, CHUNK_B, H, *x.shape[3:])
        return jnp.moveaxis(jnp.moveaxis(x, 1, 0), 3, 2)
    xs = (to_chunks(q), to_chunks(k * (DK_B ** -0.5)), to_chunks(v), to_chunks(ig), to_chunks(logf))
    lower = jnp.tril(jnp.ones((CHUNK_B, CHUNK_B), dtype=bool))

    def step(carry, xc):
        C, n, m = carry
        qj, kj, vj, ij, fj = xc
        b = jnp.cumsum(fj, axis=-1)
        dlog = jnp.where(lower, b[..., :, None] - b[..., None, :] + ij[..., None, :], -jnp.inf)
        inter = b + m[..., None]
        m_out = jnp.maximum(inter, jnp.max(dlog, axis=-1))
        s = jnp.einsum("bhtd,bhsd->bhts", qj, kj) * jnp.exp(dlog - m_out[..., None])
        w_inter = jnp.exp(inter - m_out)
        num = jnp.einsum("bhts,bhsv->bhtv", s, vj) + w_inter[..., None] * jnp.einsum("bhtd,bhdv->bhtv", qj, C)
        den = jnp.sum(s, axis=-1) + w_inter * jnp.einsum("bhtd,bhd->bht", qj, n)
        h = num / jnp.maximum(jnp.abs(den), jnp.exp(-m_out))[..., None]
        b_last = b[..., -1]
        g = b_last[..., None] - b + ij
        m_new = jnp.maximum(b_last + m, jnp.max(g, axis=-1))
        wk = jnp.exp(g - m_new[..., None])
        decay = jnp.exp(b_last + m - m_new)
        C_new = decay[..., None, None] * C + jnp.einsum("bhs,bhsd,bhsv->bhdv", wk, kj, vj)
        n_new = decay[..., None] * n + jnp.einsum("bhs,bhsd->bhd", wk, kj)
        return (C_new, n_new, m_new), h

    (C, n, m), hs = lax.scan(step, (C0.astype(f32), n0.astype(f32), m0.astype(f32)), xs)
    h = jnp.moveaxis(hs, 0, 2).reshape(B, H, L, hs.shape[-1])
    return jnp.moveaxis(h, 1, 2), (C, n, m)


def mlstm_bidir(q, k, v, gates, state0):
    C0, n0, m0 = state0
    ig = gates[:, :, 0::2]
    logf = jax.nn.log_sigmoid(gates[:, :, 1::2])
    h_f, (Cf, nf, mf) = mlstm_scan(q, k, v, ig[:, :, 0], logf[:, :, 0], C0[:, 0], n0[:, 0], m0[:, 0])
    rev = lambda x: jnp.flip(x, axis=1)
    h_r, (Cr, nr, mr) = mlstm_scan(rev(q), rev(k), rev(v), rev(ig[:, :, 1]), rev(logf[:, :, 1]),
                                   C0[:, 1], n0[:, 1], m0[:, 1])
    h = (h_f + rev(h_r)).astype(q.dtype)
    return h, (jnp.stack([Cf, Cr], axis=1), jnp.stack([nf, nr], axis=1), jnp.stack([mf, mr], axis=1))


def mix(u, p, lam_init, ctx):
    f32 = jnp.float32
    B, L, _ = u.shape
    z = u @ p["w_in"]
    za, zb, zc = jnp.split(z, [N_A, N_A + N_B], axis=-1)
    qa, ka, va = jnp.split(za, [H_A * HD_A, (H_A + KV_A) * HD_A], axis=-1)
    qa = rmsnorm(qa.reshape(B, L, H_A, HD_A), p["g_qa"])
    ka = rmsnorm(ka.reshape(B, L, KV_A, HD_A), p["g_ka"])
    va = va.reshape(B, L, KV_A, HD_A)
    qb, kb, vb, ob, gb = jnp.split(
        zb, [H_B * DK_B, 2 * H_B * DK_B, 2 * H_B * DK_B + H_B * DV_B, 2 * H_B * DK_B + 2 * H_B * DV_B], axis=-1)
    qb = qb.reshape(B, L, H_B, DK_B)
    kb = kb.reshape(B, L, H_B, DK_B)
    vb = vb.reshape(B, L, H_B, DV_B)
    gates = (gb.astype(f32) + p["b_gates"].astype(f32)).reshape(B, L, 4, H_B)
    qc, kc, vc = jnp.split(zc, [2 * H_C * D_C, 4 * H_C * D_C], axis=-1)
    qc = qc.reshape(B, L, H_C, 2, D_C)
    kc = kc.reshape(B, L, H_C, 2, D_C)
    vc = vc.reshape(B, L, H_C, DV_C)

    if ctx is None:
        ka_all, va_all, kc_all, vc_all = ka, va, kc, vc
        state0 = (jnp.zeros((B, 2, H_B, DK_B, DV_B), f32), jnp.zeros((B, 2, H_B, DK_B), f32),
                  jnp.zeros((B, 2, H_B), f32))
    else:
        ctx_ka, ctx_va, ctx_kc, ctx_vc, state0 = ctx
        cos_a, sin_a = axial_rope(L, HD_A)
        cos_c, sin_c = axial_rope(L, D_C)
        qa = apply_rope(qa, cos_a, sin_a)
        qc = apply_rope(qc, cos_c, sin_c)
        ka_all = jnp.concatenate([ctx_ka.astype(ka.dtype), apply_rope(ka, cos_a, sin_a)], axis=1)
        va_all = jnp.concatenate([ctx_va.astype(va.dtype), va], axis=1)
        kc_all = jnp.concatenate([ctx_kc.astype(kc.dtype), apply_rope(kc, cos_c, sin_c)], axis=1)
        vc_all = jnp.concatenate([ctx_vc.astype(vc.dtype), vc], axis=1)

    y_a = gqa_attention(qa.reshape(B, L, KV_A, H_A // KV_A, HD_A), ka_all, va_all).reshape(B, L, H_A * HD_A)
    lam = (jnp.exp(jnp.sum(p["lam_q1"].astype(f32) * p["lam_k1"].astype(f32)))
           - jnp.exp(jnp.sum(p["lam_q2"].astype(f32) * p["lam_k2"].astype(f32))) + lam_init)
    y_c = diff_attention(qc, kc_all, vc_all, lam)
    y_c = (rmsnorm(y_c, p["g_c"]) * (1.0 - lam_init)).reshape(B, L, H_C * DV_C)
    h_b, state = mlstm_bidir(qb, kb, vb, gates, state0)
    y_b = rmsnorm(h_b, p["g_b"]).reshape(B, L, H_B * DV_B) * jax.nn.sigmoid(ob)
    y = jnp.concatenate([y_a, y_b.astype(u.dtype), y_c], axis=-1) @ p["w_out"]
    if ctx is None:
        return y, (ka, va, kc, vc) + state
    return y, None


def layer(h, mod, p, lam_init, ctx):
    sh1, sc1, gt1, sh2, sc2, gt2, sh3, sc3, gt3 = jnp.split(mod[:, None, :], 9, axis=-1)
    u = rmsnorm(h, p["g_norm"][0]) * (1.0 + sc1) + sh1
    h = h + 0.5 * gt1 * swiglu(u, p["w_ff_in"][0], p["w_ff_out"][0])
    u = rmsnorm(h, p["g_norm"][1]) * (1.0 + sc2) + sh2
    y, cache = mix(u, p, lam_init, ctx)
    h = h + gt2 * y
    u = rmsnorm(h, p["g_norm"][2]) * (1.0 + sc3) + sh3
    h = h + 0.5 * gt3 * swiglu(u, p["w_ff_in"][1], p["w_ff_out"][1])
    return h, cache


def setup_inputs(seed: int = 0) -> dict:
    key = jax.random.key(seed)
    ks = jax.random.split(key, 32)
    f32 = jnp.float32
    nrm = lambda k, shape, s=1.0: s * jax.random.normal(k, shape, f32)
    gate_offset = jnp.repeat(jnp.array([0.0, FORGET_BIAS, 0.0, FORGET_BIAS], f32), H_B)
    return {
        "x_prompt": nrm(ks[0], (BATCH, SEQ, D_MODEL)),
        "x_sample": nrm(ks[1], (DEC_BATCH, DEC_SEQ, D_MODEL)),
        "cache_a_k": nrm(ks[2], (DEC_BATCH, DEPTH, PAST_LEN, KV_A, HD_A)),
        "cache_a_v": nrm(ks[3], (DEC_BATCH, DEPTH, PAST_LEN, KV_A, HD_A)),
        "cache_c_k": nrm(ks[4], (DEC_BATCH, DEPTH, PAST_LEN, H_C, 2, D_C)),
        "cache_c_v": nrm(ks[5], (DEC_BATCH, DEPTH, PAST_LEN, H_C, DV_C)),
        "state_b_C": nrm(ks[6], (DEC_BATCH, DEPTH, 2, H_B, DK_B, DV_B), 0.1),
        "state_b_n": nrm(ks[7], (DEC_BATCH, DEPTH, 2, H_B, DK_B), 0.1),
        "state_b_m": nrm(ks[8], (DEC_BATCH, DEPTH, 2, H_B), 0.5),
        "c": nrm(ks[9], (DEC_BATCH, D_MODEL)),
        "c_ctx": nrm(ks[10], (D_MODEL,)),
        "w_ada": nrm(ks[11], (DEPTH, D_MODEL, 9 * D_MODEL), 0.5 * D_MODEL ** -0.5),
        "b_ada": nrm(ks[12], (DEPTH, 9 * D_MODEL), 0.01),
        "g_norm": 1.0 + nrm(ks[13], (DEPTH, 3, D_MODEL), 0.02),
        "w_ff_in": nrm(ks[14], (DEPTH, 2, D_MODEL, 2 * D_FF), D_MODEL ** -0.5),
        "w_ff_out": nrm(ks[15], (DEPTH, 2, D_FF, D_MODEL), D_FF ** -0.5),
        "w_in": nrm(ks[16], (DEPTH, D_MODEL, N_IN), D_MODEL ** -0.5),
        "w_out": nrm(ks[17], (DEPTH, D_MIX, D_MODEL), D_MIX ** -0.5),
        "g_qa": 1.0 + nrm(ks[18], (DEPTH, HD_A), 0.02),
        "g_ka": 1.0 + nrm(ks[19], (DEPTH, HD_A), 0.02),
        "b_gates": gate_offset + nrm(ks[20], (DEPTH, 4 * H_B), 0.1),
        "g_b": 1.0 + nrm(ks[21], (DEPTH, DV_B), 0.02),
        "lam_q1": nrm(ks[22], (DEPTH, D_C), 0.1),
        "lam_k1": nrm(ks[23], (DEPTH, D_C), 0.1),
        "lam_q2": nrm(ks[24], (DEPTH, D_C), 0.1),
        "lam_k2": nrm(ks[25], (DEPTH, D_C), 0.1),
        "g_c": 1.0 + nrm(ks[26], (DEPTH, DV_C), 0.02),
        "g_final": 1.0 + nrm(ks[27], (D_MODEL,), 0.02),
    }


def reference(x_prompt, x_sample, cache_a_k, cache_a_v, cache_c_k, cache_c_v, state_b_C, state_b_n,
              state_b_m, c, c_ctx, w_ada, b_ada, g_norm, w_ff_in, w_ff_out, w_in, w_out, g_qa, g_ka,
              b_gates, g_b, lam_q1, lam_k1, lam_q2, lam_k2, g_c, g_final):
    hp = x_prompt
    hs = x_sample
    caches = []
    for l in range(DEPTH):
        p = {"g_norm": g_norm[l], "w_ff_in": w_ff_in[l], "w_ff_out": w_ff_out[l], "w_in": w_in[l],
             "w_out": w_out[l], "g_qa": g_qa[l], "g_ka": g_ka[l], "b_gates": b_gates[l], "g_b": g_b[l],
             "lam_q1": lam_q1[l], "lam_k1": lam_k1[l], "lam_q2": lam_q2[l], "lam_k2": lam_k2[l],
             "g_c": g_c[l]}
        lam_init = 0.8 - 0.6 * math.exp(-0.3 * l)
        mod_ctx = jax.nn.silu(c_ctx)[None, :] @ w_ada[l] + b_ada[l]
        mod_lat = jax.nn.silu(c) @ w_ada[l] + b_ada[l]
        hp, cache_l = layer(hp, mod_ctx, p, lam_init, None)
        caches.append(cache_l)
        ctx_l = (cache_a_k[:, l], cache_a_v[:, l], cache_c_k[:, l], cache_c_v[:, l],
                 (state_b_C[:, l], state_b_n[:, l], state_b_m[:, l]))
        hs, _ = layer(hs, mod_lat, p, lam_init, ctx_l)
    y_prompt = rmsnorm(hp, g_final)
    y_sample = rmsnorm(hs, g_final)
    new_a_k = jnp.stack([cl[0] for cl in caches], axis=1)
    new_a_v = jnp.stack([cl[1] for cl in caches], axis=1)
    new_c_k = jnp.stack([cl[2] for cl in caches], axis=1)
    new_c_v = jnp.stack([cl[3] for cl in caches], axis=1)
    new_b_C = jnp.stack([cl[4] for cl in caches], axis=1)
    new_b_n = jnp.stack([cl[5] for cl in caches], axis=1)
    new_b_m = jnp.stack([cl[6] for cl in caches], axis=1)
    return (y_prompt, y_sample, new_a_k, new_a_v, new_c_k, new_c_v, new_b_C, new_b_n, new_b_m)
```

```python
import functools
import math

import jax
import jax.numpy as jnp
from jax import lax
from jax.experimental import pallas as pl
from jax.experimental.pallas import tpu as pltpu

F32 = jnp.float32
BF16 = jnp.bfloat16

D_MODEL = 1024
BATCH = 16
SEQ = 256
DEPTH = 2
DEC_BATCH = 4
DEC_SEQ = 2048
PAST_LEN = 512
GRID_W = 64
ROPE_THETA = 10000.0
EPS = 1e-6
HD_A = 64
H_A = 8
KV_A = 2
DK_B = 64
H_B = 4
D_C = 32
H_C = 4
D_FF = 2816
LOG2E = 1.4426950408889634

N_CTX = BATCH * SEQ
N_LAT = DEC_BATCH * DEC_SEQ
N_TOK = N_CTX + N_LAT

TM = 512
CTX_TILES = N_CTX // TM
LAT_TILES_PER_BATCH = DEC_SEQ // TM
FF_CHUNK = D_FF // 2
TC = 256
KC = 512

Z_QA, Z_QB, Z_VB, Z_OB, Z_QC, Z_VC, Z_VA, Z_G = 0, 512, 768, 1024, 1280, 1536, 1792, 1920
Z_W = 2048
ZT_KB, ZT_KC, ZT_KA, ZT_G = 0, 256, 512, 640
ZT_W = 656

VMEM_LIMIT = 56 * 1024 * 1024


def _cparams(sem):
    return pltpu.CompilerParams(dimension_semantics=sem, vmem_limit_bytes=VMEM_LIMIT)


def _const_spec(shape):
    nd = len(shape)
    return pl.BlockSpec(shape, lambda *_: (0,) * nd, pipeline_mode=pl.Buffered(1))


def _mod_row(i):
    return jnp.maximum(i - CTX_TILES, 0) // LAT_TILES_PER_BATCH + (i >= CTX_TILES).astype(jnp.int32)


def _rms(x, g):
    ms = jnp.mean(x * x, axis=-1, keepdims=True)
    return x * lax.rsqrt(ms + EPS) * g


def _norm_mod(x, g, mod_ref, k):
    shift = mod_ref[0, k:k + 1, :]
    scale = mod_ref[0, k + 1:k + 2, :]
    return _rms(x, g) * (1.0 + scale) + shift


def _ds(c, n):
    if isinstance(c, int):
        return pl.ds(c * n, n)
    return pl.ds(pl.multiple_of(c * n, n), n)


def _log_sigmoid(x):
    return jnp.minimum(x, 0.0) - jnp.log1p(jnp.exp(-jnp.abs(x)))


def _ada_kernel(c_ref, w_ref, b_ref, o_ref):
    c = c_ref[...]
    a = (c * jax.nn.sigmoid(c)).astype(BF16)
    o_ref[0] = jnp.dot(a, w_ref[0].astype(BF16), preferred_element_type=F32) + b_ref[0]


def _ada(cc, w_ada, b_ada):
    tn = 1024
    n = 9 * D_MODEL
    return pl.pallas_call(
        _ada_kernel,
        grid=(DEPTH, n // tn),
        in_specs=[
            pl.BlockSpec((8, D_MODEL), lambda l, j: (0, 0)),
            pl.BlockSpec((1, D_MODEL, tn), lambda l, j: (l, 0, j)),
            pl.BlockSpec((1, 1, tn), lambda l, j: (l, 0, j)),
        ],
        out_specs=pl.BlockSpec((1, 8, tn), lambda l, j: (l, 0, j)),
        out_shape=jax.ShapeDtypeStruct((DEPTH, 8, n), F32),
        compiler_params=_cparams(("arbitrary", "arbitrary")),
        name="ada",
    )(cc, w_ada, b_ada.reshape(DEPTH, 1, n))


def _ffn_kernel(h_ref, mod_ref, g_ref, wi_ref, wo_ref, gf_ref, o_ref, *, k, final):
    x = h_ref[...]
    xn = _norm_mod(x, g_ref[...], mod_ref, k).astype(BF16)
    acc = jnp.zeros((TM, D_MODEL), F32)
    for j in range(D_FF // FF_CHUNK):
        lo = j * FF_CHUNK
        g = jnp.dot(xn, wi_ref[:, lo:lo + FF_CHUNK], preferred_element_type=F32)
        u = jnp.dot(xn, wi_ref[:, D_FF + lo:D_FF + lo + FF_CHUNK], preferred_element_type=F32)
        a = (g * jax.nn.sigmoid(g) * u).astype(BF16)
        acc = acc + jnp.dot(a, wo_ref[lo:lo + FF_CHUNK, :], preferred_element_type=F32)
    gate = mod_ref[0, k + 2:k + 3, :]
    y = x + 0.5 * gate * acc
    if final:
        y = _rms(y, gf_ref[...])
    o_ref[...] = y


def _ffn(h, mods, g, wi, wo, gf, *, k, final):
    return pl.pallas_call(
        functools.partial(_ffn_kernel, k=k, final=final),
        grid=(N_TOK // TM,),
        in_specs=[
            pl.BlockSpec((TM, D_MODEL), lambda i: (i, 0)),
            pl.BlockSpec((1, 9, D_MODEL), lambda i: (_mod_row(i), 0, 0)),
            _const_spec((1, D_MODEL)),
            _const_spec((D_MODEL, 2 * D_FF)),
            _const_spec((D_FF, D_MODEL)),
            _const_spec((1, D_MODEL)),
        ],
        out_specs=pl.BlockSpec((TM, D_MODEL), lambda i: (i, 0)),
        out_shape=jax.ShapeDtypeStruct((N_TOK, D_MODEL), F32),
        compiler_params=_cparams(("arbitrary",)),
        name="ffn_final" if final else "ffn",
    )(h, mods, g, wi, wo, gf)


def _proj_in_kernel(h_ref, mod_ref, g_ref, w_ref, wt_ref, bz_ref, bt_ref, z_ref, zt_ref):
    xn = _norm_mod(h_ref[...], g_ref[...], mod_ref, 3).astype(BF16)
    z_ref[...] = jnp.dot(xn, w_ref[...], preferred_element_type=F32) + bz_ref[...]
    zt = lax.dot_general(wt_ref[...], xn, (((1,), (1,)), ((), ())), preferred_element_type=F32)
    zt = zt + bt_ref[...]
    for c in range(TM // TC):
        zt_ref[c] = zt[:, c * TC:(c + 1) * TC]


def _proj_in(h, mods, g, wz, wt, bz, bt):
    return pl.pallas_call(
        _proj_in_kernel,
        grid=(N_TOK // TM,),
        in_specs=[
            pl.BlockSpec((TM, D_MODEL), lambda i: (i, 0)),
            pl.BlockSpec((1, 9, D_MODEL), lambda i: (_mod_row(i), 0, 0)),
            _const_spec((1, D_MODEL)),
            _const_spec((D_MODEL, Z_W)),
            _const_spec((ZT_W, D_MODEL)),
            _const_spec((1, Z_W)),
            _const_spec((ZT_W, 1)),
        ],
        out_specs=[
            pl.BlockSpec((TM, Z_W), lambda i: (i, 0)),
            pl.BlockSpec((TM // TC, ZT_W, TC), lambda i: (i, 0, 0)),
        ],
        out_shape=[
            jax.ShapeDtypeStruct((N_TOK, Z_W), F32),
            jax.ShapeDtypeStruct((N_TOK // TC, ZT_W, TC), F32),
        ],
        compiler_params=_cparams(("arbitrary",)),
        name="proj_in",
    )(h, mods, g, wz, wt, bz, bt)


def _proj_out_kernel(h_ref, ya_ref, yb_ref, yc_ref, mod_ref, wo_ref, o_ref):
    y = jnp.dot(ya_ref[...], wo_ref[0:512, :], preferred_element_type=F32)
    y = y + jnp.dot(yb_ref[...], wo_ref[512:768, :], preferred_element_type=F32)
    y = y + jnp.dot(yc_ref[...], wo_ref[768:1024, :], preferred_element_type=F32)
    o_ref[...] = h_ref[...] + mod_ref[0, 5:6, :] * y


def _proj_out(h, ya, yb, yc, mods, wo):
    return pl.pallas_call(
        _proj_out_kernel,
        grid=(N_TOK // TM,),
        in_specs=[
            pl.BlockSpec((TM, D_MODEL), lambda i: (i, 0)),
            pl.BlockSpec((TM, 512), lambda i: (i, 0)),
            pl.BlockSpec((TM, 256), lambda i: (i, 0)),
            pl.BlockSpec((TM, 256), lambda i: (i, 0)),
            pl.BlockSpec((1, 9, D_MODEL), lambda i: (_mod_row(i), 0, 0)),
            _const_spec((D_MODEL, D_MODEL)),
        ],
        out_specs=pl.BlockSpec((TM, D_MODEL), lambda i: (i, 0)),
        out_shape=jax.ShapeDtypeStruct((N_TOK, D_MODEL), F32),
        compiler_params=_cparams(("arbitrary",)),
        name="proj_out",
    )(h, ya, yb, yc, mods, wo)


def _swap_halves(x, width, axis):
    n = x.shape[axis]
    half = width // 2
    parts = []
    for s in range(0, n, width):
        parts.append(lax.slice_in_dim(x, s + half, s + width, axis=axis))
        parts.append(lax.slice_in_dim(x, s, s + half, axis=axis))
    return jnp.concatenate(parts, axis=axis)


def _online_softmax(q, kbuf, vbuf, g, nch, kc):
    r = q.shape[0]

    def step(c, carry):
        m, l, acc = carry
        s = jnp.dot(q, kbuf[g, c], preferred_element_type=F32)
        mn = jnp.maximum(m, jnp.max(s, axis=1, keepdims=True))
        alpha = jnp.exp2(m - mn)
        p = jnp.exp2(s - mn)
        l = alpha * l + jnp.sum(p, axis=1, keepdims=True)
        v = vbuf[g, _ds(c, kc), :]
        acc = alpha * acc + jnp.dot(p.astype(BF16), v, preferred_element_type=F32)
        return mn, l, acc

    init = (jnp.full((r, 1), -jnp.inf, F32), jnp.zeros((r, 1), F32), jnp.zeros((r, 64), F32))
    if nch == 1:
        m, l, acc = step(0, init)
    else:
        m, l, acc = lax.fori_loop(0, nch, step, init)
    return acc / l


def _attn_kernel(*refs, mode, seq, tq, has_ctx, rope, emit_k, lam_init):
    it = iter(refs)
    q_ref, v_ref, kt_ref = next(it), next(it), next(it)
    if has_ctx:
        ckt_ref, cv_ref = next(it), next(it)
    if rope:
        cq_ref, sq_ref, ct_ref, st_ref = next(it), next(it), next(it), next(it)
    if mode == "A":
        gq_ref, gk_ref = next(it), next(it)
    else:
        lam_ref, gc_ref = next(it), next(it)
    y_ref = next(it)
    if emit_k:
        kn_ref = next(it)
    kbuf, vbuf = next(it), next(it)

    n_kv = KV_A if mode == "A" else H_C
    n_ctx = PAST_LEN if has_ctx else 0
    lk = n_ctx + seq
    kc = min(KC, lk)
    nch = lk // kc
    per = kc // TC

    @pl.when(pl.program_id(1) == 0)
    def _fill():
        for h in range(n_kv):
            if has_ctx:
                for c in range(n_ctx // kc):
                    kbuf[h, c] = ckt_ref[0, h, :, c * kc:(c + 1) * kc].astype(BF16)
                vbuf[h, 0:n_ctx, :] = cv_ref[0, :, h * 64:(h + 1) * 64].astype(BF16)
            for c in range(seq // TC):
                kt = kt_ref[c, h * 64:(h + 1) * 64, :]
                if mode == "A":
                    ms = jnp.mean(kt * kt, axis=0, keepdims=True)
                    kt = kt * lax.rsqrt(ms + EPS) * gk_ref[...]
                    if emit_k:
                        kn_ref[0, h * 64:(h + 1) * 64, c * TC:(c + 1) * TC] = kt
                if rope:
                    width = HD_A if mode == "A" else D_C
                    kt = kt * ct_ref[c] + _swap_halves(kt, width, 0) * st_ref[c]
                cc = n_ctx // kc + c // per
                off = (c % per) * TC
                kbuf[h, cc, :, off:off + TC] = kt.astype(BF16)
            vbuf[h, n_ctx:lk, :] = v_ref[:, h * 64:(h + 1) * 64].astype(BF16)

    if mode == "A":
        scale = HD_A ** -0.5 * LOG2E
        grp = H_A // KV_A
        for g in range(KV_A):
            qs = []
            for j in range(grp):
                hh = g * grp + j
                x = _rms(q_ref[:, hh * 64:(hh + 1) * 64], gq_ref[...])
                if rope:
                    x = x * cq_ref[...] + _swap_halves(x, HD_A, 1) * sq_ref[...]
                qs.append((x * scale).astype(BF16))
            o = _online_softmax(jnp.concatenate(qs, axis=0), kbuf, vbuf, g, nch, kc)
            for j in range(grp):
                hh = g * grp + j
                y_ref[:, hh * 64:(hh + 1) * 64] = o[j * tq:(j + 1) * tq].astype(BF16)
    else:
        scale = D_C ** -0.5 * LOG2E
        lp = lam_ref[...]
        lam = (jnp.exp(jnp.sum(lp[0:1] * lp[1:2], axis=1, keepdims=True))
               - jnp.exp(jnp.sum(lp[2:3] * lp[3:4], axis=1, keepdims=True)) + lam_init)
        first_map = lax.broadcasted_iota(jnp.int32, (tq, 64), 1) < D_C
        for h in range(H_C):
            x = q_ref[:, h * 64:(h + 1) * 64]
            if rope:
                x = x * cq_ref[...] + _swap_halves(x, D_C, 1) * sq_ref[...]
            x = x * scale
            q2 = jnp.concatenate([jnp.where(first_map, x, 0.0), jnp.where(first_map, 0.0, x)], axis=0)
            o = _online_softmax(q2.astype(BF16), kbuf, vbuf, h, nch, kc)
            d = o[0:tq] - lam * o[tq:2 * tq]
            y_ref[:, h * 64:(h + 1) * 64] = (_rms(d, gc_ref[...]) * (1.0 - lam_init)).astype(BF16)


def _attn(mode, z, zt, *, ctx, tables, params, lam_init, latent):
    if latent:
        nb, seq, tq = DEC_BATCH, DEC_SEQ, 256
        row0 = N_CTX
    else:
        nb, seq, tq = BATCH, SEQ, 256
        row0 = 0
    nq = seq // tq
    n_kv = KV_A if mode == "A" else H_C
    n_ctx = PAST_LEN if ctx is not None else 0
    lk = n_ctx + seq
    kc = min(KC, lk)
    qw = 512 if mode == "A" else 256
    vw = n_kv * 64
    q_blk = (Z_QA if mode == "A" else Z_QC) // qw
    v_blk = (Z_VA if mode == "A" else Z_VC) // vw
    k_blk = (ZT_KA if mode == "A" else ZT_KC) // vw
    emit_k = mode == "A" and not latent

    in_specs = [
        pl.BlockSpec((tq, qw), lambda b, i: (row0 // tq + b * nq + i, q_blk)),
        pl.BlockSpec((seq, vw), lambda b, i: (row0 // seq + b, v_blk)),
        pl.BlockSpec((seq // TC, vw, TC), lambda b, i: (row0 // seq + b, k_blk, 0)),
    ]
    args = [z, z, zt]
    if ctx is not None:
        in_specs += [
            pl.BlockSpec((1, n_kv, 64, PAST_LEN), lambda b, i: (b, 0, 0, 0)),
            pl.BlockSpec((1, PAST_LEN, vw), lambda b, i: (b, 0, 0)),
        ]
        args += list(ctx)
    if tables is not None:
        in_specs += [
            pl.BlockSpec((tq, 64), lambda b, i: (i, 0)),
            pl.BlockSpec((tq, 64), lambda b, i: (i, 0)),
            _const_spec((seq // TC, 64, TC)),
            _const_spec((seq // TC, 64, TC)),
        ]
        args += list(tables)
    for p in params:
        in_specs.append(_const_spec(p.shape))
        args.append(p)

    out_specs = [pl.BlockSpec((tq, qw), lambda b, i: (b * nq + i, 0))]
    out_shape = [jax.ShapeDtypeStruct((nb * seq, qw), BF16)]
    if emit_k:
        out_specs.append(pl.BlockSpec((1, vw, seq), lambda b, i: (b, 0, 0)))
        out_shape.append(jax.ShapeDtypeStruct((nb, vw, seq), F32))

    return pl.pallas_call(
        functools.partial(_attn_kernel, mode=mode, seq=seq, tq=tq, has_ctx=ctx is not None,
                          rope=tables is not None, emit_k=emit_k, lam_init=lam_init),
        grid=(nb, nq),
        in_specs=in_specs,
        out_specs=out_specs,
        out_shape=out_shape,
        scratch_shapes=[
            pltpu.VMEM((n_kv, lk // kc, 64, kc), BF16),
            pltpu.VMEM((n_kv, lk, 64), BF16),
        ],
        compiler_params=_cparams(("arbitrary", "arbitrary")),
        name=f"attn_{mode}_{'lat' if latent else 'ctx'}",
    )(*args)


def _mlstm_kernel(q_ref, v_ref, o_ref, gc_ref, kt_ref, gt_ref, s0_ref, m0_ref, gb_ref,
                  y_ref, s_ref, m_ref, hf_ref, hr_ref, *, seq):
    nc = seq // TC
    row = lax.broadcasted_iota(jnp.int32, (TC, TC), 0)
    col = lax.broadcasted_iota(jnp.int32, (TC, TC), 1)
    lower = col <= row
    upper = row <= col
    ones_col = (lax.broadcasted_iota(jnp.int32, (TC, 64), 1) == 0).astype(F32)
    units = [(d, h) for d in range(2) for h in range(H_B)]

    def chunk(j, carry):
        out = []
        for (d, h), (s_ext, m) in zip(units, carry):
            c = j if d == 0 else nc - 1 - j
            rows = _ds(c, TC)
            seen = lower if d == 0 else upper
            seen_t = upper if d == 0 else lower
            hs = slice(h * 64, (h + 1) * 64)
            qh = q_ref[rows, hs].astype(BF16)
            kt = kt_ref[c, hs, :] * (DK_B ** -0.5)
            vx = jnp.concatenate([v_ref[rows, hs], ones_col], axis=1).astype(BF16)
            ri, rf = (2 * d) * H_B + h, (2 * d + 1) * H_B + h
            i_row = gt_ref[c, ri:ri + 1, :]
            f_row = _log_sigmoid(gt_ref[c, rf:rf + 1, :])
            f_col = _log_sigmoid(gc_ref[rows, rf:rf + 1])
            b_row = jnp.sum(jnp.where(seen_t, f_col, 0.0), axis=0, keepdims=True)
            b_col = jnp.sum(jnp.where(seen, f_row, 0.0), axis=1, keepdims=True)
            a_row = i_row - b_row
            lm = jnp.max(jnp.where(seen, a_row, -jnp.inf), axis=1, keepdims=True)
            mx = jnp.maximum(m, lm)
            p = jnp.exp(jnp.where(seen, a_row - mx, -jnp.inf))
            sp = jnp.dot(qh, kt.astype(BF16), preferred_element_type=F32) * p
            den = jnp.sum(sp, axis=1, keepdims=True)
            num = jnp.dot(sp.astype(BF16), vx, preferred_element_type=F32)
            inter = jnp.dot(qh, s_ext.astype(BF16), preferred_element_type=F32)
            w = jnp.exp(m - mx)
            num = num[:, 0:64] + w * inter[:, 0:64]
            den = den + w * inter[:, 64:65]
            hv = num / jnp.maximum(jnp.abs(den), jnp.exp(-(b_col + mx)))
            if d == 0:
                hf_ref[rows, hs] = hv
            else:
                hr_ref[rows, hs] = hv
            b_last = jnp.sum(f_row, axis=1, keepdims=True)
            mxl = jnp.maximum(m, jnp.max(a_row, axis=1, keepdims=True))
            wk = jnp.exp(a_row - mxl)
            upd = jnp.dot((kt * wk).astype(BF16), vx, preferred_element_type=F32)
            out.append((jnp.exp(m - mxl) * s_ext + upd, b_last + mxl))
        return tuple(out)

    init = tuple((s0_ref[0, u], m0_ref[0, u:u + 1, 0:1]) for u in range(2 * H_B))
    if nc == 1:
        fin = chunk(0, init)
    else:
        fin = lax.fori_loop(0, nc, chunk, init)
    for u, (s_ext, m) in enumerate(fin):
        s_ref[0, u] = s_ext
        m_ref[0, u:u + 1, :] = jnp.broadcast_to(m, (1, 128))

    def epilogue(c, _):
        rows = _ds(c, TC)
        for h in range(H_B):
            hs = slice(h * 64, (h + 1) * 64)
            x = hf_ref[rows, hs] + hr_ref[rows, hs]
            y = _rms(x, gb_ref[...]) * jax.nn.sigmoid(o_ref[rows, hs])
            y_ref[rows, hs] = y.astype(BF16)
        return 0

    if nc == 1:
        epilogue(0, 0)
    else:
        lax.fori_loop(0, nc, epilogue, 0)


def _mlstm(z, zt, s0, m0, gb, *, latent):
    if latent:
        nb, seq, row0 = DEC_BATCH, DEC_SEQ, N_CTX
    else:
        nb, seq, row0 = BATCH, SEQ, 0
    blk0 = row0 // seq
    nc = seq // TC
    zspec = lambda cb: pl.BlockSpec((seq, 256), lambda b: (blk0 + b, cb))
    return pl.pallas_call(
        functools.partial(_mlstm_kernel, seq=seq),
        grid=(nb,),
        in_specs=[
            zspec(Z_QB // 256), zspec(Z_VB // 256), zspec(Z_OB // 256),
            pl.BlockSpec((seq, 128), lambda b: (blk0 + b, Z_G // 128)),
            pl.BlockSpec((nc, 256, TC), lambda b: (blk0 + b, ZT_KB // 256, 0)),
            pl.BlockSpec((nc, 16, TC), lambda b: (blk0 + b, ZT_G // 16, 0)),
            pl.BlockSpec((1, 2 * H_B, 64, 128), lambda b: (b, 0, 0, 0)),
            pl.BlockSpec((1, 2 * H_B, 128), lambda b: (b, 0, 0)),
            _const_spec((1, 64)),
        ],
        out_specs=[
            pl.BlockSpec((seq, 256), lambda b: (b, 0)),
            pl.BlockSpec((1, 2 * H_B, 64, 128), lambda b: (b, 0, 0, 0)),
            pl.BlockSpec((1, 2 * H_B, 128), lambda b: (b, 0, 0)),
        ],
        out_shape=[
            jax.ShapeDtypeStruct((nb * seq, 256), BF16),
            jax.ShapeDtypeStruct((nb, 2 * H_B, 64, 128), F32),
            jax.ShapeDtypeStruct((nb, 2 * H_B, 128), F32),
        ],
        scratch_shapes=[pltpu.VMEM((seq, 256), F32), pltpu.VMEM((seq, 256), F32)],
        compiler_params=_cparams(("arbitrary",)),
        name=f"mlstm_{'lat' if latent else 'ctx'}",
    )(z, z, z, z, zt, zt, s0, m0, gb)


def _rope_tables(dim):
    t = jnp.arange(DEC_SEQ)
    row = (t // GRID_W).astype(F32)
    colp = (t % GRID_W).astype(F32)
    axis_dim = dim // 2
    freqs = ROPE_THETA ** (-jnp.arange(0, axis_dim, 2, dtype=F32) / axis_dim)
    ang = jnp.concatenate([row[:, None] * freqs, colp[:, None] * freqs], axis=-1)
    cos, sin = jnp.cos(ang), jnp.sin(ang)
    reps = 64 // dim
    cq = jnp.concatenate([cos, cos] * reps, axis=1)
    sq = jnp.concatenate([-sin, sin] * reps, axis=1)
    to_chunks = lambda x: x.T.reshape(64, DEC_SEQ // TC, TC).transpose(1, 0, 2)
    return cq, sq, to_chunks(cq), to_chunks(sq)


def _split_w_in(w, b_gates):
    a0, b0, c0 = 0, 768, 1808
    qa, ka, va = w[:, a0:a0 + 512], w[:, a0 + 512:a0 + 640], w[:, a0 + 640:a0 + 768]
    qb, kb = w[:, b0:b0 + 256], w[:, b0 + 256:b0 + 512]
    vb, ob = w[:, b0 + 512:b0 + 768], w[:, b0 + 768:b0 + 1024]
    gb = w[:, b0 + 1024:b0 + 1040]
    qc, kc, vc = w[:, c0:c0 + 256], w[:, c0 + 256:c0 + 512], w[:, c0 + 512:c0 + 768]
    wz = jnp.concatenate([qa, qb, vb, ob, qc, vc, va, gb, jnp.zeros((D_MODEL, 112), w.dtype)], axis=1)
    wt = jnp.concatenate([kb, kc, ka, gb], axis=1).T
    bz = jnp.zeros((1, Z_W), F32).at[0, Z_G:Z_G + 16].set(b_gates)
    bt = jnp.zeros((ZT_W, 1), F32).at[ZT_G:ZT_G + 16, 0].set(b_gates)
    return wz.astype(BF16), wt.astype(BF16), bz, bt


def kernel(x_prompt, x_sample, cache_a_k, cache_a_v, cache_c_k, cache_c_v, state_b_C, state_b_n,
           state_b_m, c, c_ctx, w_ada, b_ada, g_norm, w_ff_in, w_ff_out, w_in, w_out, g_qa, g_ka,
           b_gates, g_b, lam_q1, lam_k1, lam_q2, lam_k2, g_c, g_final):
    h = jnp.concatenate([x_prompt.reshape(N_CTX, D_MODEL), x_sample.reshape(N_LAT, D_MODEL)], axis=0)
    cc = jnp.concatenate([c_ctx[None, :], c, jnp.zeros((3, D_MODEL), F32)], axis=0)
    mods_all = _ada(cc, w_ada, b_ada).reshape(DEPTH, 8, 9, D_MODEL)
    tab_a = _rope_tables(HD_A)
    tab_c = _rope_tables(D_C)
    gf = g_final.reshape(1, D_MODEL)

    outs = {k: [] for k in ("ak", "av", "ck", "cv", "bC", "bn", "bm")}
    for l in range(DEPTH):
        mods = mods_all[l]
        lam_init = 0.8 - 0.6 * math.exp(-0.3 * l)
        gn = g_norm[l].reshape(3, 1, D_MODEL)
        h = _ffn(h, mods, gn[0], w_ff_in[l, 0].astype(BF16), w_ff_out[l, 0].astype(BF16), gf,
                 k=0, final=False)
        wz, wt, bz, bt = _split_w_in(w_in[l], b_gates[l])
        z, zt = _proj_in(h, mods, gn[1], wz, wt, bz, bt)

        gq, gk = g_qa[l].reshape(1, 64), g_ka[l].reshape(64, 1)
        lamp = jnp.stack([lam_q1[l], lam_k1[l], lam_q2[l], lam_k2[l]], axis=0)
        gc = g_c[l].reshape(1, 64)
        gb = g_b[l].reshape(1, 64)

        ya_c, kn = _attn("A", z, zt, ctx=None, tables=None, params=(gq, gk), lam_init=lam_init,
                         latent=False)
        (yc_c,) = _attn("C", z, zt, ctx=None, tables=None, params=(lamp, gc), lam_init=lam_init,
                        latent=False)
        zeros_s = jnp.zeros((BATCH, 2 * H_B, 64, 128), F32)
        zeros_m = jnp.zeros((BATCH, 2 * H_B, 128), F32)
        yb_c, s_fin, m_fin = _mlstm(z, zt, zeros_s, zeros_m, gb, latent=False)

        ckt_a = cache_a_k[:, l].transpose(0, 2, 3, 1)
        cv_a = cache_a_v[:, l].reshape(DEC_BATCH, PAST_LEN, KV_A * HD_A)
        (ya_l,) = _attn("A", z, zt, ctx=(ckt_a, cv_a), tables=tab_a, params=(gq, gk),
                        lam_init=lam_init, latent=True)
        ckt_c = cache_c_k[:, l].reshape(DEC_BATCH, PAST_LEN, H_C, 2 * D_C).transpose(0, 2, 3, 1)
        cv_c = cache_c_v[:, l].reshape(DEC_BATCH, PAST_LEN, H_C * 2 * D_C)
        (yc_l,) = _attn("C", z, zt, ctx=(ckt_c, cv_c), tables=tab_c, params=(lamp, gc),
                        lam_init=lam_init, latent=True)
        s0 = jnp.concatenate([state_b_C[:, l], state_b_n[:, l][..., None],
                              jnp.zeros((DEC_BATCH, 2, H_B, DK_B, 63), F32)], axis=-1)
        s0 = s0.reshape(DEC_BATCH, 2 * H_B, DK_B, 128)
        m0 = jnp.broadcast_to(state_b_m[:, l].reshape(DEC_BATCH, 2 * H_B, 1), (DEC_BATCH, 2 * H_B, 128))
        yb_l, _, _ = _mlstm(z, zt, s0, m0, gb, latent=True)

        ya = jnp.concatenate([ya_c, ya_l], axis=0)
        yb = jnp.concatenate([yb_c, yb_l], axis=0)
        yc = jnp.concatenate([yc_c, yc_l], axis=0)
        h = _proj_out(h, ya, yb, yc, mods, w_out[l].astype(BF16))
        h = _ffn(h, mods, gn[2], w_ff_in[l, 1].astype(BF16), w_ff_out[l, 1].astype(BF16), gf,
                 k=6, final=(l == DEPTH - 1))

        zc = z[:N_CTX]
        outs["ak"].append(kn.reshape(BATCH, KV_A, HD_A, SEQ).transpose(0, 3, 1, 2))
        outs["av"].append(zc[:, Z_VA:Z_VA + 128].reshape(BATCH, SEQ, KV_A, HD_A))
        kct = zt[:N_CTX // TC, ZT_KC:ZT_KC + 256, :]
        outs["ck"].append(kct.transpose(0, 2, 1).reshape(BATCH, SEQ, H_C, 2, D_C))
        outs["cv"].append(zc[:, Z_VC:Z_VC + 256].reshape(BATCH, SEQ, H_C, 2 * D_C))
        s_fin = s_fin.reshape(BATCH, 2, H_B, DK_B, 128)
        outs["bC"].append(s_fin[..., 0:64])
        outs["bn"].append(s_fin[..., 64])
        outs["bm"].append(m_fin[:, :, 0].reshape(BATCH, 2, H_B))

    y_prompt = h[:N_CTX].reshape(BATCH, SEQ, D_MODEL)
    y_sample = h[N_CTX:].reshape(DEC_BATCH, DEC_SEQ, D_MODEL)
    st = lambda k: jnp.stack(outs[k], axis=1)
    return (y_prompt, y_sample, st("ak"), st("av"), st("ck"), st("cv"), st("bC"), st("bn"), st("bm"))
```

```python
import functools
import math

import jax
import jax.numpy as jnp
from jax import lax
from jax.experimental import pallas as pl
from jax.experimental.pallas import tpu as pltpu

F32 = jnp.float32
BF16 = jnp.bfloat16

D_MODEL = 1024
BATCH = 16
SEQ = 256
DEPTH = 2
DEC_BATCH = 4
DEC_SEQ = 2048
PAST_LEN = 512
GRID_W = 64
ROPE_THETA = 10000.0
EPS = 1e-6
HD_A = 64
H_A = 8
KV_A = 2
DK_B = 64
H_B = 4
D_C = 32
H_C = 4
D_FF = 2816
LOG2E = 1.4426950408889634

N_CTX = BATCH * SEQ
N_LAT = DEC_BATCH * DEC_SEQ
N_TOK = N_CTX + N_LAT

TM = 512
CTX_TILES = N_CTX // TM
LAT_TILES_PER_BATCH = DEC_SEQ // TM
FF_CHUNK = D_FF // 2
TC = 256
KB = 128
TQ = 256

Z_QB, Z_VB, Z_OB, Z_KC, Z_KA, Z_G = 0, 256, 512, 768, 1024, 1152
Z_W = 1280
ZT_QA, ZT_QC, ZT_VC, ZT_KB, ZT_VA, ZT_G = 0, 512, 768, 1024, 1280, 1408
ZT_W = 1424

VMEM_LIMIT = 56 * 1024 * 1024


def _cparams(sem):
    return pltpu.CompilerParams(dimension_semantics=sem, vmem_limit_bytes=VMEM_LIMIT)


def _const_spec(shape):
    nd = len(shape)
    return pl.BlockSpec(shape, lambda *_: (0,) * nd, pipeline_mode=pl.Buffered(1))


def _mod_row(i):
    return jnp.maximum(i - CTX_TILES, 0) // LAT_TILES_PER_BATCH + (i >= CTX_TILES).astype(jnp.int32)


def _rms(x, g, axis=-1):
    ms = jnp.mean(x * x, axis=axis, keepdims=True)
    return x * lax.rsqrt(ms + EPS) * g


def _norm_mod(x, g, mod_ref, k):
    shift = mod_ref[0, k:k + 1, :]
    scale = mod_ref[0, k + 1:k + 2, :]
    return _rms(x, g) * (1.0 + scale) + shift


def _ds(c, n):
    if isinstance(c, int):
        return pl.ds(c * n, n)
    return pl.ds(pl.multiple_of(c * n, n), n)


def _log_sigmoid(x):
    return jnp.minimum(x, 0.0) - jnp.log1p(jnp.exp(-jnp.abs(x)))


def _ada_kernel(c_ref, w_ref, b_ref, o_ref):
    c = c_ref[...]
    a = (c * jax.nn.sigmoid(c)).astype(BF16)
    o_ref[0] = jnp.dot(a, w_ref[0].astype(BF16), preferred_element_type=F32) + b_ref[0]


def _ada(cc, w_ada, b_ada):
    tn = 1024
    n = 9 * D_MODEL
    return pl.pallas_call(
        _ada_kernel,
        grid=(DEPTH, n // tn),
        in_specs=[
            pl.BlockSpec((8, D_MODEL), lambda l, j: (0, 0)),
            pl.BlockSpec((1, D_MODEL, tn), lambda l, j: (l, 0, j)),
            pl.BlockSpec((1, 1, tn), lambda l, j: (l, 0, j)),
        ],
        out_specs=pl.BlockSpec((1, 8, tn), lambda l, j: (l, 0, j)),
        out_shape=jax.ShapeDtypeStruct((DEPTH, 8, n), F32),
        compiler_params=_cparams(("arbitrary", "arbitrary")),
        name="ada",
    )(cc, w_ada, b_ada.reshape(DEPTH, 1, n))


def _ffn_kernel(h_ref, mod_ref, g_ref, wi_ref, wo_ref, gf_ref, o_ref, *, k, final):
    x = h_ref[...]
    xn = _norm_mod(x, g_ref[...], mod_ref, k).astype(BF16)
    acc = jnp.zeros((TM, D_MODEL), F32)
    for j in range(D_FF // FF_CHUNK):
        lo = j * FF_CHUNK
        g = jnp.dot(xn, wi_ref[:, lo:lo + FF_CHUNK], preferred_element_type=F32)
        u = jnp.dot(xn, wi_ref[:, D_FF + lo:D_FF + lo + FF_CHUNK], preferred_element_type=F32)
        a = (g * jax.nn.sigmoid(g) * u).astype(BF16)
        acc = acc + jnp.dot(a, wo_ref[lo:lo + FF_CHUNK, :], preferred_element_type=F32)
    gate = mod_ref[0, k + 2:k + 3, :]
    y = x + 0.5 * gate * acc
    if final:
        y = _rms(y, gf_ref[...])
    o_ref[...] = y


def _ffn(h, mods, g, wi, wo, gf, *, k, final):
    return pl.pallas_call(
        functools.partial(_ffn_kernel, k=k, final=final),
        grid=(N_TOK // TM,),
        in_specs=[
            pl.BlockSpec((TM, D_MODEL), lambda i: (i, 0)),
            pl.BlockSpec((1, 9, D_MODEL), lambda i: (_mod_row(i), 0, 0)),
            _const_spec((1, D_MODEL)),
            _const_spec((D_MODEL, 2 * D_FF)),
            _const_spec((D_FF, D_MODEL)),
            _const_spec((1, D_MODEL)),
        ],
        out_specs=pl.BlockSpec((TM, D_MODEL), lambda i: (i, 0)),
        out_shape=jax.ShapeDtypeStruct((N_TOK, D_MODEL), F32),
        compiler_params=_cparams(("arbitrary",)),
        name="ffn_final" if final else "ffn",
    )(h, mods, g, wi, wo, gf)


def _proj_in_kernel(h_ref, mod_ref, g_ref, w_ref, wt_ref, bz_ref, bt_ref, z_ref, zt_ref):
    xn = _norm_mod(h_ref[...], g_ref[...], mod_ref, 3).astype(BF16)
    z_ref[...] = jnp.dot(xn, w_ref[...], preferred_element_type=F32) + bz_ref[...]
    zt = lax.dot_general(wt_ref[...], xn, (((1,), (1,)), ((), ())), preferred_element_type=F32)
    zt = zt + bt_ref[...]
    for c in range(TM // TC):
        zt_ref[c] = zt[:, c * TC:(c + 1) * TC]


def _proj_in(h, mods, g, wz, wt, bz, bt):
    return pl.pallas_call(
        _proj_in_kernel,
        grid=(N_TOK // TM,),
        in_specs=[
            pl.BlockSpec((TM, D_MODEL), lambda i: (i, 0)),
            pl.BlockSpec((1, 9, D_MODEL), lambda i: (_mod_row(i), 0, 0)),
            _const_spec((1, D_MODEL)),
            _const_spec((D_MODEL, Z_W)),
            _const_spec((ZT_W, D_MODEL)),
            _const_spec((1, Z_W)),
            _const_spec((ZT_W, 1)),
        ],
        out_specs=[
            pl.BlockSpec((TM, Z_W), lambda i: (i, 0)),
            pl.BlockSpec((TM // TC, ZT_W, TC), lambda i: (i, 0, 0)),
        ],
        out_shape=[
            jax.ShapeDtypeStruct((N_TOK, Z_W), F32),
            jax.ShapeDtypeStruct((N_TOK // TC, ZT_W, TC), F32),
        ],
        compiler_params=_cparams(("arbitrary",)),
        name="proj_in",
    )(h, mods, g, wz, wt, bz, bt)


def _proj_out_kernel(h_ref, ya_ref, yb_ref, yc_ref, mod_ref, wo_ref, o_ref):
    y = jnp.dot(ya_ref[...], wo_ref[0:512, :], preferred_element_type=F32)
    y = y + jnp.dot(yb_ref[...], wo_ref[512:768, :], preferred_element_type=F32)
    y = y + jnp.dot(yc_ref[...], wo_ref[768:1024, :], preferred_element_type=F32)
    o_ref[...] = h_ref[...] + mod_ref[0, 5:6, :] * y


def _proj_out(h, ya, yb, yc, mods, wo):
    return pl.pallas_call(
        _proj_out_kernel,
        grid=(N_TOK // TM,),
        in_specs=[
            pl.BlockSpec((TM, D_MODEL), lambda i: (i, 0)),
            pl.BlockSpec((TM, 512), lambda i: (i, 0)),
            pl.BlockSpec((TM, 256), lambda i: (i, 0)),
            pl.BlockSpec((TM, 256), lambda i: (i, 0)),
            pl.BlockSpec((1, 9, D_MODEL), lambda i: (_mod_row(i), 0, 0)),
            _const_spec((D_MODEL, D_MODEL)),
        ],
        out_specs=pl.BlockSpec((TM, D_MODEL), lambda i: (i, 0)),
        out_shape=jax.ShapeDtypeStruct((N_TOK, D_MODEL), F32),
        compiler_params=_cparams(("arbitrary",)),
        name="proj_out",
    )(h, ya, yb, yc, mods, wo)


def _swap_halves(x, width, axis):
    n = x.shape[axis]
    half = width // 2
    parts = []
    for s in range(0, n, width):
        parts.append(lax.slice_in_dim(x, s + half, s + width, axis=axis))
        parts.append(lax.slice_in_dim(x, s, s + half, axis=axis))
    return jnp.concatenate(parts, axis=axis)


def _attn_kernel(*refs, mode, seq, has_ctx, rope, emit_k, lam_init):
    it = iter(refs)
    qt_ref, k_ref, vt_ref = next(it), next(it), next(it)
    if has_ctx:
        ck_ref, cvt_ref = next(it), next(it)
    if rope:
        cqt_ref, sqt_ref, ck_tab, sk_tab = next(it), next(it), next(it), next(it)
    if mode == "A":
        gq_ref, gk_ref = next(it), next(it)
    else:
        lam_ref, gc_ref = next(it), next(it)
    y_ref = next(it)
    if emit_k:
        kn_ref = next(it)
    kbuf, vbuf, acc_ref = next(it), next(it), next(it)

    n_kv = KV_A if mode == "A" else H_C
    per_kv = 8 // n_kv
    n_ctx = PAST_LEN if has_ctx else 0
    lk = n_ctx + seq
    width = HD_A if mode == "A" else D_C

    @pl.when(pl.program_id(1) == 0)
    def _fill():
        for h in range(n_kv):
            hs = slice(h * 64, (h + 1) * 64)
            if has_ctx:
                kbuf[h, 0:n_ctx, :] = ck_ref[0, :, hs].astype(BF16)
                for j in range(n_ctx // KB):
                    vbuf[h, j] = cvt_ref[0, h, :, j * KB:(j + 1) * KB].astype(BF16)
            for c in range(seq // TC):
                rows = slice(c * TC, (c + 1) * TC)
                k = k_ref[rows, hs]
                if mode == "A":
                    k = _rms(k, gk_ref[...])
                    if emit_k:
                        kn_ref[0, rows, hs] = k
                if rope:
                    k = k * ck_tab[rows, :] + _swap_halves(k, width, 1) * sk_tab[rows, :]
                kbuf[h, n_ctx + c * TC:n_ctx + (c + 1) * TC, :] = k.astype(BF16)
                vt = vt_ref[c, hs, :]
                for s in range(TC // KB):
                    vbuf[h, n_ctx // KB + c * (TC // KB) + s] = vt[:, s * KB:(s + 1) * KB].astype(BF16)

    qts = []
    if mode == "A":
        scale = HD_A ** -0.5 * LOG2E
        for hh in range(H_A):
            x = _rms(qt_ref[0, hh * 64:(hh + 1) * 64, :], gq_ref[...], axis=0)
            if rope:
                x = x * cqt_ref[0] + _swap_halves(x, width, 0) * sqt_ref[0]
            qts.append((x * scale).astype(BF16))
    else:
        scale = D_C ** -0.5 * LOG2E
        first_map = lax.broadcasted_iota(jnp.int32, (64, TQ), 0) < D_C
        for h in range(H_C):
            x = qt_ref[0, h * 64:(h + 1) * 64, :]
            if rope:
                x = x * cqt_ref[0] + _swap_halves(x, width, 0) * sqt_ref[0]
            x = x * scale
            qts.append(jnp.where(first_map, x, 0.0).astype(BF16))
            qts.append(jnp.where(first_map, 0.0, x).astype(BF16))

    acc_ref[...] = jnp.zeros_like(acc_ref)

    def step(j, carry):
        sts = []
        for g in range(n_kv):
            kb = kbuf[g, _ds(j, KB), :]
            for r in range(per_kv):
                sts.append(jnp.dot(kb, qts[g * per_kv + r], preferred_element_type=F32))
        out, alphas, ps = [], [], []
        for u in range(8):
            m, l = carry[u]
            mn = jnp.maximum(m, jnp.max(sts[u], axis=0, keepdims=True))
            alpha = jnp.exp2(m - mn)
            p = jnp.exp2(sts[u] - mn)
            out.append((mn, alpha * l + jnp.sum(p, axis=0, keepdims=True)))
            alphas.append(alpha)
            ps.append(p.astype(BF16))
        for g in range(n_kv):
            vb = vbuf[g, j]
            for r in range(per_kv):
                u = g * per_kv + r
                acc_ref[u] = alphas[u] * acc_ref[u] + jnp.dot(vb, ps[u], preferred_element_type=F32)
        return tuple(out)

    init = tuple((jnp.full((1, TQ), -jnp.inf, F32), jnp.zeros((1, TQ), F32)) for _ in range(8))
    stats = lax.fori_loop(0, lk // KB, step, init)

    if mode == "A":
        for pair in range(4):
            o = [acc_ref[u] / stats[u][1] for u in (2 * pair, 2 * pair + 1)]
            y_ref[:, pair * 128:(pair + 1) * 128] = jnp.concatenate(o, axis=0).T.astype(BF16)
    else:
        lp = lam_ref[...]
        lam = (jnp.exp(jnp.sum(lp[0:1] * lp[1:2], axis=1, keepdims=True))
               - jnp.exp(jnp.sum(lp[2:3] * lp[3:4], axis=1, keepdims=True)) + lam_init)
        for pair in range(2):
            o = []
            for h in (2 * pair, 2 * pair + 1):
                d = acc_ref[2 * h] / stats[2 * h][1] - lam * (acc_ref[2 * h + 1] / stats[2 * h + 1][1])
                o.append(_rms(d, gc_ref[...], axis=0) * (1.0 - lam_init))
            y_ref[:, pair * 128:(pair + 1) * 128] = jnp.concatenate(o, axis=0).T.astype(BF16)


def _attn(mode, z, zt, *, ctx, tables, params, lam_init, latent):
    if latent:
        nb, seq, row0 = DEC_BATCH, DEC_SEQ, N_CTX
    else:
        nb, seq, row0 = BATCH, SEQ, 0
    nq = seq // TQ
    n_kv = KV_A if mode == "A" else H_C
    n_ctx = PAST_LEN if ctx is not None else 0
    lk = n_ctx + seq
    qw = 512 if mode == "A" else 256
    vw = n_kv * 64
    q_blk = (ZT_QA if mode == "A" else ZT_QC) // qw
    k_blk = (Z_KA if mode == "A" else Z_KC) // vw
    v_blk = (ZT_VA if mode == "A" else ZT_VC) // vw
    emit_k = mode == "A" and not latent
    blk0 = row0 // seq

    in_specs = [
        pl.BlockSpec((1, qw, TC), lambda b, i: (row0 // TC + b * nq + i, q_blk, 0)),
        pl.BlockSpec((seq, vw), lambda b, i: (blk0 + b, k_blk)),
        pl.BlockSpec((seq // TC, vw, TC), lambda b, i: (blk0 + b, v_blk, 0)),
    ]
    args = [zt, z, zt]
    if ctx is not None:
        in_specs += [
            pl.BlockSpec((1, PAST_LEN, vw), lambda b, i: (b, 0, 0)),
            pl.BlockSpec((1, n_kv, 64, PAST_LEN), lambda b, i: (b, 0, 0, 0)),
        ]
        args += list(ctx)
    if tables is not None:
        cq, sq, cqt, sqt = tables
        in_specs += [
            pl.BlockSpec((1, 64, TC), lambda b, i: (i, 0, 0)),
            pl.BlockSpec((1, 64, TC), lambda b, i: (i, 0, 0)),
            _const_spec((seq, 64)),
            _const_spec((seq, 64)),
        ]
        args += [cqt, sqt, cq, sq]
    for p in params:
        in_specs.append(_const_spec(p.shape))
        args.append(p)

    out_specs = [pl.BlockSpec((TQ, qw), lambda b, i: (b * nq + i, 0))]
    out_shape = [jax.ShapeDtypeStruct((nb * seq, qw), BF16)]
    if emit_k:
        out_specs.append(pl.BlockSpec((1, seq, vw), lambda b, i: (b, 0, 0)))
        out_shape.append(jax.ShapeDtypeStruct((nb, seq, vw), F32))

    return pl.pallas_call(
        functools.partial(_attn_kernel, mode=mode, seq=seq, has_ctx=ctx is not None,
                          rope=tables is not None, emit_k=emit_k, lam_init=lam_init),
        grid=(nb, nq),
        in_specs=in_specs,
        out_specs=out_specs,
        out_shape=out_shape,
        scratch_shapes=[
            pltpu.VMEM((n_kv, lk, 64), BF16),
            pltpu.VMEM((n_kv, lk // KB, 64, KB), BF16),
            pltpu.VMEM((8, 64, TQ), F32),
        ],
        compiler_params=_cparams(("arbitrary", "arbitrary")),
        name=f"attn_{mode}_{'lat' if latent else 'ctx'}",
    )(*args)


def _mlstm_kernel(q_ref, v_ref, o_ref, gc_ref, kt_ref, gt_ref, s0_ref, m0_ref, gb_ref,
                  y_ref, s_ref, m_ref, hf_ref, hr_ref, *, seq):
    nc = seq // TC
    row = lax.broadcasted_iota(jnp.int32, (TC, TC), 0)
    col = lax.broadcasted_iota(jnp.int32, (TC, TC), 1)
    lower = col <= row
    upper = row <= col
    ones_col = (lax.broadcasted_iota(jnp.int32, (TC, 64), 1) == 0).astype(F32)
    units = [(d, h) for d in range(2) for h in range(H_B)]

    def chunk(j, carry):
        first = []
        for (d, h), (s_ext, m) in zip(units, carry):
            c = j if d == 0 else nc - 1 - j
            rows = _ds(c, TC)
            hs = slice(h * 64, (h + 1) * 64)
            qh = q_ref[rows, hs].astype(BF16)
            kt = kt_ref[c, hs, :] * (DK_B ** -0.5)
            qk = jnp.dot(qh, kt.astype(BF16), preferred_element_type=F32)
            inter = jnp.dot(qh, s_ext.astype(BF16), preferred_element_type=F32)
            first.append((c, rows, hs, kt, qk, inter))
        second = []
        for (d, h), (s_ext, m), (c, rows, hs, kt, qk, inter) in zip(units, carry, first):
            seen = lower if d == 0 else upper
            seen_t = upper if d == 0 else lower
            ri, rf = (2 * d) * H_B + h, (2 * d + 1) * H_B + h
            i_row = gt_ref[c, ri:ri + 1, :]
            f_row = _log_sigmoid(gt_ref[c, rf:rf + 1, :])
            f_col = _log_sigmoid(gc_ref[rows, rf:rf + 1])
            b_row = jnp.sum(jnp.where(seen_t, f_col, 0.0), axis=0, keepdims=True)
            b_col = jnp.sum(jnp.where(seen, f_row, 0.0), axis=1, keepdims=True)
            a_row = i_row - b_row
            lm = jnp.max(jnp.where(seen, a_row, -jnp.inf), axis=1, keepdims=True)
            mx = jnp.maximum(m, lm)
            sp = qk * jnp.exp(jnp.where(seen, a_row - mx, -jnp.inf))
            w = jnp.exp(m - mx)
            den = jnp.sum(sp, axis=1, keepdims=True) + w * inter[:, 64:65]
            floor = jnp.exp(-(b_col + mx))
            b_last = jnp.sum(f_row, axis=1, keepdims=True)
            mxl = jnp.maximum(m, jnp.max(a_row, axis=1, keepdims=True))
            ktw = (kt * jnp.exp(a_row - mxl)).astype(BF16)
            second.append((sp.astype(BF16), w * inter[:, 0:64], den, floor, ktw,
                           jnp.exp(m - mxl), b_last + mxl))
        out = []
        for (d, h), (s_ext, m), f, s in zip(units, carry, first, second):
            rows, hs = f[1], f[2]
            spb, inter_w, den, floor, ktw, decay, m_new = s
            vx = jnp.concatenate([v_ref[rows, hs], ones_col], axis=1).astype(BF16)
            num = jnp.dot(spb, vx, preferred_element_type=F32)[:, 0:64] + inter_w
            hv = num / jnp.maximum(jnp.abs(den), floor)
            if d == 0:
                hf_ref[rows, hs] = hv
            else:
                hr_ref[rows, hs] = hv
            upd = jnp.dot(ktw, vx, preferred_element_type=F32)
            out.append((decay * s_ext + upd, m_new))
        return tuple(out)

    init = tuple((s0_ref[0, u], m0_ref[0, u:u + 1, 0:1]) for u in range(2 * H_B))
    if nc == 1:
        fin = chunk(0, init)
    else:
        fin = lax.fori_loop(0, nc, chunk, init)
    for u, (s_ext, m) in enumerate(fin):
        s_ref[0, u] = s_ext
        m_ref[0, u:u + 1, :] = jnp.broadcast_to(m, (1, 128))

    def epilogue(c, _):
        rows = _ds(c, TC)
        for h in range(H_B):
            hs = slice(h * 64, (h + 1) * 64)
            x = hf_ref[rows, hs] + hr_ref[rows, hs]
            y = _rms(x, gb_ref[...]) * jax.nn.sigmoid(o_ref[rows, hs])
            y_ref[rows, hs] = y.astype(BF16)
        return 0

    if nc == 1:
        epilogue(0, 0)
    else:
        lax.fori_loop(0, nc, epilogue, 0)


def _mlstm(z, zt, s0, m0, gb, *, latent):
    if latent:
        nb, seq, row0 = DEC_BATCH, DEC_SEQ, N_CTX
    else:
        nb, seq, row0 = BATCH, SEQ, 0
    blk0 = row0 // seq
    nc = seq // TC
    zspec = lambda cb: pl.BlockSpec((seq, 256), lambda b: (blk0 + b, cb))
    return pl.pallas_call(
        functools.partial(_mlstm_kernel, seq=seq),
        grid=(nb,),
        in_specs=[
            zspec(Z_QB // 256), zspec(Z_VB // 256), zspec(Z_OB // 256),
            pl.BlockSpec((seq, 128), lambda b: (blk0 + b, Z_G // 128)),
            pl.BlockSpec((nc, 256, TC), lambda b: (blk0 + b, ZT_KB // 256, 0)),
            pl.BlockSpec((nc, 16, TC), lambda b: (blk0 + b, ZT_G // 16, 0)),
            pl.BlockSpec((1, 2 * H_B, 64, 128), lambda b: (b, 0, 0, 0)),
            pl.BlockSpec((1, 2 * H_B, 128), lambda b: (b, 0, 0)),
            _const_spec((1, 64)),
        ],
        out_specs=[
            pl.BlockSpec((seq, 256), lambda b: (b, 0)),
            pl.BlockSpec((1, 2 * H_B, 64, 128), lambda b: (b, 0, 0, 0)),
            pl.BlockSpec((1, 2 * H_B, 128), lambda b: (b, 0, 0)),
        ],
        out_shape=[
            jax.ShapeDtypeStruct((nb * seq, 256), BF16),
            jax.ShapeDtypeStruct((nb, 2 * H_B, 64, 128), F32),
            jax.ShapeDtypeStruct((nb, 2 * H_B, 128), F32),
        ],
        scratch_shapes=[pltpu.VMEM((seq, 256), F32), pltpu.VMEM((seq, 256), F32)],
        compiler_params=_cparams(("arbitrary",)),
        name=f"mlstm_{'lat' if latent else 'ctx'}",
    )(z, z, z, z, zt, zt, s0, m0, gb)


def _rope_tables(dim):
    t = jnp.arange(DEC_SEQ)
    row = (t // GRID_W).astype(F32)
    colp = (t % GRID_W).astype(F32)
    axis_dim = dim // 2
    freqs = ROPE_THETA ** (-jnp.arange(0, axis_dim, 2, dtype=F32) / axis_dim)
    ang = jnp.concatenate([row[:, None] * freqs, colp[:, None] * freqs], axis=-1)
    cos, sin = jnp.cos(ang), jnp.sin(ang)
    reps = 64 // dim
    cq = jnp.concatenate([cos, cos] * reps, axis=1)
    sq = jnp.concatenate([-sin, sin] * reps, axis=1)
    to_chunks = lambda x: x.T.reshape(64, DEC_SEQ // TC, TC).transpose(1, 0, 2)
    return cq, sq, to_chunks(cq), to_chunks(sq)


def _split_w_in(w, b_gates):
    a0, b0, c0 = 0, 768, 1808
    qa, ka, va = w[:, a0:a0 + 512], w[:, a0 + 512:a0 + 640], w[:, a0 + 640:a0 + 768]
    qb, kb = w[:, b0:b0 + 256], w[:, b0 + 256:b0 + 512]
    vb, ob = w[:, b0 + 512:b0 + 768], w[:, b0 + 768:b0 + 1024]
    gb = w[:, b0 + 1024:b0 + 1040]
    qc, kc, vc = w[:, c0:c0 + 256], w[:, c0 + 256:c0 + 512], w[:, c0 + 512:c0 + 768]
    wz = jnp.concatenate([qb, vb, ob, kc, ka, gb, jnp.zeros((D_MODEL, 112), w.dtype)], axis=1)
    wt = jnp.concatenate([qa, qc, vc, kb, va, gb], axis=1).T
    bz = jnp.zeros((1, Z_W), F32).at[0, Z_G:Z_G + 16].set(b_gates)
    bt = jnp.zeros((ZT_W, 1), F32).at[ZT_G:ZT_G + 16, 0].set(b_gates)
    return wz.astype(BF16), wt.astype(BF16), bz, bt


def _tokens_major(x):
    return x.transpose(0, 2, 1)


def kernel(x_prompt, x_sample, cache_a_k, cache_a_v, cache_c_k, cache_c_v, state_b_C, state_b_n,
           state_b_m, c, c_ctx, w_ada, b_ada, g_norm, w_ff_in, w_ff_out, w_in, w_out, g_qa, g_ka,
           b_gates, g_b, lam_q1, lam_k1, lam_q2, lam_k2, g_c, g_final):
    h = jnp.concatenate([x_prompt.reshape(N_CTX, D_MODEL), x_sample.reshape(N_LAT, D_MODEL)], axis=0)
    cc = jnp.concatenate([c_ctx[None, :], c, jnp.zeros((3, D_MODEL), F32)], axis=0)
    mods_all = _ada(cc, w_ada, b_ada).reshape(DEPTH, 8, 9, D_MODEL)
    tab_a = _rope_tables(HD_A)
    tab_c = _rope_tables(D_C)
    gf = g_final.reshape(1, D_MODEL)

    outs = {k: [] for k in ("ak", "av", "ck", "cv", "bC", "bn", "bm")}
    for l in range(DEPTH):
        mods = mods_all[l]
        lam_init = 0.8 - 0.6 * math.exp(-0.3 * l)
        gn = g_norm[l].reshape(3, 1, D_MODEL)
        h = _ffn(h, mods, gn[0], w_ff_in[l, 0].astype(BF16), w_ff_out[l, 0].astype(BF16), gf,
                 k=0, final=False)
        wz, wt, bz, bt = _split_w_in(w_in[l], b_gates[l])
        z, zt = _proj_in(h, mods, gn[1], wz, wt, bz, bt)

        gq, gk = g_qa[l].reshape(64, 1), g_ka[l].reshape(1, 64)
        lamp = jnp.stack([lam_q1[l], lam_k1[l], lam_q2[l], lam_k2[l]], axis=0)
        gc = g_c[l].reshape(64, 1)
        gb = g_b[l].reshape(1, 64)

        ya_c, kn = _attn("A", z, zt, ctx=None, tables=None, params=(gq, gk), lam_init=lam_init,
                         latent=False)
        (yc_c,) = _attn("C", z, zt, ctx=None, tables=None, params=(lamp, gc), lam_init=lam_init,
                        latent=False)
        zeros_s = jnp.zeros((BATCH, 2 * H_B, 64, 128), F32)
        zeros_m = jnp.zeros((BATCH, 2 * H_B, 128), F32)
        yb_c, s_fin, m_fin = _mlstm(z, zt, zeros_s, zeros_m, gb, latent=False)

        ck_a = cache_a_k[:, l].reshape(DEC_BATCH, PAST_LEN, KV_A * HD_A)
        cvt_a = cache_a_v[:, l].transpose(0, 2, 3, 1)
        (ya_l,) = _attn("A", z, zt, ctx=(ck_a, cvt_a), tables=tab_a, params=(gq, gk),
                        lam_init=lam_init, latent=True)
        ck_c = cache_c_k[:, l].reshape(DEC_BATCH, PAST_LEN, H_C * 2 * D_C)
        cvt_c = cache_c_v[:, l].transpose(0, 2, 3, 1)
        (yc_l,) = _attn("C", z, zt, ctx=(ck_c, cvt_c), tables=tab_c, params=(lamp, gc),
                        lam_init=lam_init, latent=True)
        s0 = jnp.concatenate([state_b_C[:, l], state_b_n[:, l][..., None],
                              jnp.zeros((DEC_BATCH, 2, H_B, DK_B, 63), F32)], axis=-1)
        s0 = s0.reshape(DEC_BATCH, 2 * H_B, DK_B, 128)
        m0 = jnp.broadcast_to(state_b_m[:, l].reshape(DEC_BATCH, 2 * H_B, 1), (DEC_BATCH, 2 * H_B, 128))
        yb_l, _, _ = _mlstm(z, zt, s0, m0, gb, latent=True)

        ya = jnp.concatenate([ya_c, ya_l], axis=0)
        yb = jnp.concatenate([yb_c, yb_l], axis=0)
        yc = jnp.concatenate([yc_c, yc_l], axis=0)
        h = _proj_out(h, ya, yb, yc, mods, w_out[l].astype(BF16))
        h = _ffn(h, mods, gn[2], w_ff_in[l, 1].astype(BF16), w_ff_out[l, 1].astype(BF16), gf,
                 k=6, final=(l == DEPTH - 1))

        zc = z[:N_CTX]
        ztc = zt[:N_CTX // TC]
        outs["ak"].append(kn.reshape(BATCH, SEQ, KV_A, HD_A))
        outs["av"].append(_tokens_major(ztc[:, ZT_VA:ZT_VA + 128, :]).reshape(BATCH, SEQ, KV_A, HD_A))
        outs["ck"].append(zc[:, Z_KC:Z_KC + 256].reshape(BATCH, SEQ, H_C, 2, D_C))
        outs["cv"].append(_tokens_major(ztc[:, ZT_VC:ZT_VC + 256, :]).reshape(BATCH, SEQ, H_C, 2 * D_C))
        s_fin = s_fin.reshape(BATCH, 2, H_B, DK_B, 128)
        outs["bC"].append(s_fin[..., 0:64])
        outs["bn"].append(s_fin[..., 64])
        outs["bm"].append(m_fin[:, :, 0].reshape(BATCH, 2, H_B))

    y_prompt = h[:N_CTX].reshape(BATCH, SEQ, D_MODEL)
    y_sample = h[N_CTX:].reshape(DEC_BATCH, DEC_SEQ, D_MODEL)
    st = lambda k: jnp.stack(outs[k], axis=1)
    return (y_prompt, y_sample, st("ak"), st("av"), st("ck"), st("cv"), st("bC"), st("bn"), st("bm"))
```

```python
import functools
import math

import jax
import jax.numpy as jnp
from jax import lax
from jax.experimental import pallas as pl
from jax.experimental.pallas import tpu as pltpu

F32 = jnp.float32
BF16 = jnp.bfloat16

D_MODEL = 1024
BATCH = 16
SEQ = 256
DEPTH = 2
DEC_BATCH = 4
DEC_SEQ = 2048
PAST_LEN = 512
GRID_W = 64
ROPE_THETA = 10000.0
EPS = 1e-6
HD_A = 64
H_A = 8
KV_A = 2
DK_B = 64
H_B = 4
D_C = 32
H_C = 4
D_FF = 2816
LOG2E = 1.4426950408889634

N_CTX = BATCH * SEQ
N_LAT = DEC_BATCH * DEC_SEQ
N_TOK = N_CTX + N_LAT

TM = 512
CTX_TILES = N_CTX // TM
LAT_TILES_PER_BATCH = DEC_SEQ // TM
FF_CHUNK = D_FF // 2
TC = 256
SB = 256
TQ = 256
VROWS = 80

Z_QB, Z_VB, Z_OB, Z_KC, Z_KA, Z_G = 0, 256, 512, 768, 1024, 1152
Z_W = 1280
ZT_QA, ZT_QC, ZT_VC, ZT_KB, ZT_VA, ZT_G = 0, 512, 768, 1024, 1280, 1408
ZT_W = 1424

VMEM_LIMIT = 56 * 1024 * 1024


def _cparams(sem):
    return pltpu.CompilerParams(dimension_semantics=sem, vmem_limit_bytes=VMEM_LIMIT)


def _const_spec(shape):
    nd = len(shape)
    return pl.BlockSpec(shape, lambda *_: (0,) * nd, pipeline_mode=pl.Buffered(1))


def _mod_row(i):
    return jnp.maximum(i - CTX_TILES, 0) // LAT_TILES_PER_BATCH + (i >= CTX_TILES).astype(jnp.int32)


def _rms(x, g, axis=-1):
    ms = jnp.mean(x * x, axis=axis, keepdims=True)
    return x * lax.rsqrt(ms + EPS) * g


def _norm_mod(x, g, mod_ref, k):
    shift = mod_ref[0, k:k + 1, :]
    scale = mod_ref[0, k + 1:k + 2, :]
    return _rms(x, g) * (1.0 + scale) + shift


def _ds(c, n):
    if isinstance(c, int):
        return pl.ds(c * n, n)
    return pl.ds(pl.multiple_of(c * n, n), n)


def _log_sigmoid(x):
    return jnp.minimum(x, 0.0) - jnp.log1p(jnp.exp(-jnp.abs(x)))


def _ada_kernel(c_ref, w_ref, b_ref, o_ref):
    c = c_ref[...]
    a = (c * jax.nn.sigmoid(c)).astype(BF16)
    o_ref[0] = jnp.dot(a, w_ref[0].astype(BF16), preferred_element_type=F32) + b_ref[0]


def _ada(cc, w_ada, b_ada):
    tn = 1024
    n = 9 * D_MODEL
    return pl.pallas_call(
        _ada_kernel,
        grid=(DEPTH, n // tn),
        in_specs=[
            pl.BlockSpec((8, D_MODEL), lambda l, j: (0, 0)),
            pl.BlockSpec((1, D_MODEL, tn), lambda l, j: (l, 0, j)),
            pl.BlockSpec((1, 1, tn), lambda l, j: (l, 0, j)),
        ],
        out_specs=pl.BlockSpec((1, 8, tn), lambda l, j: (l, 0, j)),
        out_shape=jax.ShapeDtypeStruct((DEPTH, 8, n), F32),
        compiler_params=_cparams(("arbitrary", "arbitrary")),
        name="ada",
    )(cc, w_ada, b_ada.reshape(DEPTH, 1, n))


def _ffn_kernel(h_ref, mod_ref, g_ref, wi_ref, wo_ref, gf_ref, o_ref, *, k, final):
    x = h_ref[...]
    xn = _norm_mod(x, g_ref[...], mod_ref, k).astype(BF16)
    acc = jnp.zeros((TM, D_MODEL), F32)
    for j in range(D_FF // FF_CHUNK):
        lo = j * FF_CHUNK
        g = jnp.dot(xn, wi_ref[:, lo:lo + FF_CHUNK], preferred_element_type=F32)
        u = jnp.dot(xn, wi_ref[:, D_FF + lo:D_FF + lo + FF_CHUNK], preferred_element_type=F32)
        a = (g * jax.nn.sigmoid(g) * u).astype(BF16)
        acc = acc + jnp.dot(a, wo_ref[lo:lo + FF_CHUNK, :], preferred_element_type=F32)
    gate = mod_ref[0, k + 2:k + 3, :]
    y = x + 0.5 * gate * acc
    if final:
        y = _rms(y, gf_ref[...])
    o_ref[...] = y


def _ffn(h, mods, g, wi, wo, gf, *, k, final):
    return pl.pallas_call(
        functools.partial(_ffn_kernel, k=k, final=final),
        grid=(N_TOK // TM,),
        in_specs=[
            pl.BlockSpec((TM, D_MODEL), lambda i: (i, 0)),
            pl.BlockSpec((1, 9, D_MODEL), lambda i: (_mod_row(i), 0, 0)),
            _const_spec((1, D_MODEL)),
            _const_spec((D_MODEL, 2 * D_FF)),
            _const_spec((D_FF, D_MODEL)),
            _const_spec((1, D_MODEL)),
        ],
        out_specs=pl.BlockSpec((TM, D_MODEL), lambda i: (i, 0)),
        out_shape=jax.ShapeDtypeStruct((N_TOK, D_MODEL), F32),
        compiler_params=_cparams(("arbitrary",)),
        name="ffn_final" if final else "ffn",
    )(h, mods, g, wi, wo, gf)


def _proj_in_kernel(h_ref, mod_ref, g_ref, w_ref, wt_ref, bz_ref, bt_ref, z_ref, zt_ref):
    xn = _norm_mod(h_ref[...], g_ref[...], mod_ref, 3).astype(BF16)
    z_ref[...] = jnp.dot(xn, w_ref[...], preferred_element_type=F32) + bz_ref[...]
    zt = lax.dot_general(wt_ref[...], xn, (((1,), (1,)), ((), ())), preferred_element_type=F32)
    zt = zt + bt_ref[...]
    for c in range(TM // TC):
        zt_ref[c] = zt[:, c * TC:(c + 1) * TC]


def _proj_in(h, mods, g, wz, wt, bz, bt):
    return pl.pallas_call(
        _proj_in_kernel,
        grid=(N_TOK // TM,),
        in_specs=[
            pl.BlockSpec((TM, D_MODEL), lambda i: (i, 0)),
            pl.BlockSpec((1, 9, D_MODEL), lambda i: (_mod_row(i), 0, 0)),
            _const_spec((1, D_MODEL)),
            _const_spec((D_MODEL, Z_W)),
            _const_spec((ZT_W, D_MODEL)),
            _const_spec((1, Z_W)),
            _const_spec((ZT_W, 1)),
        ],
        out_specs=[
            pl.BlockSpec((TM, Z_W), lambda i: (i, 0)),
            pl.BlockSpec((TM // TC, ZT_W, TC), lambda i: (i, 0, 0)),
        ],
        out_shape=[
            jax.ShapeDtypeStruct((N_TOK, Z_W), F32),
            jax.ShapeDtypeStruct((N_TOK // TC, ZT_W, TC), F32),
        ],
        compiler_params=_cparams(("arbitrary",)),
        name="proj_in",
    )(h, mods, g, wz, wt, bz, bt)


def _proj_out_kernel(h_ref, ya_ref, yb_ref, yc_ref, mod_ref, wo_ref, o_ref):
    y = jnp.dot(ya_ref[...], wo_ref[0:512, :], preferred_element_type=F32)
    y = y + jnp.dot(yb_ref[...], wo_ref[512:768, :], preferred_element_type=F32)
    y = y + jnp.dot(yc_ref[...], wo_ref[768:1024, :], preferred_element_type=F32)
    o_ref[...] = h_ref[...] + mod_ref[0, 5:6, :] * y


def _proj_out(h, ya, yb, yc, mods, wo):
    return pl.pallas_call(
        _proj_out_kernel,
        grid=(N_TOK // TM,),
        in_specs=[
            pl.BlockSpec((TM, D_MODEL), lambda i: (i, 0)),
            pl.BlockSpec((TM, 512), lambda i: (i, 0)),
            pl.BlockSpec((TM, 256), lambda i: (i, 0)),
            pl.BlockSpec((TM, 256), lambda i: (i, 0)),
            pl.BlockSpec((1, 9, D_MODEL), lambda i: (_mod_row(i), 0, 0)),
            _const_spec((D_MODEL, D_MODEL)),
        ],
        out_specs=pl.BlockSpec((TM, D_MODEL), lambda i: (i, 0)),
        out_shape=jax.ShapeDtypeStruct((N_TOK, D_MODEL), F32),
        compiler_params=_cparams(("arbitrary",)),
        name="proj_out",
    )(h, ya, yb, yc, mods, wo)


def _swap_halves(x, width, axis):
    n = x.shape[axis]
    half = width // 2
    parts = []
    for s in range(0, n, width):
        parts.append(lax.slice_in_dim(x, s + half, s + width, axis=axis))
        parts.append(lax.slice_in_dim(x, s, s + half, axis=axis))
    return jnp.concatenate(parts, axis=axis)


def _attn_kernel(*refs, mode, seq, has_ctx, rope, emit_k, lam_init):
    it = iter(refs)
    qt_ref, k_ref, vt_ref = next(it), next(it), next(it)
    if has_ctx:
        ck_ref, cvt_ref = next(it), next(it)
    if rope:
        cqt_ref, sqt_ref, ck_tab, sk_tab = next(it), next(it), next(it), next(it)
    if mode == "A":
        gq_ref, gk_ref = next(it), next(it)
    else:
        lam_ref, gc_ref = next(it), next(it)
    y_ref = next(it)
    if emit_k:
        kn_ref = next(it)
    kbuf, vbuf, s_ref, p_ref = next(it), next(it), next(it), next(it)

    n_kv = KV_A if mode == "A" else H_C
    per_kv = 8 // n_kv
    n_ctx = PAST_LEN if has_ctx else 0
    lk = n_ctx + seq
    width = HD_A if mode == "A" else D_C

    @pl.when(pl.program_id(1) == 0)
    def _fill():
        ones_rows = (lax.broadcasted_iota(jnp.int32, (VROWS - 64, lk), 0) == 0).astype(BF16)
        for h in range(n_kv):
            hs = slice(h * 64, (h + 1) * 64)
            vbuf[h, 64:VROWS, :] = ones_rows
            if has_ctx:
                kbuf[h, 0:n_ctx, :] = ck_ref[0, :, hs].astype(BF16)
                vbuf[h, 0:64, 0:n_ctx] = cvt_ref[0, h].astype(BF16)
            for c in range(seq // TC):
                rows = slice(c * TC, (c + 1) * TC)
                k = k_ref[rows, hs]
                if mode == "A":
                    k = _rms(k, gk_ref[...])
                    if emit_k:
                        kn_ref[0, rows, hs] = k
                if rope:
                    k = k * ck_tab[rows, :] + _swap_halves(k, width, 1) * sk_tab[rows, :]
                kbuf[h, n_ctx + c * TC:n_ctx + (c + 1) * TC, :] = k.astype(BF16)
                vbuf[h, 0:64, n_ctx + c * TC:n_ctx + (c + 1) * TC] = vt_ref[c, hs, :].astype(BF16)

    qts = []
    if mode == "A":
        scale = HD_A ** -0.5 * LOG2E
        for hh in range(H_A):
            x = _rms(qt_ref[0, hh * 64:(hh + 1) * 64, :], gq_ref[...], axis=0)
            if rope:
                x = x * cqt_ref[0] + _swap_halves(x, width, 0) * sqt_ref[0]
            qts.append((x * scale).astype(BF16))
    else:
        scale = D_C ** -0.5 * LOG2E
        first_map = lax.broadcasted_iota(jnp.int32, (64, TQ), 0) < D_C
        for h in range(H_C):
            x = qt_ref[0, h * 64:(h + 1) * 64, :]
            if rope:
                x = x * cqt_ref[0] + _swap_halves(x, width, 0) * sqt_ref[0]
            x = x * scale
            qts.append(jnp.where(first_map, x, 0.0).astype(BF16))
            qts.append(jnp.where(first_map, 0.0, x).astype(BF16))

    m_prev = None
    for s in range(9):
        def stage_block(j, mrun, s=s, m_prev=m_prev):
            rows = _ds(j, SB)
            if s < 8:
                st = jnp.dot(kbuf[s // per_kv, rows, :], qts[s], preferred_element_type=F32)
                s_ref[s % 2, rows, :] = st
                for r in range(SB // 8):
                    mrun = jnp.maximum(mrun, st[8 * r:8 * r + 8])
            if s >= 1:
                prev = s_ref[(s - 1) % 2, rows, :]
                p_ref[rows, (s - 1) * TQ:s * TQ] = jnp.exp2(prev - m_prev).astype(BF16)
            return mrun

        mrun = lax.fori_loop(0, lk // SB, stage_block, jnp.full((8, TQ), -jnp.inf, F32),
                             unroll=True)
        m_prev = jnp.max(mrun, axis=0, keepdims=True)

    outs = []
    for u in range(8):
        ox = jnp.dot(vbuf[u // per_kv], p_ref[:, u * TQ:(u + 1) * TQ], preferred_element_type=F32)
        outs.append(ox[0:64] / ox[64:65])

    if mode == "A":
        for pair in range(4):
            o = jnp.concatenate(outs[2 * pair:2 * pair + 2], axis=0)
            y_ref[:, pair * 128:(pair + 1) * 128] = o.T.astype(BF16)
    else:
        lp = lam_ref[...]
        lam = (jnp.exp(jnp.sum(lp[0:1] * lp[1:2], axis=1, keepdims=True))
               - jnp.exp(jnp.sum(lp[2:3] * lp[3:4], axis=1, keepdims=True)) + lam_init)
        for pair in range(2):
            o = []
            for h in (2 * pair, 2 * pair + 1):
                d = outs[2 * h] - lam * outs[2 * h + 1]
                o.append(_rms(d, gc_ref[...], axis=0) * (1.0 - lam_init))
            y_ref[:, pair * 128:(pair + 1) * 128] = jnp.concatenate(o, axis=0).T.astype(BF16)


def _attn(mode, z, zt, *, ctx, tables, params, lam_init, latent):
    if latent:
        nb, seq, row0 = DEC_BATCH, DEC_SEQ, N_CTX
    else:
        nb, seq, row0 = BATCH, SEQ, 0
    nq = seq // TQ
    n_kv = KV_A if mode == "A" else H_C
    n_ctx = PAST_LEN if ctx is not None else 0
    lk = n_ctx + seq
    qw = 512 if mode == "A" else 256
    vw = n_kv * 64
    q_blk = (ZT_QA if mode == "A" else ZT_QC) // qw
    k_blk = (Z_KA if mode == "A" else Z_KC) // vw
    v_blk = (ZT_VA if mode == "A" else ZT_VC) // vw
    emit_k = mode == "A" and not latent
    blk0 = row0 // seq

    in_specs = [
        pl.BlockSpec((1, qw, TC), lambda b, i: (row0 // TC + b * nq + i, q_blk, 0)),
        pl.BlockSpec((seq, vw), lambda b, i: (blk0 + b, k_blk)),
        pl.BlockSpec((seq // TC, vw, TC), lambda b, i: (blk0 + b, v_blk, 0)),
    ]
    args = [zt, z, zt]
    if ctx is not None:
        in_specs += [
            pl.BlockSpec((1, PAST_LEN, vw), lambda b, i: (b, 0, 0)),
            pl.BlockSpec((1, n_kv, 64, PAST_LEN), lambda b, i: (b, 0, 0, 0)),
        ]
        args += list(ctx)
    if tables is not None:
        cq, sq, cqt, sqt = tables
        in_specs += [
            pl.BlockSpec((1, 64, TC), lambda b, i: (i, 0, 0)),
            pl.BlockSpec((1, 64, TC), lambda b, i: (i, 0, 0)),
            _const_spec((seq, 64)),
            _const_spec((seq, 64)),
        ]
        args += [cqt, sqt, cq, sq]
    for p in params:
        in_specs.append(_const_spec(p.shape))
        args.append(p)

    out_specs = [pl.BlockSpec((TQ, qw), lambda b, i: (b * nq + i, 0))]
    out_shape = [jax.ShapeDtypeStruct((nb * seq, qw), BF16)]
    if emit_k:
        out_specs.append(pl.BlockSpec((1, seq, vw), lambda b, i: (b, 0, 0)))
        out_shape.append(jax.ShapeDtypeStruct((nb, seq, vw), F32))

    return pl.pallas_call(
        functools.partial(_attn_kernel, mode=mode, seq=seq, has_ctx=ctx is not None,
                          rope=tables is not None, emit_k=emit_k, lam_init=lam_init),
        grid=(nb, nq),
        in_specs=in_specs,
        out_specs=out_specs,
        out_shape=out_shape,
        scratch_shapes=[
            pltpu.VMEM((n_kv, lk, 64), BF16),
            pltpu.VMEM((n_kv, VROWS, lk), BF16),
            pltpu.VMEM((2, lk, TQ), F32),
            pltpu.VMEM((lk, 8 * TQ), BF16),
        ],
        compiler_params=_cparams(("arbitrary", "arbitrary")),
        name=f"attn_{mode}_{'lat' if latent else 'ctx'}",
    )(*args)


def _mlstm_kernel(q_ref, v_ref, o_ref, gc_ref, kt_ref, gt_ref, s0_ref, m0_ref, gb_ref,
                  y_ref, s_ref, m_ref, hf_ref, hr_ref, *, seq):
    nc = seq // TC
    row = lax.broadcasted_iota(jnp.int32, (TC, TC), 0)
    col = lax.broadcasted_iota(jnp.int32, (TC, TC), 1)
    lower = col <= row
    upper = row <= col
    ones_col = (lax.broadcasted_iota(jnp.int32, (TC, 64), 1) == 0).astype(F32)
    units = [(d, h) for d in range(2) for h in range(H_B)]

    def chunk(j, carry):
        first = []
        for (d, h), (s_ext, m) in zip(units, carry):
            c = j if d == 0 else nc - 1 - j
            rows = _ds(c, TC)
            hs = slice(h * 64, (h + 1) * 64)
            qh = q_ref[rows, hs].astype(BF16)
            kt = kt_ref[c, hs, :] * (DK_B ** -0.5)
            qk = jnp.dot(qh, kt.astype(BF16), preferred_element_type=F32)
            inter = jnp.dot(qh, s_ext.astype(BF16), preferred_element_type=F32)
            first.append((c, rows, hs, kt, qk, inter))
        second = []
        for (d, h), (s_ext, m), (c, rows, hs, kt, qk, inter) in zip(units, carry, first):
            seen = lower if d == 0 else upper
            seen_t = upper if d == 0 else lower
            ri, rf = (2 * d) * H_B + h, (2 * d + 1) * H_B + h
            i_row = gt_ref[c, ri:ri + 1, :]
            f_row = _log_sigmoid(gt_ref[c, rf:rf + 1, :])
            f_col = _log_sigmoid(gc_ref[rows, rf:rf + 1])
            b_row = jnp.sum(jnp.where(seen_t, f_col, 0.0), axis=0, keepdims=True)
            b_col = jnp.sum(jnp.where(seen, f_row, 0.0), axis=1, keepdims=True)
            a_row = i_row - b_row
            lm = jnp.max(jnp.where(seen, a_row, -jnp.inf), axis=1, keepdims=True)
            mx = jnp.maximum(m, lm)
            sp = qk * jnp.exp(jnp.where(seen, a_row - mx, -jnp.inf))
            w = jnp.exp(m - mx)
            den = jnp.sum(sp, axis=1, keepdims=True) + w * inter[:, 64:65]
            floor = jnp.exp(-(b_col + mx))
            b_last = jnp.sum(f_row, axis=1, keepdims=True)
            mxl = jnp.maximum(m, jnp.max(a_row, axis=1, keepdims=True))
            ktw = (kt * jnp.exp(a_row - mxl)).astype(BF16)
            second.append((sp.astype(BF16), w * inter[:, 0:64], den, floor, ktw,
                           jnp.exp(m - mxl), b_last + mxl))
        out = []
        for (d, h), (s_ext, m), f, s in zip(units, carry, first, second):
            rows, hs = f[1], f[2]
            spb, inter_w, den, floor, ktw, decay, m_new = s
            vx = jnp.concatenate([v_ref[rows, hs], ones_col], axis=1).astype(BF16)
            num = jnp.dot(spb, vx, preferred_element_type=F32)[:, 0:64] + inter_w
            hv = num / jnp.maximum(jnp.abs(den), floor)
            if d == 0:
                hf_ref[rows, hs] = hv
            else:
                hr_ref[rows, hs] = hv
            upd = jnp.dot(ktw, vx, preferred_element_type=F32)
            out.append((decay * s_ext + upd, m_new))
        return tuple(out)

    init = tuple((s0_ref[0, u], m0_ref[0, u:u + 1, 0:1]) for u in range(2 * H_B))
    if nc == 1:
        fin = chunk(0, init)
    else:
        fin = lax.fori_loop(0, nc, chunk, init)
    for u, (s_ext, m) in enumerate(fin):
        s_ref[0, u] = s_ext
        m_ref[0, u:u + 1, :] = jnp.broadcast_to(m, (1, 128))

    def epilogue(c, _):
        rows = _ds(c, TC)
        for h in range(H_B):
            hs = slice(h * 64, (h + 1) * 64)
            x = hf_ref[rows, hs] + hr_ref[rows, hs]
            y = _rms(x, gb_ref[...]) * jax.nn.sigmoid(o_ref[rows, hs])
            y_ref[rows, hs] = y.astype(BF16)
        return 0

    if nc == 1:
        epilogue(0, 0)
    else:
        lax.fori_loop(0, nc, epilogue, 0)


def _mlstm(z, zt, s0, m0, gb, *, latent):
    if latent:
        nb, seq, row0 = DEC_BATCH, DEC_SEQ, N_CTX
    else:
        nb, seq, row0 = BATCH, SEQ, 0
    blk0 = row0 // seq
    nc = seq // TC
    zspec = lambda cb: pl.BlockSpec((seq, 256), lambda b: (blk0 + b, cb))
    return pl.pallas_call(
        functools.partial(_mlstm_kernel, seq=seq),
        grid=(nb,),
        in_specs=[
            zspec(Z_QB // 256), zspec(Z_VB // 256), zspec(Z_OB // 256),
            pl.BlockSpec((seq, 128), lambda b: (blk0 + b, Z_G // 128)),
            pl.BlockSpec((nc, 256, TC), lambda b: (blk0 + b, ZT_KB // 256, 0)),
            pl.BlockSpec((nc, 16, TC), lambda b: (blk0 + b, ZT_G // 16, 0)),
            pl.BlockSpec((1, 2 * H_B, 64, 128), lambda b: (b, 0, 0, 0)),
            pl.BlockSpec((1, 2 * H_B, 128), lambda b: (b, 0, 0)),
            _const_spec((1, 64)),
        ],
        out_specs=[
            pl.BlockSpec((seq, 256), lambda b: (b, 0)),
            pl.BlockSpec((1, 2 * H_B, 64, 128), lambda b: (b, 0, 0, 0)),
            pl.BlockSpec((1, 2 * H_B, 128), lambda b: (b, 0, 0)),
        ],
        out_shape=[
            jax.ShapeDtypeStruct((nb * seq, 256), BF16),
            jax.ShapeDtypeStruct((nb, 2 * H_B, 64, 128), F32),
            jax.ShapeDtypeStruct((nb, 2 * H_B, 128), F32),
        ],
        scratch_shapes=[pltpu.VMEM((seq, 256), F32), pltpu.VMEM((seq, 256), F32)],
        compiler_params=_cparams(("arbitrary",)),
        name=f"mlstm_{'lat' if latent else 'ctx'}",
    )(z, z, z, z, zt, zt, s0, m0, gb)


def _rope_tables(dim):
    t = jnp.arange(DEC_SEQ)
    row = (t // GRID_W).astype(F32)
    colp = (t % GRID_W).astype(F32)
    axis_dim = dim // 2
    freqs = ROPE_THETA ** (-jnp.arange(0, axis_dim, 2, dtype=F32) / axis_dim)
    ang = jnp.concatenate([row[:, None] * freqs, colp[:, None] * freqs], axis=-1)
    cos, sin = jnp.cos(ang), jnp.sin(ang)
    reps = 64 // dim
    cq = jnp.concatenate([cos, cos] * reps, axis=1)
    sq = jnp.concatenate([-sin, sin] * reps, axis=1)
    to_chunks = lambda x: x.T.reshape(64, DEC_SEQ // TC, TC).transpose(1, 0, 2)
    return cq, sq, to_chunks(cq), to_chunks(sq)


def _split_w_in(w, b_gates):
    a0, b0, c0 = 0, 768, 1808
    qa, ka, va = w[:, a0:a0 + 512], w[:, a0 + 512:a0 + 640], w[:, a0 + 640:a0 + 768]
    qb, kb = w[:, b0:b0 + 256], w[:, b0 + 256:b0 + 512]
    vb, ob = w[:, b0 + 512:b0 + 768], w[:, b0 + 768:b0 + 1024]
    gb = w[:, b0 + 1024:b0 + 1040]
    qc, kc, vc = w[:, c0:c0 + 256], w[:, c0 + 256:c0 + 512], w[:, c0 + 512:c0 + 768]
    wz = jnp.concatenate([qb, vb, ob, kc, ka, gb, jnp.zeros((D_MODEL, 112), w.dtype)], axis=1)
    wt = jnp.concatenate([qa, qc, vc, kb, va, gb], axis=1).T
    bz = jnp.zeros((1, Z_W), F32).at[0, Z_G:Z_G + 16].set(b_gates)
    bt = jnp.zeros((ZT_W, 1), F32).at[ZT_G:ZT_G + 16, 0].set(b_gates)
    return wz.astype(BF16), wt.astype(BF16), bz, bt


def _tokens_major(x):
    return x.transpose(0, 2, 1)


def kernel(x_prompt, x_sample, cache_a_k, cache_a_v, cache_c_k, cache_c_v, state_b_C, state_b_n,
           state_b_m, c, c_ctx, w_ada, b_ada, g_norm, w_ff_in, w_ff_out, w_in, w_out, g_qa, g_ka,
           b_gates, g_b, lam_q1, lam_k1, lam_q2, lam_k2, g_c, g_final):
    h = jnp.concatenate([x_prompt.reshape(N_CTX, D_MODEL), x_sample.reshape(N_LAT, D_MODEL)], axis=0)
    cc = jnp.concatenate([c_ctx[None, :], c, jnp.zeros((3, D_MODEL), F32)], axis=0)
    mods_all = _ada(cc, w_ada, b_ada).reshape(DEPTH, 8, 9, D_MODEL)
    tab_a = _rope_tables(HD_A)
    tab_c = _rope_tables(D_C)
    gf = g_final.reshape(1, D_MODEL)

    outs = {k: [] for k in ("ak", "av", "ck", "cv", "bC", "bn", "bm")}
    for l in range(DEPTH):
        mods = mods_all[l]
        lam_init = 0.8 - 0.6 * math.exp(-0.3 * l)
        gn = g_norm[l].reshape(3, 1, D_MODEL)
        h = _ffn(h, mods, gn[0], w_ff_in[l, 0].astype(BF16), w_ff_out[l, 0].astype(BF16), gf,
                 k=0, final=False)
        wz, wt, bz, bt = _split_w_in(w_in[l], b_gates[l])
        z, zt = _proj_in(h, mods, gn[1], wz, wt, bz, bt)

        gq, gk = g_qa[l].reshape(64, 1), g_ka[l].reshape(1, 64)
        lamp = jnp.stack([lam_q1[l], lam_k1[l], lam_q2[l], lam_k2[l]], axis=0)
        gc = g_c[l].reshape(64, 1)
        gb = g_b[l].reshape(1, 64)

        ya_c, kn = _attn("A", z, zt, ctx=None, tables=None, params=(gq, gk), lam_init=lam_init,
                         latent=False)
        (yc_c,) = _attn("C", z, zt, ctx=None, tables=None, params=(lamp, gc), lam_init=lam_init,
                        latent=False)
        zeros_s = jnp.zeros((BATCH, 2 * H_B, 64, 128), F32)
        zeros_m = jnp.zeros((BATCH, 2 * H_B, 128), F32)
        yb_c, s_fin, m_fin = _mlstm(z, zt, zeros_s, zeros_m, gb, latent=False)

        ck_a = cache_a_k[:, l].reshape(DEC_BATCH, PAST_LEN, KV_A * HD_A)
        cvt_a = cache_a_v[:, l].transpose(0, 2, 3, 1)
        (ya_l,) = _attn("A", z, zt, ctx=(ck_a, cvt_a), tables=tab_a, params=(gq, gk),
                        lam_init=lam_init, latent=True)
        ck_c = cache_c_k[:, l].reshape(DEC_BATCH, PAST_LEN, H_C * 2 * D_C)
        cvt_c = cache_c_v[:, l].transpose(0, 2, 3, 1)
        (yc_l,) = _attn("C", z, zt, ctx=(ck_c, cvt_c), tables=tab_c, params=(lamp, gc),
                        lam_init=lam_init, latent=True)
        s0 = jnp.concatenate([state_b_C[:, l], state_b_n[:, l][..., None],
                              jnp.zeros((DEC_BATCH, 2, H_B, DK_B, 63), F32)], axis=-1)
        s0 = s0.reshape(DEC_BATCH, 2 * H_B, DK_B, 128)
        m0 = jnp.broadcast_to(state_b_m[:, l].reshape(DEC_BATCH, 2 * H_B, 1), (DEC_BATCH, 2 * H_B, 128))
        yb_l, _, _ = _mlstm(z, zt, s0, m0, gb, latent=True)

        ya = jnp.concatenate([ya_c, ya_l], axis=0)
        yb = jnp.concatenate([yb_c, yb_l], axis=0)
        yc = jnp.concatenate([yc_c, yc_l], axis=0)
        h = _proj_out(h, ya, yb, yc, mods, w_out[l].astype(BF16))
        h = _ffn(h, mods, gn[2], w_ff_in[l, 1].astype(BF16), w_ff_out[l, 1].astype(BF16), gf,
                 k=6, final=(l == DEPTH - 1))

        zc = z[:N_CTX]
        ztc = zt[:N_CTX // TC]
        outs["ak"].append(kn.reshape(BATCH, SEQ, KV_A, HD_A))
        outs["av"].append(_tokens_major(ztc[:, ZT_VA:ZT_VA + 128, :]).reshape(BATCH, SEQ, KV_A, HD_A))
        outs["ck"].append(zc[:, Z_KC:Z_KC + 256].reshape(BATCH, SEQ, H_C, 2, D_C))
        outs["cv"].append(_tokens_major(ztc[:, ZT_VC:ZT_VC + 256, :]).reshape(BATCH, SEQ, H_C, 2 * D_C))
        s_fin = s_fin.reshape(BATCH, 2, H_B, DK_B, 128)
        outs["bC"].append(s_fin[..., 0:64])
        outs["bn"].append(s_fin[..., 64])
        outs["bm"].append(m_fin[:, :, 0].reshape(BATCH, 2, H_B))

    y_prompt = h[:N_CTX].reshape(BATCH, SEQ, D_MODEL)
    y_sample = h[N_CTX:].reshape(DEC_BATCH, DEC_SEQ, D_MODEL)
    st = lambda k: jnp.stack(outs[k], axis=1)
    return (y_prompt, y_sample, st("ak"), st("av"), st("ck"), st("cv"), st("bC"), st("bn"), st("bm"))
```

```python
import functools
import math

import jax
import jax.numpy as jnp
from jax import lax
from jax.experimental import pallas as pl
from jax.experimental.pallas import tpu as pltpu

F32 = jnp.float32
BF16 = jnp.bfloat16

D_MODEL = 1024
BATCH = 16
SEQ = 256
DEPTH = 2
DEC_BATCH = 4
DEC_SEQ = 2048
PAST_LEN = 512
GRID_W = 64
ROPE_THETA = 10000.0
EPS = 1e-6
HD_A = 64
H_A = 8
KV_A = 2
DK_B = 64
H_B = 4
D_C = 32
H_C = 4
D_FF = 2816
LOG2E = 1.4426950408889634

N_CTX = BATCH * SEQ
N_LAT = DEC_BATCH * DEC_SEQ
N_TOK = N_CTX + N_LAT

TM = 512
CTX_TILES = N_CTX // TM
LAT_TILES_PER_BATCH = DEC_SEQ // TM
FF_CHUNKS = ((0, 1536), (1536, D_FF))
TC = 256
SB = 256
TQ = 256
VROWS = 80

Z_QB, Z_VB, Z_OB, Z_KC, Z_KA, Z_G = 0, 256, 512, 768, 1024, 1152
Z_W = 1280
ZT_QA, ZT_QC, ZT_VC, ZT_KB, ZT_VA, ZT_G = 0, 512, 768, 1024, 1280, 1408
ZT_W = 1424

VMEM_LIMIT = 56 * 1024 * 1024


def _cparams(sem):
    return pltpu.CompilerParams(dimension_semantics=sem, vmem_limit_bytes=VMEM_LIMIT)


def _const_spec(shape):
    nd = len(shape)
    return pl.BlockSpec(shape, lambda *_: (0,) * nd, pipeline_mode=pl.Buffered(1))


def _mod_row(i):
    return jnp.maximum(i - CTX_TILES, 0) // LAT_TILES_PER_BATCH + (i >= CTX_TILES).astype(jnp.int32)


def _rms(x, g, axis=-1):
    ms = jnp.mean(x * x, axis=axis, keepdims=True)
    return x * lax.rsqrt(ms + EPS) * g


def _norm_mod(x, g, mod_ref, k):
    shift = mod_ref[0, k:k + 1, :]
    scale = mod_ref[0, k + 1:k + 2, :]
    return _rms(x, g) * (1.0 + scale) + shift


def _ds(c, n):
    if isinstance(c, int):
        return pl.ds(c * n, n)
    return pl.ds(pl.multiple_of(c * n, n), n)


def _log_sigmoid(x):
    return jnp.minimum(x, 0.0) - jnp.log1p(jnp.exp(-jnp.abs(x)))


def _ada_kernel(c_ref, w_ref, b_ref, o_ref):
    c = c_ref[...]
    a = (c * jax.nn.sigmoid(c)).astype(BF16)
    o_ref[0] = jnp.dot(a, w_ref[0].astype(BF16), preferred_element_type=F32) + b_ref[0]


def _ada(cc, w_ada, b_ada):
    tn = 1024
    n = 9 * D_MODEL
    return pl.pallas_call(
        _ada_kernel,
        grid=(DEPTH, n // tn),
        in_specs=[
            pl.BlockSpec((8, D_MODEL), lambda l, j: (0, 0)),
            pl.BlockSpec((1, D_MODEL, tn), lambda l, j: (l, 0, j)),
            pl.BlockSpec((1, 1, tn), lambda l, j: (l, 0, j)),
        ],
        out_specs=pl.BlockSpec((1, 8, tn), lambda l, j: (l, 0, j)),
        out_shape=jax.ShapeDtypeStruct((DEPTH, 8, n), F32),
        compiler_params=_cparams(("arbitrary", "arbitrary")),
        name="ada",
    )(cc, w_ada, b_ada.reshape(DEPTH, 1, n))


def _ffn_kernel(*refs, k, first, final):
    is_ctx = pl.program_id(0) < CTX_TILES
    if first:
        hc_ref, hl_ref, mod_ref, g_ref, wi_ref, wo_ref, gf_ref = refs[:7]
        x = jnp.where(is_ctx, hc_ref[...], hl_ref[...])
    else:
        h_ref, mod_ref, g_ref, wi_ref, wo_ref, gf_ref = refs[:6]
        x = h_ref[...]
    xn = _norm_mod(x, g_ref[...], mod_ref, k).astype(BF16)
    acc = jnp.zeros((TM, D_MODEL), F32)
    for lo, hi in FF_CHUNKS:
        g = jnp.dot(xn, wi_ref[:, lo:hi], preferred_element_type=F32)
        u = jnp.dot(xn, wi_ref[:, D_FF + lo:D_FF + hi], preferred_element_type=F32)
        a = (g * jax.nn.sigmoid(g) * u).astype(BF16)
        acc = acc + jnp.dot(a, wo_ref[lo:hi, :], preferred_element_type=F32)
    gate = mod_ref[0, k + 2:k + 3, :]
    y = x + 0.5 * gate * acc
    if final:
        yc_ref, yl_ref = refs[-2:]
        y = _rms(y, gf_ref[...])

        @pl.when(is_ctx)
        def _():
            yc_ref[...] = y

        @pl.when(jnp.logical_not(is_ctx))
        def _():
            yl_ref[...] = y
    else:
        refs[-1][...] = y


def _ctx_lat_specs(width):
    return [
        pl.BlockSpec((TM, width), lambda i: (jnp.minimum(i, CTX_TILES - 1), 0)),
        pl.BlockSpec((TM, width), lambda i: (jnp.maximum(i - CTX_TILES, 0), 0)),
    ]


def _ffn(h, mods, g, wi, wo, gf, *, k, first=False, final=False):
    flat_spec = pl.BlockSpec((TM, D_MODEL), lambda i: (i, 0))
    flat_shape = jax.ShapeDtypeStruct((N_TOK, D_MODEL), F32)
    return pl.pallas_call(
        functools.partial(_ffn_kernel, k=k, first=first, final=final),
        grid=(N_TOK // TM,),
        in_specs=(_ctx_lat_specs(D_MODEL) if first else [flat_spec]) + [
            pl.BlockSpec((1, 9, D_MODEL), lambda i: (_mod_row(i), 0, 0)),
            _const_spec((1, D_MODEL)),
            _const_spec((D_MODEL, 2 * D_FF)),
            _const_spec((D_FF, D_MODEL)),
            _const_spec((1, D_MODEL)),
        ],
        out_specs=_ctx_lat_specs(D_MODEL) if final else flat_spec,
        out_shape=[jax.ShapeDtypeStruct((N_CTX, D_MODEL), F32),
                   jax.ShapeDtypeStruct((N_LAT, D_MODEL), F32)] if final else flat_shape,
        compiler_params=_cparams(("arbitrary",)),
        name="ffn_final" if final else ("ffn_first" if first else "ffn"),
    )(*(h if first else (h,)), mods, g, wi, wo, gf)


def _proj_in_kernel(h_ref, mod_ref, g_ref, w_ref, wt_ref, bz_ref, bt_ref, z_ref, zt_ref):
    xn = _norm_mod(h_ref[...], g_ref[...], mod_ref, 3).astype(BF16)
    z_ref[...] = jnp.dot(xn, w_ref[...], preferred_element_type=F32) + bz_ref[...]
    zt = lax.dot_general(wt_ref[...], xn, (((1,), (1,)), ((), ())), preferred_element_type=F32)
    zt = zt + bt_ref[...]
    for c in range(TM // TC):
        zt_ref[c] = zt[:, c * TC:(c + 1) * TC]


def _proj_in(h, mods, g, wz, wt, bz, bt):
    return pl.pallas_call(
        _proj_in_kernel,
        grid=(N_TOK // TM,),
        in_specs=[
            pl.BlockSpec((TM, D_MODEL), lambda i: (i, 0)),
            pl.BlockSpec((1, 9, D_MODEL), lambda i: (_mod_row(i), 0, 0)),
            _const_spec((1, D_MODEL)),
            _const_spec((D_MODEL, Z_W)),
            _const_spec((ZT_W, D_MODEL)),
            _const_spec((1, Z_W)),
            _const_spec((ZT_W, 1)),
        ],
        out_specs=[
            pl.BlockSpec((TM, Z_W), lambda i: (i, 0)),
            pl.BlockSpec((TM // TC, ZT_W, TC), lambda i: (i, 0, 0)),
        ],
        out_shape=[
            jax.ShapeDtypeStruct((N_TOK, Z_W), F32),
            jax.ShapeDtypeStruct((N_TOK // TC, ZT_W, TC), F32),
        ],
        compiler_params=_cparams(("arbitrary",)),
        name="proj_in",
    )(h, mods, g, wz, wt, bz, bt)


def _proj_out_kernel(h_ref, yac_ref, yal_ref, ybc_ref, ybl_ref, ycc_ref, ycl_ref, mod_ref, wo_ref, o_ref):
    is_ctx = pl.program_id(0) < CTX_TILES
    pick = lambda c_ref, l_ref: jnp.where(is_ctx, c_ref[...], l_ref[...])
    y = jnp.dot(pick(yac_ref, yal_ref), wo_ref[0:512, :], preferred_element_type=F32)
    y = y + jnp.dot(pick(ybc_ref, ybl_ref), wo_ref[512:768, :], preferred_element_type=F32)
    y = y + jnp.dot(pick(ycc_ref, ycl_ref), wo_ref[768:1024, :], preferred_element_type=F32)
    o_ref[...] = h_ref[...] + mod_ref[0, 5:6, :] * y


def _proj_out(h, ya, yb, yc, mods, wo):
    return pl.pallas_call(
        _proj_out_kernel,
        grid=(N_TOK // TM,),
        in_specs=[pl.BlockSpec((TM, D_MODEL), lambda i: (i, 0))]
        + _ctx_lat_specs(512) + _ctx_lat_specs(256) + _ctx_lat_specs(256) + [
            pl.BlockSpec((1, 9, D_MODEL), lambda i: (_mod_row(i), 0, 0)),
            _const_spec((D_MODEL, D_MODEL)),
        ],
        out_specs=pl.BlockSpec((TM, D_MODEL), lambda i: (i, 0)),
        out_shape=jax.ShapeDtypeStruct((N_TOK, D_MODEL), F32),
        compiler_params=_cparams(("arbitrary",)),
        name="proj_out",
    )(h, *ya, *yb, *yc, mods, wo)


def _swap_halves(x, width, axis):
    n = x.shape[axis]
    half = width // 2
    parts = []
    for s in range(0, n, width):
        parts.append(lax.slice_in_dim(x, s + half, s + width, axis=axis))
        parts.append(lax.slice_in_dim(x, s, s + half, axis=axis))
    return jnp.concatenate(parts, axis=axis)


def _attn_kernel(*refs, mode, seq, has_ctx, rope, emit_k, lam_init):
    it = iter(refs)
    qt_ref, k_ref, vt_ref = next(it), next(it), next(it)
    if has_ctx:
        ck_ref, cvt_ref = next(it), next(it)
    if rope:
        cqt_ref, sqt_ref, ck_tab, sk_tab = next(it), next(it), next(it), next(it)
    if mode == "A":
        gq_ref, gk_ref = next(it), next(it)
    else:
        lam_ref, gc_ref = next(it), next(it)
    y_ref = next(it)
    if emit_k:
        kn_ref = next(it)
    kbuf, vbuf, s_ref, p_ref = next(it), next(it), next(it), next(it)

    n_kv = KV_A if mode == "A" else H_C
    per_kv = 8 // n_kv
    n_ctx = PAST_LEN if has_ctx else 0
    lk = n_ctx + seq
    width = HD_A if mode == "A" else D_C

    @pl.when(pl.program_id(1) == 0)
    def _fill():
        ones_rows = (lax.broadcasted_iota(jnp.int32, (VROWS - 64, lk), 0) == 0).astype(BF16)
        for h in range(n_kv):
            hs = slice(h * 64, (h + 1) * 64)
            vbuf[h, 64:VROWS, :] = ones_rows
            if has_ctx:
                kbuf[h, 0:n_ctx, :] = ck_ref[0, :, hs].astype(BF16)
                vbuf[h, 0:64, 0:n_ctx] = cvt_ref[0, h].astype(BF16)
            for c in range(seq // TC):
                rows = slice(c * TC, (c + 1) * TC)
                k = k_ref[rows, hs]
                if mode == "A":
                    k = _rms(k, gk_ref[...])
                    if emit_k:
                        kn_ref[0, rows, hs] = k
                if rope:
                    k = k * ck_tab[rows, :] + _swap_halves(k, width, 1) * sk_tab[rows, :]
                kbuf[h, n_ctx + c * TC:n_ctx + (c + 1) * TC, :] = k.astype(BF16)
                vbuf[h, 0:64, n_ctx + c * TC:n_ctx + (c + 1) * TC] = vt_ref[c, hs, :].astype(BF16)

    qts = []
    if mode == "A":
        scale = HD_A ** -0.5 * LOG2E
        for hh in range(H_A):
            x = _rms(qt_ref[0, hh * 64:(hh + 1) * 64, :], gq_ref[...], axis=0)
            if rope:
                x = x * cqt_ref[0] + _swap_halves(x, width, 0) * sqt_ref[0]
            qts.append((x * scale).astype(BF16))
    else:
        scale = D_C ** -0.5 * LOG2E
        first_map = lax.broadcasted_iota(jnp.int32, (64, TQ), 0) < D_C
        for h in range(H_C):
            x = qt_ref[0, h * 64:(h + 1) * 64, :]
            if rope:
                x = x * cqt_ref[0] + _swap_halves(x, width, 0) * sqt_ref[0]
            x = x * scale
            qts.append(jnp.where(first_map, x, 0.0).astype(BF16))
            qts.append(jnp.where(first_map, 0.0, x).astype(BF16))

    m_prev = None
    for s in range(9):
        def stage_block(j, mrun, s=s, m_prev=m_prev):
            rows = _ds(j, SB)
            if s < 8:
                st = jnp.dot(kbuf[s // per_kv, rows, :], qts[s], preferred_element_type=F32)
                s_ref[s % 2, rows, :] = st
                for r in range(SB // 8):
                    mrun = jnp.maximum(mrun, st[8 * r:8 * r + 8])
            if s >= 1:
                prev = s_ref[(s - 1) % 2, rows, :]
                p_ref[rows, (s - 1) * TQ:s * TQ] = jnp.exp2(prev - m_prev).astype(BF16)
            return mrun

        mrun = lax.fori_loop(0, lk // SB, stage_block, jnp.full((8, TQ), -jnp.inf, F32),
                             unroll=True)
        m_prev = jnp.max(mrun, axis=0, keepdims=True)

    outs = []
    for u in range(8):
        ox = jnp.dot(vbuf[u // per_kv], p_ref[:, u * TQ:(u + 1) * TQ], preferred_element_type=F32)
        outs.append(ox[0:64] / ox[64:65])

    if mode == "A":
        for pair in range(4):
            o = jnp.concatenate(outs[2 * pair:2 * pair + 2], axis=0)
            y_ref[:, pair * 128:(pair + 1) * 128] = o.T.astype(BF16)
    else:
        lp = lam_ref[...]
        lam = (jnp.exp(jnp.sum(lp[0:1] * lp[1:2], axis=1, keepdims=True))
               - jnp.exp(jnp.sum(lp[2:3] * lp[3:4], axis=1, keepdims=True)) + lam_init)
        for pair in range(2):
            o = []
            for h in (2 * pair, 2 * pair + 1):
                d = outs[2 * h] - lam * outs[2 * h + 1]
                o.append(_rms(d, gc_ref[...], axis=0) * (1.0 - lam_init))
            y_ref[:, pair * 128:(pair + 1) * 128] = jnp.concatenate(o, axis=0).T.astype(BF16)


def _attn(mode, z, zt, *, ctx, tables, params, lam_init, latent):
    if latent:
        nb, seq, row0 = DEC_BATCH, DEC_SEQ, N_CTX
    else:
        nb, seq, row0 = BATCH, SEQ, 0
    nq = seq // TQ
    n_kv = KV_A if mode == "A" else H_C
    n_ctx = PAST_LEN if ctx is not None else 0
    lk = n_ctx + seq
    qw = 512 if mode == "A" else 256
    vw = n_kv * 64
    q_blk = (ZT_QA if mode == "A" else ZT_QC) // qw
    k_blk = (Z_KA if mode == "A" else Z_KC) // vw
    v_blk = (ZT_VA if mode == "A" else ZT_VC) // vw
    emit_k = mode == "A" and not latent
    blk0 = row0 // seq

    in_specs = [
        pl.BlockSpec((1, qw, TC), lambda b, i: (row0 // TC + b * nq + i, q_blk, 0)),
        pl.BlockSpec((seq, vw), lambda b, i: (blk0 + b, k_blk)),
        pl.BlockSpec((seq // TC, vw, TC), lambda b, i: (blk0 + b, v_blk, 0)),
    ]
    args = [zt, z, zt]
    if ctx is not None:
        in_specs += [
            pl.BlockSpec((1, PAST_LEN, vw), lambda b, i: (b, 0, 0)),
            pl.BlockSpec((1, n_kv, 64, PAST_LEN), lambda b, i: (b, 0, 0, 0)),
        ]
        args += list(ctx)
    if tables is not None:
        cq, sq, cqt, sqt = tables
        in_specs += [
            pl.BlockSpec((1, 64, TC), lambda b, i: (i, 0, 0)),
            pl.BlockSpec((1, 64, TC), lambda b, i: (i, 0, 0)),
            _const_spec((seq, 64)),
            _const_spec((seq, 64)),
        ]
        args += [cqt, sqt, cq, sq]
    for p in params:
        in_specs.append(_const_spec(p.shape))
        args.append(p)

    out_specs = [pl.BlockSpec((TQ, qw), lambda b, i: (b * nq + i, 0))]
    out_shape = [jax.ShapeDtypeStruct((nb * seq, qw), BF16)]
    if emit_k:
        out_specs.append(pl.BlockSpec((1, seq, vw), lambda b, i: (b, 0, 0)))
        out_shape.append(jax.ShapeDtypeStruct((nb, seq, vw), F32))

    return pl.pallas_call(
        functools.partial(_attn_kernel, mode=mode, seq=seq, has_ctx=ctx is not None,
                          rope=tables is not None, emit_k=emit_k, lam_init=lam_init),
        grid=(nb, nq),
        in_specs=in_specs,
        out_specs=out_specs,
        out_shape=out_shape,
        scratch_shapes=[
            pltpu.VMEM((n_kv, lk, 64), BF16),
            pltpu.VMEM((n_kv, VROWS, lk), BF16),
            pltpu.VMEM((2, lk, TQ), F32),
            pltpu.VMEM((lk, 8 * TQ), BF16),
        ],
        compiler_params=_cparams(("arbitrary", "arbitrary")),
        name=f"attn_{mode}_{'lat' if latent else 'ctx'}",
    )(*args)


def _mlstm_kernel(q_ref, v_ref, o_ref, gc_ref, kt_ref, gt_ref, s0_ref, m0_ref, gb_ref,
                  y_ref, s_ref, m_ref, hf_ref, hr_ref, *, seq):
    nc = seq // TC
    row = lax.broadcasted_iota(jnp.int32, (TC, TC), 0)
    col = lax.broadcasted_iota(jnp.int32, (TC, TC), 1)
    lower = col <= row
    upper = row <= col
    ones_col = (lax.broadcasted_iota(jnp.int32, (TC, 64), 1) == 0).astype(F32)
    units = [(d, h) for d in range(2) for h in range(H_B)]

    def chunk(j, carry):
        first = []
        for (d, h), (s_ext, m) in zip(units, carry):
            c = j if d == 0 else nc - 1 - j
            rows = _ds(c, TC)
            hs = slice(h * 64, (h + 1) * 64)
            qh = q_ref[rows, hs].astype(BF16)
            kt = kt_ref[c, hs, :] * (DK_B ** -0.5)
            qk = jnp.dot(qh, kt.astype(BF16), preferred_element_type=F32)
            inter = jnp.dot(qh, s_ext.astype(BF16), preferred_element_type=F32)
            first.append((c, rows, hs, kt, qk, inter))
        second = []
        for (d, h), (s_ext, m), (c, rows, hs, kt, qk, inter) in zip(units, carry, first):
            seen = lower if d == 0 else upper
            seen_t = upper if d == 0 else lower
            ri, rf = (2 * d) * H_B + h, (2 * d + 1) * H_B + h
            i_row = gt_ref[c, ri:ri + 1, :]
            f_row = _log_sigmoid(gt_ref[c, rf:rf + 1, :])
            f_col = _log_sigmoid(gc_ref[rows, rf:rf + 1])
            b_row = jnp.sum(jnp.where(seen_t, f_col, 0.0), axis=0, keepdims=True)
            b_col = jnp.sum(jnp.where(seen, f_row, 0.0), axis=1, keepdims=True)
            a_row = i_row - b_row
            lm = jnp.max(jnp.where(seen, a_row, -jnp.inf), axis=1, keepdims=True)
            mx = jnp.maximum(m, lm)
            sp = qk * jnp.exp(jnp.where(seen, a_row - mx, -jnp.inf))
            w = jnp.exp(m - mx)
            den = jnp.sum(sp, axis=1, keepdims=True) + w * inter[:, 64:65]
            floor = jnp.exp(-(b_col + mx))
            b_last = jnp.sum(f_row, axis=1, keepdims=True)
            mxl = jnp.maximum(m, jnp.max(a_row, axis=1, keepdims=True))
            ktw = (kt * jnp.exp(a_row - mxl)).astype(BF16)
            second.append((sp.astype(BF16), w * inter[:, 0:64], den, floor, ktw,
                           jnp.exp(m - mxl), b_last + mxl))
        out = []
        for (d, h), (s_ext, m), f, s in zip(units, carry, first, second):
            rows, hs = f[1], f[2]
            spb, inter_w, den, floor, ktw, decay, m_new = s
            vx = jnp.concatenate([v_ref[rows, hs], ones_col], axis=1).astype(BF16)
            num = jnp.dot(spb, vx, preferred_element_type=F32)[:, 0:64] + inter_w
            hv = num / jnp.maximum(jnp.abs(den), floor)
            if d == 0:
                hf_ref[rows, hs] = hv
            else:
                hr_ref[rows, hs] = hv
            upd = jnp.dot(ktw, vx, preferred_element_type=F32)
            out.append((decay * s_ext + upd, m_new))
        return tuple(out)

    init = tuple((s0_ref[0, u], m0_ref[0, u:u + 1, 0:1]) for u in range(2 * H_B))
    if nc == 1:
        fin = chunk(0, init)
    else:
        fin = lax.fori_loop(0, nc, chunk, init)
    for u, (s_ext, m) in enumerate(fin):
        s_ref[0, u] = s_ext
        m_ref[0, u:u + 1, :] = jnp.broadcast_to(m, (1, 128))

    def epilogue(c, _):
        rows = _ds(c, TC)
        for h in range(H_B):
            hs = slice(h * 64, (h + 1) * 64)
            x = hf_ref[rows, hs] + hr_ref[rows, hs]
            y = _rms(x, gb_ref[...]) * jax.nn.sigmoid(o_ref[rows, hs])
            y_ref[rows, hs] = y.astype(BF16)
        return 0

    if nc == 1:
        epilogue(0, 0)
    else:
        lax.fori_loop(0, nc, epilogue, 0)


def _mlstm(z, zt, s0, m0, gb, *, latent):
    if latent:
        nb, seq, row0 = DEC_BATCH, DEC_SEQ, N_CTX
    else:
        nb, seq, row0 = BATCH, SEQ, 0
    blk0 = row0 // seq
    nc = seq // TC
    zspec = lambda cb: pl.BlockSpec((seq, 256), lambda b: (blk0 + b, cb))
    return pl.pallas_call(
        functools.partial(_mlstm_kernel, seq=seq),
        grid=(nb,),
        in_specs=[
            zspec(Z_QB // 256), zspec(Z_VB // 256), zspec(Z_OB // 256),
            pl.BlockSpec((seq, 128), lambda b: (blk0 + b, Z_G // 128)),
            pl.BlockSpec((nc, 256, TC), lambda b: (blk0 + b, ZT_KB // 256, 0)),
            pl.BlockSpec((nc, 16, TC), lambda b: (blk0 + b, ZT_G // 16, 0)),
            pl.BlockSpec((1, 2 * H_B, 64, 128), lambda b: (b, 0, 0, 0)),
            pl.BlockSpec((1, 2 * H_B, 128), lambda b: (b, 0, 0)),
            _const_spec((1, 64)),
        ],
        out_specs=[
            pl.BlockSpec((seq, 256), lambda b: (b, 0)),
            pl.BlockSpec((1, 2 * H_B, 64, 128), lambda b: (b, 0, 0, 0)),
            pl.BlockSpec((1, 2 * H_B, 128), lambda b: (b, 0, 0)),
        ],
        out_shape=[
            jax.ShapeDtypeStruct((nb * seq, 256), BF16),
            jax.ShapeDtypeStruct((nb, 2 * H_B, 64, 128), F32),
            jax.ShapeDtypeStruct((nb, 2 * H_B, 128), F32),
        ],
        scratch_shapes=[pltpu.VMEM((seq, 256), F32), pltpu.VMEM((seq, 256), F32)],
        compiler_params=_cparams(("arbitrary",)),
        name=f"mlstm_{'lat' if latent else 'ctx'}",
    )(z, z, z, z, zt, zt, s0, m0, gb)


def _rope_tables(dim):
    t = jnp.arange(DEC_SEQ)
    row = (t // GRID_W).astype(F32)
    colp = (t % GRID_W).astype(F32)
    axis_dim = dim // 2
    freqs = ROPE_THETA ** (-jnp.arange(0, axis_dim, 2, dtype=F32) / axis_dim)
    ang = jnp.concatenate([row[:, None] * freqs, colp[:, None] * freqs], axis=-1)
    cos, sin = jnp.cos(ang), jnp.sin(ang)
    reps = 64 // dim
    cq = jnp.concatenate([cos, cos] * reps, axis=1)
    sq = jnp.concatenate([-sin, sin] * reps, axis=1)
    to_chunks = lambda x: x.T.reshape(64, DEC_SEQ // TC, TC).transpose(1, 0, 2)
    return cq, sq, to_chunks(cq), to_chunks(sq)


def _split_w_in(w, b_gates):
    a0, b0, c0 = 0, 768, 1808
    qa, ka, va = w[:, a0:a0 + 512], w[:, a0 + 512:a0 + 640], w[:, a0 + 640:a0 + 768]
    qb, kb = w[:, b0:b0 + 256], w[:, b0 + 256:b0 + 512]
    vb, ob = w[:, b0 + 512:b0 + 768], w[:, b0 + 768:b0 + 1024]
    gb = w[:, b0 + 1024:b0 + 1040]
    qc, kc, vc = w[:, c0:c0 + 256], w[:, c0 + 256:c0 + 512], w[:, c0 + 512:c0 + 768]
    wz = jnp.concatenate([qb, vb, ob, kc, ka, gb, jnp.zeros((D_MODEL, 112), w.dtype)], axis=1)
    wt = jnp.concatenate([qa, qc, vc, kb, va, gb], axis=1).T
    bz = jnp.zeros((1, Z_W), F32).at[0, Z_G:Z_G + 16].set(b_gates)
    bt = jnp.zeros((ZT_W, 1), F32).at[ZT_G:ZT_G + 16, 0].set(b_gates)
    return wz.astype(BF16), wt.astype(BF16), bz, bt


def _tokens_major(x):
    return x.transpose(0, 2, 1)


def kernel(x_prompt, x_sample, cache_a_k, cache_a_v, cache_c_k, cache_c_v, state_b_C, state_b_n,
           state_b_m, c, c_ctx, w_ada, b_ada, g_norm, w_ff_in, w_ff_out, w_in, w_out, g_qa, g_ka,
           b_gates, g_b, lam_q1, lam_k1, lam_q2, lam_k2, g_c, g_final):
    h = (x_prompt.reshape(N_CTX, D_MODEL), x_sample.reshape(N_LAT, D_MODEL))
    cc = jnp.concatenate([c_ctx[None, :], c, jnp.zeros((3, D_MODEL), F32)], axis=0)
    mods_all = _ada(cc, w_ada, b_ada).reshape(DEPTH, 8, 9, D_MODEL)
    tab_a = _rope_tables(HD_A)
    tab_c = _rope_tables(D_C)
    gf = g_final.reshape(1, D_MODEL)

    outs = {k: [] for k in ("ak", "av", "ck", "cv", "bC", "bn", "bm")}
    for l in range(DEPTH):
        mods = mods_all[l]
        lam_init = 0.8 - 0.6 * math.exp(-0.3 * l)
        gn = g_norm[l].reshape(3, 1, D_MODEL)
        h = _ffn(h, mods, gn[0], w_ff_in[l, 0].astype(BF16), w_ff_out[l, 0].astype(BF16), gf,
                 k=0, first=(l == 0))
        wz, wt, bz, bt = _split_w_in(w_in[l], b_gates[l])
        z, zt = _proj_in(h, mods, gn[1], wz, wt, bz, bt)

        gq, gk = g_qa[l].reshape(64, 1), g_ka[l].reshape(1, 64)
        lamp = jnp.stack([lam_q1[l], lam_k1[l], lam_q2[l], lam_k2[l]], axis=0)
        gc = g_c[l].reshape(64, 1)
        gb = g_b[l].reshape(1, 64)

        ya_c, kn = _attn("A", z, zt, ctx=None, tables=None, params=(gq, gk), lam_init=lam_init,
                         latent=False)
        (yc_c,) = _attn("C", z, zt, ctx=None, tables=None, params=(lamp, gc), lam_init=lam_init,
                        latent=False)
        zeros_s = jnp.zeros((BATCH, 2 * H_B, 64, 128), F32)
        zeros_m = jnp.zeros((BATCH, 2 * H_B, 128), F32)
        yb_c, s_fin, m_fin = _mlstm(z, zt, zeros_s, zeros_m, gb, latent=False)

        ck_a = cache_a_k[:, l].reshape(DEC_BATCH, PAST_LEN, KV_A * HD_A)
        cvt_a = cache_a_v[:, l].transpose(0, 2, 3, 1)
        (ya_l,) = _attn("A", z, zt, ctx=(ck_a, cvt_a), tables=tab_a, params=(gq, gk),
                        lam_init=lam_init, latent=True)
        ck_c = cache_c_k[:, l].reshape(DEC_BATCH, PAST_LEN, H_C * 2 * D_C)
        cvt_c = cache_c_v[:, l].transpose(0, 2, 3, 1)
        (yc_l,) = _attn("C", z, zt, ctx=(ck_c, cvt_c), tables=tab_c, params=(lamp, gc),
                        lam_init=lam_init, latent=True)
        s0 = jnp.concatenate([state_b_C[:, l], state_b_n[:, l][..., None],
                              jnp.zeros((DEC_BATCH, 2, H_B, DK_B, 63), F32)], axis=-1)
        s0 = s0.reshape(DEC_BATCH, 2 * H_B, DK_B, 128)
        m0 = jnp.broadcast_to(state_b_m[:, l].reshape(DEC_BATCH, 2 * H_B, 1), (DEC_BATCH, 2 * H_B, 128))
        yb_l, _, _ = _mlstm(z, zt, s0, m0, gb, latent=True)

        h = _proj_out(h, (ya_c, ya_l), (yb_c, yb_l), (yc_c, yc_l), mods, w_out[l].astype(BF16))
        h = _ffn(h, mods, gn[2], w_ff_in[l, 1].astype(BF16), w_ff_out[l, 1].astype(BF16), gf,
                 k=6, final=(l == DEPTH - 1))

        zc = z[:N_CTX]
        ztc = zt[:N_CTX // TC]
        outs["ak"].append(kn.reshape(BATCH, SEQ, KV_A, HD_A))
        outs["av"].append(_tokens_major(ztc[:, ZT_VA:ZT_VA + 128, :]).reshape(BATCH, SEQ, KV_A, HD_A))
        outs["ck"].append(zc[:, Z_KC:Z_KC + 256].reshape(BATCH, SEQ, H_C, 2, D_C))
        outs["cv"].append(_tokens_major(ztc[:, ZT_VC:ZT_VC + 256, :]).reshape(BATCH, SEQ, H_C, 2 * D_C))
        s_fin = s_fin.reshape(BATCH, 2, H_B, DK_B, 128)
        outs["bC"].append(s_fin[..., 0:64])
        outs["bn"].append(s_fin[..., 64])
        outs["bm"].append(m_fin[:, :, 0].reshape(BATCH, 2, H_B))

    y_prompt = h[0].reshape(BATCH, SEQ, D_MODEL)
    y_sample = h[1].reshape(DEC_BATCH, DEC_SEQ, D_MODEL)
    st = lambda k: jnp.stack(outs[k], axis=1)
    return (y_prompt, y_sample, st("ak"), st("av"), st("ck"), st("cv"), st("bC"), st("bn"), st("bm"))
```

```python
import functools
import math

import jax
import jax.numpy as jnp
from jax import lax
from jax.experimental import pallas as pl
from jax.experimental.pallas import tpu as pltpu

F32 = jnp.float32
BF16 = jnp.bfloat16

D_MODEL = 1024
BATCH = 16
SEQ = 256
DEPTH = 2
DEC_BATCH = 4
DEC_SEQ = 2048
PAST_LEN = 512
GRID_W = 64
ROPE_THETA = 10000.0
EPS = 1e-6
HD_A = 64
H_A = 8
KV_A = 2
DK_B = 64
H_B = 4
D_C = 32
H_C = 4
D_FF = 2816
LOG2E = 1.4426950408889634

N_CTX = BATCH * SEQ
N_LAT = DEC_BATCH * DEC_SEQ
N_TOK = N_CTX + N_LAT

TM = 512
CTX_TILES = N_CTX // TM
LAT_TILES_PER_BATCH = DEC_SEQ // TM
FF_CHUNKS = ((0, 1536), (1536, D_FF))
TC = 256
SB = 256
TQ = 256
VROWS = 80

Z_KB, Z_KC, Z_KA = 0, 256, 512
Z_W = 640
ZT_QA, ZT_QC, ZT_VC, ZT_QB, ZT_VB, ZT_OB, ZT_VA, ZT_G = 0, 512, 768, 1024, 1280, 1536, 1792, 1920
ZT_W = 1936

VMEM_LIMIT = 56 * 1024 * 1024


def _cparams(sem):
    return pltpu.CompilerParams(dimension_semantics=sem, vmem_limit_bytes=VMEM_LIMIT)


def _const_spec(shape):
    nd = len(shape)
    return pl.BlockSpec(shape, lambda *_: (0,) * nd, pipeline_mode=pl.Buffered(1))


def _mod_row(i):
    return jnp.maximum(i - CTX_TILES, 0) // LAT_TILES_PER_BATCH + (i >= CTX_TILES).astype(jnp.int32)


def _rms(x, g, axis=-1):
    ms = jnp.mean(x * x, axis=axis, keepdims=True)
    return x * lax.rsqrt(ms + EPS) * g


def _norm_mod(x, g, mod_ref, k):
    shift = mod_ref[0, k:k + 1, :]
    scale = mod_ref[0, k + 1:k + 2, :]
    return _rms(x, g) * (1.0 + scale) + shift


def _ds(c, n):
    if isinstance(c, int):
        return pl.ds(c * n, n)
    return pl.ds(pl.multiple_of(c * n, n), n)


def _log_sigmoid(x):
    return jnp.minimum(x, 0.0) - jnp.log1p(jnp.exp(-jnp.abs(x)))


def _ada_kernel(c_ref, w_ref, b_ref, o_ref):
    c = c_ref[...]
    a = (c * jax.nn.sigmoid(c)).astype(BF16)
    o_ref[0] = jnp.dot(a, w_ref[0].astype(BF16), preferred_element_type=F32) + b_ref[0]


def _ada(cc, w_ada, b_ada):
    tn = 1024
    n = 9 * D_MODEL
    return pl.pallas_call(
        _ada_kernel,
        grid=(DEPTH, n // tn),
        in_specs=[
            pl.BlockSpec((8, D_MODEL), lambda l, j: (0, 0)),
            pl.BlockSpec((1, D_MODEL, tn), lambda l, j: (l, 0, j)),
            pl.BlockSpec((1, 1, tn), lambda l, j: (l, 0, j)),
        ],
        out_specs=pl.BlockSpec((1, 8, tn), lambda l, j: (l, 0, j)),
        out_shape=jax.ShapeDtypeStruct((DEPTH, 8, n), F32),
        compiler_params=_cparams(("arbitrary", "arbitrary")),
        name="ada",
    )(cc, w_ada, b_ada.reshape(DEPTH, 1, n))


def _ffn_kernel(*refs, k, first, final):
    is_ctx = pl.program_id(0) < CTX_TILES
    if first:
        hc_ref, hl_ref, mod_ref, g_ref, wi_ref, wo_ref, gf_ref = refs[:7]
        x = jnp.where(is_ctx, hc_ref[...], hl_ref[...])
    else:
        h_ref, mod_ref, g_ref, wi_ref, wo_ref, gf_ref = refs[:6]
        x = h_ref[...]
    xn = _norm_mod(x, g_ref[...], mod_ref, k).astype(BF16)
    acc = jnp.zeros((TM, D_MODEL), F32)
    for lo, hi in FF_CHUNKS:
        g = jnp.dot(xn, wi_ref[:, lo:hi], preferred_element_type=F32)
        u = jnp.dot(xn, wi_ref[:, D_FF + lo:D_FF + hi], preferred_element_type=F32)
        a = (g * jax.nn.sigmoid(g) * u).astype(BF16)
        acc = acc + jnp.dot(a, wo_ref[lo:hi, :], preferred_element_type=F32)
    gate = mod_ref[0, k + 2:k + 3, :]
    y = x + 0.5 * gate * acc
    if final:
        yc_ref, yl_ref = refs[-2:]
        y = _rms(y, gf_ref[...])

        @pl.when(is_ctx)
        def _():
            yc_ref[...] = y

        @pl.when(jnp.logical_not(is_ctx))
        def _():
            yl_ref[...] = y
    else:
        refs[-1][...] = y


def _ctx_lat_specs(width):
    return [
        pl.BlockSpec((TM, width), lambda i: (jnp.minimum(i, CTX_TILES - 1), 0)),
        pl.BlockSpec((TM, width), lambda i: (jnp.maximum(i - CTX_TILES, 0), 0)),
    ]


def _ffn(h, mods, g, wi, wo, gf, *, k, first=False, final=False):
    flat_spec = pl.BlockSpec((TM, D_MODEL), lambda i: (i, 0))
    flat_shape = jax.ShapeDtypeStruct((N_TOK, D_MODEL), F32)
    return pl.pallas_call(
        functools.partial(_ffn_kernel, k=k, first=first, final=final),
        grid=(N_TOK // TM,),
        in_specs=(_ctx_lat_specs(D_MODEL) if first else [flat_spec]) + [
            pl.BlockSpec((1, 9, D_MODEL), lambda i: (_mod_row(i), 0, 0)),
            _const_spec((1, D_MODEL)),
            _const_spec((D_MODEL, 2 * D_FF)),
            _const_spec((D_FF, D_MODEL)),
            _const_spec((1, D_MODEL)),
        ],
        out_specs=_ctx_lat_specs(D_MODEL) if final else flat_spec,
        out_shape=[jax.ShapeDtypeStruct((N_CTX, D_MODEL), F32),
                   jax.ShapeDtypeStruct((N_LAT, D_MODEL), F32)] if final else flat_shape,
        compiler_params=_cparams(("arbitrary",)),
        name="ffn_final" if final else ("ffn_first" if first else "ffn"),
    )(*(h if first else (h,)), mods, g, wi, wo, gf)


def _proj_in_kernel(h_ref, mod_ref, g_ref, w_ref, wt_ref, bt_ref, z_ref, zt_ref):
    xn = _norm_mod(h_ref[...], g_ref[...], mod_ref, 3).astype(BF16)
    z_ref[...] = jnp.dot(xn, w_ref[...], preferred_element_type=F32)
    zt = lax.dot_general(wt_ref[...], xn, (((1,), (1,)), ((), ())), preferred_element_type=F32)
    zt = zt + bt_ref[...]
    for c in range(TM // TC):
        zt_ref[c] = zt[:, c * TC:(c + 1) * TC]


def _proj_in(h, mods, g, wz, wt, bt):
    return pl.pallas_call(
        _proj_in_kernel,
        grid=(N_TOK // TM,),
        in_specs=[
            pl.BlockSpec((TM, D_MODEL), lambda i: (i, 0)),
            pl.BlockSpec((1, 9, D_MODEL), lambda i: (_mod_row(i), 0, 0)),
            _const_spec((1, D_MODEL)),
            _const_spec((D_MODEL, Z_W)),
            _const_spec((ZT_W, D_MODEL)),
            _const_spec((ZT_W, 1)),
        ],
        out_specs=[
            pl.BlockSpec((TM, Z_W), lambda i: (i, 0)),
            pl.BlockSpec((TM // TC, ZT_W, TC), lambda i: (i, 0, 0)),
        ],
        out_shape=[
            jax.ShapeDtypeStruct((N_TOK, Z_W), F32),
            jax.ShapeDtypeStruct((N_TOK // TC, ZT_W, TC), F32),
        ],
        compiler_params=_cparams(("arbitrary",)),
        name="proj_in",
    )(h, mods, g, wz, wt, bt)


def _proj_out_kernel(h_ref, yac_ref, yal_ref, ybc_ref, ybl_ref, ycc_ref, ycl_ref, mod_ref, wo_ref, o_ref):
    is_ctx = pl.program_id(0) < CTX_TILES
    pick = lambda c_ref, l_ref: jnp.where(is_ctx, c_ref[...], l_ref[...])
    y = jnp.dot(pick(yac_ref, yal_ref), wo_ref[0:512, :], preferred_element_type=F32)
    y = y + jnp.dot(pick(ybc_ref, ybl_ref), wo_ref[512:768, :], preferred_element_type=F32)
    y = y + jnp.dot(pick(ycc_ref, ycl_ref), wo_ref[768:1024, :], preferred_element_type=F32)
    o_ref[...] = h_ref[...] + mod_ref[0, 5:6, :] * y


def _proj_out(h, ya, yb, yc, mods, wo):
    return pl.pallas_call(
        _proj_out_kernel,
        grid=(N_TOK // TM,),
        in_specs=[pl.BlockSpec((TM, D_MODEL), lambda i: (i, 0))]
        + _ctx_lat_specs(512) + _ctx_lat_specs(256) + _ctx_lat_specs(256) + [
            pl.BlockSpec((1, 9, D_MODEL), lambda i: (_mod_row(i), 0, 0)),
            _const_spec((D_MODEL, D_MODEL)),
        ],
        out_specs=pl.BlockSpec((TM, D_MODEL), lambda i: (i, 0)),
        out_shape=jax.ShapeDtypeStruct((N_TOK, D_MODEL), F32),
        compiler_params=_cparams(("arbitrary",)),
        name="proj_out",
    )(h, *ya, *yb, *yc, mods, wo)


def _swap_halves(x, width, axis):
    n = x.shape[axis]
    half = width // 2
    parts = []
    for s in range(0, n, width):
        parts.append(lax.slice_in_dim(x, s + half, s + width, axis=axis))
        parts.append(lax.slice_in_dim(x, s, s + half, axis=axis))
    return jnp.concatenate(parts, axis=axis)


def _attn_kernel(*refs, mode, seq, has_ctx, rope, emit_k, lam_init):
    it = iter(refs)
    qt_ref, k_ref, vt_ref = next(it), next(it), next(it)
    if has_ctx:
        ck_ref, cvt_ref = next(it), next(it)
    if rope:
        cqt_ref, sqt_ref, ck_tab, sk_tab = next(it), next(it), next(it), next(it)
    if mode == "A":
        gq_ref, gk_ref = next(it), next(it)
    else:
        lam_ref, gc_ref = next(it), next(it)
    y_ref = next(it)
    if emit_k:
        kn_ref = next(it)
    kbuf, vbuf, s_ref, p_ref = next(it), next(it), next(it), next(it)

    n_kv = KV_A if mode == "A" else H_C
    per_kv = 8 // n_kv
    n_ctx = PAST_LEN if has_ctx else 0
    lk = n_ctx + seq
    width = HD_A if mode == "A" else D_C

    @pl.when(pl.program_id(1) == 0)
    def _fill():
        ones_rows = (lax.broadcasted_iota(jnp.int32, (VROWS - 64, lk), 0) == 0).astype(BF16)
        for h in range(n_kv):
            hs = slice(h * 64, (h + 1) * 64)
            vbuf[h, 64:VROWS, :] = ones_rows
            if has_ctx:
                kbuf[h, 0:n_ctx, :] = ck_ref[0, :, hs].astype(BF16)
                vbuf[h, 0:64, 0:n_ctx] = cvt_ref[0, h].astype(BF16)
            for c in range(seq // TC):
                rows = slice(c * TC, (c + 1) * TC)
                k = k_ref[rows, hs]
                if mode == "A":
                    k = _rms(k, gk_ref[...])
                    if emit_k:
                        kn_ref[0, rows, hs] = k
                if rope:
                    k = k * ck_tab[rows, :] + _swap_halves(k, width, 1) * sk_tab[rows, :]
                kbuf[h, n_ctx + c * TC:n_ctx + (c + 1) * TC, :] = k.astype(BF16)
                vbuf[h, 0:64, n_ctx + c * TC:n_ctx + (c + 1) * TC] = vt_ref[c, hs, :].astype(BF16)

    qts = []
    if mode == "A":
        scale = HD_A ** -0.5 * LOG2E
        for hh in range(H_A):
            x = _rms(qt_ref[0, hh * 64:(hh + 1) * 64, :], gq_ref[...], axis=0)
            if rope:
                x = x * cqt_ref[0] + _swap_halves(x, width, 0) * sqt_ref[0]
            qts.append((x * scale).astype(BF16))
    else:
        scale = D_C ** -0.5 * LOG2E
        first_map = lax.broadcasted_iota(jnp.int32, (64, TQ), 0) < D_C
        for h in range(H_C):
            x = qt_ref[0, h * 64:(h + 1) * 64, :]
            if rope:
                x = x * cqt_ref[0] + _swap_halves(x, width, 0) * sqt_ref[0]
            x = x * scale
            qts.append(jnp.where(first_map, x, 0.0).astype(BF16))
            qts.append(jnp.where(first_map, 0.0, x).astype(BF16))

    m_prev = None
    for s in range(9):
        def stage_block(j, mrun, s=s, m_prev=m_prev):
            rows = _ds(j, SB)
            if s < 8:
                st = jnp.dot(kbuf[s // per_kv, rows, :], qts[s], preferred_element_type=F32)
                s_ref[s % 2, rows, :] = st
                for r in range(SB // 8):
                    mrun = jnp.maximum(mrun, st[8 * r:8 * r + 8])
            if s >= 1:
                prev = s_ref[(s - 1) % 2, rows, :]
                p_ref[rows, (s - 1) * TQ:s * TQ] = jnp.exp2(prev - m_prev).astype(BF16)
            return mrun

        mrun = lax.fori_loop(0, lk // SB, stage_block, jnp.full((8, TQ), -jnp.inf, F32),
                             unroll=True)
        m_prev = jnp.max(mrun, axis=0, keepdims=True)

    outs = []
    for u in range(8):
        ox = jnp.dot(vbuf[u // per_kv], p_ref[:, u * TQ:(u + 1) * TQ], preferred_element_type=F32)
        outs.append(ox[0:64] / ox[64:65])

    if mode == "A":
        for pair in range(4):
            o = jnp.concatenate(outs[2 * pair:2 * pair + 2], axis=0)
            y_ref[:, pair * 128:(pair + 1) * 128] = o.T.astype(BF16)
    else:
        lp = lam_ref[...]
        lam = (jnp.exp(jnp.sum(lp[0:1] * lp[1:2], axis=1, keepdims=True))
               - jnp.exp(jnp.sum(lp[2:3] * lp[3:4], axis=1, keepdims=True)) + lam_init)
        for pair in range(2):
            o = []
            for h in (2 * pair, 2 * pair + 1):
                d = outs[2 * h] - lam * outs[2 * h + 1]
                o.append(_rms(d, gc_ref[...], axis=0) * (1.0 - lam_init))
            y_ref[:, pair * 128:(pair + 1) * 128] = jnp.concatenate(o, axis=0).T.astype(BF16)


def _attn(mode, z, zt, *, ctx, tables, params, lam_init, latent):
    if latent:
        nb, seq, row0 = DEC_BATCH, DEC_SEQ, N_CTX
    else:
        nb, seq, row0 = BATCH, SEQ, 0
    nq = seq // TQ
    n_kv = KV_A if mode == "A" else H_C
    n_ctx = PAST_LEN if ctx is not None else 0
    lk = n_ctx + seq
    qw = 512 if mode == "A" else 256
    vw = n_kv * 64
    q_blk = (ZT_QA if mode == "A" else ZT_QC) // qw
    k_blk = (Z_KA if mode == "A" else Z_KC) // vw
    v_blk = (ZT_VA if mode == "A" else ZT_VC) // vw
    emit_k = mode == "A" and not latent
    blk0 = row0 // seq

    in_specs = [
        pl.BlockSpec((1, qw, TC), lambda b, i: (row0 // TC + b * nq + i, q_blk, 0)),
        pl.BlockSpec((seq, vw), lambda b, i: (blk0 + b, k_blk)),
        pl.BlockSpec((seq // TC, vw, TC), lambda b, i: (blk0 + b, v_blk, 0)),
    ]
    args = [zt, z, zt]
    if ctx is not None:
        in_specs += [
            pl.BlockSpec((1, PAST_LEN, vw), lambda b, i: (b, 0, 0)),
            pl.BlockSpec((1, n_kv, 64, PAST_LEN), lambda b, i: (b, 0, 0, 0)),
        ]
        args += list(ctx)
    if tables is not None:
        cq, sq, cqt, sqt = tables
        in_specs += [
            pl.BlockSpec((1, 64, TC), lambda b, i: (i, 0, 0)),
            pl.BlockSpec((1, 64, TC), lambda b, i: (i, 0, 0)),
            _const_spec((seq, 64)),
            _const_spec((seq, 64)),
        ]
        args += [cqt, sqt, cq, sq]
    for p in params:
        in_specs.append(_const_spec(p.shape))
        args.append(p)

    out_specs = [pl.BlockSpec((TQ, qw), lambda b, i: (b * nq + i, 0))]
    out_shape = [jax.ShapeDtypeStruct((nb * seq, qw), BF16)]
    if emit_k:
        out_specs.append(pl.BlockSpec((1, seq, vw), lambda b, i: (b, 0, 0)))
        out_shape.append(jax.ShapeDtypeStruct((nb, seq, vw), F32))

    return pl.pallas_call(
        functools.partial(_attn_kernel, mode=mode, seq=seq, has_ctx=ctx is not None,
                          rope=tables is not None, emit_k=emit_k, lam_init=lam_init),
        grid=(nb, nq),
        in_specs=in_specs,
        out_specs=out_specs,
        out_shape=out_shape,
        scratch_shapes=[
            pltpu.VMEM((n_kv, lk, 64), BF16),
            pltpu.VMEM((n_kv, VROWS, lk), BF16),
            pltpu.VMEM((2, lk, TQ), F32),
            pltpu.VMEM((lk, 8 * TQ), BF16),
        ],
        compiler_params=_cparams(("arbitrary", "arbitrary")),
        name=f"attn_{mode}_{'lat' if latent else 'ctx'}",
    )(*args)


def _scan_lanes(x, op, fill, reverse):
    lane = lax.broadcasted_iota(jnp.int32, x.shape, 1)
    k = 1
    while k < TC:
        if reverse:
            x = op(x, jnp.where(lane < TC - k, pltpu.roll(x, TC - k, 1), fill))
        else:
            x = op(x, jnp.where(lane >= k, pltpu.roll(x, k, 1), fill))
        k *= 2
    return x


def _mlstm_kernel(k_ref, qt_ref, vt_ref, ot_ref, gt_ref, s0_ref, m0_ref, gb_ref,
                  y_ref, s_ref, m_ref, hf_ref, hr_ref, *, seq):
    nc = seq // TC
    s_idx = lax.broadcasted_iota(jnp.int32, (TC, TC), 0)
    t_idx = lax.broadcasted_iota(jnp.int32, (TC, TC), 1)
    seen = (s_idx <= t_idx, s_idx >= t_idx)
    is_fwd = lax.broadcasted_iota(jnp.int32, (8, TC), 0) < H_B
    ones_rows = (lax.broadcasted_iota(jnp.int32, (64, TC), 0) == 0).astype(F32)
    units = [(d, h) for d in range(2) for h in range(H_B)]

    def chunk(j, carry):
        states, m = carry
        cs = (j, nc - 1 - j)
        gi = jnp.concatenate([gt_ref[cs[0], 0:4, :], gt_ref[cs[1], 8:12, :]], axis=0)
        gf = jnp.concatenate([gt_ref[cs[0], 4:8, :], gt_ref[cs[1], 12:16, :]], axis=0)
        f = _log_sigmoid(gf)
        b = jnp.where(is_fwd, _scan_lanes(f, jnp.add, 0.0, False), _scan_lanes(f, jnp.add, 0.0, True))
        a = gi - b
        lm = jnp.where(is_fwd, _scan_lanes(a, jnp.maximum, -jnp.inf, False),
                       _scan_lanes(a, jnp.maximum, -jnp.inf, True))
        mx = jnp.maximum(m, lm)
        w = jnp.exp(m - mx)
        floor = jnp.exp(-(b + mx))
        mxl = jnp.maximum(m, jnp.max(a, axis=1, keepdims=True))
        wk = jnp.exp(a - mxl)
        decay = jnp.exp(m - mxl)
        m_new = jnp.sum(f, axis=1, keepdims=True) + mxl
        cols = jnp.concatenate([a, wk, jnp.zeros((128 - 16, TC), F32)], axis=0).T

        first = []
        for (d, h), st in zip(units, states):
            hs = slice(h * 64, (h + 1) * 64)
            kh = k_ref[_ds(cs[d], TC), hs] * (DK_B ** -0.5)
            qt = qt_ref[cs[d], hs, :].astype(BF16)
            lhs = jnp.concatenate([kh.astype(BF16), st.astype(BF16)], axis=0)
            first.append((kh, jnp.dot(lhs, qt, preferred_element_type=F32)))
        second = []
        for u, ((d, h), (kh, both)) in enumerate(zip(units, first)):
            dt = jnp.where(seen[d], cols[:, u:u + 1] - mx[u:u + 1, :], -jnp.inf)
            sp = both[0:TC] * jnp.exp(dt)
            den = jnp.sum(sp, axis=0, keepdims=True) + w[u:u + 1, :] * both[TC + 64:TC + 65]
            second.append((sp.astype(BF16), (kh * cols[:, 8 + u:9 + u]).astype(BF16), den))
        new_states = []
        for u, ((d, h), st, (kh, both), (spb, kw, den)) in enumerate(zip(units, states, first, second)):
            hs = slice(h * 64, (h + 1) * 64)
            vx = jnp.concatenate([vt_ref[cs[d], hs, :], ones_rows], axis=0).astype(BF16)
            num = jnp.dot(vx, spb, preferred_element_type=F32)[0:64] + w[u:u + 1, :] * both[TC:TC + 64]
            hv = num / jnp.maximum(jnp.abs(den), floor[u:u + 1, :])
            if d == 0:
                hf_ref[cs[d], hs, :] = hv
            else:
                hr_ref[cs[d], hs, :] = hv
            new_states.append(decay[u:u + 1, :] * st + jnp.dot(vx, kw, preferred_element_type=F32))
        return tuple(new_states), m_new

    init = (tuple(s0_ref[0, u] for u in range(2 * H_B)), m0_ref[0][:, 0:1])
    if nc == 1:
        states, m = chunk(0, init)
    else:
        states, m = lax.fori_loop(0, nc, chunk, init)
    for u, st in enumerate(states):
        s_ref[0, u] = st
    m_ref[0] = jnp.broadcast_to(m, (2 * H_B, 128))

    def epilogue(c, carry):
        ys = []
        for h in range(H_B):
            hs = slice(h * 64, (h + 1) * 64)
            x = hf_ref[c, hs, :] + hr_ref[c, hs, :]
            ys.append(_rms(x, gb_ref[...], axis=0) * jax.nn.sigmoid(ot_ref[c, hs, :]))
        y_ref[_ds(c, TC), :] = jnp.concatenate(ys, axis=0).T.astype(BF16)
        return carry

    if nc == 1:
        epilogue(0, 0)
    else:
        lax.fori_loop(0, nc, epilogue, 0)


def _mlstm(z, zt, s0, m0, gb, *, latent):
    if latent:
        nb, seq, row0 = DEC_BATCH, DEC_SEQ, N_CTX
    else:
        nb, seq, row0 = BATCH, SEQ, 0
    blk0 = row0 // seq
    nc = seq // TC
    ztspec = lambda rb: pl.BlockSpec((nc, 256, TC), lambda b: (blk0 + b, rb, 0))
    return pl.pallas_call(
        functools.partial(_mlstm_kernel, seq=seq),
        grid=(nb,),
        in_specs=[
            pl.BlockSpec((seq, 256), lambda b: (blk0 + b, Z_KB // 256)),
            ztspec(ZT_QB // 256), ztspec(ZT_VB // 256), ztspec(ZT_OB // 256),
            pl.BlockSpec((nc, 16, TC), lambda b: (blk0 + b, ZT_G // 16, 0)),
            pl.BlockSpec((1, 2 * H_B, 128, 64), lambda b: (b, 0, 0, 0)),
            pl.BlockSpec((1, 2 * H_B, 128), lambda b: (b, 0, 0)),
            _const_spec((64, 1)),
        ],
        out_specs=[
            pl.BlockSpec((seq, 256), lambda b: (b, 0)),
            pl.BlockSpec((1, 2 * H_B, 128, 64), lambda b: (b, 0, 0, 0)),
            pl.BlockSpec((1, 2 * H_B, 128), lambda b: (b, 0, 0)),
        ],
        out_shape=[
            jax.ShapeDtypeStruct((nb * seq, 256), BF16),
            jax.ShapeDtypeStruct((nb, 2 * H_B, 128, 64), F32),
            jax.ShapeDtypeStruct((nb, 2 * H_B, 128), F32),
        ],
        scratch_shapes=[pltpu.VMEM((nc, 256, TC), F32), pltpu.VMEM((nc, 256, TC), F32)],
        compiler_params=_cparams(("arbitrary",)),
        name=f"mlstm_{'lat' if latent else 'ctx'}",
    )(z, zt, zt, zt, zt, s0, m0, gb)


def _rope_tables(dim):
    t = jnp.arange(DEC_SEQ)
    row = (t // GRID_W).astype(F32)
    colp = (t % GRID_W).astype(F32)
    axis_dim = dim // 2
    freqs = ROPE_THETA ** (-jnp.arange(0, axis_dim, 2, dtype=F32) / axis_dim)
    ang = jnp.concatenate([row[:, None] * freqs, colp[:, None] * freqs], axis=-1)
    cos, sin = jnp.cos(ang), jnp.sin(ang)
    reps = 64 // dim
    cq = jnp.concatenate([cos, cos] * reps, axis=1)
    sq = jnp.concatenate([-sin, sin] * reps, axis=1)
    to_chunks = lambda x: x.T.reshape(64, DEC_SEQ // TC, TC).transpose(1, 0, 2)
    return cq, sq, to_chunks(cq), to_chunks(sq)


def _split_w_in(w, b_gates):
    a0, b0, c0 = 0, 768, 1808
    qa, ka, va = w[:, a0:a0 + 512], w[:, a0 + 512:a0 + 640], w[:, a0 + 640:a0 + 768]
    qb, kb = w[:, b0:b0 + 256], w[:, b0 + 256:b0 + 512]
    vb, ob = w[:, b0 + 512:b0 + 768], w[:, b0 + 768:b0 + 1024]
    gb = w[:, b0 + 1024:b0 + 1040]
    qc, kc, vc = w[:, c0:c0 + 256], w[:, c0 + 256:c0 + 512], w[:, c0 + 512:c0 + 768]
    wz = jnp.concatenate([kb, kc, ka], axis=1)
    wt = jnp.concatenate([qa, qc, vc, qb, vb, ob, va, gb], axis=1).T
    bt = jnp.zeros((ZT_W, 1), F32).at[ZT_G:ZT_G + 16, 0].set(b_gates)
    return wz.astype(BF16), wt.astype(BF16), bt


def _tokens_major(x):
    return x.transpose(0, 2, 1)


def kernel(x_prompt, x_sample, cache_a_k, cache_a_v, cache_c_k, cache_c_v, state_b_C, state_b_n,
           state_b_m, c, c_ctx, w_ada, b_ada, g_norm, w_ff_in, w_ff_out, w_in, w_out, g_qa, g_ka,
           b_gates, g_b, lam_q1, lam_k1, lam_q2, lam_k2, g_c, g_final):
    h = (x_prompt.reshape(N_CTX, D_MODEL), x_sample.reshape(N_LAT, D_MODEL))
    cc = jnp.concatenate([c_ctx[None, :], c, jnp.zeros((3, D_MODEL), F32)], axis=0)
    mods_all = _ada(cc, w_ada, b_ada).reshape(DEPTH, 8, 9, D_MODEL)
    tab_a = _rope_tables(HD_A)
    tab_c = _rope_tables(D_C)
    gf = g_final.reshape(1, D_MODEL)

    outs = {k: [] for k in ("ak", "av", "ck", "cv", "bC", "bn", "bm")}
    for l in range(DEPTH):
        mods = mods_all[l]
        lam_init = 0.8 - 0.6 * math.exp(-0.3 * l)
        gn = g_norm[l].reshape(3, 1, D_MODEL)
        h = _ffn(h, mods, gn[0], w_ff_in[l, 0].astype(BF16), w_ff_out[l, 0].astype(BF16), gf,
                 k=0, first=(l == 0))
        wz, wt, bt = _split_w_in(w_in[l], b_gates[l])
        z, zt = _proj_in(h, mods, gn[1], wz, wt, bt)

        gq, gk = g_qa[l].reshape(64, 1), g_ka[l].reshape(1, 64)
        lamp = jnp.stack([lam_q1[l], lam_k1[l], lam_q2[l], lam_k2[l]], axis=0)
        gc = g_c[l].reshape(64, 1)
        gb = g_b[l].reshape(64, 1)

        ya_c, kn = _attn("A", z, zt, ctx=None, tables=None, params=(gq, gk), lam_init=lam_init,
                         latent=False)
        (yc_c,) = _attn("C", z, zt, ctx=None, tables=None, params=(lamp, gc), lam_init=lam_init,
                        latent=False)
        zeros_s = jnp.zeros((BATCH, 2 * H_B, 128, 64), F32)
        zeros_m = jnp.zeros((BATCH, 2 * H_B, 128), F32)
        yb_c, s_fin, m_fin = _mlstm(z, zt, zeros_s, zeros_m, gb, latent=False)

        ck_a = cache_a_k[:, l].reshape(DEC_BATCH, PAST_LEN, KV_A * HD_A)
        cvt_a = cache_a_v[:, l].transpose(0, 2, 3, 1)
        (ya_l,) = _attn("A", z, zt, ctx=(ck_a, cvt_a), tables=tab_a, params=(gq, gk),
                        lam_init=lam_init, latent=True)
        ck_c = cache_c_k[:, l].reshape(DEC_BATCH, PAST_LEN, H_C * 2 * D_C)
        cvt_c = cache_c_v[:, l].transpose(0, 2, 3, 1)
        (yc_l,) = _attn("C", z, zt, ctx=(ck_c, cvt_c), tables=tab_c, params=(lamp, gc),
                        lam_init=lam_init, latent=True)
        s0 = jnp.concatenate([jnp.swapaxes(state_b_C[:, l], -1, -2), state_b_n[:, l][..., None, :],
                              jnp.zeros((DEC_BATCH, 2, H_B, 63, DK_B), F32)], axis=-2)
        s0 = s0.reshape(DEC_BATCH, 2 * H_B, 128, DK_B)
        m0 = jnp.broadcast_to(state_b_m[:, l].reshape(DEC_BATCH, 2 * H_B, 1), (DEC_BATCH, 2 * H_B, 128))
        yb_l, _, _ = _mlstm(z, zt, s0, m0, gb, latent=True)

        h = _proj_out(h, (ya_c, ya_l), (yb_c, yb_l), (yc_c, yc_l), mods, w_out[l].astype(BF16))
        h = _ffn(h, mods, gn[2], w_ff_in[l, 1].astype(BF16), w_ff_out[l, 1].astype(BF16), gf,
                 k=6, final=(l == DEPTH - 1))

        zc = z[:N_CTX]
        ztc = zt[:N_CTX // TC]
        outs["ak"].append(kn.reshape(BATCH, SEQ, KV_A, HD_A))
        outs["av"].append(_tokens_major(ztc[:, ZT_VA:ZT_VA + 128, :]).reshape(BATCH, SEQ, KV_A, HD_A))
        outs["ck"].append(zc[:, Z_KC:Z_KC + 256].reshape(BATCH, SEQ, H_C, 2, D_C))
        outs["cv"].append(_tokens_major(ztc[:, ZT_VC:ZT_VC + 256, :]).reshape(BATCH, SEQ, H_C, 2 * D_C))
        s_fin = s_fin.reshape(BATCH, 2, H_B, 128, DK_B)
        outs["bC"].append(jnp.swapaxes(s_fin[..., 0:64, :], -1, -2))
        outs["bn"].append(s_fin[..., 64, :])
        outs["bm"].append(m_fin[:, :, 0].reshape(BATCH, 2, H_B))

    y_prompt = h[0].reshape(BATCH, SEQ, D_MODEL)
    y_sample = h[1].reshape(DEC_BATCH, DEC_SEQ, D_MODEL)
    st = lambda k: jnp.stack(outs[k], axis=1)
    return (y_prompt, y_sample, st("ak"), st("av"), st("ck"), st("cv"), st("bC"), st("bn"), st("bm"))
```

```python
import functools
import math

import jax
import jax.numpy as jnp
from jax import lax
from jax.experimental import pallas as pl
from jax.experimental.pallas import tpu as pltpu

F32 = jnp.float32
BF16 = jnp.bfloat16

D_MODEL = 1024
BATCH = 16
SEQ = 256
DEPTH = 2
DEC_BATCH = 4
DEC_SEQ = 2048
PAST_LEN = 512
GRID_W = 64
ROPE_THETA = 10000.0
EPS = 1e-6
HD_A = 64
H_A = 8
KV_A = 2
DK_B = 64
H_B = 4
D_C = 32
H_C = 4
D_FF = 2816
LOG2E = 1.4426950408889634

N_CTX = BATCH * SEQ
N_LAT = DEC_BATCH * DEC_SEQ
N_TOK = N_CTX + N_LAT

TM = 512
CTX_TILES = N_CTX // TM
LAT_TILES_PER_BATCH = DEC_SEQ // TM
FF_CHUNKS = ((0, 1536), (1536, D_FF))
TC = 256
SB = 256
TQ = 256
VROWS = 80

Z_KB, Z_KC, Z_KA = 0, 256, 512
Z_W = 640
ZT_QA, ZT_QC, ZT_VC, ZT_QB, ZT_VB, ZT_OB, ZT_VA, ZT_G = 0, 512, 768, 1024, 1280, 1536, 1792, 1920
ZT_W = 1936

VMEM_LIMIT = 56 * 1024 * 1024


def _cparams(sem):
    return pltpu.CompilerParams(dimension_semantics=sem, vmem_limit_bytes=VMEM_LIMIT)


def _const_spec(shape):
    nd = len(shape)
    return pl.BlockSpec(shape, lambda *_: (0,) * nd, pipeline_mode=pl.Buffered(1))


def _mod_row(i):
    return jnp.maximum(i - CTX_TILES, 0) // LAT_TILES_PER_BATCH + (i >= CTX_TILES).astype(jnp.int32)


def _rms(x, g, axis=-1):
    ms = jnp.mean(x * x, axis=axis, keepdims=True)
    return x * lax.rsqrt(ms + EPS) * g


def _norm_mod(x, g, mod_ref, k):
    shift = mod_ref[0, k:k + 1, :]
    scale = mod_ref[0, k + 1:k + 2, :]
    return _rms(x, g) * (1.0 + scale) + shift


def _ds(c, n):
    if isinstance(c, int):
        return pl.ds(c * n, n)
    return pl.ds(pl.multiple_of(c * n, n), n)


def _log_sigmoid(x):
    return jnp.minimum(x, 0.0) - jnp.log1p(jnp.exp(-jnp.abs(x)))


def _ada_kernel(c_ref, w_ref, b_ref, o_ref):
    c = c_ref[...]
    a = (c * jax.nn.sigmoid(c)).astype(BF16)
    o_ref[0] = jnp.dot(a, w_ref[0].astype(BF16), preferred_element_type=F32) + b_ref[0]


def _ada(cc, w_ada, b_ada):
    tn = 1024
    n = 9 * D_MODEL
    return pl.pallas_call(
        _ada_kernel,
        grid=(DEPTH, n // tn),
        in_specs=[
            pl.BlockSpec((8, D_MODEL), lambda l, j: (0, 0)),
            pl.BlockSpec((1, D_MODEL, tn), lambda l, j: (l, 0, j)),
            pl.BlockSpec((1, 1, tn), lambda l, j: (l, 0, j)),
        ],
        out_specs=pl.BlockSpec((1, 8, tn), lambda l, j: (l, 0, j)),
        out_shape=jax.ShapeDtypeStruct((DEPTH, 8, n), F32),
        compiler_params=_cparams(("arbitrary", "arbitrary")),
        name="ada",
    )(cc, w_ada, b_ada.reshape(DEPTH, 1, n))


def _ffn_kernel(*refs, k, first, mix, final):
    is_ctx = pl.program_id(0) < CTX_TILES
    pick = lambda c_ref, l_ref: jnp.where(is_ctx, c_ref[...], l_ref[...])
    it = iter(refs)
    x = pick(next(it), next(it)) if first else next(it)[...]
    if mix:
        ya, yb, yc = (pick(next(it), next(it)) for _ in range(3))
        wmix_ref = next(it)
    mod_ref, g_ref, wi_ref, wo_ref, gf_ref = (next(it) for _ in range(5))
    if mix:
        y = jnp.dot(ya, wmix_ref[0:512, :], preferred_element_type=F32)
        y = y + jnp.dot(yb, wmix_ref[512:768, :], preferred_element_type=F32)
        y = y + jnp.dot(yc, wmix_ref[768:1024, :], preferred_element_type=F32)
        x = x + mod_ref[0, 5:6, :] * y
    xn = _norm_mod(x, g_ref[...], mod_ref, k).astype(BF16)
    acc = jnp.zeros((TM, D_MODEL), F32)
    for lo, hi in FF_CHUNKS:
        g = jnp.dot(xn, wi_ref[:, lo:hi], preferred_element_type=F32)
        u = jnp.dot(xn, wi_ref[:, D_FF + lo:D_FF + hi], preferred_element_type=F32)
        a = (g * jax.nn.sigmoid(g) * u).astype(BF16)
        acc = acc + jnp.dot(a, wo_ref[lo:hi, :], preferred_element_type=F32)
    gate = mod_ref[0, k + 2:k + 3, :]
    y = x + 0.5 * gate * acc
    if final:
        yc_ref, yl_ref = refs[-2:]
        y = _rms(y, gf_ref[...])

        @pl.when(is_ctx)
        def _():
            yc_ref[...] = y

        @pl.when(jnp.logical_not(is_ctx))
        def _():
            yl_ref[...] = y
    else:
        refs[-1][...] = y


def _ctx_lat_specs(width):
    return [
        pl.BlockSpec((TM, width), lambda i: (jnp.minimum(i, CTX_TILES - 1), 0)),
        pl.BlockSpec((TM, width), lambda i: (jnp.maximum(i - CTX_TILES, 0), 0)),
    ]


def _ffn(h, mods, g, wi, wo, gf, *, k, first=False, mix=None, final=False):
    flat_spec = pl.BlockSpec((TM, D_MODEL), lambda i: (i, 0))
    flat_shape = jax.ShapeDtypeStruct((N_TOK, D_MODEL), F32)
    mix_specs, mix_args = [], []
    if mix is not None:
        mix_specs = (_ctx_lat_specs(512) + _ctx_lat_specs(256) + _ctx_lat_specs(256)
                     + [_const_spec((D_MODEL, D_MODEL))])
        mix_args = [*mix[0], *mix[1], *mix[2], mix[3]]
    return pl.pallas_call(
        functools.partial(_ffn_kernel, k=k, first=first, mix=mix is not None, final=final),
        grid=(N_TOK // TM,),
        in_specs=(_ctx_lat_specs(D_MODEL) if first else [flat_spec]) + mix_specs + [
            pl.BlockSpec((1, 9, D_MODEL), lambda i: (_mod_row(i), 0, 0)),
            _const_spec((1, D_MODEL)),
            _const_spec((D_MODEL, 2 * D_FF)),
            _const_spec((D_FF, D_MODEL)),
            _const_spec((1, D_MODEL)),
        ],
        out_specs=_ctx_lat_specs(D_MODEL) if final else flat_spec,
        out_shape=[jax.ShapeDtypeStruct((N_CTX, D_MODEL), F32),
                   jax.ShapeDtypeStruct((N_LAT, D_MODEL), F32)] if final else flat_shape,
        compiler_params=_cparams(("arbitrary",)),
        name="ffn_final" if final else ("ffn_first" if first else ("ffn_mix" if mix else "ffn")),
    )(*(h if first else (h,)), *mix_args, mods, g, wi, wo, gf)


def _proj_in_kernel(h_ref, mod_ref, g_ref, w_ref, wt_ref, bt_ref, z_ref, zt_ref):
    xn = _norm_mod(h_ref[...], g_ref[...], mod_ref, 3).astype(BF16)
    z_ref[...] = jnp.dot(xn, w_ref[...], preferred_element_type=F32)
    zt = lax.dot_general(wt_ref[...], xn, (((1,), (1,)), ((), ())), preferred_element_type=F32)
    zt = zt + bt_ref[...]
    for c in range(TM // TC):
        zt_ref[c] = zt[:, c * TC:(c + 1) * TC]


def _proj_in(h, mods, g, wz, wt, bt):
    return pl.pallas_call(
        _proj_in_kernel,
        grid=(N_TOK // TM,),
        in_specs=[
            pl.BlockSpec((TM, D_MODEL), lambda i: (i, 0)),
            pl.BlockSpec((1, 9, D_MODEL), lambda i: (_mod_row(i), 0, 0)),
            _const_spec((1, D_MODEL)),
            _const_spec((D_MODEL, Z_W)),
            _const_spec((ZT_W, D_MODEL)),
            _const_spec((ZT_W, 1)),
        ],
        out_specs=[
            pl.BlockSpec((TM, Z_W), lambda i: (i, 0)),
            pl.BlockSpec((TM // TC, ZT_W, TC), lambda i: (i, 0, 0)),
        ],
        out_shape=[
            jax.ShapeDtypeStruct((N_TOK, Z_W), F32),
            jax.ShapeDtypeStruct((N_TOK // TC, ZT_W, TC), F32),
        ],
        compiler_params=_cparams(("arbitrary",)),
        name="proj_in",
    )(h, mods, g, wz, wt, bt)


def _swap_halves(x, width, axis):
    n = x.shape[axis]
    half = width // 2
    parts = []
    for s in range(0, n, width):
        parts.append(lax.slice_in_dim(x, s + half, s + width, axis=axis))
        parts.append(lax.slice_in_dim(x, s, s + half, axis=axis))
    return jnp.concatenate(parts, axis=axis)


def _attn_kernel(*refs, mode, seq, has_ctx, rope, emit_k, lam_init):
    it = iter(refs)
    qt_ref, k_ref, vt_ref = next(it), next(it), next(it)
    if has_ctx:
        ck_ref, cvt_ref = next(it), next(it)
    if rope:
        cqt_ref, sqt_ref, ck_tab, sk_tab = next(it), next(it), next(it), next(it)
    if mode == "A":
        gq_ref, gk_ref = next(it), next(it)
    else:
        lam_ref, gc_ref = next(it), next(it)
    y_ref = next(it)
    if emit_k:
        kn_ref = next(it)
    kbuf, vbuf, s_ref, p_ref = next(it), next(it), next(it), next(it)

    n_kv = KV_A if mode == "A" else H_C
    per_kv = 8 // n_kv
    n_ctx = PAST_LEN if has_ctx else 0
    lk = n_ctx + seq
    width = HD_A if mode == "A" else D_C

    @pl.when(pl.program_id(1) == 0)
    def _fill():
        ones_rows = (lax.broadcasted_iota(jnp.int32, (VROWS - 64, lk), 0) == 0).astype(BF16)
        for h in range(n_kv):
            hs = slice(h * 64, (h + 1) * 64)
            vbuf[h, 64:VROWS, :] = ones_rows
            if has_ctx:
                kbuf[h, 0:n_ctx, :] = ck_ref[0, :, hs].astype(BF16)
                vbuf[h, 0:64, 0:n_ctx] = cvt_ref[0, h].astype(BF16)
            for c in range(seq // TC):
                rows = slice(c * TC, (c + 1) * TC)
                k = k_ref[rows, hs]
                if mode == "A":
                    k = _rms(k, gk_ref[...])
                    if emit_k:
                        kn_ref[0, rows, hs] = k
                if rope:
                    k = k * ck_tab[rows, :] + _swap_halves(k, width, 1) * sk_tab[rows, :]
                kbuf[h, n_ctx + c * TC:n_ctx + (c + 1) * TC, :] = k.astype(BF16)
                vbuf[h, 0:64, n_ctx + c * TC:n_ctx + (c + 1) * TC] = vt_ref[c, hs, :].astype(BF16)

    qts = []
    if mode == "A":
        scale = HD_A ** -0.5 * LOG2E
        for hh in range(H_A):
            x = _rms(qt_ref[0, hh * 64:(hh + 1) * 64, :], gq_ref[...], axis=0)
            if rope:
                x = x * cqt_ref[0] + _swap_halves(x, width, 0) * sqt_ref[0]
            qts.append((x * scale).astype(BF16))
    else:
        scale = D_C ** -0.5 * LOG2E
        first_map = lax.broadcasted_iota(jnp.int32, (64, TQ), 0) < D_C
        for h in range(H_C):
            x = qt_ref[0, h * 64:(h + 1) * 64, :]
            if rope:
                x = x * cqt_ref[0] + _swap_halves(x, width, 0) * sqt_ref[0]
            x = x * scale
            qts.append(jnp.where(first_map, x, 0.0).astype(BF16))
            qts.append(jnp.where(first_map, 0.0, x).astype(BF16))

    m_prev = None
    outs = []
    for s in range(9):
        def stage_block(j, mrun, s=s, m_prev=m_prev):
            rows = _ds(j, SB)
            if s < 8:
                st = jnp.dot(kbuf[s // per_kv, rows, :], qts[s], preferred_element_type=F32)
                s_ref[s % 2, rows, :] = st
                for r in range(SB // 8):
                    mrun = jnp.maximum(mrun, st[8 * r:8 * r + 8])
            if s >= 1:
                prev = s_ref[(s - 1) % 2, rows, :]
                p_ref[rows, (s - 1) * TQ:s * TQ] = jnp.exp2(prev - m_prev).astype(BF16)
            return mrun

        mrun = lax.fori_loop(0, lk // SB, stage_block, jnp.full((8, TQ), -jnp.inf, F32),
                             unroll=True)
        m_prev = jnp.max(mrun, axis=0, keepdims=True)
        if s >= 1:
            u = s - 1
            ox = jnp.dot(vbuf[u // per_kv], p_ref[:, u * TQ:(u + 1) * TQ], preferred_element_type=F32)
            outs.append(ox[0:64] / ox[64:65])

    if mode == "A":
        for pair in range(4):
            o = jnp.concatenate(outs[2 * pair:2 * pair + 2], axis=0)
            y_ref[:, pair * 128:(pair + 1) * 128] = o.T.astype(BF16)
    else:
        lp = lam_ref[...]
        lam = (jnp.exp(jnp.sum(lp[0:1] * lp[1:2], axis=1, keepdims=True))
               - jnp.exp(jnp.sum(lp[2:3] * lp[3:4], axis=1, keepdims=True)) + lam_init)
        for pair in range(2):
            o = []
            for h in (2 * pair, 2 * pair + 1):
                d = outs[2 * h] - lam * outs[2 * h + 1]
                o.append(_rms(d, gc_ref[...], axis=0) * (1.0 - lam_init))
            y_ref[:, pair * 128:(pair + 1) * 128] = jnp.concatenate(o, axis=0).T.astype(BF16)


def _attn(mode, z, zt, *, ctx, tables, params, lam_init, latent):
    if latent:
        nb, seq, row0 = DEC_BATCH, DEC_SEQ, N_CTX
    else:
        nb, seq, row0 = BATCH, SEQ, 0
    nq = seq // TQ
    n_kv = KV_A if mode == "A" else H_C
    n_ctx = PAST_LEN if ctx is not None else 0
    lk = n_ctx + seq
    qw = 512 if mode == "A" else 256
    vw = n_kv * 64
    q_blk = (ZT_QA if mode == "A" else ZT_QC) // qw
    k_blk = (Z_KA if mode == "A" else Z_KC) // vw
    v_blk = (ZT_VA if mode == "A" else ZT_VC) // vw
    emit_k = mode == "A" and not latent
    blk0 = row0 // seq

    in_specs = [
        pl.BlockSpec((1, qw, TC), lambda b, i: (row0 // TC + b * nq + i, q_blk, 0)),
        pl.BlockSpec((seq, vw), lambda b, i: (blk0 + b, k_blk)),
        pl.BlockSpec((seq // TC, vw, TC), lambda b, i: (blk0 + b, v_blk, 0)),
    ]
    args = [zt, z, zt]
    if ctx is not None:
        in_specs += [
            pl.BlockSpec((1, PAST_LEN, vw), lambda b, i: (b, 0, 0)),
            pl.BlockSpec((1, n_kv, 64, PAST_LEN), lambda b, i: (b, 0, 0, 0)),
        ]
        args += list(ctx)
    if tables is not None:
        cq, sq, cqt, sqt = tables
        in_specs += [
            pl.BlockSpec((1, 64, TC), lambda b, i: (i, 0, 0)),
            pl.BlockSpec((1, 64, TC), lambda b, i: (i, 0, 0)),
            _const_spec((seq, 64)),
            _const_spec((seq, 64)),
        ]
        args += [cqt, sqt, cq, sq]
    for p in params:
        in_specs.append(_const_spec(p.shape))
        args.append(p)

    out_specs = [pl.BlockSpec((TQ, qw), lambda b, i: (b * nq + i, 0))]
    out_shape = [jax.ShapeDtypeStruct((nb * seq, qw), BF16)]
    if emit_k:
        out_specs.append(pl.BlockSpec((1, seq, vw), lambda b, i: (b, 0, 0)))
        out_shape.append(jax.ShapeDtypeStruct((nb, seq, vw), F32))

    return pl.pallas_call(
        functools.partial(_attn_kernel, mode=mode, seq=seq, has_ctx=ctx is not None,
                          rope=tables is not None, emit_k=emit_k, lam_init=lam_init),
        grid=(nb, nq),
        in_specs=in_specs,
        out_specs=out_specs,
        out_shape=out_shape,
        scratch_shapes=[
            pltpu.VMEM((n_kv, lk, 64), BF16),
            pltpu.VMEM((n_kv, VROWS, lk), BF16),
            pltpu.VMEM((2, lk, TQ), F32),
            pltpu.VMEM((lk, 8 * TQ), BF16),
        ],
        compiler_params=_cparams(("arbitrary", "arbitrary")),
        name=f"attn_{mode}_{'lat' if latent else 'ctx'}",
    )(*args)


def _scan_lanes(x, op, fill, reverse):
    lane = lax.broadcasted_iota(jnp.int32, x.shape, 1)
    k = 1
    while k < TC:
        if reverse:
            x = op(x, jnp.where(lane < TC - k, pltpu.roll(x, TC - k, 1), fill))
        else:
            x = op(x, jnp.where(lane >= k, pltpu.roll(x, k, 1), fill))
        k *= 2
    return x


def _mlstm_kernel(k_ref, qt_ref, vt_ref, ot_ref, gt_ref, s0_ref, m0_ref, gb_ref,
                  y_ref, s_ref, m_ref, hf_ref, hr_ref, *, seq):
    nc = seq // TC
    s_idx = lax.broadcasted_iota(jnp.int32, (TC, TC), 0)
    t_idx = lax.broadcasted_iota(jnp.int32, (TC, TC), 1)
    seen = (s_idx <= t_idx, s_idx >= t_idx)
    is_fwd = lax.broadcasted_iota(jnp.int32, (8, TC), 0) < H_B
    ones_rows = (lax.broadcasted_iota(jnp.int32, (64, TC), 0) == 0).astype(F32)
    units = [(d, h) for d in range(2) for h in range(H_B)]

    def chunk(j, carry):
        states, m = carry
        cs = (j, nc - 1 - j)
        gi = jnp.concatenate([gt_ref[cs[0], 0:4, :], gt_ref[cs[1], 8:12, :]], axis=0)
        gf = jnp.concatenate([gt_ref[cs[0], 4:8, :], gt_ref[cs[1], 12:16, :]], axis=0)
        f = _log_sigmoid(gf)
        b = jnp.where(is_fwd, _scan_lanes(f, jnp.add, 0.0, False), _scan_lanes(f, jnp.add, 0.0, True))
        a = gi - b
        lm = jnp.where(is_fwd, _scan_lanes(a, jnp.maximum, -jnp.inf, False),
                       _scan_lanes(a, jnp.maximum, -jnp.inf, True))
        mx = jnp.maximum(m, lm)
        w = jnp.exp(m - mx)
        floor = jnp.exp(-(b + mx))
        mxl = jnp.maximum(m, jnp.max(a, axis=1, keepdims=True))
        wk = jnp.exp(a - mxl)
        decay = jnp.exp(m - mxl)
        m_new = jnp.sum(f, axis=1, keepdims=True) + mxl
        cols = jnp.concatenate([a, wk, jnp.zeros((128 - 16, TC), F32)], axis=0).T

        first = []
        for (d, h), st in zip(units, states):
            hs = slice(h * 64, (h + 1) * 64)
            kh = k_ref[_ds(cs[d], TC), hs] * (DK_B ** -0.5)
            qt = qt_ref[cs[d], hs, :].astype(BF16)
            lhs = jnp.concatenate([kh.astype(BF16), st.astype(BF16)], axis=0)
            first.append((kh, jnp.dot(lhs, qt, preferred_element_type=F32)))
        second = []
        for u, ((d, h), (kh, both)) in enumerate(zip(units, first)):
            dt = jnp.where(seen[d], cols[:, u:u + 1] - mx[u:u + 1, :], -jnp.inf)
            sp = both[0:TC] * jnp.exp(dt)
            den = jnp.sum(sp, axis=0, keepdims=True) + w[u:u + 1, :] * both[TC + 64:TC + 65]
            second.append((sp.astype(BF16), (kh * cols[:, 8 + u:9 + u]).astype(BF16), den))
        new_states = []
        for u, ((d, h), st, (kh, both), (spb, kw, den)) in enumerate(zip(units, states, first, second)):
            hs = slice(h * 64, (h + 1) * 64)
            vx = jnp.concatenate([vt_ref[cs[d], hs, :], ones_rows], axis=0).astype(BF16)
            num = jnp.dot(vx, spb, preferred_element_type=F32)[0:64] + w[u:u + 1, :] * both[TC:TC + 64]
            hv = num / jnp.maximum(jnp.abs(den), floor[u:u + 1, :])
            if d == 0:
                hf_ref[cs[d], hs, :] = hv
            else:
                hr_ref[cs[d], hs, :] = hv
            new_states.append(decay[u:u + 1, :] * st + jnp.dot(vx, kw, preferred_element_type=F32))
        return tuple(new_states), m_new

    init = (tuple(s0_ref[0, u] for u in range(2 * H_B)), m0_ref[0][:, 0:1])
    if nc == 1:
        states, m = chunk(0, init)
    else:
        states, m = lax.fori_loop(0, nc, chunk, init)
    for u, st in enumerate(states):
        s_ref[0, u] = st
    m_ref[0] = jnp.broadcast_to(m, (2 * H_B, 128))

    def epilogue(c, carry):
        ys = []
        for h in range(H_B):
            hs = slice(h * 64, (h + 1) * 64)
            x = hf_ref[c, hs, :] + hr_ref[c, hs, :]
            ys.append(_rms(x, gb_ref[...], axis=0) * jax.nn.sigmoid(ot_ref[c, hs, :]))
        y_ref[_ds(c, TC), :] = jnp.concatenate(ys, axis=0).T.astype(BF16)
        return carry

    if nc == 1:
        epilogue(0, 0)
    else:
        lax.fori_loop(0, nc, epilogue, 0)


def _mlstm(z, zt, s0, m0, gb, *, latent):
    if latent:
        nb, seq, row0 = DEC_BATCH, DEC_SEQ, N_CTX
    else:
        nb, seq, row0 = BATCH, SEQ, 0
    blk0 = row0 // seq
    nc = seq // TC
    ztspec = lambda rb: pl.BlockSpec((nc, 256, TC), lambda b: (blk0 + b, rb, 0))
    return pl.pallas_call(
        functools.partial(_mlstm_kernel, seq=seq),
        grid=(nb,),
        in_specs=[
            pl.BlockSpec((seq, 256), lambda b: (blk0 + b, Z_KB // 256)),
            ztspec(ZT_QB // 256), ztspec(ZT_VB // 256), ztspec(ZT_OB // 256),
            pl.BlockSpec((nc, 16, TC), lambda b: (blk0 + b, ZT_G // 16, 0)),
            pl.BlockSpec((1, 2 * H_B, 128, 64), lambda b: (b, 0, 0, 0)),
            pl.BlockSpec((1, 2 * H_B, 128), lambda b: (b, 0, 0)),
            _const_spec((64, 1)),
        ],
        out_specs=[
            pl.BlockSpec((seq, 256), lambda b: (b, 0)),
            pl.BlockSpec((1, 2 * H_B, 128, 64), lambda b: (b, 0, 0, 0)),
            pl.BlockSpec((1, 2 * H_B, 128), lambda b: (b, 0, 0)),
        ],
        out_shape=[
            jax.ShapeDtypeStruct((nb * seq, 256), BF16),
            jax.ShapeDtypeStruct((nb, 2 * H_B, 128, 64), F32),
            jax.ShapeDtypeStruct((nb, 2 * H_B, 128), F32),
        ],
        scratch_shapes=[pltpu.VMEM((nc, 256, TC), F32), pltpu.VMEM((nc, 256, TC), F32)],
        compiler_params=_cparams(("arbitrary",)),
        name=f"mlstm_{'lat' if latent else 'ctx'}",
    )(z, zt, zt, zt, zt, s0, m0, gb)


def _rope_tables(dim):
    t = jnp.arange(DEC_SEQ)
    row = (t // GRID_W).astype(F32)
    colp = (t % GRID_W).astype(F32)
    axis_dim = dim // 2
    freqs = ROPE_THETA ** (-jnp.arange(0, axis_dim, 2, dtype=F32) / axis_dim)
    ang = jnp.concatenate([row[:, None] * freqs, colp[:, None] * freqs], axis=-1)
    cos, sin = jnp.cos(ang), jnp.sin(ang)
    reps = 64 // dim
    cq = jnp.concatenate([cos, cos] * reps, axis=1)
    sq = jnp.concatenate([-sin, sin] * reps, axis=1)
    to_chunks = lambda x: x.T.reshape(64, DEC_SEQ // TC, TC).transpose(1, 0, 2)
    return cq, sq, to_chunks(cq), to_chunks(sq)


def _split_w_in(w, b_gates):
    a0, b0, c0 = 0, 768, 1808
    qa, ka, va = w[:, a0:a0 + 512], w[:, a0 + 512:a0 + 640], w[:, a0 + 640:a0 + 768]
    qb, kb = w[:, b0:b0 + 256], w[:, b0 + 256:b0 + 512]
    vb, ob = w[:, b0 + 512:b0 + 768], w[:, b0 + 768:b0 + 1024]
    gb = w[:, b0 + 1024:b0 + 1040]
    qc, kc, vc = w[:, c0:c0 + 256], w[:, c0 + 256:c0 + 512], w[:, c0 + 512:c0 + 768]
    wz = jnp.concatenate([kb, kc, ka], axis=1)
    wt = jnp.concatenate([qa, qc, vc, qb, vb, ob, va, gb], axis=1).T
    bt = jnp.zeros((ZT_W, 1), F32).at[ZT_G:ZT_G + 16, 0].set(b_gates)
    return wz.astype(BF16), wt.astype(BF16), bt


def _tokens_major(x):
    return x.transpose(0, 2, 1)


def kernel(x_prompt, x_sample, cache_a_k, cache_a_v, cache_c_k, cache_c_v, state_b_C, state_b_n,
           state_b_m, c, c_ctx, w_ada, b_ada, g_norm, w_ff_in, w_ff_out, w_in, w_out, g_qa, g_ka,
           b_gates, g_b, lam_q1, lam_k1, lam_q2, lam_k2, g_c, g_final):
    h = (x_prompt.reshape(N_CTX, D_MODEL), x_sample.reshape(N_LAT, D_MODEL))
    cc = jnp.concatenate([c_ctx[None, :], c, jnp.zeros((3, D_MODEL), F32)], axis=0)
    mods_all = _ada(cc, w_ada, b_ada).reshape(DEPTH, 8, 9, D_MODEL)
    tab_a = _rope_tables(HD_A)
    tab_c = _rope_tables(D_C)
    gf = g_final.reshape(1, D_MODEL)

    outs = {k: [] for k in ("ak", "av", "ck", "cv", "bC", "bn", "bm")}
    for l in range(DEPTH):
        mods = mods_all[l]
        lam_init = 0.8 - 0.6 * math.exp(-0.3 * l)
        gn = g_norm[l].reshape(3, 1, D_MODEL)
        h = _ffn(h, mods, gn[0], w_ff_in[l, 0].astype(BF16), w_ff_out[l, 0].astype(BF16), gf,
                 k=0, first=(l == 0))
        wz, wt, bt = _split_w_in(w_in[l], b_gates[l])
        z, zt = _proj_in(h, mods, gn[1], wz, wt, bt)

        gq, gk = g_qa[l].reshape(64, 1), g_ka[l].reshape(1, 64)
        lamp = jnp.stack([lam_q1[l], lam_k1[l], lam_q2[l], lam_k2[l]], axis=0)
        gc = g_c[l].reshape(64, 1)
        gb = g_b[l].reshape(64, 1)

        ya_c, kn = _attn("A", z, zt, ctx=None, tables=None, params=(gq, gk), lam_init=lam_init,
                         latent=False)
        (yc_c,) = _attn("C", z, zt, ctx=None, tables=None, params=(lamp, gc), lam_init=lam_init,
                        latent=False)
        zeros_s = jnp.zeros((BATCH, 2 * H_B, 128, 64), F32)
        zeros_m = jnp.zeros((BATCH, 2 * H_B, 128), F32)
        yb_c, s_fin, m_fin = _mlstm(z, zt, zeros_s, zeros_m, gb, latent=False)

        ck_a = cache_a_k[:, l].reshape(DEC_BATCH, PAST_LEN, KV_A * HD_A)
        cvt_a = cache_a_v[:, l].transpose(0, 2, 3, 1)
        (ya_l,) = _attn("A", z, zt, ctx=(ck_a, cvt_a), tables=tab_a, params=(gq, gk),
                        lam_init=lam_init, latent=True)
        ck_c = cache_c_k[:, l].reshape(DEC_BATCH, PAST_LEN, H_C * 2 * D_C)
        cvt_c = cache_c_v[:, l].transpose(0, 2, 3, 1)
        (yc_l,) = _attn("C", z, zt, ctx=(ck_c, cvt_c), tables=tab_c, params=(lamp, gc),
                        lam_init=lam_init, latent=True)
        s0 = jnp.concatenate([jnp.swapaxes(state_b_C[:, l], -1, -2), state_b_n[:, l][..., None, :],
                              jnp.zeros((DEC_BATCH, 2, H_B, 63, DK_B), F32)], axis=-2)
        s0 = s0.reshape(DEC_BATCH, 2 * H_B, 128, DK_B)
        m0 = jnp.broadcast_to(state_b_m[:, l].reshape(DEC_BATCH, 2 * H_B, 1), (DEC_BATCH, 2 * H_B, 128))
        yb_l, _, _ = _mlstm(z, zt, s0, m0, gb, latent=True)

        h = _ffn(h, mods, gn[2], w_ff_in[l, 1].astype(BF16), w_ff_out[l, 1].astype(BF16), gf, k=6,
                 mix=((ya_c, ya_l), (yb_c, yb_l), (yc_c, yc_l), w_out[l].astype(BF16)),
                 final=(l == DEPTH - 1))

        zc = z[:N_CTX]
        ztc = zt[:N_CTX // TC]
        outs["ak"].append(kn.reshape(BATCH, SEQ, KV_A, HD_A))
        outs["av"].append(_tokens_major(ztc[:, ZT_VA:ZT_VA + 128, :]).reshape(BATCH, SEQ, KV_A, HD_A))
        outs["ck"].append(zc[:, Z_KC:Z_KC + 256].reshape(BATCH, SEQ, H_C, 2, D_C))
        outs["cv"].append(_tokens_major(ztc[:, ZT_VC:ZT_VC + 256, :]).reshape(BATCH, SEQ, H_C, 2 * D_C))
        s_fin = s_fin.reshape(BATCH, 2, H_B, 128, DK_B)
        outs["bC"].append(jnp.swapaxes(s_fin[..., 0:64, :], -1, -2))
        outs["bn"].append(s_fin[..., 64, :])
        outs["bm"].append(m_fin[:, :, 0].reshape(BATCH, 2, H_B))

    y_prompt = h[0].reshape(BATCH, SEQ, D_MODEL)
    y_sample = h[1].reshape(DEC_BATCH, DEC_SEQ, D_MODEL)
    st = lambda k: jnp.stack(outs[k], axis=1)
    return (y_prompt, y_sample, st("ak"), st("av"), st("ck"), st("cv"), st("bC"), st("bn"), st("bm"))
```

```python
import functools
import math

import jax
import jax.numpy as jnp
from jax import lax
from jax.experimental import pallas as pl
from jax.experimental.pallas import tpu as pltpu

F32 = jnp.float32
BF16 = jnp.bfloat16

D_MODEL = 1024
BATCH = 16
SEQ = 256
DEPTH = 2
DEC_BATCH = 4
DEC_SEQ = 2048
PAST_LEN = 512
GRID_W = 64
ROPE_THETA = 10000.0
EPS = 1e-6
HD_A = 64
H_A = 8
KV_A = 2
DK_B = 64
H_B = 4
D_C = 32
H_C = 4
D_FF = 2816
LOG2E = 1.4426950408889634

N_CTX = BATCH * SEQ
N_LAT = DEC_BATCH * DEC_SEQ
N_TOK = N_CTX + N_LAT

TM = 512
CTX_TILES = N_CTX // TM
LAT_TILES_PER_BATCH = DEC_SEQ // TM
FF_CHUNKS = ((0, 1536), (1536, D_FF))
TC = 256
SB = 256
TQ = 256
VROWS = 80

Z_KB, Z_KC, Z_KA = 0, 256, 512
Z_W = 640
ZT_QA, ZT_QC, ZT_VC, ZT_QB, ZT_VB, ZT_OB, ZT_VA, ZT_G = 0, 512, 768, 1024, 1280, 1536, 1792, 1920
ZT_W = 1936

VMEM_LIMIT = 56 * 1024 * 1024


def _cparams(sem):
    return pltpu.CompilerParams(dimension_semantics=sem, vmem_limit_bytes=VMEM_LIMIT)


def _const_spec(shape, lead=()):
    nd = len(shape)
    return pl.BlockSpec((None,) * len(lead) + tuple(shape), lambda *_: tuple(lead) + (0,) * nd,
                        pipeline_mode=pl.Buffered(1))


def _mod_row(i):
    return jnp.maximum(i - CTX_TILES, 0) // LAT_TILES_PER_BATCH + (i >= CTX_TILES).astype(jnp.int32)


def _rms(x, g, axis=-1):
    ms = jnp.mean(x * x, axis=axis, keepdims=True)
    return x * lax.rsqrt(ms + EPS) * g


def _norm_mod(x, g, mod_ref, k):
    shift = mod_ref[0, k:k + 1, :]
    scale = mod_ref[0, k + 1:k + 2, :]
    return _rms(x, g) * (1.0 + scale) + shift


def _ds(c, n):
    if isinstance(c, int):
        return pl.ds(c * n, n)
    return pl.ds(pl.multiple_of(c * n, n), n)


def _log_sigmoid(x):
    return jnp.minimum(x, 0.0) - jnp.log1p(jnp.exp(-jnp.abs(x)))


def _ada_kernel(c_ref, w_ref, b_ref, o_ref):
    c = c_ref[...]
    a = (c * jax.nn.sigmoid(c)).astype(BF16)
    o_ref[0] = jnp.dot(a, w_ref[0].astype(BF16), preferred_element_type=F32) + b_ref[0]


def _ada(cc, w_ada, b_ada):
    tn = 1024
    n = 9 * D_MODEL
    return pl.pallas_call(
        _ada_kernel,
        grid=(DEPTH, n // tn),
        in_specs=[
            pl.BlockSpec((8, D_MODEL), lambda l, j: (0, 0)),
            pl.BlockSpec((1, D_MODEL, tn), lambda l, j: (l, 0, j)),
            pl.BlockSpec((1, 1, tn), lambda l, j: (l, 0, j)),
        ],
        out_specs=pl.BlockSpec((1, 8, tn), lambda l, j: (l, 0, j)),
        out_shape=jax.ShapeDtypeStruct((DEPTH, 8, n), F32),
        compiler_params=_cparams(("arbitrary", "arbitrary")),
        name="ada",
    )(cc, w_ada, b_ada.reshape(DEPTH, 1, n))


def _ffn_kernel(*refs, k, first, mix, final):
    is_ctx = pl.program_id(0) < CTX_TILES
    pick = lambda c_ref, l_ref: jnp.where(is_ctx, c_ref[...], l_ref[...])
    it = iter(refs)
    x = pick(next(it), next(it)) if first else next(it)[...]
    if mix:
        ya, yb, yc = (pick(next(it), next(it)) for _ in range(3))
        wmix_ref = next(it)
    mod_ref, g_ref, wi_ref, wo_ref, gf_ref = (next(it) for _ in range(5))
    if mix:
        y = jnp.dot(ya, wmix_ref[0:512, :], preferred_element_type=F32)
        y = y + jnp.dot(yb, wmix_ref[512:768, :], preferred_element_type=F32)
        y = y + jnp.dot(yc, wmix_ref[768:1024, :], preferred_element_type=F32)
        x = x + mod_ref[0, 5:6, :] * y
    xn = _norm_mod(x, g_ref[...], mod_ref, k).astype(BF16)
    acc = jnp.zeros((TM, D_MODEL), F32)
    for lo, hi in FF_CHUNKS:
        g = jnp.dot(xn, wi_ref[:, lo:hi], preferred_element_type=F32)
        u = jnp.dot(xn, wi_ref[:, D_FF + lo:D_FF + hi], preferred_element_type=F32)
        a = (g * jax.nn.sigmoid(g) * u).astype(BF16)
        acc = acc + jnp.dot(a, wo_ref[lo:hi, :], preferred_element_type=F32)
    gate = mod_ref[0, k + 2:k + 3, :]
    y = x + 0.5 * gate * acc
    if final:
        yc_ref, yl_ref = refs[-2:]
        y = _rms(y, gf_ref[...])

        @pl.when(is_ctx)
        def _():
            yc_ref[...] = y

        @pl.when(jnp.logical_not(is_ctx))
        def _():
            yl_ref[...] = y
    else:
        refs[-1][...] = y


def _ctx_lat_specs(width):
    return [
        pl.BlockSpec((TM, width), lambda i: (jnp.minimum(i, CTX_TILES - 1), 0)),
        pl.BlockSpec((TM, width), lambda i: (jnp.maximum(i - CTX_TILES, 0), 0)),
    ]


def _mod_spec(l):
    return pl.BlockSpec((None, 1, 9, D_MODEL), lambda i: (l, _mod_row(i), 0, 0))


def _ffn(h, mods, g, wi, wo, gf, *, l, half, first=False, mix=None, final=False):
    k = 6 * half
    flat_spec = pl.BlockSpec((TM, D_MODEL), lambda i: (i, 0))
    flat_shape = jax.ShapeDtypeStruct((N_TOK, D_MODEL), F32)
    mix_specs, mix_args = [], []
    if mix is not None:
        mix_specs = (_ctx_lat_specs(512) + _ctx_lat_specs(256) + _ctx_lat_specs(256)
                     + [_const_spec((D_MODEL, D_MODEL), (l,))])
        mix_args = [*mix[0], *mix[1], *mix[2], mix[3]]
    return pl.pallas_call(
        functools.partial(_ffn_kernel, k=k, first=first, mix=mix is not None, final=final),
        grid=(N_TOK // TM,),
        in_specs=(_ctx_lat_specs(D_MODEL) if first else [flat_spec]) + mix_specs + [
            _mod_spec(l),
            _const_spec((1, D_MODEL), (l, 2 * half)),
            _const_spec((D_MODEL, 2 * D_FF), (l, half)),
            _const_spec((D_FF, D_MODEL), (l, half)),
            _const_spec((1, D_MODEL)),
        ],
        out_specs=_ctx_lat_specs(D_MODEL) if final else flat_spec,
        out_shape=[jax.ShapeDtypeStruct((N_CTX, D_MODEL), F32),
                   jax.ShapeDtypeStruct((N_LAT, D_MODEL), F32)] if final else flat_shape,
        compiler_params=_cparams(("arbitrary",)),
        name="ffn_final" if final else ("ffn_first" if first else ("ffn_mix" if mix else "ffn")),
    )(*(h if first else (h,)), *mix_args, mods, g, wi, wo, gf)


def _proj_in_kernel(h_ref, mod_ref, g_ref, w_ref, wt_ref, bt_ref, z_ref, zt_ref):
    xn = _norm_mod(h_ref[...], g_ref[...], mod_ref, 3).astype(BF16)
    z_ref[...] = jnp.dot(xn, w_ref[...], preferred_element_type=F32)
    zt = lax.dot_general(wt_ref[...], xn, (((1,), (1,)), ((), ())), preferred_element_type=F32)
    zt = zt + bt_ref[...]
    for c in range(TM // TC):
        zt_ref[c] = zt[:, c * TC:(c + 1) * TC]


def _proj_in(h, mods, g, wz, wt, bt, *, l):
    return pl.pallas_call(
        _proj_in_kernel,
        grid=(N_TOK // TM,),
        in_specs=[
            pl.BlockSpec((TM, D_MODEL), lambda i: (i, 0)),
            _mod_spec(l),
            _const_spec((1, D_MODEL), (l, 1)),
            _const_spec((D_MODEL, Z_W), (l,)),
            _const_spec((ZT_W, D_MODEL), (l,)),
            _const_spec((ZT_W, 1), (l,)),
        ],
        out_specs=[
            pl.BlockSpec((TM, Z_W), lambda i: (i, 0)),
            pl.BlockSpec((TM // TC, ZT_W, TC), lambda i: (i, 0, 0)),
        ],
        out_shape=[
            jax.ShapeDtypeStruct((N_TOK, Z_W), F32),
            jax.ShapeDtypeStruct((N_TOK // TC, ZT_W, TC), F32),
        ],
        compiler_params=_cparams(("arbitrary",)),
        name="proj_in",
    )(h, mods, g, wz, wt, bt)


def _swap_halves(x, width, axis):
    n = x.shape[axis]
    half = width // 2
    parts = []
    for s in range(0, n, width):
        parts.append(lax.slice_in_dim(x, s + half, s + width, axis=axis))
        parts.append(lax.slice_in_dim(x, s, s + half, axis=axis))
    return jnp.concatenate(parts, axis=axis)


def _attn_kernel(*refs, mode, seq, has_ctx, rope, emit_k, lam_init):
    it = iter(refs)
    qt_ref, k_ref, vt_ref = next(it), next(it), next(it)
    if has_ctx:
        ck_ref, cvt_ref = next(it), next(it)
    if rope:
        cqt_ref, sqt_ref, ck_tab, sk_tab = next(it), next(it), next(it), next(it)
    if mode == "A":
        gq_ref, gk_ref = next(it), next(it)
    else:
        lam_ref, gc_ref = next(it), next(it)
    y_ref = next(it)
    if emit_k:
        kn_ref = next(it)
    kbuf, vbuf, s_ref, p_ref = next(it), next(it), next(it), next(it)

    n_kv = KV_A if mode == "A" else H_C
    per_kv = 8 // n_kv
    n_ctx = PAST_LEN if has_ctx else 0
    lk = n_ctx + seq
    width = HD_A if mode == "A" else D_C

    @pl.when(pl.program_id(1) == 0)
    def _fill():
        ones_rows = (lax.broadcasted_iota(jnp.int32, (VROWS - 64, lk), 0) == 0).astype(BF16)
        for h in range(n_kv):
            hs = slice(h * 64, (h + 1) * 64)
            vbuf[h, 64:VROWS, :] = ones_rows
            if has_ctx:
                kbuf[h, 0:n_ctx, :] = ck_ref[0, :, hs].astype(BF16)
                vbuf[h, 0:64, 0:n_ctx] = cvt_ref[0, h].astype(BF16)
            for c in range(seq // TC):
                rows = slice(c * TC, (c + 1) * TC)
                k = k_ref[rows, hs]
                if mode == "A":
                    k = _rms(k, gk_ref[...])
                    if emit_k:
                        kn_ref[0, rows, hs] = k
                if rope:
                    k = k * ck_tab[rows, :] + _swap_halves(k, width, 1) * sk_tab[rows, :]
                kbuf[h, n_ctx + c * TC:n_ctx + (c + 1) * TC, :] = k.astype(BF16)
                vbuf[h, 0:64, n_ctx + c * TC:n_ctx + (c + 1) * TC] = vt_ref[c, hs, :].astype(BF16)

    qts = []
    if mode == "A":
        scale = HD_A ** -0.5 * LOG2E
        for hh in range(H_A):
            x = _rms(qt_ref[0, hh * 64:(hh + 1) * 64, :], gq_ref[...], axis=0)
            if rope:
                x = x * cqt_ref[0] + _swap_halves(x, width, 0) * sqt_ref[0]
            qts.append((x * scale).astype(BF16))
    else:
        scale = D_C ** -0.5 * LOG2E
        first_map = lax.broadcasted_iota(jnp.int32, (64, TQ), 0) < D_C
        for h in range(H_C):
            x = qt_ref[0, h * 64:(h + 1) * 64, :]
            if rope:
                x = x * cqt_ref[0] + _swap_halves(x, width, 0) * sqt_ref[0]
            x = x * scale
            qts.append(jnp.where(first_map, x, 0.0).astype(BF16))
            qts.append(jnp.where(first_map, 0.0, x).astype(BF16))

    m_prev = None
    outs = []
    for s in range(9):
        def stage_block(j, mrun, s=s, m_prev=m_prev):
            rows = _ds(j, SB)
            if s < 8:
                st = jnp.dot(kbuf[s // per_kv, rows, :], qts[s], preferred_element_type=F32)
                s_ref[s % 2, rows, :] = st
                for r in range(SB // 8):
                    mrun = jnp.maximum(mrun, st[8 * r:8 * r + 8])
            if s >= 1:
                prev = s_ref[(s - 1) % 2, rows, :]
                p_ref[rows, (s - 1) * TQ:s * TQ] = jnp.exp2(prev - m_prev).astype(BF16)
            return mrun

        mrun = lax.fori_loop(0, lk // SB, stage_block, jnp.full((8, TQ), -jnp.inf, F32),
                             unroll=True)
        m_prev = jnp.max(mrun, axis=0, keepdims=True)
        if s >= 1:
            u = s - 1
            ox = jnp.dot(vbuf[u // per_kv], p_ref[:, u * TQ:(u + 1) * TQ], preferred_element_type=F32)
            outs.append(ox[0:64] / ox[64:65])

    if mode == "A":
        for pair in range(4):
            o = jnp.concatenate(outs[2 * pair:2 * pair + 2], axis=0)
            y_ref[:, pair * 128:(pair + 1) * 128] = o.T.astype(BF16)
    else:
        lp = lam_ref[...]
        lam = (jnp.exp(jnp.sum(lp[0:1] * lp[1:2], axis=1, keepdims=True))
               - jnp.exp(jnp.sum(lp[2:3] * lp[3:4], axis=1, keepdims=True)) + lam_init)
        for pair in range(2):
            o = []
            for h in (2 * pair, 2 * pair + 1):
                d = outs[2 * h] - lam * outs[2 * h + 1]
                o.append(_rms(d, gc_ref[...], axis=0) * (1.0 - lam_init))
            y_ref[:, pair * 128:(pair + 1) * 128] = jnp.concatenate(o, axis=0).T.astype(BF16)


def _attn(mode, z, zt, *, ctx, tables, params, l, lam_init, latent):
    if latent:
        nb, seq, row0 = DEC_BATCH, DEC_SEQ, N_CTX
    else:
        nb, seq, row0 = BATCH, SEQ, 0
    nq = seq // TQ
    n_kv = KV_A if mode == "A" else H_C
    n_ctx = PAST_LEN if ctx is not None else 0
    lk = n_ctx + seq
    qw = 512 if mode == "A" else 256
    vw = n_kv * 64
    q_blk = (ZT_QA if mode == "A" else ZT_QC) // qw
    k_blk = (Z_KA if mode == "A" else Z_KC) // vw
    v_blk = (ZT_VA if mode == "A" else ZT_VC) // vw
    emit_k = mode == "A" and not latent
    blk0 = row0 // seq

    in_specs = [
        pl.BlockSpec((1, qw, TC), lambda b, i: (row0 // TC + b * nq + i, q_blk, 0)),
        pl.BlockSpec((seq, vw), lambda b, i: (blk0 + b, k_blk)),
        pl.BlockSpec((seq // TC, vw, TC), lambda b, i: (blk0 + b, v_blk, 0)),
    ]
    args = [zt, z, zt]
    if ctx is not None:
        in_specs += [
            pl.BlockSpec((1, None, PAST_LEN, vw), lambda b, i: (b, l, 0, 0)),
            pl.BlockSpec((1, None, n_kv, 64, PAST_LEN), lambda b, i: (b, l, 0, 0, 0)),
        ]
        args += list(ctx)
    if tables is not None:
        cq, sq, cqt, sqt = tables
        in_specs += [
            pl.BlockSpec((1, 64, TC), lambda b, i: (i, 0, 0)),
            pl.BlockSpec((1, 64, TC), lambda b, i: (i, 0, 0)),
            _const_spec((seq, 64)),
            _const_spec((seq, 64)),
        ]
        args += [cqt, sqt, cq, sq]
    for p in params:
        in_specs.append(_const_spec(p.shape[1:], (l,)))
        args.append(p)

    out_specs = [pl.BlockSpec((TQ, qw), lambda b, i: (b * nq + i, 0))]
    out_shape = [jax.ShapeDtypeStruct((nb * seq, qw), BF16)]
    if emit_k:
        out_specs.append(pl.BlockSpec((1, seq, vw), lambda b, i: (b, 0, 0)))
        out_shape.append(jax.ShapeDtypeStruct((nb, seq, vw), F32))

    return pl.pallas_call(
        functools.partial(_attn_kernel, mode=mode, seq=seq, has_ctx=ctx is not None,
                          rope=tables is not None, emit_k=emit_k, lam_init=lam_init),
        grid=(nb, nq),
        in_specs=in_specs,
        out_specs=out_specs,
        out_shape=out_shape,
        scratch_shapes=[
            pltpu.VMEM((n_kv, lk, 64), BF16),
            pltpu.VMEM((n_kv, VROWS, lk), BF16),
            pltpu.VMEM((2, lk, TQ), F32),
            pltpu.VMEM((lk, 8 * TQ), BF16),
        ],
        compiler_params=_cparams(("arbitrary", "arbitrary")),
        name=f"attn_{mode}_{'lat' if latent else 'ctx'}",
    )(*args)


def _scan_lanes(x, op, fill, reverse):
    lane = lax.broadcasted_iota(jnp.int32, x.shape, 1)
    k = 1
    while k < TC:
        if reverse:
            x = op(x, jnp.where(lane < TC - k, pltpu.roll(x, TC - k, 1), fill))
        else:
            x = op(x, jnp.where(lane >= k, pltpu.roll(x, k, 1), fill))
        k *= 2
    return x


def _mlstm_kernel(k_ref, qt_ref, vt_ref, ot_ref, gt_ref, s0_ref, m0_ref, gb_ref,
                  y_ref, s_ref, m_ref, hf_ref, hr_ref, *, seq):
    nc = seq // TC
    s_idx = lax.broadcasted_iota(jnp.int32, (TC, TC), 0)
    t_idx = lax.broadcasted_iota(jnp.int32, (TC, TC), 1)
    seen = (s_idx <= t_idx, s_idx >= t_idx)
    is_fwd = lax.broadcasted_iota(jnp.int32, (8, TC), 0) < H_B
    ones_rows = (lax.broadcasted_iota(jnp.int32, (64, TC), 0) == 0).astype(F32)
    units = [(d, h) for d in range(2) for h in range(H_B)]

    def chunk(j, carry):
        states, m = carry
        cs = (j, nc - 1 - j)
        gi = jnp.concatenate([gt_ref[cs[0], 0:4, :], gt_ref[cs[1], 8:12, :]], axis=0)
        gf = jnp.concatenate([gt_ref[cs[0], 4:8, :], gt_ref[cs[1], 12:16, :]], axis=0)
        f = _log_sigmoid(gf)
        b = jnp.where(is_fwd, _scan_lanes(f, jnp.add, 0.0, False), _scan_lanes(f, jnp.add, 0.0, True))
        a = gi - b
        lm = jnp.where(is_fwd, _scan_lanes(a, jnp.maximum, -jnp.inf, False),
                       _scan_lanes(a, jnp.maximum, -jnp.inf, True))
        mx = jnp.maximum(m, lm)
        w = jnp.exp(m - mx)
        floor = jnp.exp(-(b + mx))
        mxl = jnp.maximum(m, jnp.max(a, axis=1, keepdims=True))
        wk = jnp.exp(a - mxl)
        decay = jnp.exp(m - mxl)
        m_new = jnp.sum(f, axis=1, keepdims=True) + mxl
        cols = jnp.concatenate([a, wk, jnp.zeros((128 - 16, TC), F32)], axis=0).T

        first = []
        for (d, h), st in zip(units, states):
            hs = slice(h * 64, (h + 1) * 64)
            kh = k_ref[_ds(cs[d], TC), hs] * (DK_B ** -0.5)
            qt = qt_ref[cs[d], hs, :].astype(BF16)
            lhs = jnp.concatenate([kh.astype(BF16), st.astype(BF16)], axis=0)
            first.append((kh, jnp.dot(lhs, qt, preferred_element_type=F32)))
        second = []
        for u, ((d, h), (kh, both)) in enumerate(zip(units, first)):
            dt = jnp.where(seen[d], cols[:, u:u + 1] - mx[u:u + 1, :], -jnp.inf)
            sp = both[0:TC] * jnp.exp(dt)
            den = jnp.sum(sp, axis=0, keepdims=True) + w[u:u + 1, :] * both[TC + 64:TC + 65]
            second.append((sp.astype(BF16), (kh * cols[:, 8 + u:9 + u]).astype(BF16), den))
        new_states = []
        for u, ((d, h), st, (kh, both), (spb, kw, den)) in enumerate(zip(units, states, first, second)):
            hs = slice(h * 64, (h + 1) * 64)
            vx = jnp.concatenate([vt_ref[cs[d], hs, :], ones_rows], axis=0).astype(BF16)
            num = jnp.dot(vx, spb, preferred_element_type=F32)[0:64] + w[u:u + 1, :] * both[TC:TC + 64]
            hv = num / jnp.maximum(jnp.abs(den), floor[u:u + 1, :])
            if d == 0:
                hf_ref[cs[d], hs, :] = hv
            else:
                hr_ref[cs[d], hs, :] = hv
            new_states.append(decay[u:u + 1, :] * st + jnp.dot(vx, kw, preferred_element_type=F32))
        return tuple(new_states), m_new

    init = (tuple(s0_ref[0, u] for u in range(2 * H_B)), m0_ref[0][:, 0:1])
    if nc == 1:
        states, m = chunk(0, init)
    else:
        states, m = lax.fori_loop(0, nc, chunk, init)
    for u, st in enumerate(states):
        s_ref[0, u] = st
    m_ref[0] = jnp.broadcast_to(m, (2 * H_B, 128))

    def epilogue(c, carry):
        ys = []
        for h in range(H_B):
            hs = slice(h * 64, (h + 1) * 64)
            x = hf_ref[c, hs, :] + hr_ref[c, hs, :]
            ys.append(_rms(x, gb_ref[...], axis=0) * jax.nn.sigmoid(ot_ref[c, hs, :]))
        y_ref[_ds(c, TC), :] = jnp.concatenate(ys, axis=0).T.astype(BF16)
        return carry

    if nc == 1:
        epilogue(0, 0)
    else:
        lax.fori_loop(0, nc, epilogue, 0)


def _mlstm(z, zt, s0, m0, gb, *, l, latent):
    ls = min(l, s0.shape[1] - 1)
    if latent:
        nb, seq, row0 = DEC_BATCH, DEC_SEQ, N_CTX
    else:
        nb, seq, row0 = BATCH, SEQ, 0
    blk0 = row0 // seq
    nc = seq // TC
    ztspec = lambda rb: pl.BlockSpec((nc, 256, TC), lambda b: (blk0 + b, rb, 0))
    return pl.pallas_call(
        functools.partial(_mlstm_kernel, seq=seq),
        grid=(nb,),
        in_specs=[
            pl.BlockSpec((seq, 256), lambda b: (blk0 + b, Z_KB // 256)),
            ztspec(ZT_QB // 256), ztspec(ZT_VB // 256), ztspec(ZT_OB // 256),
            pl.BlockSpec((nc, 16, TC), lambda b: (blk0 + b, ZT_G // 16, 0)),
            pl.BlockSpec((1, None, 2 * H_B, 128, 64), lambda b: (b, ls, 0, 0, 0)),
            pl.BlockSpec((1, None, 2 * H_B, 128), lambda b: (b, ls, 0, 0)),
            _const_spec((64, 1), (l,)),
        ],
        out_specs=[
            pl.BlockSpec((seq, 256), lambda b: (b, 0)),
            pl.BlockSpec((1, 2 * H_B, 128, 64), lambda b: (b, 0, 0, 0)),
            pl.BlockSpec((1, 2 * H_B, 128), lambda b: (b, 0, 0)),
        ],
        out_shape=[
            jax.ShapeDtypeStruct((nb * seq, 256), BF16),
            jax.ShapeDtypeStruct((nb, 2 * H_B, 128, 64), F32),
            jax.ShapeDtypeStruct((nb, 2 * H_B, 128), F32),
        ],
        scratch_shapes=[pltpu.VMEM((nc, 256, TC), F32), pltpu.VMEM((nc, 256, TC), F32)],
        compiler_params=_cparams(("arbitrary",)),
        name=f"mlstm_{'lat' if latent else 'ctx'}",
    )(z, zt, zt, zt, zt, s0, m0, gb)


def _rope_tables(dim):
    t = jnp.arange(DEC_SEQ)
    row = (t // GRID_W).astype(F32)
    colp = (t % GRID_W).astype(F32)
    axis_dim = dim // 2
    freqs = ROPE_THETA ** (-jnp.arange(0, axis_dim, 2, dtype=F32) / axis_dim)
    ang = jnp.concatenate([row[:, None] * freqs, colp[:, None] * freqs], axis=-1)
    cos, sin = jnp.cos(ang), jnp.sin(ang)
    reps = 64 // dim
    cq = jnp.concatenate([cos, cos] * reps, axis=1)
    sq = jnp.concatenate([-sin, sin] * reps, axis=1)
    to_chunks = lambda x: x.T.reshape(64, DEC_SEQ // TC, TC).transpose(1, 0, 2)
    return cq, sq, to_chunks(cq), to_chunks(sq)


def _split_w_in(w, b_gates):
    a0, b0, c0 = 0, 768, 1808
    qa, ka, va = w[..., a0:a0 + 512], w[..., a0 + 512:a0 + 640], w[..., a0 + 640:a0 + 768]
    qb, kb = w[..., b0:b0 + 256], w[..., b0 + 256:b0 + 512]
    vb, ob = w[..., b0 + 512:b0 + 768], w[..., b0 + 768:b0 + 1024]
    gb = w[..., b0 + 1024:b0 + 1040]
    qc, kc, vc = w[..., c0:c0 + 256], w[..., c0 + 256:c0 + 512], w[..., c0 + 512:c0 + 768]
    wz = jnp.concatenate([kb, kc, ka], axis=-1)
    wt = jnp.swapaxes(jnp.concatenate([qa, qc, vc, qb, vb, ob, va, gb], axis=-1), -1, -2)
    bt = jnp.zeros((DEPTH, ZT_W, 1), F32).at[:, ZT_G:ZT_G + 16, 0].set(b_gates)
    return wz.astype(BF16), wt.astype(BF16), bt


def _tokens_major(x):
    return x.transpose(0, 2, 1)


def kernel(x_prompt, x_sample, cache_a_k, cache_a_v, cache_c_k, cache_c_v, state_b_C, state_b_n,
           state_b_m, c, c_ctx, w_ada, b_ada, g_norm, w_ff_in, w_ff_out, w_in, w_out, g_qa, g_ka,
           b_gates, g_b, lam_q1, lam_k1, lam_q2, lam_k2, g_c, g_final):
    h = (x_prompt.reshape(N_CTX, D_MODEL), x_sample.reshape(N_LAT, D_MODEL))
    cc = jnp.concatenate([c_ctx[None, :], c, jnp.zeros((3, D_MODEL), F32)], axis=0)
    mods = _ada(cc, w_ada, b_ada).reshape(DEPTH, 8, 9, D_MODEL)
    tab_a = _rope_tables(HD_A)
    tab_c = _rope_tables(D_C)

    gf = g_final.reshape(1, D_MODEL)
    gn = g_norm.reshape(DEPTH, 3, 1, D_MODEL)
    wi, wo, wmix = w_ff_in.astype(BF16), w_ff_out.astype(BF16), w_out.astype(BF16)
    wz, wt, bt = _split_w_in(w_in, b_gates)
    par_a = (g_qa.reshape(DEPTH, 64, 1), g_ka.reshape(DEPTH, 1, 64))
    par_c = (jnp.stack([lam_q1, lam_k1, lam_q2, lam_k2], axis=1), g_c.reshape(DEPTH, 64, 1))
    gb = g_b.reshape(DEPTH, 64, 1)
    ctx_a = (cache_a_k.reshape(DEC_BATCH, DEPTH, PAST_LEN, KV_A * HD_A), cache_a_v.transpose(0, 1, 3, 4, 2))
    ctx_c = (cache_c_k.reshape(DEC_BATCH, DEPTH, PAST_LEN, H_C * 2 * D_C), cache_c_v.transpose(0, 1, 3, 4, 2))
    s0 = jnp.concatenate([jnp.swapaxes(state_b_C, -1, -2), state_b_n[..., None, :],
                          jnp.zeros((DEC_BATCH, DEPTH, 2, H_B, 63, DK_B), F32)], axis=-2)
    s0 = s0.reshape(DEC_BATCH, DEPTH, 2 * H_B, 128, DK_B)
    m0 = jnp.broadcast_to(state_b_m.reshape(DEC_BATCH, DEPTH, 2 * H_B, 1), (DEC_BATCH, DEPTH, 2 * H_B, 128))
    zeros_s = jnp.zeros((BATCH, 1, 2 * H_B, 128, DK_B), F32)
    zeros_m = jnp.zeros((BATCH, 1, 2 * H_B, 128), F32)

    outs = {k: [] for k in ("ak", "av", "ck", "cv", "bC", "bn", "bm")}
    for l in range(DEPTH):
        lam_init = 0.8 - 0.6 * math.exp(-0.3 * l)
        h = _ffn(h, mods, gn, wi, wo, gf, l=l, half=0, first=(l == 0))
        z, zt = _proj_in(h, mods, gn, wz, wt, bt, l=l)

        ya_c, kn = _attn("A", z, zt, ctx=None, tables=None, params=par_a, l=l, lam_init=lam_init,
                         latent=False)
        (yc_c,) = _attn("C", z, zt, ctx=None, tables=None, params=par_c, l=l, lam_init=lam_init,
                        latent=False)
        yb_c, s_fin, m_fin = _mlstm(z, zt, zeros_s, zeros_m, gb, l=l, latent=False)

        (ya_l,) = _attn("A", z, zt, ctx=ctx_a, tables=tab_a, params=par_a, l=l, lam_init=lam_init,
                        latent=True)
        (yc_l,) = _attn("C", z, zt, ctx=ctx_c, tables=tab_c, params=par_c, l=l, lam_init=lam_init,
                        latent=True)
        yb_l, _, _ = _mlstm(z, zt, s0, m0, gb, l=l, latent=True)

        h = _ffn(h, mods, gn, wi, wo, gf, l=l, half=1,
                 mix=((ya_c, ya_l), (yb_c, yb_l), (yc_c, yc_l), wmix), final=(l == DEPTH - 1))

        zc = z[:N_CTX]
        ztc = zt[:N_CTX // TC]
        outs["ak"].append(kn.reshape(BATCH, SEQ, KV_A, HD_A))
        outs["av"].append(_tokens_major(ztc[:, ZT_VA:ZT_VA + 128, :]).reshape(BATCH, SEQ, KV_A, HD_A))
        outs["ck"].append(zc[:, Z_KC:Z_KC + 256].reshape(BATCH, SEQ, H_C, 2, D_C))
        outs["cv"].append(_tokens_major(ztc[:, ZT_VC:ZT_VC + 256, :]).reshape(BATCH, SEQ, H_C, 2 * D_C))
        s_fin = s_fin.reshape(BATCH, 2, H_B, 128, DK_B)
        outs["bC"].append(jnp.swapaxes(s_fin[..., 0:64, :], -1, -2))
        outs["bn"].append(s_fin[..., 64, :])
        outs["bm"].append(m_fin[:, :, 0].reshape(BATCH, 2, H_B))

    y_prompt = h[0].reshape(BATCH, SEQ, D_MODEL)
    y_sample = h[1].reshape(DEC_BATCH, DEC_SEQ, D_MODEL)
    st = lambda k: jnp.stack(outs[k], axis=1)
    return (y_prompt, y_sample, st("ak"), st("av"), st("ck"), st("cv"), st("bC"), st("bn"), st("bm"))
```

```python
import functools
import math

import jax
import jax.numpy as jnp
from jax import lax
from jax.experimental import pallas as pl
from jax.experimental.pallas import tpu as pltpu

F32 = jnp.float32
BF16 = jnp.bfloat16

D_MODEL = 1024
BATCH = 16
SEQ = 256
DEPTH = 2
DEC_BATCH = 4
DEC_SEQ = 2048
PAST_LEN = 512
GRID_W = 64
ROPE_THETA = 10000.0
EPS = 1e-6
HD_A = 64
H_A = 8
KV_A = 2
DK_B = 64
H_B = 4
D_C = 32
H_C = 4
D_FF = 2816
LOG2E = 1.4426950408889634

N_CTX = BATCH * SEQ
N_LAT = DEC_BATCH * DEC_SEQ
N_TOK = N_CTX + N_LAT

TM = 512
CTX_TILES = N_CTX // TM
LAT_TILES_PER_BATCH = DEC_SEQ // TM
FF_CHUNKS = ((0, 1536), (1536, D_FF))
TC = 256
SB = 256
TQ = 256
VROWS = 80

Z_KB, Z_KC, Z_KA = 0, 256, 512
Z_W = 640
ZT_QA, ZT_QC, ZT_VC, ZT_QB, ZT_VB, ZT_OB, ZT_VA, ZT_G = 0, 512, 768, 1024, 1280, 1536, 1792, 1920
ZT_W = 1936

VMEM_LIMIT = 56 * 1024 * 1024


def _cparams(sem):
    return pltpu.CompilerParams(dimension_semantics=sem, vmem_limit_bytes=VMEM_LIMIT)


def _const_spec(shape, lead=()):
    nd = len(shape)
    return pl.BlockSpec((None,) * len(lead) + tuple(shape), lambda *_: tuple(lead) + (0,) * nd,
                        pipeline_mode=pl.Buffered(1))


def _mod_row(i):
    return jnp.maximum(i - CTX_TILES, 0) // LAT_TILES_PER_BATCH + (i >= CTX_TILES).astype(jnp.int32)


def _rms(x, g, axis=-1):
    ms = jnp.mean(x * x, axis=axis, keepdims=True)
    return x * lax.rsqrt(ms + EPS) * g


def _norm_mod(x, g, mod_ref, k):
    shift = mod_ref[0, k:k + 1, :]
    scale = mod_ref[0, k + 1:k + 2, :]
    return _rms(x, g) * (1.0 + scale) + shift


def _ds(c, n):
    if isinstance(c, int):
        return pl.ds(c * n, n)
    return pl.ds(pl.multiple_of(c * n, n), n)


def _log_sigmoid(x):
    return jnp.minimum(x, 0.0) - jnp.log1p(jnp.exp(-jnp.abs(x)))


def _ada_kernel(c_ref, w_ref, b_ref, o_ref):
    c = c_ref[...]
    a = (c * jax.nn.sigmoid(c)).astype(BF16)
    o_ref[0] = jnp.dot(a, w_ref[0].astype(BF16), preferred_element_type=F32) + b_ref[0]


def _ada(cc, w_ada, b_ada):
    tn = 1024
    n = 9 * D_MODEL
    return pl.pallas_call(
        _ada_kernel,
        grid=(DEPTH, n // tn),
        in_specs=[
            pl.BlockSpec((8, D_MODEL), lambda l, j: (0, 0)),
            pl.BlockSpec((1, D_MODEL, tn), lambda l, j: (l, 0, j)),
            pl.BlockSpec((1, 1, tn), lambda l, j: (l, 0, j)),
        ],
        out_specs=pl.BlockSpec((1, 8, tn), lambda l, j: (l, 0, j)),
        out_shape=jax.ShapeDtypeStruct((DEPTH, 8, n), F32),
        compiler_params=_cparams(("arbitrary", "arbitrary")),
        name="ada",
    )(cc, w_ada, b_ada.reshape(DEPTH, 1, n))


def _ffn_kernel(*refs, k, first, mix, final):
    is_ctx = pl.program_id(0) < CTX_TILES
    pick = lambda c_ref, l_ref: jnp.where(is_ctx, c_ref[...], l_ref[...])
    it = iter(refs)
    x = pick(next(it), next(it)) if first else next(it)[...]
    if mix:
        ya, yb, yc = (pick(next(it), next(it)) for _ in range(3))
        wmix_ref = next(it)
    mod_ref, g_ref, wi_ref, wo_ref, gf_ref = (next(it) for _ in range(5))
    if mix:
        y = jnp.dot(ya, wmix_ref[0:512, :], preferred_element_type=F32)
        y = y + jnp.dot(yb, wmix_ref[512:768, :], preferred_element_type=F32)
        y = y + jnp.dot(yc, wmix_ref[768:1024, :], preferred_element_type=F32)
        x = x + mod_ref[0, 5:6, :] * y
    xn = _norm_mod(x, g_ref[...], mod_ref, k).astype(BF16)
    acc = jnp.zeros((TM, D_MODEL), F32)
    for lo, hi in FF_CHUNKS:
        g = jnp.dot(xn, wi_ref[:, lo:hi], preferred_element_type=F32)
        u = jnp.dot(xn, wi_ref[:, D_FF + lo:D_FF + hi], preferred_element_type=F32)
        a = (g * jax.nn.sigmoid(g) * u).astype(BF16)
        acc = acc + jnp.dot(a, wo_ref[lo:hi, :], preferred_element_type=F32)
    gate = mod_ref[0, k + 2:k + 3, :]
    y = x + 0.5 * gate * acc
    if final:
        yc_ref, yl_ref = refs[-2:]
        y = _rms(y, gf_ref[...])

        @pl.when(is_ctx)
        def _():
            yc_ref[...] = y

        @pl.when(jnp.logical_not(is_ctx))
        def _():
            yl_ref[...] = y
    else:
        refs[-1][...] = y


def _ctx_lat_specs(width):
    return [
        pl.BlockSpec((TM, width), lambda i: (jnp.minimum(i, CTX_TILES - 1), 0)),
        pl.BlockSpec((TM, width), lambda i: (jnp.maximum(i - CTX_TILES, 0), 0)),
    ]


def _mod_spec(l):
    return pl.BlockSpec((None, 1, 9, D_MODEL), lambda i: (l, _mod_row(i), 0, 0))


def _ffn(h, mods, g, wi, wo, gf, *, l, half, first=False, mix=None, final=False):
    k = 6 * half
    flat_spec = pl.BlockSpec((TM, D_MODEL), lambda i: (i, 0))
    flat_shape = jax.ShapeDtypeStruct((N_TOK, D_MODEL), F32)
    mix_specs, mix_args = [], []
    if mix is not None:
        mix_specs = (_ctx_lat_specs(512) + _ctx_lat_specs(256) + _ctx_lat_specs(256)
                     + [_const_spec((D_MODEL, D_MODEL), (l,))])
        mix_args = [*mix[0], *mix[1], *mix[2], mix[3]]
    return pl.pallas_call(
        functools.partial(_ffn_kernel, k=k, first=first, mix=mix is not None, final=final),
        grid=(N_TOK // TM,),
        in_specs=(_ctx_lat_specs(D_MODEL) if first else [flat_spec]) + mix_specs + [
            _mod_spec(l),
            _const_spec((1, D_MODEL), (l, 2 * half)),
            _const_spec((D_MODEL, 2 * D_FF), (l, half)),
            _const_spec((D_FF, D_MODEL), (l, half)),
            _const_spec((1, D_MODEL)),
        ],
        out_specs=_ctx_lat_specs(D_MODEL) if final else flat_spec,
        out_shape=[jax.ShapeDtypeStruct((N_CTX, D_MODEL), F32),
                   jax.ShapeDtypeStruct((N_LAT, D_MODEL), F32)] if final else flat_shape,
        compiler_params=_cparams(("arbitrary",)),
        name="ffn_final" if final else ("ffn_first" if first else ("ffn_mix" if mix else "ffn")),
    )(*(h if first else (h,)), *mix_args, mods, g, wi, wo, gf)


def _proj_in_kernel(h_ref, mod_ref, g_ref, w_ref, wt_ref, bt_ref, z_ref, zt_ref):
    xn = _norm_mod(h_ref[...], g_ref[...], mod_ref, 3).astype(BF16)
    z_ref[...] = jnp.dot(xn, w_ref[...], preferred_element_type=F32)
    zt = lax.dot_general(wt_ref[...], xn, (((1,), (1,)), ((), ())), preferred_element_type=F32)
    zt = zt + bt_ref[...]
    for c in range(TM // TC):
        zt_ref[c] = zt[:, c * TC:(c + 1) * TC]


def _proj_in(h, mods, g, wz, wt, bt, *, l):
    return pl.pallas_call(
        _proj_in_kernel,
        grid=(N_TOK // TM,),
        in_specs=[
            pl.BlockSpec((TM, D_MODEL), lambda i: (i, 0)),
            _mod_spec(l),
            _const_spec((1, D_MODEL), (l, 1)),
            _const_spec((D_MODEL, Z_W), (l,)),
            _const_spec((ZT_W, D_MODEL), (l,)),
            _const_spec((ZT_W, 1), (l,)),
        ],
        out_specs=[
            pl.BlockSpec((TM, Z_W), lambda i: (i, 0)),
            pl.BlockSpec((TM // TC, ZT_W, TC), lambda i: (i, 0, 0)),
        ],
        out_shape=[
            jax.ShapeDtypeStruct((N_TOK, Z_W), F32),
            jax.ShapeDtypeStruct((N_TOK // TC, ZT_W, TC), F32),
        ],
        compiler_params=_cparams(("arbitrary",)),
        name="proj_in",
    )(h, mods, g, wz, wt, bt)


def _swap_halves(x, width, axis):
    n = x.shape[axis]
    half = width // 2
    parts = []
    for s in range(0, n, width):
        parts.append(lax.slice_in_dim(x, s + half, s + width, axis=axis))
        parts.append(lax.slice_in_dim(x, s, s + half, axis=axis))
    return jnp.concatenate(parts, axis=axis)


def _attn_kernel(*refs, mode, seq, has_ctx, rope, emit_k, lam_init):
    it = iter(refs)
    qt_ref, k_ref, vt_ref = next(it), next(it), next(it)
    if has_ctx:
        ck_ref, cvt_ref = next(it), next(it)
    if rope:
        cqt_ref, sqt_ref, ck_tab, sk_tab = next(it), next(it), next(it), next(it)
    if mode == "A":
        gq_ref, gk_ref = next(it), next(it)
    else:
        lam_ref, gc_ref = next(it), next(it)
    y_ref = next(it)
    if emit_k:
        kn_ref = next(it)
    kbuf, vbuf, s_ref, p_ref = next(it), next(it), next(it), next(it)

    n_kv = KV_A if mode == "A" else H_C
    per_kv = 8 // n_kv
    n_ctx = PAST_LEN if has_ctx else 0
    lk = n_ctx + seq
    width = HD_A if mode == "A" else D_C

    @pl.when(pl.program_id(1) == 0)
    def _fill():
        ones_rows = (lax.broadcasted_iota(jnp.int32, (VROWS - 64, lk), 0) == 0).astype(BF16)
        for h in range(n_kv):
            hs = slice(h * 64, (h + 1) * 64)
            vbuf[h, 64:VROWS, :] = ones_rows
            if has_ctx:
                kbuf[h, 0:n_ctx, :] = ck_ref[0, :, hs].astype(BF16)
                vbuf[h, 0:64, 0:n_ctx] = cvt_ref[0, h].astype(BF16)
            for c in range(seq // TC):
                rows = slice(c * TC, (c + 1) * TC)
                k = k_ref[rows, hs]
                if mode == "A":
                    k = _rms(k, gk_ref[...])
                    if emit_k:
                        kn_ref[0, rows, hs] = k
                if rope:
                    k = k * ck_tab[rows, :] + _swap_halves(k, width, 1) * sk_tab[rows, :]
                kbuf[h, n_ctx + c * TC:n_ctx + (c + 1) * TC, :] = k.astype(BF16)
                vbuf[h, 0:64, n_ctx + c * TC:n_ctx + (c + 1) * TC] = vt_ref[c, hs, :].astype(BF16)

    qts = []
    if mode == "A":
        scale = HD_A ** -0.5 * LOG2E
        for hh in range(H_A):
            x = _rms(qt_ref[0, hh * 64:(hh + 1) * 64, :], gq_ref[...], axis=0)
            if rope:
                x = x * cqt_ref[0] + _swap_halves(x, width, 0) * sqt_ref[0]
            qts.append((x * scale).astype(BF16))
    else:
        scale = D_C ** -0.5 * LOG2E
        first_map = lax.broadcasted_iota(jnp.int32, (64, TQ), 0) < D_C
        for h in range(H_C):
            x = qt_ref[0, h * 64:(h + 1) * 64, :]
            if rope:
                x = x * cqt_ref[0] + _swap_halves(x, width, 0) * sqt_ref[0]
            x = x * scale
            qts.append(jnp.where(first_map, x, 0.0).astype(BF16))
            qts.append(jnp.where(first_map, 0.0, x).astype(BF16))

    m_prev = None
    outs = []
    for s in range(9):
        def stage_block(j, mrun, s=s, m_prev=m_prev):
            rows = _ds(j, SB)
            if s < 8:
                st = jnp.dot(kbuf[s // per_kv, rows, :], qts[s], preferred_element_type=F32)
                s_ref[s % 2, rows, :] = st
                for r in range(SB // 8):
                    mrun = jnp.maximum(mrun, st[8 * r:8 * r + 8])
            if s >= 1:
                prev = s_ref[(s - 1) % 2, rows, :]
                p_ref[s - 1, rows, :] = jnp.exp2(prev - m_prev).astype(BF16)
            return mrun

        mrun = lax.fori_loop(0, lk // SB, stage_block, jnp.full((8, TQ), -jnp.inf, F32),
                             unroll=True)
        m_prev = jnp.max(mrun, axis=0, keepdims=True)
        if s >= 1:
            u = s - 1
            ox = jnp.dot(vbuf[u // per_kv], p_ref[u], preferred_element_type=F32)
            outs.append(ox[0:64] / ox[64:65])

    if mode == "A":
        for pair in range(4):
            o = jnp.concatenate(outs[2 * pair:2 * pair + 2], axis=0)
            y_ref[:, pair * 128:(pair + 1) * 128] = o.T.astype(BF16)
    else:
        lp = lam_ref[...]
        lam = (jnp.exp(jnp.sum(lp[0:1] * lp[1:2], axis=1, keepdims=True))
               - jnp.exp(jnp.sum(lp[2:3] * lp[3:4], axis=1, keepdims=True)) + lam_init)
        for pair in range(2):
            o = []
            for h in (2 * pair, 2 * pair + 1):
                d = outs[2 * h] - lam * outs[2 * h + 1]
                o.append(_rms(d, gc_ref[...], axis=0) * (1.0 - lam_init))
            y_ref[:, pair * 128:(pair + 1) * 128] = jnp.concatenate(o, axis=0).T.astype(BF16)


def _attn(mode, z, zt, *, ctx, tables, params, l, lam_init, latent):
    if latent:
        nb, seq, row0 = DEC_BATCH, DEC_SEQ, N_CTX
    else:
        nb, seq, row0 = BATCH, SEQ, 0
    nq = seq // TQ
    n_kv = KV_A if mode == "A" else H_C
    n_ctx = PAST_LEN if ctx is not None else 0
    lk = n_ctx + seq
    qw = 512 if mode == "A" else 256
    vw = n_kv * 64
    q_blk = (ZT_QA if mode == "A" else ZT_QC) // qw
    k_blk = (Z_KA if mode == "A" else Z_KC) // vw
    v_blk = (ZT_VA if mode == "A" else ZT_VC) // vw
    emit_k = mode == "A" and not latent
    blk0 = row0 // seq

    in_specs = [
        pl.BlockSpec((1, qw, TC), lambda b, i: (row0 // TC + b * nq + i, q_blk, 0)),
        pl.BlockSpec((seq, vw), lambda b, i: (blk0 + b, k_blk)),
        pl.BlockSpec((seq // TC, vw, TC), lambda b, i: (blk0 + b, v_blk, 0)),
    ]
    args = [zt, z, zt]
    if ctx is not None:
        in_specs += [
            pl.BlockSpec((1, None, PAST_LEN, vw), lambda b, i: (b, l, 0, 0)),
            pl.BlockSpec((1, None, n_kv, 64, PAST_LEN), lambda b, i: (b, l, 0, 0, 0)),
        ]
        args += list(ctx)
    if tables is not None:
        cq, sq, cqt, sqt = tables
        in_specs += [
            pl.BlockSpec((1, 64, TC), lambda b, i: (i, 0, 0)),
            pl.BlockSpec((1, 64, TC), lambda b, i: (i, 0, 0)),
            _const_spec((seq, 64)),
            _const_spec((seq, 64)),
        ]
        args += [cqt, sqt, cq, sq]
    for p in params:
        in_specs.append(_const_spec(p.shape[1:], (l,)))
        args.append(p)

    out_specs = [pl.BlockSpec((TQ, qw), lambda b, i: (b * nq + i, 0))]
    out_shape = [jax.ShapeDtypeStruct((nb * seq, qw), BF16)]
    if emit_k:
        out_specs.append(pl.BlockSpec((1, seq, vw), lambda b, i: (b, 0, 0)))
        out_shape.append(jax.ShapeDtypeStruct((nb, seq, vw), F32))

    return pl.pallas_call(
        functools.partial(_attn_kernel, mode=mode, seq=seq, has_ctx=ctx is not None,
                          rope=tables is not None, emit_k=emit_k, lam_init=lam_init),
        grid=(nb, nq),
        in_specs=in_specs,
        out_specs=out_specs,
        out_shape=out_shape,
        scratch_shapes=[
            pltpu.VMEM((n_kv, lk, 64), BF16),
            pltpu.VMEM((n_kv, VROWS, lk), BF16),
            pltpu.VMEM((2, lk, TQ), F32),
            pltpu.VMEM((8, lk, TQ), BF16),
        ],
        compiler_params=_cparams(("arbitrary", "arbitrary")),
        name=f"attn_{mode}_{'lat' if latent else 'ctx'}",
    )(*args)


N_CHUNKS = N_TOK // TC
GS_A, GS_B, GS_LM, GS_BLAST, GS_ALAST, GS_ROWS = 0, 8, 16, 24, 32, 40


def _scan_lanes(x, op, fill, reverse):
    axis = x.ndim - 1
    lane = lax.broadcasted_iota(jnp.int32, x.shape, axis)
    k = 1
    while k < TC:
        if reverse:
            x = op(x, jnp.where(lane < TC - k, pltpu.roll(x, TC - k, axis), fill))
        else:
            x = op(x, jnp.where(lane >= k, pltpu.roll(x, k, axis), fill))
        k *= 2
    return x


def _gate_stats_kernel(gi_ref, gf_ref, st_ref, ac_ref):
    is_fwd = lax.broadcasted_iota(jnp.int32, (N_CHUNKS, 8, TC), 1) < H_B
    f = _log_sigmoid(gf_ref[...])
    b = jnp.where(is_fwd, _scan_lanes(f, jnp.add, 0.0, False), _scan_lanes(f, jnp.add, 0.0, True))
    a = gi_ref[...] - b
    lm = jnp.where(is_fwd, _scan_lanes(a, jnp.maximum, -jnp.inf, False),
                   _scan_lanes(a, jnp.maximum, -jnp.inf, True))
    st_ref[:, GS_A:GS_A + 8, :] = a
    st_ref[:, GS_B:GS_B + 8, :] = b
    st_ref[:, GS_LM:GS_LM + 8, :] = lm
    st_ref[:, GS_BLAST:GS_BLAST + 8, :] = jnp.broadcast_to(jnp.sum(f, axis=2, keepdims=True), a.shape)
    st_ref[:, GS_ALAST:GS_ALAST + 8, :] = jnp.broadcast_to(jnp.max(a, axis=2, keepdims=True), a.shape)

    def transpose_chunk(c, carry):
        ac_ref[c] = jnp.concatenate([st_ref[c, GS_A:GS_A + 8, :], jnp.zeros((120, TC), F32)], axis=0).T
        return carry

    lax.fori_loop(0, N_CHUNKS, transpose_chunk, 0, unroll=4)


def _gate_stats(zt):
    gspec = lambda r: pl.BlockSpec((N_CHUNKS, 8, TC), lambda i: (0, (ZT_G + r) // 8, 0))
    return pl.pallas_call(
        _gate_stats_kernel,
        grid=(1,),
        in_specs=[gspec(0), gspec(8)],
        out_specs=[
            pl.BlockSpec((N_CHUNKS, GS_ROWS, TC), lambda i: (0, 0, 0)),
            pl.BlockSpec((N_CHUNKS, TC, 128), lambda i: (0, 0, 0)),
        ],
        out_shape=[
            jax.ShapeDtypeStruct((N_CHUNKS, GS_ROWS, TC), F32),
            jax.ShapeDtypeStruct((N_CHUNKS, TC, 128), F32),
        ],
        compiler_params=_cparams(("arbitrary",)),
        name="gate_stats",
    )(zt, zt)


def _mlstm_kernel(k_ref, qt_ref, vt_ref, ot_ref, st_ref, ac_ref, s0_ref, m0_ref, gb_ref,
                  y_ref, s_ref, m_ref, hf_ref, hr_ref, *, seq):
    nc = seq // TC
    s_idx = lax.broadcasted_iota(jnp.int32, (TC, TC), 0)
    t_idx = lax.broadcasted_iota(jnp.int32, (TC, TC), 1)
    seen = (s_idx <= t_idx, s_idx >= t_idx)
    ones_rows = (lax.broadcasted_iota(jnp.int32, (64, TC), 0) == 0).astype(F32)
    units = [(d, h) for d in range(2) for h in range(H_B)]

    def chunk(j, carry):
        states, m = carry
        cs = (j, nc - 1 - j)
        stat = lambda r: jnp.concatenate([st_ref[cs[0], r:r + 4, :], st_ref[cs[1], r + 4:r + 8, :]], axis=0)
        a, b, lm = stat(GS_A), stat(GS_B), stat(GS_LM)
        mx = jnp.maximum(m, lm)
        w = jnp.exp(m - mx)
        floor = jnp.exp(-(b + mx))
        mxl = jnp.maximum(m, stat(GS_ALAST)[:, 0:1])
        decay = jnp.exp(m - mxl)
        m_new = stat(GS_BLAST)[:, 0:1] + mxl

        first = []
        for (d, h), st in zip(units, states):
            hs = slice(h * 64, (h + 1) * 64)
            kh = k_ref[_ds(cs[d], TC), hs] * (DK_B ** -0.5)
            qt = qt_ref[cs[d], hs, :].astype(BF16)
            lhs = jnp.concatenate([kh.astype(BF16), st.astype(BF16)], axis=0)
            first.append((kh, jnp.dot(lhs, qt, preferred_element_type=F32)))
        second = []
        for u, ((d, h), (kh, both)) in enumerate(zip(units, first)):
            a_col = ac_ref[cs[d], :, u:u + 1]
            dt = jnp.where(seen[d], a_col - mx[u:u + 1, :], -jnp.inf)
            sp = both[0:TC] * jnp.exp(dt)
            den = jnp.sum(sp, axis=0, keepdims=True) + w[u:u + 1, :] * both[TC + 64:TC + 65]
            wk = jnp.exp(a_col - mxl[u:u + 1, :])
            second.append((sp.astype(BF16), (kh * wk).astype(BF16), den))
        new_states = []
        for u, ((d, h), st, (kh, both), (spb, kw, den)) in enumerate(zip(units, states, first, second)):
            hs = slice(h * 64, (h + 1) * 64)
            vx = jnp.concatenate([vt_ref[cs[d], hs, :], ones_rows], axis=0).astype(BF16)
            num = jnp.dot(vx, spb, preferred_element_type=F32)[0:64] + w[u:u + 1, :] * both[TC:TC + 64]
            hv = num / jnp.maximum(jnp.abs(den), floor[u:u + 1, :])
            if d == 0:
                hf_ref[cs[d], hs, :] = hv
            else:
                hr_ref[cs[d], hs, :] = hv
            new_states.append(decay[u:u + 1, :] * st + jnp.dot(vx, kw, preferred_element_type=F32))
        return tuple(new_states), m_new

    init = (tuple(s0_ref[0, u] for u in range(2 * H_B)), m0_ref[0][:, 0:1])
    if nc == 1:
        states, m = chunk(0, init)
    else:
        states, m = lax.fori_loop(0, nc, chunk, init)
    for u, st in enumerate(states):
        s_ref[0, u] = st
    m_ref[0] = jnp.broadcast_to(m, (2 * H_B, 128))

    def epilogue(c, carry):
        ys = []
        for h in range(H_B):
            hs = slice(h * 64, (h + 1) * 64)
            x = hf_ref[c, hs, :] + hr_ref[c, hs, :]
            ys.append(_rms(x, gb_ref[...], axis=0) * jax.nn.sigmoid(ot_ref[c, hs, :]))
        y_ref[_ds(c, TC), :] = jnp.concatenate(ys, axis=0).T.astype(BF16)
        return carry

    if nc == 1:
        epilogue(0, 0)
    else:
        lax.fori_loop(0, nc, epilogue, 0)


def _mlstm(z, zt, stats, acols, s0, m0, gb, *, l, latent):
    ls = min(l, s0.shape[1] - 1)
    if latent:
        nb, seq, row0 = DEC_BATCH, DEC_SEQ, N_CTX
    else:
        nb, seq, row0 = BATCH, SEQ, 0
    blk0 = row0 // seq
    nc = seq // TC
    ztspec = lambda rb: pl.BlockSpec((nc, 256, TC), lambda b: (blk0 + b, rb, 0))
    return pl.pallas_call(
        functools.partial(_mlstm_kernel, seq=seq),
        grid=(nb,),
        in_specs=[
            pl.BlockSpec((seq, 256), lambda b: (blk0 + b, Z_KB // 256)),
            ztspec(ZT_QB // 256), ztspec(ZT_VB // 256), ztspec(ZT_OB // 256),
            pl.BlockSpec((nc, GS_ROWS, TC), lambda b: (blk0 + b, 0, 0)),
            pl.BlockSpec((nc, TC, 128), lambda b: (blk0 + b, 0, 0)),
            pl.BlockSpec((1, None, 2 * H_B, 128, 64), lambda b: (b, ls, 0, 0, 0)),
            pl.BlockSpec((1, None, 2 * H_B, 128), lambda b: (b, ls, 0, 0)),
            _const_spec((64, 1), (l,)),
        ],
        out_specs=[
            pl.BlockSpec((seq, 256), lambda b: (b, 0)),
            pl.BlockSpec((1, 2 * H_B, 128, 64), lambda b: (b, 0, 0, 0)),
            pl.BlockSpec((1, 2 * H_B, 128), lambda b: (b, 0, 0)),
        ],
        out_shape=[
            jax.ShapeDtypeStruct((nb * seq, 256), BF16),
            jax.ShapeDtypeStruct((nb, 2 * H_B, 128, 64), F32),
            jax.ShapeDtypeStruct((nb, 2 * H_B, 128), F32),
        ],
        scratch_shapes=[pltpu.VMEM((nc, 256, TC), F32), pltpu.VMEM((nc, 256, TC), F32)],
        compiler_params=_cparams(("arbitrary",)),
        name=f"mlstm_{'lat' if latent else 'ctx'}",
    )(z, zt, zt, zt, stats, acols, s0, m0, gb)


def _rope_tables(dim):
    t = jnp.arange(DEC_SEQ)
    row = (t // GRID_W).astype(F32)
    colp = (t % GRID_W).astype(F32)
    axis_dim = dim // 2
    freqs = ROPE_THETA ** (-jnp.arange(0, axis_dim, 2, dtype=F32) / axis_dim)
    ang = jnp.concatenate([row[:, None] * freqs, colp[:, None] * freqs], axis=-1)
    cos, sin = jnp.cos(ang), jnp.sin(ang)
    reps = 64 // dim
    cq = jnp.concatenate([cos, cos] * reps, axis=1)
    sq = jnp.concatenate([-sin, sin] * reps, axis=1)
    to_chunks = lambda x: x.T.reshape(64, DEC_SEQ // TC, TC).transpose(1, 0, 2)
    return cq, sq, to_chunks(cq), to_chunks(sq)


def _split_w_in(w, b_gates):
    a0, b0, c0 = 0, 768, 1808
    qa, ka, va = w[..., a0:a0 + 512], w[..., a0 + 512:a0 + 640], w[..., a0 + 640:a0 + 768]
    qb, kb = w[..., b0:b0 + 256], w[..., b0 + 256:b0 + 512]
    vb, ob = w[..., b0 + 512:b0 + 768], w[..., b0 + 768:b0 + 1024]
    regroup = lambda g: jnp.concatenate([g[..., 0:4], g[..., 8:12], g[..., 4:8], g[..., 12:16]], axis=-1)
    gb = regroup(w[..., b0 + 1024:b0 + 1040])
    qc, kc, vc = w[..., c0:c0 + 256], w[..., c0 + 256:c0 + 512], w[..., c0 + 512:c0 + 768]
    wz = jnp.concatenate([kb, kc, ka], axis=-1)
    wt = jnp.swapaxes(jnp.concatenate([qa, qc, vc, qb, vb, ob, va, gb], axis=-1), -1, -2)
    bt = jnp.zeros((DEPTH, ZT_W, 1), F32).at[:, ZT_G:ZT_G + 16, 0].set(regroup(b_gates))
    return wz.astype(BF16), wt.astype(BF16), bt


def _tokens_major(x):
    return x.transpose(0, 2, 1)


def kernel(x_prompt, x_sample, cache_a_k, cache_a_v, cache_c_k, cache_c_v, state_b_C, state_b_n,
           state_b_m, c, c_ctx, w_ada, b_ada, g_norm, w_ff_in, w_ff_out, w_in, w_out, g_qa, g_ka,
           b_gates, g_b, lam_q1, lam_k1, lam_q2, lam_k2, g_c, g_final):
    h = (x_prompt.reshape(N_CTX, D_MODEL), x_sample.reshape(N_LAT, D_MODEL))
    cc = jnp.concatenate([c_ctx[None, :], c, jnp.zeros((3, D_MODEL), F32)], axis=0)
    mods = _ada(cc, w_ada, b_ada).reshape(DEPTH, 8, 9, D_MODEL)
    tab_a = _rope_tables(HD_A)
    tab_c = _rope_tables(D_C)

    gf = g_final.reshape(1, D_MODEL)
    gn = g_norm.reshape(DEPTH, 3, 1, D_MODEL)
    wi, wo, wmix = w_ff_in.astype(BF16), w_ff_out.astype(BF16), w_out.astype(BF16)
    wz, wt, bt = _split_w_in(w_in, b_gates)
    par_a = (g_qa.reshape(DEPTH, 64, 1), g_ka.reshape(DEPTH, 1, 64))
    par_c = (jnp.stack([lam_q1, lam_k1, lam_q2, lam_k2], axis=1), g_c.reshape(DEPTH, 64, 1))
    gb = g_b.reshape(DEPTH, 64, 1)
    ctx_a = (cache_a_k.reshape(DEC_BATCH, DEPTH, PAST_LEN, KV_A * HD_A), cache_a_v.transpose(0, 1, 3, 4, 2))
    ctx_c = (cache_c_k.reshape(DEC_BATCH, DEPTH, PAST_LEN, H_C * 2 * D_C), cache_c_v.transpose(0, 1, 3, 4, 2))
    s0 = jnp.concatenate([jnp.swapaxes(state_b_C, -1, -2), state_b_n[..., None, :],
                          jnp.zeros((DEC_BATCH, DEPTH, 2, H_B, 63, DK_B), F32)], axis=-2)
    s0 = s0.reshape(DEC_BATCH, DEPTH, 2 * H_B, 128, DK_B)
    m0 = jnp.broadcast_to(state_b_m.reshape(DEC_BATCH, DEPTH, 2 * H_B, 1), (DEC_BATCH, DEPTH, 2 * H_B, 128))
    zeros_s = jnp.zeros((BATCH, 1, 2 * H_B, 128, DK_B), F32)
    zeros_m = jnp.zeros((BATCH, 1, 2 * H_B, 128), F32)

    outs = {k: [] for k in ("ak", "av", "ck", "cv", "bC", "bn", "bm")}
    for l in range(DEPTH):
        lam_init = 0.8 - 0.6 * math.exp(-0.3 * l)
        h = _ffn(h, mods, gn, wi, wo, gf, l=l, half=0, first=(l == 0))
        z, zt = _proj_in(h, mods, gn, wz, wt, bt, l=l)

        ya_c, kn = _attn("A", z, zt, ctx=None, tables=None, params=par_a, l=l, lam_init=lam_init,
                         latent=False)
        (yc_c,) = _attn("C", z, zt, ctx=None, tables=None, params=par_c, l=l, lam_init=lam_init,
                        latent=False)
        stats, acols = _gate_stats(zt)
        yb_c, s_fin, m_fin = _mlstm(z, zt, stats, acols, zeros_s, zeros_m, gb, l=l, latent=False)

        (ya_l,) = _attn("A", z, zt, ctx=ctx_a, tables=tab_a, params=par_a, l=l, lam_init=lam_init,
                        latent=True)
        (yc_l,) = _attn("C", z, zt, ctx=ctx_c, tables=tab_c, params=par_c, l=l, lam_init=lam_init,
                        latent=True)
        yb_l, _, _ = _mlstm(z, zt, stats, acols, s0, m0, gb, l=l, latent=True)

        h = _ffn(h, mods, gn, wi, wo, gf, l=l, half=1,
                 mix=((ya_c, ya_l), (yb_c, yb_l), (yc_c, yc_l), wmix), final=(l == DEPTH - 1))

        zc = z[:N_CTX]
        ztc = zt[:N_CTX // TC]
        outs["ak"].append(kn.reshape(BATCH, SEQ, KV_A, HD_A))
        outs["av"].append(_tokens_major(ztc[:, ZT_VA:ZT_VA + 128, :]).reshape(BATCH, SEQ, KV_A, HD_A))
        outs["ck"].append(zc[:, Z_KC:Z_KC + 256].reshape(BATCH, SEQ, H_C, 2, D_C))
        outs["cv"].append(_tokens_major(ztc[:, ZT_VC:ZT_VC + 256, :]).reshape(BATCH, SEQ, H_C, 2 * D_C))
        s_fin = s_fin.reshape(BATCH, 2, H_B, 128, DK_B)
        outs["bC"].append(jnp.swapaxes(s_fin[..., 0:64, :], -1, -2))
        outs["bn"].append(s_fin[..., 64, :])
        outs["bm"].append(m_fin[:, :, 0].reshape(BATCH, 2, H_B))

    y_prompt = h[0].reshape(BATCH, SEQ, D_MODEL)
    y_sample = h[1].reshape(DEC_BATCH, DEC_SEQ, D_MODEL)
    st = lambda k: jnp.stack(outs[k], axis=1)
    return (y_prompt, y_sample, st("ak"), st("av"), st("ck"), st("cv"), st("bC"), st("bn"), st("bm"))
```

```python
import functools
import math

import jax
import jax.numpy as jnp
from jax import lax
from jax.experimental import pallas as pl
from jax.experimental.pallas import tpu as pltpu

F32 = jnp.float32
BF16 = jnp.bfloat16

D_MODEL = 1024
BATCH = 16
SEQ = 256
DEPTH = 2
DEC_BATCH = 4
DEC_SEQ = 2048
PAST_LEN = 512
GRID_W = 64
ROPE_THETA = 10000.0
EPS = 1e-6
HD_A = 64
H_A = 8
KV_A = 2
DK_B = 64
H_B = 4
D_C = 32
H_C = 4
D_FF = 2816
LOG2E = 1.4426950408889634

N_CTX = BATCH * SEQ
N_LAT = DEC_BATCH * DEC_SEQ
N_TOK = N_CTX + N_LAT

TM = 512
CTX_TILES = N_CTX // TM
LAT_TILES_PER_BATCH = DEC_SEQ // TM
FF_CHUNKS = ((0, 1536), (1536, D_FF))
FF_ROWS = 256
TC = 256
SB = 256
TQ = 256
VROWS = 80

Z_KB, Z_KC, Z_KA = 0, 256, 512
Z_W = 640
ZT_QA, ZT_QC, ZT_VC, ZT_QB, ZT_VB, ZT_OB, ZT_VA, ZT_G = 0, 512, 768, 1024, 1280, 1536, 1792, 1920
ZT_W = 1936

VMEM_LIMIT = 56 * 1024 * 1024


def _cparams(sem):
    return pltpu.CompilerParams(dimension_semantics=sem, vmem_limit_bytes=VMEM_LIMIT)


def _const_spec(shape, lead=()):
    nd = len(shape)
    return pl.BlockSpec((None,) * len(lead) + tuple(shape), lambda *_: tuple(lead) + (0,) * nd,
                        pipeline_mode=pl.Buffered(1))


def _mod_row(i):
    return jnp.maximum(i - CTX_TILES, 0) // LAT_TILES_PER_BATCH + (i >= CTX_TILES).astype(jnp.int32)


def _rms(x, g, axis=-1):
    ms = jnp.mean(x * x, axis=axis, keepdims=True)
    return x * lax.rsqrt(ms + EPS) * g


def _norm_mod(x, g, mod_ref, k):
    shift = mod_ref[0, k:k + 1, :]
    scale = mod_ref[0, k + 1:k + 2, :]
    return _rms(x, g) * (1.0 + scale) + shift


def _ds(c, n):
    if isinstance(c, int):
        return pl.ds(c * n, n)
    return pl.ds(pl.multiple_of(c * n, n), n)


def _log_sigmoid(x):
    return jnp.minimum(x, 0.0) - jnp.log1p(jnp.exp(-jnp.abs(x)))


def _ada_kernel(c_ref, w_ref, b_ref, o_ref):
    c = c_ref[...]
    a = (c * jax.nn.sigmoid(c)).astype(BF16)
    o_ref[0] = jnp.dot(a, w_ref[0].astype(BF16), preferred_element_type=F32) + b_ref[0]


def _ada(cc, w_ada, b_ada):
    tn = 1024
    n = 9 * D_MODEL
    return pl.pallas_call(
        _ada_kernel,
        grid=(DEPTH, n // tn),
        in_specs=[
            pl.BlockSpec((8, D_MODEL), lambda l, j: (0, 0)),
            pl.BlockSpec((1, D_MODEL, tn), lambda l, j: (l, 0, j)),
            pl.BlockSpec((1, 1, tn), lambda l, j: (l, 0, j)),
        ],
        out_specs=pl.BlockSpec((1, 8, tn), lambda l, j: (l, 0, j)),
        out_shape=jax.ShapeDtypeStruct((DEPTH, 8, n), F32),
        compiler_params=_cparams(("arbitrary", "arbitrary")),
        name="ada",
    )(cc, w_ada, b_ada.reshape(DEPTH, 1, n))


def _ffn_kernel(*refs, k, first, mix, final):
    is_ctx = pl.program_id(0) < CTX_TILES
    it = iter(refs)
    x_refs = (next(it), next(it)) if first else (next(it),)
    if mix:
        y_refs = [(next(it), next(it)) for _ in range(3)]
        wmix_ref = next(it)
    mod_ref, g_ref, wi_ref, wo_ref, gf_ref = (next(it) for _ in range(5))

    ys = []
    for r in range(TM // FF_ROWS):
        rows = slice(r * FF_ROWS, (r + 1) * FF_ROWS)
        pick = lambda c_ref, l_ref: jnp.where(is_ctx, c_ref[rows, :], l_ref[rows, :])
        x = pick(*x_refs) if first else x_refs[0][rows, :]
        if mix:
            ya, yb, yc = (pick(*pair) for pair in y_refs)
            y = jnp.dot(ya, wmix_ref[0:512, :], preferred_element_type=F32)
            y = y + jnp.dot(yb, wmix_ref[512:768, :], preferred_element_type=F32)
            y = y + jnp.dot(yc, wmix_ref[768:1024, :], preferred_element_type=F32)
            x = x + mod_ref[0, 5:6, :] * y
        xn = _norm_mod(x, g_ref[...], mod_ref, k).astype(BF16)
        acc = jnp.zeros((FF_ROWS, D_MODEL), F32)
        for lo, hi in FF_CHUNKS:
            g = jnp.dot(xn, wi_ref[:, lo:hi], preferred_element_type=F32)
            u = jnp.dot(xn, wi_ref[:, D_FF + lo:D_FF + hi], preferred_element_type=F32)
            a = (g * jax.nn.sigmoid(g) * u).astype(BF16)
            acc = acc + jnp.dot(a, wo_ref[lo:hi, :], preferred_element_type=F32)
        y = x + 0.5 * mod_ref[0, k + 2:k + 3, :] * acc
        if final:
            ys.append(_rms(y, gf_ref[...]))
        else:
            refs[-1][rows, :] = y
    if final:
        yc_ref, yl_ref = refs[-2:]
        y = jnp.concatenate(ys, axis=0)

        @pl.when(is_ctx)
        def _():
            yc_ref[...] = y

        @pl.when(jnp.logical_not(is_ctx))
        def _():
            yl_ref[...] = y


def _ctx_lat_specs(width):
    return [
        pl.BlockSpec((TM, width), lambda i: (jnp.minimum(i, CTX_TILES - 1), 0)),
        pl.BlockSpec((TM, width), lambda i: (jnp.maximum(i - CTX_TILES, 0), 0)),
    ]


def _mod_spec(l):
    return pl.BlockSpec((None, 1, 9, D_MODEL), lambda i: (l, _mod_row(i), 0, 0))


def _ffn(h, mods, g, wi, wo, gf, *, l, half, first=False, mix=None, final=False):
    k = 6 * half
    flat_spec = pl.BlockSpec((TM, D_MODEL), lambda i: (i, 0))
    flat_shape = jax.ShapeDtypeStruct((N_TOK, D_MODEL), F32)
    mix_specs, mix_args = [], []
    if mix is not None:
        mix_specs = (_ctx_lat_specs(512) + _ctx_lat_specs(256) + _ctx_lat_specs(256)
                     + [_const_spec((D_MODEL, D_MODEL), (l,))])
        mix_args = [*mix[0], *mix[1], *mix[2], mix[3]]
    return pl.pallas_call(
        functools.partial(_ffn_kernel, k=k, first=first, mix=mix is not None, final=final),
        grid=(N_TOK // TM,),
        in_specs=(_ctx_lat_specs(D_MODEL) if first else [flat_spec]) + mix_specs + [
            _mod_spec(l),
            _const_spec((1, D_MODEL), (l, 2 * half)),
            _const_spec((D_MODEL, 2 * D_FF), (l, half)),
            _const_spec((D_FF, D_MODEL), (l, half)),
            _const_spec((1, D_MODEL)),
        ],
        out_specs=_ctx_lat_specs(D_MODEL) if final else flat_spec,
        out_shape=[jax.ShapeDtypeStruct((N_CTX, D_MODEL), F32),
                   jax.ShapeDtypeStruct((N_LAT, D_MODEL), F32)] if final else flat_shape,
        compiler_params=_cparams(("arbitrary",)),
        name="ffn_final" if final else ("ffn_first" if first else ("ffn_mix" if mix else "ffn")),
    )(*(h if first else (h,)), *mix_args, mods, g, wi, wo, gf)


def _proj_in_kernel(h_ref, mod_ref, g_ref, w_ref, wt_ref, bt_ref, z_ref, zt_ref):
    xn = _norm_mod(h_ref[...], g_ref[...], mod_ref, 3).astype(BF16)
    z_ref[...] = jnp.dot(xn, w_ref[...], preferred_element_type=F32)
    zt = lax.dot_general(wt_ref[...], xn, (((1,), (1,)), ((), ())), preferred_element_type=F32)
    zt = zt + bt_ref[...]
    for c in range(TM // TC):
        zt_ref[c] = zt[:, c * TC:(c + 1) * TC]


def _proj_in(h, mods, g, wz, wt, bt, *, l):
    return pl.pallas_call(
        _proj_in_kernel,
        grid=(N_TOK // TM,),
        in_specs=[
            pl.BlockSpec((TM, D_MODEL), lambda i: (i, 0)),
            _mod_spec(l),
            _const_spec((1, D_MODEL), (l, 1)),
            _const_spec((D_MODEL, Z_W), (l,)),
            _const_spec((ZT_W, D_MODEL), (l,)),
            _const_spec((ZT_W, 1), (l,)),
        ],
        out_specs=[
            pl.BlockSpec((TM, Z_W), lambda i: (i, 0)),
            pl.BlockSpec((TM // TC, ZT_W, TC), lambda i: (i, 0, 0)),
        ],
        out_shape=[
            jax.ShapeDtypeStruct((N_TOK, Z_W), F32),
            jax.ShapeDtypeStruct((N_TOK // TC, ZT_W, TC), F32),
        ],
        compiler_params=_cparams(("arbitrary",)),
        name="proj_in",
    )(h, mods, g, wz, wt, bt)


def _swap_halves(x, width, axis):
    n = x.shape[axis]
    half = width // 2
    parts = []
    for s in range(0, n, width):
        parts.append(lax.slice_in_dim(x, s + half, s + width, axis=axis))
        parts.append(lax.slice_in_dim(x, s, s + half, axis=axis))
    return jnp.concatenate(parts, axis=axis)


def _attn_kernel(*refs, mode, seq, has_ctx, rope, emit_k, lam_init):
    it = iter(refs)
    qt_ref, k_ref, vt_ref = next(it), next(it), next(it)
    if has_ctx:
        ck_ref, cvt_ref = next(it), next(it)
    if rope:
        cqt_ref, sqt_ref, ck_tab, sk_tab = next(it), next(it), next(it), next(it)
    if mode == "A":
        gq_ref, gk_ref = next(it), next(it)
    else:
        lam_ref, gc_ref = next(it), next(it)
    y_ref = next(it)
    if emit_k:
        kn_ref = next(it)
    kbuf, vbuf, s_ref, p_ref = next(it), next(it), next(it), next(it)

    n_kv = KV_A if mode == "A" else H_C
    per_kv = 8 // n_kv
    n_ctx = PAST_LEN if has_ctx else 0
    lk = n_ctx + seq
    width = HD_A if mode == "A" else D_C

    @pl.when(pl.program_id(1) == 0)
    def _fill():
        ones_rows = (lax.broadcasted_iota(jnp.int32, (VROWS - 64, lk), 0) == 0).astype(BF16)
        for h in range(n_kv):
            hs = slice(h * 64, (h + 1) * 64)
            vbuf[h, 64:VROWS, :] = ones_rows
            if has_ctx:
                kbuf[h, 0:n_ctx, :] = ck_ref[0, :, hs].astype(BF16)
                vbuf[h, 0:64, 0:n_ctx] = cvt_ref[0, h].astype(BF16)
            for c in range(seq // TC):
                rows = slice(c * TC, (c + 1) * TC)
                k = k_ref[rows, hs]
                if mode == "A":
                    k = _rms(k, gk_ref[...])
                    if emit_k:
                        kn_ref[0, rows, hs] = k
                if rope:
                    k = k * ck_tab[rows, :] + _swap_halves(k, width, 1) * sk_tab[rows, :]
                kbuf[h, n_ctx + c * TC:n_ctx + (c + 1) * TC, :] = k.astype(BF16)
                vbuf[h, 0:64, n_ctx + c * TC:n_ctx + (c + 1) * TC] = vt_ref[c, hs, :].astype(BF16)

    qts = []
    if mode == "A":
        scale = HD_A ** -0.5 * LOG2E
        for hh in range(H_A):
            x = _rms(qt_ref[0, hh * 64:(hh + 1) * 64, :], gq_ref[...], axis=0)
            if rope:
                x = x * cqt_ref[0] + _swap_halves(x, width, 0) * sqt_ref[0]
            qts.append((x * scale).astype(BF16))
    else:
        scale = D_C ** -0.5 * LOG2E
        first_map = lax.broadcasted_iota(jnp.int32, (64, TQ), 0) < D_C
        for h in range(H_C):
            x = qt_ref[0, h * 64:(h + 1) * 64, :]
            if rope:
                x = x * cqt_ref[0] + _swap_halves(x, width, 0) * sqt_ref[0]
            x = x * scale
            qts.append(jnp.where(first_map, x, 0.0).astype(BF16))
            qts.append(jnp.where(first_map, 0.0, x).astype(BF16))

    m_prev = None
    outs = []
    for s in range(9):
        def stage_block(j, mrun, s=s, m_prev=m_prev):
            rows = _ds(j, SB)
            if s < 8:
                st = jnp.dot(kbuf[s // per_kv, rows, :], qts[s], preferred_element_type=F32)
                s_ref[s % 2, rows, :] = st
                for r in range(SB // 8):
                    mrun = jnp.maximum(mrun, st[8 * r:8 * r + 8])
            if s >= 1:
                prev = s_ref[(s - 1) % 2, rows, :]
                p_ref[rows, (s - 1) * TQ:s * TQ] = jnp.exp2(prev - m_prev).astype(BF16)
            return mrun

        mrun = lax.fori_loop(0, lk // SB, stage_block, jnp.full((8, TQ), -jnp.inf, F32),
                             unroll=True)
        m_prev = jnp.max(mrun, axis=0, keepdims=True)
        if s >= 1:
            u = s - 1
            ox = jnp.dot(vbuf[u // per_kv], p_ref[:, u * TQ:(u + 1) * TQ], preferred_element_type=F32)
            outs.append(ox[0:64] / ox[64:65])

    if mode == "A":
        for pair in range(4):
            o = jnp.concatenate(outs[2 * pair:2 * pair + 2], axis=0)
            y_ref[:, pair * 128:(pair + 1) * 128] = o.T.astype(BF16)
    else:
        lp = lam_ref[...]
        lam = (jnp.exp(jnp.sum(lp[0:1] * lp[1:2], axis=1, keepdims=True))
               - jnp.exp(jnp.sum(lp[2:3] * lp[3:4], axis=1, keepdims=True)) + lam_init)
        for pair in range(2):
            o = []
            for h in (2 * pair, 2 * pair + 1):
                d = outs[2 * h] - lam * outs[2 * h + 1]
                o.append(_rms(d, gc_ref[...], axis=0) * (1.0 - lam_init))
            y_ref[:, pair * 128:(pair + 1) * 128] = jnp.concatenate(o, axis=0).T.astype(BF16)


def _attn(mode, z, zt, *, ctx, tables, params, l, lam_init, latent):
    if latent:
        nb, seq, row0 = DEC_BATCH, DEC_SEQ, N_CTX
    else:
        nb, seq, row0 = BATCH, SEQ, 0
    nq = seq // TQ
    n_kv = KV_A if mode == "A" else H_C
    n_ctx = PAST_LEN if ctx is not None else 0
    lk = n_ctx + seq
    qw = 512 if mode == "A" else 256
    vw = n_kv * 64
    q_blk = (ZT_QA if mode == "A" else ZT_QC) // qw
    k_blk = (Z_KA if mode == "A" else Z_KC) // vw
    v_blk = (ZT_VA if mode == "A" else ZT_VC) // vw
    emit_k = mode == "A" and not latent
    blk0 = row0 // seq

    in_specs = [
        pl.BlockSpec((1, qw, TC), lambda b, i: (row0 // TC + b * nq + i, q_blk, 0)),
        pl.BlockSpec((seq, vw), lambda b, i: (blk0 + b, k_blk)),
        pl.BlockSpec((seq // TC, vw, TC), lambda b, i: (blk0 + b, v_blk, 0)),
    ]
    args = [zt, z, zt]
    if ctx is not None:
        in_specs += [
            pl.BlockSpec((1, None, PAST_LEN, vw), lambda b, i: (b, l, 0, 0)),
            pl.BlockSpec((1, None, n_kv, 64, PAST_LEN), lambda b, i: (b, l, 0, 0, 0)),
        ]
        args += list(ctx)
    if tables is not None:
        cq, sq, cqt, sqt = tables
        in_specs += [
            pl.BlockSpec((1, 64, TC), lambda b, i: (i, 0, 0)),
            pl.BlockSpec((1, 64, TC), lambda b, i: (i, 0, 0)),
            _const_spec((seq, 64)),
            _const_spec((seq, 64)),
        ]
        args += [cqt, sqt, cq, sq]
    for p in params:
        in_specs.append(_const_spec(p.shape[1:], (l,)))
        args.append(p)

    out_specs = [pl.BlockSpec((TQ, qw), lambda b, i: (b * nq + i, 0))]
    out_shape = [jax.ShapeDtypeStruct((nb * seq, qw), BF16)]
    if emit_k:
        out_specs.append(pl.BlockSpec((1, seq, vw), lambda b, i: (b, 0, 0)))
        out_shape.append(jax.ShapeDtypeStruct((nb, seq, vw), F32))

    return pl.pallas_call(
        functools.partial(_attn_kernel, mode=mode, seq=seq, has_ctx=ctx is not None,
                          rope=tables is not None, emit_k=emit_k, lam_init=lam_init),
        grid=(nb, nq),
        in_specs=in_specs,
        out_specs=out_specs,
        out_shape=out_shape,
        scratch_shapes=[
            pltpu.VMEM((n_kv, lk, 64), BF16),
            pltpu.VMEM((n_kv, VROWS, lk), BF16),
            pltpu.VMEM((2, lk, TQ), F32),
            pltpu.VMEM((lk, 8 * TQ), BF16),
        ],
        compiler_params=_cparams(("arbitrary", "arbitrary")),
        name=f"attn_{mode}_{'lat' if latent else 'ctx'}",
    )(*args)


N_CHUNKS = N_TOK // TC
GS_A, GS_B, GS_LM, GS_BLAST, GS_ALAST, GS_ROWS = 0, 8, 16, 24, 32, 40


def _scan_lanes(x, op, fill, reverse):
    axis = x.ndim - 1
    lane = lax.broadcasted_iota(jnp.int32, x.shape, axis)
    k = 1
    while k < TC:
        if reverse:
            x = op(x, jnp.where(lane < TC - k, pltpu.roll(x, TC - k, axis), fill))
        else:
            x = op(x, jnp.where(lane >= k, pltpu.roll(x, k, axis), fill))
        k *= 2
    return x


def _gate_stats_kernel(g_ref, st_ref, ac_ref):
    is_fwd = lax.broadcasted_iota(jnp.int32, (N_CHUNKS, 8, TC), 1) < H_B
    g = g_ref[...]
    gi = jnp.concatenate([g[:, 0:4], g[:, 8:12]], axis=1)
    f = _log_sigmoid(jnp.concatenate([g[:, 4:8], g[:, 12:16]], axis=1))
    b = jnp.where(is_fwd, _scan_lanes(f, jnp.add, 0.0, False), _scan_lanes(f, jnp.add, 0.0, True))
    a = gi - b
    lm = jnp.where(is_fwd, _scan_lanes(a, jnp.maximum, -jnp.inf, False),
                   _scan_lanes(a, jnp.maximum, -jnp.inf, True))
    st_ref[:, GS_A:GS_A + 8, :] = a
    st_ref[:, GS_B:GS_B + 8, :] = b
    st_ref[:, GS_LM:GS_LM + 8, :] = lm
    st_ref[:, GS_BLAST:GS_BLAST + 8, :] = jnp.broadcast_to(jnp.sum(f, axis=2, keepdims=True), a.shape)
    st_ref[:, GS_ALAST:GS_ALAST + 8, :] = jnp.broadcast_to(jnp.max(a, axis=2, keepdims=True), a.shape)

    def transpose_chunk(c, carry):
        ac_ref[c] = jnp.concatenate([st_ref[c, GS_A:GS_A + 8, :], jnp.zeros((120, TC), F32)], axis=0).T
        return carry

    lax.fori_loop(0, N_CHUNKS, transpose_chunk, 0, unroll=4)


def _gate_stats(zt):
    return pl.pallas_call(
        _gate_stats_kernel,
        grid=(1,),
        in_specs=[pl.BlockSpec((N_CHUNKS, 16, TC), lambda i: (0, ZT_G // 16, 0))],
        out_specs=[
            pl.BlockSpec((N_CHUNKS, GS_ROWS, TC), lambda i: (0, 0, 0)),
            pl.BlockSpec((N_CHUNKS, TC, 128), lambda i: (0, 0, 0)),
        ],
        out_shape=[
            jax.ShapeDtypeStruct((N_CHUNKS, GS_ROWS, TC), F32),
            jax.ShapeDtypeStruct((N_CHUNKS, TC, 128), F32),
        ],
        compiler_params=_cparams(("arbitrary",)),
        name="gate_stats",
    )(zt)


def _mlstm_kernel(k_ref, qt_ref, vt_ref, ot_ref, st_ref, ac_ref, s0_ref, m0_ref, gb_ref,
                  y_ref, s_ref, m_ref, hf_ref, hr_ref, *, seq):
    nc = seq // TC
    s_idx = lax.broadcasted_iota(jnp.int32, (TC, TC), 0)
    t_idx = lax.broadcasted_iota(jnp.int32, (TC, TC), 1)
    seen = (s_idx <= t_idx, s_idx >= t_idx)
    ones_rows = (lax.broadcasted_iota(jnp.int32, (64, TC), 0) == 0).astype(F32)
    units = [(d, h) for d in range(2) for h in range(H_B)]

    def chunk(j, carry):
        states, m = carry
        cs = (j, nc - 1 - j)
        stat = lambda r: jnp.concatenate([st_ref[cs[0], r:r + 4, :], st_ref[cs[1], r + 4:r + 8, :]], axis=0)
        a, b, lm = stat(GS_A), stat(GS_B), stat(GS_LM)
        mx = jnp.maximum(m, lm)
        w = jnp.exp(m - mx)
        floor = jnp.exp(-(b + mx))
        mxl = jnp.maximum(m, stat(GS_ALAST)[:, 0:1])
        decay = jnp.exp(m - mxl)
        m_new = stat(GS_BLAST)[:, 0:1] + mxl

        first = []
        for (d, h), st in zip(units, states):
            hs = slice(h * 64, (h + 1) * 64)
            kh = k_ref[_ds(cs[d], TC), hs] * (DK_B ** -0.5)
            qt = qt_ref[cs[d], hs, :].astype(BF16)
            lhs = jnp.concatenate([kh.astype(BF16), st.astype(BF16)], axis=0)
            first.append((kh, jnp.dot(lhs, qt, preferred_element_type=F32)))
        second = []
        for u, ((d, h), (kh, both)) in enumerate(zip(units, first)):
            a_col = ac_ref[cs[d], :, u:u + 1]
            dt = jnp.where(seen[d], a_col - mx[u:u + 1, :], -jnp.inf)
            sp = both[0:TC] * jnp.exp(dt)
            den = jnp.sum(sp, axis=0, keepdims=True) + w[u:u + 1, :] * both[TC + 64:TC + 65]
            wk = jnp.exp(a_col - mxl[u:u + 1, :])
            second.append((sp.astype(BF16), (kh * wk).astype(BF16), den))
        new_states = []
        for u, ((d, h), st, (kh, both), (spb, kw, den)) in enumerate(zip(units, states, first, second)):
            hs = slice(h * 64, (h + 1) * 64)
            vx = jnp.concatenate([vt_ref[cs[d], hs, :], ones_rows], axis=0).astype(BF16)
            num = jnp.dot(vx, spb, preferred_element_type=F32)[0:64] + w[u:u + 1, :] * both[TC:TC + 64]
            hv = num / jnp.maximum(jnp.abs(den), floor[u:u + 1, :])
            if d == 0:
                hf_ref[cs[d], hs, :] = hv
            else:
                hr_ref[cs[d], hs, :] = hv
            new_states.append(decay[u:u + 1, :] * st + jnp.dot(vx, kw, preferred_element_type=F32))
        return tuple(new_states), m_new

    init = (tuple(s0_ref[0, u] for u in range(2 * H_B)), m0_ref[0][:, 0:1])
    if nc == 1:
        states, m = chunk(0, init)
    else:
        states, m = lax.fori_loop(0, nc, chunk, init)
    for u, st in enumerate(states):
        s_ref[0, u] = st
    m_ref[0] = jnp.broadcast_to(m, (2 * H_B, 128))

    def epilogue(c, carry):
        ys = []
        for h in range(H_B):
            hs = slice(h * 64, (h + 1) * 64)
            x = hf_ref[c, hs, :] + hr_ref[c, hs, :]
            ys.append(_rms(x, gb_ref[...], axis=0) * jax.nn.sigmoid(ot_ref[c, hs, :]))
        y_ref[_ds(c, TC), :] = jnp.concatenate(ys, axis=0).T.astype(BF16)
        return carry

    if nc == 1:
        epilogue(0, 0)
    else:
        lax.fori_loop(0, nc, epilogue, 0)


def _mlstm(z, zt, stats, acols, s0, m0, gb, *, l, latent):
    ls = min(l, s0.shape[1] - 1)
    if latent:
        nb, seq, row0 = DEC_BATCH, DEC_SEQ, N_CTX
    else:
        nb, seq, row0 = BATCH, SEQ, 0
    blk0 = row0 // seq
    nc = seq // TC
    ztspec = lambda rb: pl.BlockSpec((nc, 256, TC), lambda b: (blk0 + b, rb, 0))
    return pl.pallas_call(
        functools.partial(_mlstm_kernel, seq=seq),
        grid=(nb,),
        in_specs=[
            pl.BlockSpec((seq, 256), lambda b: (blk0 + b, Z_KB // 256)),
            ztspec(ZT_QB // 256), ztspec(ZT_VB // 256), ztspec(ZT_OB // 256),
            pl.BlockSpec((nc, GS_ROWS, TC), lambda b: (blk0 + b, 0, 0)),
            pl.BlockSpec((nc, TC, 128), lambda b: (blk0 + b, 0, 0)),
            pl.BlockSpec((1, None, 2 * H_B, 128, 64), lambda b: (b, ls, 0, 0, 0)),
            pl.BlockSpec((1, None, 2 * H_B, 128), lambda b: (b, ls, 0, 0)),
            _const_spec((64, 1), (l,)),
        ],
        out_specs=[
            pl.BlockSpec((seq, 256), lambda b: (b, 0)),
            pl.BlockSpec((1, 2 * H_B, 128, 64), lambda b: (b, 0, 0, 0)),
            pl.BlockSpec((1, 2 * H_B, 128), lambda b: (b, 0, 0)),
        ],
        out_shape=[
            jax.ShapeDtypeStruct((nb * seq, 256), BF16),
            jax.ShapeDtypeStruct((nb, 2 * H_B, 128, 64), F32),
            jax.ShapeDtypeStruct((nb, 2 * H_B, 128), F32),
        ],
        scratch_shapes=[pltpu.VMEM((nc, 256, TC), F32), pltpu.VMEM((nc, 256, TC), F32)],
        compiler_params=_cparams(("arbitrary",)),
        name=f"mlstm_{'lat' if latent else 'ctx'}",
    )(z, zt, zt, zt, stats, acols, s0, m0, gb)


def _rope_tables(dim):
    t = jnp.arange(DEC_SEQ)
    row = (t // GRID_W).astype(F32)
    colp = (t % GRID_W).astype(F32)
    axis_dim = dim // 2
    freqs = ROPE_THETA ** (-jnp.arange(0, axis_dim, 2, dtype=F32) / axis_dim)
    ang = jnp.concatenate([row[:, None] * freqs, colp[:, None] * freqs], axis=-1)
    cos, sin = jnp.cos(ang), jnp.sin(ang)
    reps = 64 // dim
    cq = jnp.concatenate([cos, cos] * reps, axis=1)
    sq = jnp.concatenate([-sin, sin] * reps, axis=1)
    to_chunks = lambda x: x.T.reshape(64, DEC_SEQ // TC, TC).transpose(1, 0, 2)
    return cq, sq, to_chunks(cq), to_chunks(sq)


def _split_w_in(w, b_gates):
    a0, b0, c0 = 0, 768, 1808
    qa, ka, va = w[..., a0:a0 + 512], w[..., a0 + 512:a0 + 640], w[..., a0 + 640:a0 + 768]
    qb, kb = w[..., b0:b0 + 256], w[..., b0 + 256:b0 + 512]
    vb, ob = w[..., b0 + 512:b0 + 768], w[..., b0 + 768:b0 + 1024]
    gb = w[..., b0 + 1024:b0 + 1040]
    qc, kc, vc = w[..., c0:c0 + 256], w[..., c0 + 256:c0 + 512], w[..., c0 + 512:c0 + 768]
    wz = jnp.concatenate([kb, kc, ka], axis=-1)
    wt = jnp.swapaxes(jnp.concatenate([qa, qc, vc, qb, vb, ob, va, gb], axis=-1), -1, -2)
    bt = jnp.zeros((DEPTH, ZT_W, 1), F32).at[:, ZT_G:ZT_G + 16, 0].set(b_gates)
    return wz.astype(BF16), wt.astype(BF16), bt


def _tokens_major(x):
    return x.transpose(0, 2, 1)


def kernel(x_prompt, x_sample, cache_a_k, cache_a_v, cache_c_k, cache_c_v, state_b_C, state_b_n,
           state_b_m, c, c_ctx, w_ada, b_ada, g_norm, w_ff_in, w_ff_out, w_in, w_out, g_qa, g_ka,
           b_gates, g_b, lam_q1, lam_k1, lam_q2, lam_k2, g_c, g_final):
    h = (x_prompt.reshape(N_CTX, D_MODEL), x_sample.reshape(N_LAT, D_MODEL))
    cc = jnp.concatenate([c_ctx[None, :], c, jnp.zeros((3, D_MODEL), F32)], axis=0)
    mods = _ada(cc, w_ada, b_ada).reshape(DEPTH, 8, 9, D_MODEL)
    tab_a = _rope_tables(HD_A)
    tab_c = _rope_tables(D_C)

    gf = g_final.reshape(1, D_MODEL)
    gn = g_norm.reshape(DEPTH, 3, 1, D_MODEL)
    wi, wo, wmix = w_ff_in.astype(BF16), w_ff_out.astype(BF16), w_out.astype(BF16)
    wz, wt, bt = _split_w_in(w_in, b_gates)
    par_a = (g_qa.reshape(DEPTH, 64, 1), g_ka.reshape(DEPTH, 1, 64))
    par_c = (jnp.stack([lam_q1, lam_k1, lam_q2, lam_k2], axis=1), g_c.reshape(DEPTH, 64, 1))
    gb = g_b.reshape(DEPTH, 64, 1)
    ctx_a = (cache_a_k.reshape(DEC_BATCH, DEPTH, PAST_LEN, KV_A * HD_A), cache_a_v.transpose(0, 1, 3, 4, 2))
    ctx_c = (cache_c_k.reshape(DEC_BATCH, DEPTH, PAST_LEN, H_C * 2 * D_C), cache_c_v.transpose(0, 1, 3, 4, 2))
    s0 = jnp.concatenate([jnp.swapaxes(state_b_C, -1, -2), state_b_n[..., None, :],
                          jnp.zeros((DEC_BATCH, DEPTH, 2, H_B, 63, DK_B), F32)], axis=-2)
    s0 = s0.reshape(DEC_BATCH, DEPTH, 2 * H_B, 128, DK_B)
    m0 = jnp.broadcast_to(state_b_m.reshape(DEC_BATCH, DEPTH, 2 * H_B, 1), (DEC_BATCH, DEPTH, 2 * H_B, 128))
    zeros_s = jnp.zeros((BATCH, 1, 2 * H_B, 128, DK_B), F32)
    zeros_m = jnp.zeros((BATCH, 1, 2 * H_B, 128), F32)

    outs = {k: [] for k in ("ak", "av", "ck", "cv", "bC", "bn", "bm")}
    for l in range(DEPTH):
        lam_init = 0.8 - 0.6 * math.exp(-0.3 * l)
        h = _ffn(h, mods, gn, wi, wo, gf, l=l, half=0, first=(l == 0))
        z, zt = _proj_in(h, mods, gn, wz, wt, bt, l=l)

        ya_c, kn = _attn("A", z, zt, ctx=None, tables=None, params=par_a, l=l, lam_init=lam_init,
                         latent=False)
        (yc_c,) = _attn("C", z, zt, ctx=None, tables=None, params=par_c, l=l, lam_init=lam_init,
                        latent=False)
        stats, acols = _gate_stats(zt)
        yb_c, s_fin, m_fin = _mlstm(z, zt, stats, acols, zeros_s, zeros_m, gb, l=l, latent=False)

        (ya_l,) = _attn("A", z, zt, ctx=ctx_a, tables=tab_a, params=par_a, l=l, lam_init=lam_init,
                        latent=True)
        (yc_l,) = _attn("C", z, zt, ctx=ctx_c, tables=tab_c, params=par_c, l=l, lam_init=lam_init,
                        latent=True)
        yb_l, _, _ = _mlstm(z, zt, stats, acols, s0, m0, gb, l=l, latent=True)

        h = _ffn(h, mods, gn, wi, wo, gf, l=l, half=1,
                 mix=((ya_c, ya_l), (yb_c, yb_l), (yc_c, yc_l), wmix), final=(l == DEPTH - 1))

        zc = z[:N_CTX]
        ztc = zt[:N_CTX // TC]
        outs["ak"].append(kn.reshape(BATCH, SEQ, KV_A, HD_A))
        outs["av"].append(_tokens_major(ztc[:, ZT_VA:ZT_VA + 128, :]).reshape(BATCH, SEQ, KV_A, HD_A))
        outs["ck"].append(zc[:, Z_KC:Z_KC + 256].reshape(BATCH, SEQ, H_C, 2, D_C))
        outs["cv"].append(_tokens_major(ztc[:, ZT_VC:ZT_VC + 256, :]).reshape(BATCH, SEQ, H_C, 2 * D_C))
        s_fin = s_fin.reshape(BATCH, 2, H_B, 128, DK_B)
        outs["bC"].append(jnp.swapaxes(s_fin[..., 0:64, :], -1, -2))
        outs["bn"].append(s_fin[..., 64, :])
        outs["bm"].append(m_fin[:, :, 0].reshape(BATCH, 2, H_B))

    y_prompt = h[0].reshape(BATCH, SEQ, D_MODEL)
    y_sample = h[1].reshape(DEC_BATCH, DEC_SEQ, D_MODEL)
    st = lambda k: jnp.stack(outs[k], axis=1)
    return (y_prompt, y_sample, st("ak"), st("av"), st("ck"), st("cv"), st("bC"), st("bn"), st("bm"))
```

```python
import functools
import math

import jax
import jax.numpy as jnp
from jax import lax
from jax.experimental import pallas as pl
from jax.experimental.pallas import tpu as pltpu

F32 = jnp.float32
BF16 = jnp.bfloat16

D_MODEL = 1024
BATCH = 16
SEQ = 256
DEPTH = 2
DEC_BATCH = 4
DEC_SEQ = 2048
PAST_LEN = 512
GRID_W = 64
ROPE_THETA = 10000.0
EPS = 1e-6
HD_A = 64
H_A = 8
KV_A = 2
DK_B = 64
H_B = 4
D_C = 32
H_C = 4
D_FF = 2816
LOG2E = 1.4426950408889634

N_CTX = BATCH * SEQ
N_LAT = DEC_BATCH * DEC_SEQ
N_TOK = N_CTX + N_LAT

TM = 512
CTX_TILES = N_CTX // TM
LAT_TILES_PER_BATCH = DEC_SEQ // TM
FF_CHUNKS = ((0, 1536), (1536, D_FF))
FF_ROWS = 256
TC = 256
SB = 256
TQ = 256
VROWS = 80

Z_KB, Z_KC, Z_KA = 0, 256, 512
Z_W = 640
ZT_QA, ZT_QC, ZT_VC, ZT_QB, ZT_VB, ZT_OB, ZT_VA, ZT_G = 0, 512, 768, 1024, 1280, 1536, 1792, 1920
ZT_W = 1936

VMEM_LIMIT = 56 * 1024 * 1024


def _cparams(sem):
    return pltpu.CompilerParams(dimension_semantics=sem, vmem_limit_bytes=VMEM_LIMIT)


def _const_spec(shape, lead=()):
    nd = len(shape)
    return pl.BlockSpec((None,) * len(lead) + tuple(shape), lambda *_: tuple(lead) + (0,) * nd,
                        pipeline_mode=pl.Buffered(1))


def _mod_row(i):
    return jnp.maximum(i - CTX_TILES, 0) // LAT_TILES_PER_BATCH + (i >= CTX_TILES).astype(jnp.int32)


def _rms(x, g, axis=-1):
    ms = jnp.mean(x * x, axis=axis, keepdims=True)
    return x * lax.rsqrt(ms + EPS) * g


def _norm_mod(x, g, mod_ref, k):
    shift = mod_ref[0, k:k + 1, :]
    scale = mod_ref[0, k + 1:k + 2, :]
    return _rms(x, g) * (1.0 + scale) + shift


def _ds(c, n):
    if isinstance(c, int):
        return pl.ds(c * n, n)
    return pl.ds(pl.multiple_of(c * n, n), n)


def _log_sigmoid(x):
    return jnp.minimum(x, 0.0) - jnp.log1p(jnp.exp(-jnp.abs(x)))


def _ada_kernel(c_ref, w_ref, b_ref, o_ref):
    c = c_ref[...]
    a = (c * jax.nn.sigmoid(c)).astype(BF16)
    o_ref[0] = jnp.dot(a, w_ref[0].astype(BF16), preferred_element_type=F32) + b_ref[0]


def _ada(cc, w_ada, b_ada):
    tn = 1024
    n = 9 * D_MODEL
    return pl.pallas_call(
        _ada_kernel,
        grid=(DEPTH, n // tn),
        in_specs=[
            pl.BlockSpec((8, D_MODEL), lambda l, j: (0, 0)),
            pl.BlockSpec((1, D_MODEL, tn), lambda l, j: (l, 0, j)),
            pl.BlockSpec((1, 1, tn), lambda l, j: (l, 0, j)),
        ],
        out_specs=pl.BlockSpec((1, 8, tn), lambda l, j: (l, 0, j)),
        out_shape=jax.ShapeDtypeStruct((DEPTH, 8, n), F32),
        compiler_params=_cparams(("arbitrary", "arbitrary")),
        name="ada",
    )(cc, w_ada, b_ada.reshape(DEPTH, 1, n))


def _ffn_kernel(*refs, k, first, mix, final):
    is_ctx = pl.program_id(0) < CTX_TILES
    it = iter(refs)
    x_refs = (next(it), next(it)) if first else (next(it),)
    if mix:
        y_refs = [(next(it), next(it)) for _ in range(3)]
        wmix_ref = next(it)
    mod_ref, g_ref, wi_ref, wo_ref, gf_ref = (next(it) for _ in range(5))

    nrows = TM if mix else FF_ROWS
    ys = []
    for r in range(TM // nrows):
        rows = slice(r * nrows, (r + 1) * nrows)
        pick = lambda c_ref, l_ref: jnp.where(is_ctx, c_ref[rows, :], l_ref[rows, :])
        x = pick(*x_refs) if first else x_refs[0][rows, :]
        if mix:
            ya, yb, yc = (pick(*pair) for pair in y_refs)
            y = jnp.dot(ya, wmix_ref[0:512, :], preferred_element_type=F32)
            y = y + jnp.dot(yb, wmix_ref[512:768, :], preferred_element_type=F32)
            y = y + jnp.dot(yc, wmix_ref[768:1024, :], preferred_element_type=F32)
            x = x + mod_ref[0, 5:6, :] * y
        xn = _norm_mod(x, g_ref[...], mod_ref, k).astype(BF16)
        acc = jnp.zeros((nrows, D_MODEL), F32)
        for lo, hi in FF_CHUNKS:
            g = jnp.dot(xn, wi_ref[:, lo:hi], preferred_element_type=F32)
            u = jnp.dot(xn, wi_ref[:, D_FF + lo:D_FF + hi], preferred_element_type=F32)
            a = (g * jax.nn.sigmoid(g) * u).astype(BF16)
            acc = acc + jnp.dot(a, wo_ref[lo:hi, :], preferred_element_type=F32)
        y = x + 0.5 * mod_ref[0, k + 2:k + 3, :] * acc
        if final:
            ys.append(_rms(y, gf_ref[...]))
        else:
            refs[-1][rows, :] = y
    if final:
        yc_ref, yl_ref = refs[-2:]
        y = jnp.concatenate(ys, axis=0)

        @pl.when(is_ctx)
        def _():
            yc_ref[...] = y

        @pl.when(jnp.logical_not(is_ctx))
        def _():
            yl_ref[...] = y


def _ctx_lat_specs(width):
    return [
        pl.BlockSpec((TM, width), lambda i: (jnp.minimum(i, CTX_TILES - 1), 0)),
        pl.BlockSpec((TM, width), lambda i: (jnp.maximum(i - CTX_TILES, 0), 0)),
    ]


def _mod_spec(l):
    return pl.BlockSpec((None, 1, 9, D_MODEL), lambda i: (l, _mod_row(i), 0, 0))


def _ffn(h, mods, g, wi, wo, gf, *, l, half, first=False, mix=None, final=False):
    k = 6 * half
    flat_spec = pl.BlockSpec((TM, D_MODEL), lambda i: (i, 0))
    flat_shape = jax.ShapeDtypeStruct((N_TOK, D_MODEL), F32)
    mix_specs, mix_args = [], []
    if mix is not None:
        mix_specs = (_ctx_lat_specs(512) + _ctx_lat_specs(256) + _ctx_lat_specs(256)
                     + [_const_spec((D_MODEL, D_MODEL), (l,))])
        mix_args = [*mix[0], *mix[1], *mix[2], mix[3]]
    return pl.pallas_call(
        functools.partial(_ffn_kernel, k=k, first=first, mix=mix is not None, final=final),
        grid=(N_TOK // TM,),
        in_specs=(_ctx_lat_specs(D_MODEL) if first else [flat_spec]) + mix_specs + [
            _mod_spec(l),
            _const_spec((1, D_MODEL), (l, 2 * half)),
            _const_spec((D_MODEL, 2 * D_FF), (l, half)),
            _const_spec((D_FF, D_MODEL), (l, half)),
            _const_spec((1, D_MODEL)),
        ],
        out_specs=_ctx_lat_specs(D_MODEL) if final else flat_spec,
        out_shape=[jax.ShapeDtypeStruct((N_CTX, D_MODEL), F32),
                   jax.ShapeDtypeStruct((N_LAT, D_MODEL), F32)] if final else flat_shape,
        compiler_params=_cparams(("arbitrary",)),
        name="ffn_final" if final else ("ffn_first" if first else ("ffn_mix" if mix else "ffn")),
    )(*(h if first else (h,)), *mix_args, mods, g, wi, wo, gf)


def _proj_in_kernel(h_ref, mod_ref, g_ref, w_ref, wt_ref, bt_ref, z_ref, zt_ref):
    xn = _norm_mod(h_ref[...], g_ref[...], mod_ref, 3).astype(BF16)
    z_ref[...] = jnp.dot(xn, w_ref[...], preferred_element_type=F32)
    zt = lax.dot_general(wt_ref[...], xn, (((1,), (1,)), ((), ())), preferred_element_type=F32)
    zt = zt + bt_ref[...]
    for c in range(TM // TC):
        zt_ref[c] = zt[:, c * TC:(c + 1) * TC]


def _proj_in(h, mods, g, wz, wt, bt, *, l):
    return pl.pallas_call(
        _proj_in_kernel,
        grid=(N_TOK // TM,),
        in_specs=[
            pl.BlockSpec((TM, D_MODEL), lambda i: (i, 0)),
            _mod_spec(l),
            _const_spec((1, D_MODEL), (l, 1)),
            _const_spec((D_MODEL, Z_W), (l,)),
            _const_spec((ZT_W, D_MODEL), (l,)),
            _const_spec((ZT_W, 1), (l,)),
        ],
        out_specs=[
            pl.BlockSpec((TM, Z_W), lambda i: (i, 0)),
            pl.BlockSpec((TM // TC, ZT_W, TC), lambda i: (i, 0, 0)),
        ],
        out_shape=[
            jax.ShapeDtypeStruct((N_TOK, Z_W), F32),
            jax.ShapeDtypeStruct((N_TOK // TC, ZT_W, TC), F32),
        ],
        compiler_params=_cparams(("arbitrary",)),
        name="proj_in",
    )(h, mods, g, wz, wt, bt)


def _swap_halves(x, width, axis):
    n = x.shape[axis]
    half = width // 2
    parts = []
    for s in range(0, n, width):
        parts.append(lax.slice_in_dim(x, s + half, s + width, axis=axis))
        parts.append(lax.slice_in_dim(x, s, s + half, axis=axis))
    return jnp.concatenate(parts, axis=axis)


def _attn_kernel(*refs, mode, seq, has_ctx, rope, emit_k, lam_init):
    it = iter(refs)
    qt_ref, k_ref, vt_ref = next(it), next(it), next(it)
    if has_ctx:
        ck_ref, cvt_ref = next(it), next(it)
    if rope:
        cqt_ref, sqt_ref, ck_tab, sk_tab = next(it), next(it), next(it), next(it)
    if mode == "A":
        gq_ref, gk_ref = next(it), next(it)
    else:
        lam_ref, gc_ref = next(it), next(it)
    y_ref = next(it)
    if emit_k:
        kn_ref = next(it)
    kbuf, vbuf, s_ref, p_ref = next(it), next(it), next(it), next(it)

    n_kv = KV_A if mode == "A" else H_C
    per_kv = 8 // n_kv
    n_ctx = PAST_LEN if has_ctx else 0
    lk = n_ctx + seq
    width = HD_A if mode == "A" else D_C

    @pl.when(pl.program_id(1) == 0)
    def _fill():
        ones_rows = (lax.broadcasted_iota(jnp.int32, (VROWS - 64, lk), 0) == 0).astype(BF16)
        for h in range(n_kv):
            hs = slice(h * 64, (h + 1) * 64)
            vbuf[h, 64:VROWS, :] = ones_rows
            if has_ctx:
                kbuf[h, 0:n_ctx, :] = ck_ref[0, :, hs].astype(BF16)
                vbuf[h, 0:64, 0:n_ctx] = cvt_ref[0, h].astype(BF16)
            for c in range(seq // TC):
                rows = slice(c * TC, (c + 1) * TC)
                k = k_ref[rows, hs]
                if mode == "A":
                    k = _rms(k, gk_ref[...])
                    if emit_k:
                        kn_ref[0, rows, hs] = k
                if rope:
                    k = k * ck_tab[rows, :] + _swap_halves(k, width, 1) * sk_tab[rows, :]
                kbuf[h, n_ctx + c * TC:n_ctx + (c + 1) * TC, :] = k.astype(BF16)
                vbuf[h, 0:64, n_ctx + c * TC:n_ctx + (c + 1) * TC] = vt_ref[c, hs, :].astype(BF16)

    qts = []
    if mode == "A":
        scale = HD_A ** -0.5 * LOG2E
        for hh in range(H_A):
            x = _rms(qt_ref[0, hh * 64:(hh + 1) * 64, :], gq_ref[...], axis=0)
            if rope:
                x = x * cqt_ref[0] + _swap_halves(x, width, 0) * sqt_ref[0]
            qts.append((x * scale).astype(BF16))
    else:
        scale = D_C ** -0.5 * LOG2E
        first_map = lax.broadcasted_iota(jnp.int32, (64, TQ), 0) < D_C
        for h in range(H_C):
            x = qt_ref[0, h * 64:(h + 1) * 64, :]
            if rope:
                x = x * cqt_ref[0] + _swap_halves(x, width, 0) * sqt_ref[0]
            x = x * scale
            qts.append(jnp.where(first_map, x, 0.0).astype(BF16))
            qts.append(jnp.where(first_map, 0.0, x).astype(BF16))

    m_prev = None
    outs = []
    if lk == SB:
        sts = [jnp.dot(kbuf[u // per_kv], qts[u], preferred_element_type=F32) for u in range(8)]
        ps = [jnp.exp2(st - jnp.max(st, axis=0, keepdims=True)).astype(BF16) for st in sts]
        for u in range(8):
            ox = jnp.dot(vbuf[u // per_kv], ps[u], preferred_element_type=F32)
            outs.append(ox[0:64] / ox[64:65])
    for s in range(9 if lk > SB else 0):
        mrun = jnp.full((8, TQ), -jnp.inf, F32)
        for j in range(lk // SB):
            rows = slice(j * SB, (j + 1) * SB)
            if s < 8:
                st = jnp.dot(kbuf[s // per_kv, rows, :], qts[s], preferred_element_type=F32)
                s_ref[s % 2, rows, :] = st
                for r in range(SB // 8):
                    mrun = jnp.maximum(mrun, st[8 * r:8 * r + 8])
            if s >= 1:
                prev = s_ref[(s - 1) % 2, rows, :]
                p_ref[rows, (s - 1) * TQ:s * TQ] = jnp.exp2(prev - m_prev).astype(BF16)
        m_prev = jnp.max(mrun, axis=0, keepdims=True)
        if s >= 1:
            u = s - 1
            ox = jnp.dot(vbuf[u // per_kv], p_ref[:, u * TQ:(u + 1) * TQ], preferred_element_type=F32)
            outs.append(ox[0:64] / ox[64:65])

    if mode == "A":
        for pair in range(4):
            o = jnp.concatenate(outs[2 * pair:2 * pair + 2], axis=0)
            y_ref[:, pair * 128:(pair + 1) * 128] = o.T.astype(BF16)
    else:
        lp = lam_ref[...]
        lam = (jnp.exp(jnp.sum(lp[0:1] * lp[1:2], axis=1, keepdims=True))
               - jnp.exp(jnp.sum(lp[2:3] * lp[3:4], axis=1, keepdims=True)) + lam_init)
        for pair in range(2):
            o = []
            for h in (2 * pair, 2 * pair + 1):
                d = outs[2 * h] - lam * outs[2 * h + 1]
                o.append(_rms(d, gc_ref[...], axis=0) * (1.0 - lam_init))
            y_ref[:, pair * 128:(pair + 1) * 128] = jnp.concatenate(o, axis=0).T.astype(BF16)


def _attn(mode, z, zt, *, ctx, tables, params, l, lam_init, latent):
    if latent:
        nb, seq, row0 = DEC_BATCH, DEC_SEQ, N_CTX
    else:
        nb, seq, row0 = BATCH, SEQ, 0
    nq = seq // TQ
    n_kv = KV_A if mode == "A" else H_C
    n_ctx = PAST_LEN if ctx is not None else 0
    lk = n_ctx + seq
    qw = 512 if mode == "A" else 256
    vw = n_kv * 64
    q_blk = (ZT_QA if mode == "A" else ZT_QC) // qw
    k_blk = (Z_KA if mode == "A" else Z_KC) // vw
    v_blk = (ZT_VA if mode == "A" else ZT_VC) // vw
    emit_k = mode == "A" and not latent
    blk0 = row0 // seq

    in_specs = [
        pl.BlockSpec((1, qw, TC), lambda b, i: (row0 // TC + b * nq + i, q_blk, 0)),
        pl.BlockSpec((seq, vw), lambda b, i: (blk0 + b, k_blk)),
        pl.BlockSpec((seq // TC, vw, TC), lambda b, i: (blk0 + b, v_blk, 0)),
    ]
    args = [zt, z, zt]
    if ctx is not None:
        in_specs += [
            pl.BlockSpec((1, None, PAST_LEN, vw), lambda b, i: (b, l, 0, 0)),
            pl.BlockSpec((1, None, n_kv, 64, PAST_LEN), lambda b, i: (b, l, 0, 0, 0)),
        ]
        args += list(ctx)
    if tables is not None:
        cq, sq, cqt, sqt = tables
        in_specs += [
            pl.BlockSpec((1, 64, TC), lambda b, i: (i, 0, 0)),
            pl.BlockSpec((1, 64, TC), lambda b, i: (i, 0, 0)),
            _const_spec((seq, 64)),
            _const_spec((seq, 64)),
        ]
        args += [cqt, sqt, cq, sq]
    for p in params:
        in_specs.append(_const_spec(p.shape[1:], (l,)))
        args.append(p)

    out_specs = [pl.BlockSpec((TQ, qw), lambda b, i: (b * nq + i, 0))]
    out_shape = [jax.ShapeDtypeStruct((nb * seq, qw), BF16)]
    if emit_k:
        out_specs.append(pl.BlockSpec((1, seq, vw), lambda b, i: (b, 0, 0)))
        out_shape.append(jax.ShapeDtypeStruct((nb, seq, vw), F32))

    return pl.pallas_call(
        functools.partial(_attn_kernel, mode=mode, seq=seq, has_ctx=ctx is not None,
                          rope=tables is not None, emit_k=emit_k, lam_init=lam_init),
        grid=(nb, nq),
        in_specs=in_specs,
        out_specs=out_specs,
        out_shape=out_shape,
        scratch_shapes=[
            pltpu.VMEM((n_kv, lk, 64), BF16),
            pltpu.VMEM((n_kv, VROWS, lk), BF16),
            pltpu.VMEM((2, lk, TQ), F32),
            pltpu.VMEM((lk, 8 * TQ), BF16),
        ],
        compiler_params=_cparams(("arbitrary", "arbitrary")),
        name=f"attn_{mode}_{'lat' if latent else 'ctx'}",
    )(*args)


N_CHUNKS = N_TOK // TC
GS_A, GS_B, GS_LM, GS_BLAST, GS_ALAST, GS_ROWS = 0, 8, 16, 24, 32, 40


def _scan_lanes(x, op, fill, reverse):
    axis = x.ndim - 1
    lane = lax.broadcasted_iota(jnp.int32, x.shape, axis)
    k = 1
    while k < TC:
        if reverse:
            x = op(x, jnp.where(lane < TC - k, pltpu.roll(x, TC - k, axis), fill))
        else:
            x = op(x, jnp.where(lane >= k, pltpu.roll(x, k, axis), fill))
        k *= 2
    return x


def _gate_stats_kernel(g_ref, st_ref, ac_ref):
    is_fwd = lax.broadcasted_iota(jnp.int32, (N_CHUNKS, 8, TC), 1) < H_B
    g = g_ref[...]
    gi = jnp.concatenate([g[:, 0:4], g[:, 8:12]], axis=1)
    f = _log_sigmoid(jnp.concatenate([g[:, 4:8], g[:, 12:16]], axis=1))
    b = jnp.where(is_fwd, _scan_lanes(f, jnp.add, 0.0, False), _scan_lanes(f, jnp.add, 0.0, True))
    a = gi - b
    lm = jnp.where(is_fwd, _scan_lanes(a, jnp.maximum, -jnp.inf, False),
                   _scan_lanes(a, jnp.maximum, -jnp.inf, True))
    st_ref[:, GS_A:GS_A + 8, :] = a
    st_ref[:, GS_B:GS_B + 8, :] = b
    st_ref[:, GS_LM:GS_LM + 8, :] = lm
    st_ref[:, GS_BLAST:GS_BLAST + 8, :] = jnp.broadcast_to(jnp.sum(f, axis=2, keepdims=True), a.shape)
    st_ref[:, GS_ALAST:GS_ALAST + 8, :] = jnp.broadcast_to(jnp.max(a, axis=2, keepdims=True), a.shape)

    def transpose_chunk(c, carry):
        ac_ref[c] = jnp.concatenate([st_ref[c, GS_A:GS_A + 8, :], jnp.zeros((120, TC), F32)], axis=0).T
        return carry

    lax.fori_loop(0, N_CHUNKS, transpose_chunk, 0, unroll=4)


def _gate_stats(zt):
    return pl.pallas_call(
        _gate_stats_kernel,
        grid=(1,),
        in_specs=[pl.BlockSpec((N_CHUNKS, 16, TC), lambda i: (0, ZT_G // 16, 0))],
        out_specs=[
            pl.BlockSpec((N_CHUNKS, GS_ROWS, TC), lambda i: (0, 0, 0)),
            pl.BlockSpec((N_CHUNKS, TC, 128), lambda i: (0, 0, 0)),
        ],
        out_shape=[
            jax.ShapeDtypeStruct((N_CHUNKS, GS_ROWS, TC), F32),
            jax.ShapeDtypeStruct((N_CHUNKS, TC, 128), F32),
        ],
        compiler_params=_cparams(("arbitrary",)),
        name="gate_stats",
    )(zt)


def _mlstm_kernel(k_ref, qt_ref, vt_ref, ot_ref, st_ref, ac_ref, s0_ref, m0_ref, gb_ref,
                  y_ref, s_ref, m_ref, hf_ref, hr_ref, *, seq):
    nc = seq // TC
    s_idx = lax.broadcasted_iota(jnp.int32, (TC, TC), 0)
    t_idx = lax.broadcasted_iota(jnp.int32, (TC, TC), 1)
    seen = (s_idx <= t_idx, s_idx >= t_idx)
    ones_rows = (lax.broadcasted_iota(jnp.int32, (64, TC), 0) == 0).astype(F32)
    units = [(d, h) for d in range(2) for h in range(H_B)]

    def chunk(j, carry):
        states, m = carry
        cs = (j, nc - 1 - j)
        stat = lambda r: jnp.concatenate([st_ref[cs[0], r:r + 4, :], st_ref[cs[1], r + 4:r + 8, :]], axis=0)
        a, b, lm = stat(GS_A), stat(GS_B), stat(GS_LM)
        mx = jnp.maximum(m, lm)
        w = jnp.exp(m - mx)
        floor = jnp.exp(-(b + mx))
        mxl = jnp.maximum(m, stat(GS_ALAST)[:, 0:1])
        decay = jnp.exp(m - mxl)
        m_new = stat(GS_BLAST)[:, 0:1] + mxl

        first = []
        for (d, h), st in zip(units, states):
            hs = slice(h * 64, (h + 1) * 64)
            kh = k_ref[_ds(cs[d], TC), hs] * (DK_B ** -0.5)
            qt = qt_ref[cs[d], hs, :].astype(BF16)
            lhs = jnp.concatenate([kh.astype(BF16), st.astype(BF16)], axis=0)
            first.append((kh, jnp.dot(lhs, qt, preferred_element_type=F32)))
        second = []
        for u, ((d, h), (kh, both)) in enumerate(zip(units, first)):
            a_col = ac_ref[cs[d], :, u:u + 1]
            dt = jnp.where(seen[d], a_col - mx[u:u + 1, :], -jnp.inf)
            sp = both[0:TC] * jnp.exp(dt)
            den = jnp.sum(sp, axis=0, keepdims=True) + w[u:u + 1, :] * both[TC + 64:TC + 65]
            wk = jnp.exp(a_col - mxl[u:u + 1, :])
            second.append((sp.astype(BF16), (kh * wk).astype(BF16), den))
        new_states = []
        for u, ((d, h), st, (kh, both), (spb, kw, den)) in enumerate(zip(units, states, first, second)):
            hs = slice(h * 64, (h + 1) * 64)
            vx = jnp.concatenate([vt_ref[cs[d], hs, :], ones_rows], axis=0).astype(BF16)
            num = jnp.dot(vx, spb, preferred_element_type=F32)[0:64] + w[u:u + 1, :] * both[TC:TC + 64]
            hv = num / jnp.maximum(jnp.abs(den), floor[u:u + 1, :])
            if d == 0:
                hf_ref[cs[d], hs, :] = hv
            else:
                hr_ref[cs[d], hs, :] = hv
            new_states.append(decay[u:u + 1, :] * st + jnp.dot(vx, kw, preferred_element_type=F32))
        return tuple(new_states), m_new

    init = (tuple(s0_ref[0, u] for u in range(2 * H_B)), m0_ref[0][:, 0:1])
    if nc == 1:
        states, m = chunk(0, init)
    else:
        states, m = lax.fori_loop(0, nc, chunk, init)
    for u, st in enumerate(states):
        s_ref[0, u] = st
    m_ref[0] = jnp.broadcast_to(m, (2 * H_B, 128))

    def epilogue(c, carry):
        ys = []
        for h in range(H_B):
            hs = slice(h * 64, (h + 1) * 64)
            x = hf_ref[c, hs, :] + hr_ref[c, hs, :]
            ys.append(_rms(x, gb_ref[...], axis=0) * jax.nn.sigmoid(ot_ref[c, hs, :]))
        y_ref[_ds(c, TC), :] = jnp.concatenate(ys, axis=0).T.astype(BF16)
        return carry

    if nc == 1:
        epilogue(0, 0)
    else:
        lax.fori_loop(0, nc, epilogue, 0)


def _mlstm(z, zt, stats, acols, s0, m0, gb, *, l, latent):
    ls = min(l, s0.shape[1] - 1)
    if latent:
        nb, seq, row0 = DEC_BATCH, DEC_SEQ, N_CTX
    else:
        nb, seq, row0 = BATCH, SEQ, 0
    blk0 = row0 // seq
    nc = seq // TC
    ztspec = lambda rb: pl.BlockSpec((nc, 256, TC), lambda b: (blk0 + b, rb, 0))
    return pl.pallas_call(
        functools.partial(_mlstm_kernel, seq=seq),
        grid=(nb,),
        in_specs=[
            pl.BlockSpec((seq, 256), lambda b: (blk0 + b, Z_KB // 256)),
            ztspec(ZT_QB // 256), ztspec(ZT_VB // 256), ztspec(ZT_OB // 256),
            pl.BlockSpec((nc, GS_ROWS, TC), lambda b: (blk0 + b, 0, 0)),
            pl.BlockSpec((nc, TC, 128), lambda b: (blk0 + b, 0, 0)),
            pl.BlockSpec((1, None, 2 * H_B, 128, 64), lambda b: (b, ls, 0, 0, 0)),
            pl.BlockSpec((1, None, 2 * H_B, 128), lambda b: (b, ls, 0, 0)),
            _const_spec((64, 1), (l,)),
        ],
        out_specs=[
            pl.BlockSpec((seq, 256), lambda b: (b, 0)),
            pl.BlockSpec((1, 2 * H_B, 128, 64), lambda b: (b, 0, 0, 0)),
            pl.BlockSpec((1, 2 * H_B, 128), lambda b: (b, 0, 0)),
        ],
        out_shape=[
            jax.ShapeDtypeStruct((nb * seq, 256), BF16),
            jax.ShapeDtypeStruct((nb, 2 * H_B, 128, 64), F32),
            jax.ShapeDtypeStruct((nb, 2 * H_B, 128), F32),
        ],
        scratch_shapes=[pltpu.VMEM((nc, 256, TC), F32), pltpu.VMEM((nc, 256, TC), F32)],
        compiler_params=_cparams(("arbitrary",)),
        name=f"mlstm_{'lat' if latent else 'ctx'}",
    )(z, zt, zt, zt, stats, acols, s0, m0, gb)


def _rope_tables(dim):
    t = jnp.arange(DEC_SEQ)
    row = (t // GRID_W).astype(F32)
    colp = (t % GRID_W).astype(F32)
    axis_dim = dim // 2
    freqs = ROPE_THETA ** (-jnp.arange(0, axis_dim, 2, dtype=F32) / axis_dim)
    ang = jnp.concatenate([row[:, None] * freqs, colp[:, None] * freqs], axis=-1)
    cos, sin = jnp.cos(ang), jnp.sin(ang)
    reps = 64 // dim
    cq = jnp.concatenate([cos, cos] * reps, axis=1)
    sq = jnp.concatenate([-sin, sin] * reps, axis=1)
    to_chunks = lambda x: x.T.reshape(64, DEC_SEQ // TC, TC).transpose(1, 0, 2)
    return cq, sq, to_chunks(cq), to_chunks(sq)


def _split_w_in(w, b_gates):
    a0, b0, c0 = 0, 768, 1808
    qa, ka, va = w[..., a0:a0 + 512], w[..., a0 + 512:a0 + 640], w[..., a0 + 640:a0 + 768]
    qb, kb = w[..., b0:b0 + 256], w[..., b0 + 256:b0 + 512]
    vb, ob = w[..., b0 + 512:b0 + 768], w[..., b0 + 768:b0 + 1024]
    gb = w[..., b0 + 1024:b0 + 1040]
    qc, kc, vc = w[..., c0:c0 + 256], w[..., c0 + 256:c0 + 512], w[..., c0 + 512:c0 + 768]
    wz = jnp.concatenate([kb, kc, ka], axis=-1)
    wt = jnp.swapaxes(jnp.concatenate([qa, qc, vc, qb, vb, ob, va, gb], axis=-1), -1, -2)
    bt = jnp.zeros((DEPTH, ZT_W, 1), F32).at[:, ZT_G:ZT_G + 16, 0].set(b_gates)
    return wz.astype(BF16), wt.astype(BF16), bt


def _tokens_major(x):
    return x.transpose(0, 2, 1)


def kernel(x_prompt, x_sample, cache_a_k, cache_a_v, cache_c_k, cache_c_v, state_b_C, state_b_n,
           state_b_m, c, c_ctx, w_ada, b_ada, g_norm, w_ff_in, w_ff_out, w_in, w_out, g_qa, g_ka,
           b_gates, g_b, lam_q1, lam_k1, lam_q2, lam_k2, g_c, g_final):
    h = (x_prompt.reshape(N_CTX, D_MODEL), x_sample.reshape(N_LAT, D_MODEL))
    cc = jnp.concatenate([c_ctx[None, :], c, jnp.zeros((3, D_MODEL), F32)], axis=0)
    mods = _ada(cc, w_ada, b_ada).reshape(DEPTH, 8, 9, D_MODEL)
    tab_a = _rope_tables(HD_A)
    tab_c = _rope_tables(D_C)

    gf = g_final.reshape(1, D_MODEL)
    gn = g_norm.reshape(DEPTH, 3, 1, D_MODEL)
    wi, wo, wmix = w_ff_in.astype(BF16), w_ff_out.astype(BF16), w_out.astype(BF16)
    wz, wt, bt = _split_w_in(w_in, b_gates)
    par_a = (g_qa.reshape(DEPTH, 64, 1), g_ka.reshape(DEPTH, 1, 64))
    par_c = (jnp.stack([lam_q1, lam_k1, lam_q2, lam_k2], axis=1), g_c.reshape(DEPTH, 64, 1))
    gb = g_b.reshape(DEPTH, 64, 1)
    ctx_a = (cache_a_k.reshape(DEC_BATCH, DEPTH, PAST_LEN, KV_A * HD_A), cache_a_v.transpose(0, 1, 3, 4, 2))
    ctx_c = (cache_c_k.reshape(DEC_BATCH, DEPTH, PAST_LEN, H_C * 2 * D_C), cache_c_v.transpose(0, 1, 3, 4, 2))
    s0 = jnp.concatenate([jnp.swapaxes(state_b_C, -1, -2), state_b_n[..., None, :],
                          jnp.zeros((DEC_BATCH, DEPTH, 2, H_B, 63, DK_B), F32)], axis=-2)
    s0 = s0.reshape(DEC_BATCH, DEPTH, 2 * H_B, 128, DK_B)
    m0 = jnp.broadcast_to(state_b_m.reshape(DEC_BATCH, DEPTH, 2 * H_B, 1), (DEC_BATCH, DEPTH, 2 * H_B, 128))
    zeros_s = jnp.zeros((BATCH, 1, 2 * H_B, 128, DK_B), F32)
    zeros_m = jnp.zeros((BATCH, 1, 2 * H_B, 128), F32)

    outs = {k: [] for k in ("ak", "av", "ck", "cv", "bC", "bn", "bm")}
    for l in range(DEPTH):
        lam_init = 0.8 - 0.6 * math.exp(-0.3 * l)
        h = _ffn(h, mods, gn, wi, wo, gf, l=l, half=0, first=(l == 0))
        z, zt = _proj_in(h, mods, gn, wz, wt, bt, l=l)

        ya_c, kn = _attn("A", z, zt, ctx=None, tables=None, params=par_a, l=l, lam_init=lam_init,
                         latent=False)
        (yc_c,) = _attn("C", z, zt, ctx=None, tables=None, params=par_c, l=l, lam_init=lam_init,
                        latent=False)
        stats, acols = _gate_stats(zt)
        yb_c, s_fin, m_fin = _mlstm(z, zt, stats, acols, zeros_s, zeros_m, gb, l=l, latent=False)

        (ya_l,) = _attn("A", z, zt, ctx=ctx_a, tables=tab_a, params=par_a, l=l, lam_init=lam_init,
                        latent=True)
        (yc_l,) = _attn("C", z, zt, ctx=ctx_c, tables=tab_c, params=par_c, l=l, lam_init=lam_init,
                        latent=True)
        yb_l, _, _ = _mlstm(z, zt, stats, acols, s0, m0, gb, l=l, latent=True)

        h = _ffn(h, mods, gn, wi, wo, gf, l=l, half=1,
                 mix=((ya_c, ya_l), (yb_c, yb_l), (yc_c, yc_l), wmix), final=(l == DEPTH - 1))

        zc = z[:N_CTX]
        ztc = zt[:N_CTX // TC]
        outs["ak"].append(kn.reshape(BATCH, SEQ, KV_A, HD_A))
        outs["av"].append(_tokens_major(ztc[:, ZT_VA:ZT_VA + 128, :]).reshape(BATCH, SEQ, KV_A, HD_A))
        outs["ck"].append(zc[:, Z_KC:Z_KC + 256].reshape(BATCH, SEQ, H_C, 2, D_C))
        outs["cv"].append(_tokens_major(ztc[:, ZT_VC:ZT_VC + 256, :]).reshape(BATCH, SEQ, H_C, 2 * D_C))
        s_fin = s_fin.reshape(BATCH, 2, H_B, 128, DK_B)
        outs["bC"].append(jnp.swapaxes(s_fin[..., 0:64, :], -1, -2))
        outs["bn"].append(s_fin[..., 64, :])
        outs["bm"].append(m_fin[:, :, 0].reshape(BATCH, 2, H_B))

    y_prompt = h[0].reshape(BATCH, SEQ, D_MODEL)
    y_sample = h[1].reshape(DEC_BATCH, DEC_SEQ, D_MODEL)
    st = lambda k: jnp.stack(outs[k], axis=1)
    return (y_prompt, y_sample, st("ak"), st("av"), st("ck"), st("cv"), st("bC"), st("bn"), st("bm"))
```

```python
import functools
import math

import jax
import jax.numpy as jnp
from jax import lax
from jax.experimental import pallas as pl
from jax.experimental.pallas import tpu as pltpu

F32 = jnp.float32
BF16 = jnp.bfloat16

D_MODEL = 1024
BATCH = 16
SEQ = 256
DEPTH = 2
DEC_BATCH = 4
DEC_SEQ = 2048
PAST_LEN = 512
GRID_W = 64
ROPE_THETA = 10000.0
EPS = 1e-6
HD_A = 64
H_A = 8
KV_A = 2
DK_B = 64
H_B = 4
D_C = 32
H_C = 4
D_FF = 2816
LOG2E = 1.4426950408889634

N_CTX = BATCH * SEQ
N_LAT = DEC_BATCH * DEC_SEQ
N_TOK = N_CTX + N_LAT

TM = 512
CTX_TILES = N_CTX // TM
LAT_TILES_PER_BATCH = DEC_SEQ // TM
FF_CHUNKS = ((0, 1536), (1536, D_FF))
FF_ROWS = 256
TC = 256
SB = 256
TQ = 256
VROWS = 80
BOUND_SLACK = 1.0 + 2.0 ** -6
MAX_BOUND_GAP = 60.0

Z_KB, Z_KC, Z_KA = 0, 256, 512
Z_W = 640
ZT_QA, ZT_QC, ZT_VC, ZT_QB, ZT_VB, ZT_OB, ZT_VA, ZT_G = 0, 512, 768, 1024, 1280, 1536, 1792, 1920
ZT_W = 1936

VMEM_LIMIT = 56 * 1024 * 1024


def _cparams(sem):
    return pltpu.CompilerParams(dimension_semantics=sem, vmem_limit_bytes=VMEM_LIMIT)


def _const_spec(shape, lead=()):
    nd = len(shape)
    return pl.BlockSpec((None,) * len(lead) + tuple(shape), lambda *_: tuple(lead) + (0,) * nd,
                        pipeline_mode=pl.Buffered(1))


def _mod_row(i):
    return jnp.maximum(i - CTX_TILES, 0) // LAT_TILES_PER_BATCH + (i >= CTX_TILES).astype(jnp.int32)


def _rms(x, g, axis=-1):
    ms = jnp.mean(x * x, axis=axis, keepdims=True)
    return x * lax.rsqrt(ms + EPS) * g


def _norm_mod(x, g, mod_ref, k):
    shift = mod_ref[0, k:k + 1, :]
    scale = mod_ref[0, k + 1:k + 2, :]
    return _rms(x, g) * (1.0 + scale) + shift


def _ds(c, n):
    if isinstance(c, int):
        return pl.ds(c * n, n)
    return pl.ds(pl.multiple_of(c * n, n), n)


def _log_sigmoid(x):
    return jnp.minimum(x, 0.0) - jnp.log1p(jnp.exp(-jnp.abs(x)))


def _ada_kernel(c_ref, w_ref, b_ref, o_ref):
    c = c_ref[...]
    a = (c * jax.nn.sigmoid(c)).astype(BF16)
    o_ref[0] = jnp.dot(a, w_ref[0].astype(BF16), preferred_element_type=F32) + b_ref[0]


def _ada(cc, w_ada, b_ada):
    tn = 1024
    n = 9 * D_MODEL
    return pl.pallas_call(
        _ada_kernel,
        grid=(DEPTH, n // tn),
        in_specs=[
            pl.BlockSpec((8, D_MODEL), lambda l, j: (0, 0)),
            pl.BlockSpec((1, D_MODEL, tn), lambda l, j: (l, 0, j)),
            pl.BlockSpec((1, 1, tn), lambda l, j: (l, 0, j)),
        ],
        out_specs=pl.BlockSpec((1, 8, tn), lambda l, j: (l, 0, j)),
        out_shape=jax.ShapeDtypeStruct((DEPTH, 8, n), F32),
        compiler_params=_cparams(("arbitrary", "arbitrary")),
        name="ada",
    )(cc, w_ada, b_ada.reshape(DEPTH, 1, n))


def _ffn_kernel(*refs, k, first, mix, final):
    is_ctx = pl.program_id(0) < CTX_TILES
    it = iter(refs)
    x_refs = (next(it), next(it)) if first else (next(it),)
    if mix:
        y_refs = [(next(it), next(it)) for _ in range(3)]
        wmix_ref = next(it)
    mod_ref, g_ref, wi_ref, wo_ref, gf_ref = (next(it) for _ in range(5))

    nrows = TM if mix else FF_ROWS
    ys = []
    for r in range(TM // nrows):
        rows = slice(r * nrows, (r + 1) * nrows)
        pick = lambda c_ref, l_ref: jnp.where(is_ctx, c_ref[rows, :], l_ref[rows, :])
        x = pick(*x_refs) if first else x_refs[0][rows, :]
        if mix:
            ya, yb, yc = (pick(*pair) for pair in y_refs)
            y = jnp.dot(ya, wmix_ref[0:512, :], preferred_element_type=F32)
            y = y + jnp.dot(yb, wmix_ref[512:768, :], preferred_element_type=F32)
            y = y + jnp.dot(yc, wmix_ref[768:1024, :], preferred_element_type=F32)
            x = x + mod_ref[0, 5:6, :] * y
        xn = _norm_mod(x, g_ref[...], mod_ref, k).astype(BF16)
        acc = jnp.zeros((nrows, D_MODEL), F32)
        for lo, hi in FF_CHUNKS:
            g = jnp.dot(xn, wi_ref[:, lo:hi], preferred_element_type=F32)
            u = jnp.dot(xn, wi_ref[:, D_FF + lo:D_FF + hi], preferred_element_type=F32)
            a = (g * jax.nn.sigmoid(g) * u).astype(BF16)
            acc = acc + jnp.dot(a, wo_ref[lo:hi, :], preferred_element_type=F32)
        y = x + 0.5 * mod_ref[0, k + 2:k + 3, :] * acc
        if final:
            ys.append(_rms(y, gf_ref[...]))
        else:
            refs[-1][rows, :] = y
    if final:
        yc_ref, yl_ref = refs[-2:]
        y = jnp.concatenate(ys, axis=0)

        @pl.when(is_ctx)
        def _():
            yc_ref[...] = y

        @pl.when(jnp.logical_not(is_ctx))
        def _():
            yl_ref[...] = y


def _ctx_lat_specs(width):
    return [
        pl.BlockSpec((TM, width), lambda i: (jnp.minimum(i, CTX_TILES - 1), 0)),
        pl.BlockSpec((TM, width), lambda i: (jnp.maximum(i - CTX_TILES, 0), 0)),
    ]


def _mod_spec(l):
    return pl.BlockSpec((None, 1, 9, D_MODEL), lambda i: (l, _mod_row(i), 0, 0))


def _ffn(h, mods, g, wi, wo, gf, *, l, half, first=False, mix=None, final=False):
    k = 6 * half
    flat_spec = pl.BlockSpec((TM, D_MODEL), lambda i: (i, 0))
    flat_shape = jax.ShapeDtypeStruct((N_TOK, D_MODEL), F32)
    mix_specs, mix_args = [], []
    if mix is not None:
        mix_specs = (_ctx_lat_specs(512) + _ctx_lat_specs(256) + _ctx_lat_specs(256)
                     + [_const_spec((D_MODEL, D_MODEL), (l,))])
        mix_args = [*mix[0], *mix[1], *mix[2], mix[3]]
    return pl.pallas_call(
        functools.partial(_ffn_kernel, k=k, first=first, mix=mix is not None, final=final),
        grid=(N_TOK // TM,),
        in_specs=(_ctx_lat_specs(D_MODEL) if first else [flat_spec]) + mix_specs + [
            _mod_spec(l),
            _const_spec((1, D_MODEL), (l, 2 * half)),
            _const_spec((D_MODEL, 2 * D_FF), (l, half)),
            _const_spec((D_FF, D_MODEL), (l, half)),
            _const_spec((1, D_MODEL)),
        ],
        out_specs=_ctx_lat_specs(D_MODEL) if final else flat_spec,
        out_shape=[jax.ShapeDtypeStruct((N_CTX, D_MODEL), F32),
                   jax.ShapeDtypeStruct((N_LAT, D_MODEL), F32)] if final else flat_shape,
        compiler_params=_cparams(("arbitrary",)),
        name="ffn_final" if final else ("ffn_first" if first else ("ffn_mix" if mix else "ffn")),
    )(*(h if first else (h,)), *mix_args, mods, g, wi, wo, gf)


def _proj_in_kernel(h_ref, mod_ref, g_ref, w_ref, wt_ref, bt_ref, z_ref, zt_ref):
    xn = _norm_mod(h_ref[...], g_ref[...], mod_ref, 3).astype(BF16)
    z_ref[...] = jnp.dot(xn, w_ref[...], preferred_element_type=F32)
    zt = lax.dot_general(wt_ref[...], xn, (((1,), (1,)), ((), ())), preferred_element_type=F32)
    zt = zt + bt_ref[...]
    for c in range(TM // TC):
        zt_ref[c] = zt[:, c * TC:(c + 1) * TC]


def _proj_in(h, mods, g, wz, wt, bt, *, l):
    return pl.pallas_call(
        _proj_in_kernel,
        grid=(N_TOK // TM,),
        in_specs=[
            pl.BlockSpec((TM, D_MODEL), lambda i: (i, 0)),
            _mod_spec(l),
            _const_spec((1, D_MODEL), (l, 1)),
            _const_spec((D_MODEL, Z_W), (l,)),
            _const_spec((ZT_W, D_MODEL), (l,)),
            _const_spec((ZT_W, 1), (l,)),
        ],
        out_specs=[
            pl.BlockSpec((TM, Z_W), lambda i: (i, 0)),
            pl.BlockSpec((TM // TC, ZT_W, TC), lambda i: (i, 0, 0)),
        ],
        out_shape=[
            jax.ShapeDtypeStruct((N_TOK, Z_W), F32),
            jax.ShapeDtypeStruct((N_TOK // TC, ZT_W, TC), F32),
        ],
        compiler_params=_cparams(("arbitrary",)),
        name="proj_in",
    )(h, mods, g, wz, wt, bt)


def _swap_halves(x, width, axis):
    n = x.shape[axis]
    half = width // 2
    parts = []
    for s in range(0, n, width):
        parts.append(lax.slice_in_dim(x, s + half, s + width, axis=axis))
        parts.append(lax.slice_in_dim(x, s, s + half, axis=axis))
    return jnp.concatenate(parts, axis=axis)


def _attn_kernel(*refs, mode, seq, has_ctx, rope, emit_k, lam_init):
    it = iter(refs)
    qt_ref, k_ref, vt_ref = next(it), next(it), next(it)
    if has_ctx:
        ck_ref, cvt_ref = next(it), next(it)
    if rope:
        cqt_ref, sqt_ref, ck_tab, sk_tab = next(it), next(it), next(it), next(it)
    if mode == "A":
        gq_ref, gk_ref = next(it), next(it)
    else:
        lam_ref, gc_ref = next(it), next(it)
    y_ref = next(it)
    if emit_k:
        kn_ref = next(it)
    kbuf, vbuf, p_ref, kmax_ref = next(it), next(it), next(it), next(it)

    n_kv = KV_A if mode == "A" else H_C
    per_kv = 8 // n_kv
    n_ctx = PAST_LEN if has_ctx else 0
    lk = n_ctx + seq
    width = HD_A if mode == "A" else D_C

    @pl.when(pl.program_id(1) == 0)
    def _fill():
        ones_rows = (lax.broadcasted_iota(jnp.int32, (VROWS - 64, lk), 0) == 0).astype(BF16)
        ones_cols = jnp.ones((64, 128), BF16)

        def max_sq_norm(kb):
            if lk == SB:
                return jnp.zeros((1, 1), F32)
            kf = kb.astype(F32)
            sq = jnp.dot((kf * kf).astype(BF16), ones_cols, preferred_element_type=F32)
            return jnp.max(sq, axis=0, keepdims=True)[:, 0:1]

        for h in range(n_kv):
            hs = slice(h * 64, (h + 1) * 64)
            vbuf[h, 64:VROWS, :] = ones_rows
            kmax2 = jnp.zeros((1, 1), F32)
            if has_ctx:
                kb = ck_ref[0, :, hs].astype(BF16)
                kmax2 = jnp.maximum(kmax2, max_sq_norm(kb))
                kbuf[h, 0:n_ctx, :] = kb
                vbuf[h, 0:64, 0:n_ctx] = cvt_ref[0, h].astype(BF16)
            for c in range(seq // TC):
                rows = slice(c * TC, (c + 1) * TC)
                k = k_ref[rows, hs]
                if mode == "A":
                    k = _rms(k, gk_ref[...])
                    if emit_k:
                        kn_ref[0, rows, hs] = k
                if rope:
                    k = k * ck_tab[rows, :] + _swap_halves(k, width, 1) * sk_tab[rows, :]
                kb = k.astype(BF16)
                kmax2 = jnp.maximum(kmax2, max_sq_norm(kb))
                kbuf[h, n_ctx + c * TC:n_ctx + (c + 1) * TC, :] = kb
                vbuf[h, 0:64, n_ctx + c * TC:n_ctx + (c + 1) * TC] = vt_ref[c, hs, :].astype(BF16)
            kmax_ref[h] = jnp.broadcast_to(jnp.sqrt(kmax2), (8, 128))

    qts = []
    if mode == "A":
        scale = HD_A ** -0.5 * LOG2E
        for hh in range(H_A):
            x = _rms(qt_ref[0, hh * 64:(hh + 1) * 64, :], gq_ref[...], axis=0)
            if rope:
                x = x * cqt_ref[0] + _swap_halves(x, width, 0) * sqt_ref[0]
            qts.append((x * scale).astype(BF16))
    else:
        scale = D_C ** -0.5 * LOG2E
        first_map = lax.broadcasted_iota(jnp.int32, (64, TQ), 0) < D_C
        for h in range(H_C):
            x = qt_ref[0, h * 64:(h + 1) * 64, :]
            if rope:
                x = x * cqt_ref[0] + _swap_halves(x, width, 0) * sqt_ref[0]
            x = x * scale
            qts.append(jnp.where(first_map, x, 0.0).astype(BF16))
            qts.append(jnp.where(first_map, 0.0, x).astype(BF16))

    def value_matmul(u):
        ox = jnp.dot(vbuf[u // per_kv], p_ref[:, u * TQ:(u + 1) * TQ], preferred_element_type=F32)
        return ox[0:64] / ox[64:65]

    def emit(outs):
        if mode == "A":
            for pair in range(4):
                o = jnp.concatenate(outs[2 * pair:2 * pair + 2], axis=0)
                y_ref[:, pair * 128:(pair + 1) * 128] = o.T.astype(BF16)
        else:
            lp = lam_ref[...]
            lam = (jnp.exp(jnp.sum(lp[0:1] * lp[1:2], axis=1, keepdims=True))
                   - jnp.exp(jnp.sum(lp[2:3] * lp[3:4], axis=1, keepdims=True)) + lam_init)
            for pair in range(2):
                o = []
                for h in (2 * pair, 2 * pair + 1):
                    d = outs[2 * h] - lam * outs[2 * h + 1]
                    o.append(_rms(d, gc_ref[...], axis=0) * (1.0 - lam_init))
                y_ref[:, pair * 128:(pair + 1) * 128] = jnp.concatenate(o, axis=0).T.astype(BF16)

    outs = []
    if lk == SB:
        sts = [jnp.dot(kbuf[u // per_kv], qts[u], preferred_element_type=F32) for u in range(8)]
        ps = [jnp.exp2(st - jnp.max(st, axis=0, keepdims=True)).astype(BF16) for st in sts]
        for u in range(8):
            ox = jnp.dot(vbuf[u // per_kv], ps[u], preferred_element_type=F32)
            outs.append(ox[0:64] / ox[64:65])
        emit(outs)
        return

    mhats, ms = [], []
    for u in range(8):
        g = u // per_kv
        qn = jnp.sqrt(jnp.sum(jnp.square(qts[u].astype(F32)), axis=0, keepdims=True))
        mhat = qn * kmax_ref[g][0:1, 0:1] * BOUND_SLACK
        mrun = jnp.full((8, TQ), -jnp.inf, F32)
        for j in range(lk // SB):
            rows = slice(j * SB, (j + 1) * SB)
            st = jnp.dot(kbuf[g, rows, :], qts[u], preferred_element_type=F32)
            p_ref[rows, u * TQ:(u + 1) * TQ] = jnp.exp2(st - mhat).astype(BF16)
            for r in range(SB // 8):
                mrun = jnp.maximum(mrun, st[8 * r:8 * r + 8])
        mhats.append(mhat)
        ms.append(jnp.max(mrun, axis=0, keepdims=True))
        if u >= 1:
            outs.append(value_matmul(u - 1))
    outs.append(value_matmul(7))
    emit(outs)

    worst_gap = functools.reduce(jnp.maximum, [jnp.max(mh - m) for mh, m in zip(mhats, ms)])

    @pl.when(worst_gap > MAX_BOUND_GAP)
    def _redo():
        redone = []
        for u in range(8):
            def block(j, carry, u=u):
                rows = _ds(j, SB)
                st = jnp.dot(kbuf[u // per_kv, rows, :], qts[u], preferred_element_type=F32)
                p_ref[rows, u * TQ:(u + 1) * TQ] = jnp.exp2(st - ms[u]).astype(BF16)
                return carry

            lax.fori_loop(0, lk // SB, block, 0)
            redone.append(value_matmul(u))
        emit(redone)


def _attn(mode, z, zt, *, ctx, tables, params, l, lam_init, latent):
    if latent:
        nb, seq, row0 = DEC_BATCH, DEC_SEQ, N_CTX
    else:
        nb, seq, row0 = BATCH, SEQ, 0
    nq = seq // TQ
    n_kv = KV_A if mode == "A" else H_C
    n_ctx = PAST_LEN if ctx is not None else 0
    lk = n_ctx + seq
    qw = 512 if mode == "A" else 256
    vw = n_kv * 64
    q_blk = (ZT_QA if mode == "A" else ZT_QC) // qw
    k_blk = (Z_KA if mode == "A" else Z_KC) // vw
    v_blk = (ZT_VA if mode == "A" else ZT_VC) // vw
    emit_k = mode == "A" and not latent
    blk0 = row0 // seq

    in_specs = [
        pl.BlockSpec((1, qw, TC), lambda b, i: (row0 // TC + b * nq + i, q_blk, 0)),
        pl.BlockSpec((seq, vw), lambda b, i: (blk0 + b, k_blk)),
        pl.BlockSpec((seq // TC, vw, TC), lambda b, i: (blk0 + b, v_blk, 0)),
    ]
    args = [zt, z, zt]
    if ctx is not None:
        in_specs += [
            pl.BlockSpec((1, None, PAST_LEN, vw), lambda b, i: (b, l, 0, 0)),
            pl.BlockSpec((1, None, n_kv, 64, PAST_LEN), lambda b, i: (b, l, 0, 0, 0)),
        ]
        args += list(ctx)
    if tables is not None:
        cq, sq, cqt, sqt = tables
        in_specs += [
            pl.BlockSpec((1, 64, TC), lambda b, i: (i, 0, 0)),
            pl.BlockSpec((1, 64, TC), lambda b, i: (i, 0, 0)),
            _const_spec((seq, 64)),
            _const_spec((seq, 64)),
        ]
        args += [cqt, sqt, cq, sq]
    for p in params:
        in_specs.append(_const_spec(p.shape[1:], (l,)))
        args.append(p)

    out_specs = [pl.BlockSpec((TQ, qw), lambda b, i: (b * nq + i, 0))]
    out_shape = [jax.ShapeDtypeStruct((nb * seq, qw), BF16)]
    if emit_k:
        out_specs.append(pl.BlockSpec((1, seq, vw), lambda b, i: (b, 0, 0)))
        out_shape.append(jax.ShapeDtypeStruct((nb, seq, vw), F32))

    return pl.pallas_call(
        functools.partial(_attn_kernel, mode=mode, seq=seq, has_ctx=ctx is not None,
                          rope=tables is not None, emit_k=emit_k, lam_init=lam_init),
        grid=(nb, nq),
        in_specs=in_specs,
        out_specs=out_specs,
        out_shape=out_shape,
        scratch_shapes=[
            pltpu.VMEM((n_kv, lk, 64), BF16),
            pltpu.VMEM((n_kv, VROWS, lk), BF16),
            pltpu.VMEM((lk, 8 * TQ), BF16),
            pltpu.VMEM((n_kv, 8, 128), F32),
        ],
        compiler_params=_cparams(("arbitrary", "arbitrary")),
        name=f"attn_{mode}_{'lat' if latent else 'ctx'}",
    )(*args)


N_CHUNKS = N_TOK // TC
GS_A, GS_B, GS_LM, GS_BLAST, GS_ALAST, GS_ROWS = 0, 8, 16, 24, 32, 40


def _scan_lanes(x, op, fill, reverse):
    axis = x.ndim - 1
    lane = lax.broadcasted_iota(jnp.int32, x.shape, axis)
    k = 1
    while k < TC:
        if reverse:
            x = op(x, jnp.where(lane < TC - k, pltpu.roll(x, TC - k, axis), fill))
        else:
            x = op(x, jnp.where(lane >= k, pltpu.roll(x, k, axis), fill))
        k *= 2
    return x


def _gate_stats_kernel(g_ref, st_ref, ac_ref):
    is_fwd = lax.broadcasted_iota(jnp.int32, (N_CHUNKS, 8, TC), 1) < H_B
    g = g_ref[...]
    gi = jnp.concatenate([g[:, 0:4], g[:, 8:12]], axis=1)
    f = _log_sigmoid(jnp.concatenate([g[:, 4:8], g[:, 12:16]], axis=1))
    b = jnp.where(is_fwd, _scan_lanes(f, jnp.add, 0.0, False), _scan_lanes(f, jnp.add, 0.0, True))
    a = gi - b
    lm = jnp.where(is_fwd, _scan_lanes(a, jnp.maximum, -jnp.inf, False),
                   _scan_lanes(a, jnp.maximum, -jnp.inf, True))
    st_ref[:, GS_A:GS_A + 8, :] = a
    st_ref[:, GS_B:GS_B + 8, :] = b
    st_ref[:, GS_LM:GS_LM + 8, :] = lm
    st_ref[:, GS_BLAST:GS_BLAST + 8, :] = jnp.broadcast_to(jnp.sum(f, axis=2, keepdims=True), a.shape)
    st_ref[:, GS_ALAST:GS_ALAST + 8, :] = jnp.broadcast_to(jnp.max(a, axis=2, keepdims=True), a.shape)

    def transpose_chunk(c, carry):
        ac_ref[c] = jnp.concatenate([st_ref[c, GS_A:GS_A + 8, :], jnp.zeros((120, TC), F32)], axis=0).T
        return carry

    lax.fori_loop(0, N_CHUNKS, transpose_chunk, 0, unroll=4)


def _gate_stats(zt):
    return pl.pallas_call(
        _gate_stats_kernel,
        grid=(1,),
        in_specs=[pl.BlockSpec((N_CHUNKS, 16, TC), lambda i: (0, ZT_G // 16, 0))],
        out_specs=[
            pl.BlockSpec((N_CHUNKS, GS_ROWS, TC), lambda i: (0, 0, 0)),
            pl.BlockSpec((N_CHUNKS, TC, 128), lambda i: (0, 0, 0)),
        ],
        out_shape=[
            jax.ShapeDtypeStruct((N_CHUNKS, GS_ROWS, TC), F32),
            jax.ShapeDtypeStruct((N_CHUNKS, TC, 128), F32),
        ],
        compiler_params=_cparams(("arbitrary",)),
        name="gate_stats",
    )(zt)


def _mlstm_kernel(k_ref, qt_ref, vt_ref, ot_ref, st_ref, ac_ref, s0_ref, m0_ref, gb_ref,
                  y_ref, s_ref, m_ref, hf_ref, hr_ref, *, seq):
    nc = seq // TC
    s_idx = lax.broadcasted_iota(jnp.int32, (TC, TC), 0)
    t_idx = lax.broadcasted_iota(jnp.int32, (TC, TC), 1)
    seen = (s_idx <= t_idx, s_idx >= t_idx)
    ones_rows = (lax.broadcasted_iota(jnp.int32, (64, TC), 0) == 0).astype(F32)
    units = [(d, h) for d in range(2) for h in range(H_B)]

    def chunk(j, carry):
        states, m = carry
        cs = (j, nc - 1 - j)
        stat = lambda r: jnp.concatenate([st_ref[cs[0], r:r + 4, :], st_ref[cs[1], r + 4:r + 8, :]], axis=0)
        a, b, lm = stat(GS_A), stat(GS_B), stat(GS_LM)
        mx = jnp.maximum(m, lm)
        w = jnp.exp(m - mx)
        floor = jnp.exp(-(b + mx))
        mxl = jnp.maximum(m, stat(GS_ALAST)[:, 0:1])
        decay = jnp.exp(m - mxl)
        m_new = stat(GS_BLAST)[:, 0:1] + mxl

        first = []
        for (d, h), st in zip(units, states):
            hs = slice(h * 64, (h + 1) * 64)
            kh = k_ref[_ds(cs[d], TC), hs] * (DK_B ** -0.5)
            qt = qt_ref[cs[d], hs, :].astype(BF16)
            lhs = jnp.concatenate([kh.astype(BF16), st.astype(BF16)], axis=0)
            first.append((kh, jnp.dot(lhs, qt, preferred_element_type=F32)))
        second = []
        for u, ((d, h), (kh, both)) in enumerate(zip(units, first)):
            a_col = ac_ref[cs[d], :, u:u + 1]
            dt = jnp.where(seen[d], a_col - mx[u:u + 1, :], -jnp.inf)
            sp = both[0:TC] * jnp.exp(dt)
            den = jnp.sum(sp, axis=0, keepdims=True) + w[u:u + 1, :] * both[TC + 64:TC + 65]
            wk = jnp.exp(a_col - mxl[u:u + 1, :])
            second.append((sp.astype(BF16), (kh * wk).astype(BF16), den))
        new_states = []
        for u, ((d, h), st, (kh, both), (spb, kw, den)) in enumerate(zip(units, states, first, second)):
            hs = slice(h * 64, (h + 1) * 64)
            vx = jnp.concatenate([vt_ref[cs[d], hs, :], ones_rows], axis=0).astype(BF16)
            num = jnp.dot(vx, spb, preferred_element_type=F32)[0:64] + w[u:u + 1, :] * both[TC:TC + 64]
            hv = num / jnp.maximum(jnp.abs(den), floor[u:u + 1, :])
            if d == 0:
                hf_ref[cs[d], hs, :] = hv
            else:
                hr_ref[cs[d], hs, :] = hv
            new_states.append(decay[u:u + 1, :] * st + jnp.dot(vx, kw, preferred_element_type=F32))
        return tuple(new_states), m_new

    init = (tuple(s0_ref[0, u] for u in range(2 * H_B)), m0_ref[0][:, 0:1])
    if nc == 1:
        states, m = chunk(0, init)
    else:
        states, m = lax.fori_loop(0, nc, chunk, init)
    for u, st in enumerate(states):
        s_ref[0, u] = st
    m_ref[0] = jnp.broadcast_to(m, (2 * H_B, 128))

    def epilogue(c, carry):
        ys = []
        for h in range(H_B):
            hs = slice(h * 64, (h + 1) * 64)
            x = hf_ref[c, hs, :] + hr_ref[c, hs, :]
            ys.append(_rms(x, gb_ref[...], axis=0) * jax.nn.sigmoid(ot_ref[c, hs, :]))
        y_ref[_ds(c, TC), :] = jnp.concatenate(ys, axis=0).T.astype(BF16)
        return carry

    if nc == 1:
        epilogue(0, 0)
    else:
        lax.fori_loop(0, nc, epilogue, 0)


def _mlstm(z, zt, stats, acols, s0, m0, gb, *, l, latent):
    ls = min(l, s0.shape[1] - 1)
    if latent:
        nb, seq, row0 = DEC_BATCH, DEC_SEQ, N_CTX
    else:
        nb, seq, row0 = BATCH, SEQ, 0
    blk0 = row0 // seq
    nc = seq // TC
    ztspec = lambda rb: pl.BlockSpec((nc, 256, TC), lambda b: (blk0 + b, rb, 0))
    return pl.pallas_call(
        functools.partial(_mlstm_kernel, seq=seq),
        grid=(nb,),
        in_specs=[
            pl.BlockSpec((seq, 256), lambda b: (blk0 + b, Z_KB // 256)),
            ztspec(ZT_QB // 256), ztspec(ZT_VB // 256), ztspec(ZT_OB // 256),
            pl.BlockSpec((nc, GS_ROWS, TC), lambda b: (blk0 + b, 0, 0)),
            pl.BlockSpec((nc, TC, 128), lambda b: (blk0 + b, 0, 0)),
            pl.BlockSpec((1, None, 2 * H_B, 128, 64), lambda b: (b, ls, 0, 0, 0)),
            pl.BlockSpec((1, None, 2 * H_B, 128), lambda b: (b, ls, 0, 0)),
            _const_spec((64, 1), (l,)),
        ],
        out_specs=[
            pl.BlockSpec((seq, 256), lambda b: (b, 0)),
            pl.BlockSpec((1, 2 * H_B, 128, 64), lambda b: (b, 0, 0, 0)),
            pl.BlockSpec((1, 2 * H_B, 128), lambda b: (b, 0, 0)),
        ],
        out_shape=[
            jax.ShapeDtypeStruct((nb * seq, 256), BF16),
            jax.ShapeDtypeStruct((nb, 2 * H_B, 128, 64), F32),
            jax.ShapeDtypeStruct((nb, 2 * H_B, 128), F32),
        ],
        scratch_shapes=[pltpu.VMEM((nc, 256, TC), F32), pltpu.VMEM((nc, 256, TC), F32)],
        compiler_params=_cparams(("arbitrary",)),
        name=f"mlstm_{'lat' if latent else 'ctx'}",
    )(z, zt, zt, zt, stats, acols, s0, m0, gb)


def _rope_tables(dim):
    t = jnp.arange(DEC_SEQ)
    row = (t // GRID_W).astype(F32)
    colp = (t % GRID_W).astype(F32)
    axis_dim = dim // 2
    freqs = ROPE_THETA ** (-jnp.arange(0, axis_dim, 2, dtype=F32) / axis_dim)
    ang = jnp.concatenate([row[:, None] * freqs, colp[:, None] * freqs], axis=-1)
    cos, sin = jnp.cos(ang), jnp.sin(ang)
    reps = 64 // dim
    cq = jnp.concatenate([cos, cos] * reps, axis=1)
    sq = jnp.concatenate([-sin, sin] * reps, axis=1)
    to_chunks = lambda x: x.T.reshape(64, DEC_SEQ // TC, TC).transpose(1, 0, 2)
    return cq, sq, to_chunks(cq), to_chunks(sq)


def _split_w_in(w, b_gates):
    a0, b0, c0 = 0, 768, 1808
    qa, ka, va = w[..., a0:a0 + 512], w[..., a0 + 512:a0 + 640], w[..., a0 + 640:a0 + 768]
    qb, kb = w[..., b0:b0 + 256], w[..., b0 + 256:b0 + 512]
    vb, ob = w[..., b0 + 512:b0 + 768], w[..., b0 + 768:b0 + 1024]
    gb = w[..., b0 + 1024:b0 + 1040]
    qc, kc, vc = w[..., c0:c0 + 256], w[..., c0 + 256:c0 + 512], w[..., c0 + 512:c0 + 768]
    wz = jnp.concatenate([kb, kc, ka], axis=-1)
    wt = jnp.swapaxes(jnp.concatenate([qa, qc, vc, qb, vb, ob, va, gb], axis=-1), -1, -2)
    bt = jnp.zeros((DEPTH, ZT_W, 1), F32).at[:, ZT_G:ZT_G + 16, 0].set(b_gates)
    return wz.astype(BF16), wt.astype(BF16), bt


def _tokens_major(x):
    return x.transpose(0, 2, 1)


def kernel(x_prompt, x_sample, cache_a_k, cache_a_v, cache_c_k, cache_c_v, state_b_C, state_b_n,
           state_b_m, c, c_ctx, w_ada, b_ada, g_norm, w_ff_in, w_ff_out, w_in, w_out, g_qa, g_ka,
           b_gates, g_b, lam_q1, lam_k1, lam_q2, lam_k2, g_c, g_final):
    h = (x_prompt.reshape(N_CTX, D_MODEL), x_sample.reshape(N_LAT, D_MODEL))
    cc = jnp.concatenate([c_ctx[None, :], c, jnp.zeros((3, D_MODEL), F32)], axis=0)
    mods = _ada(cc, w_ada, b_ada).reshape(DEPTH, 8, 9, D_MODEL)
    tab_a = _rope_tables(HD_A)
    tab_c = _rope_tables(D_C)

    gf = g_final.reshape(1, D_MODEL)
    gn = g_norm.reshape(DEPTH, 3, 1, D_MODEL)
    wi, wo, wmix = w_ff_in.astype(BF16), w_ff_out.astype(BF16), w_out.astype(BF16)
    wz, wt, bt = _split_w_in(w_in, b_gates)
    par_a = (g_qa.reshape(DEPTH, 64, 1), g_ka.reshape(DEPTH, 1, 64))
    par_c = (jnp.stack([lam_q1, lam_k1, lam_q2, lam_k2], axis=1), g_c.reshape(DEPTH, 64, 1))
    gb = g_b.reshape(DEPTH, 64, 1)
    ctx_a = (cache_a_k.reshape(DEC_BATCH, DEPTH, PAST_LEN, KV_A * HD_A), cache_a_v.transpose(0, 1, 3, 4, 2))
    ctx_c = (cache_c_k.reshape(DEC_BATCH, DEPTH, PAST_LEN, H_C * 2 * D_C), cache_c_v.transpose(0, 1, 3, 4, 2))
    s0 = jnp.concatenate([jnp.swapaxes(state_b_C, -1, -2), state_b_n[..., None, :],
                          jnp.zeros((DEC_BATCH, DEPTH, 2, H_B, 63, DK_B), F32)], axis=-2)
    s0 = s0.reshape(DEC_BATCH, DEPTH, 2 * H_B, 128, DK_B)
    m0 = jnp.broadcast_to(state_b_m.reshape(DEC_BATCH, DEPTH, 2 * H_B, 1), (DEC_BATCH, DEPTH, 2 * H_B, 128))
    zeros_s = jnp.zeros((BATCH, 1, 2 * H_B, 128, DK_B), F32)
    zeros_m = jnp.zeros((BATCH, 1, 2 * H_B, 128), F32)

    outs = {k: [] for k in ("ak", "av", "ck", "cv", "bC", "bn", "bm")}
    for l in range(DEPTH):
        lam_init = 0.8 - 0.6 * math.exp(-0.3 * l)
        h = _ffn(h, mods, gn, wi, wo, gf, l=l, half=0, first=(l == 0))
        z, zt = _proj_in(h, mods, gn, wz, wt, bt, l=l)

        ya_c, kn = _attn("A", z, zt, ctx=None, tables=None, params=par_a, l=l, lam_init=lam_init,
                         latent=False)
        (yc_c,) = _attn("C", z, zt, ctx=None, tables=None, params=par_c, l=l, lam_init=lam_init,
                        latent=False)
        stats, acols = _gate_stats(zt)
        yb_c, s_fin, m_fin = _mlstm(z, zt, stats, acols, zeros_s, zeros_m, gb, l=l, latent=False)

        (ya_l,) = _attn("A", z, zt, ctx=ctx_a, tables=tab_a, params=par_a, l=l, lam_init=lam_init,
                        latent=True)
        (yc_l,) = _attn("C", z, zt, ctx=ctx_c, tables=tab_c, params=par_c, l=l, lam_init=lam_init,
                        latent=True)
        yb_l, _, _ = _mlstm(z, zt, stats, acols, s0, m0, gb, l=l, latent=True)

        h = _ffn(h, mods, gn, wi, wo, gf, l=l, half=1,
                 mix=((ya_c, ya_l), (yb_c, yb_l), (yc_c, yc_l), wmix), final=(l == DEPTH - 1))

        zc = z[:N_CTX]
        ztc = zt[:N_CTX // TC]
        outs["ak"].append(kn.reshape(BATCH, SEQ, KV_A, HD_A))
        outs["av"].append(_tokens_major(ztc[:, ZT_VA:ZT_VA + 128, :]).reshape(BATCH, SEQ, KV_A, HD_A))
        outs["ck"].append(zc[:, Z_KC:Z_KC + 256].reshape(BATCH, SEQ, H_C, 2, D_C))
        outs["cv"].append(_tokens_major(ztc[:, ZT_VC:ZT_VC + 256, :]).reshape(BATCH, SEQ, H_C, 2 * D_C))
        s_fin = s_fin.reshape(BATCH, 2, H_B, 128, DK_B)
        outs["bC"].append(jnp.swapaxes(s_fin[..., 0:64, :], -1, -2))
        outs["bn"].append(s_fin[..., 64, :])
        outs["bm"].append(m_fin[:, :, 0].reshape(BATCH, 2, H_B))

    y_prompt = h[0].reshape(BATCH, SEQ, D_MODEL)
    y_sample = h[1].reshape(DEC_BATCH, DEC_SEQ, D_MODEL)
    st = lambda k: jnp.stack(outs[k], axis=1)
    return (y_prompt, y_sample, st("ak"), st("av"), st("ck"), st("cv"), st("bC"), st("bn"), st("bm"))
```

```python
import functools
import math

import jax
import jax.numpy as jnp
from jax import lax
from jax.experimental import pallas as pl
from jax.experimental.pallas import tpu as pltpu

F32 = jnp.float32
BF16 = jnp.bfloat16

D_MODEL = 1024
BATCH = 16
SEQ = 256
DEPTH = 2
DEC_BATCH = 4
DEC_SEQ = 2048
PAST_LEN = 512
GRID_W = 64
ROPE_THETA = 10000.0
EPS = 1e-6
HD_A = 64
H_A = 8
KV_A = 2
DK_B = 64
H_B = 4
D_C = 32
H_C = 4
D_FF = 2816
LOG2E = 1.4426950408889634

N_CTX = BATCH * SEQ
N_LAT = DEC_BATCH * DEC_SEQ
N_TOK = N_CTX + N_LAT

TM = 512
CTX_TILES = N_CTX // TM
LAT_TILES_PER_BATCH = DEC_SEQ // TM
FF_CHUNKS = ((0, 1536), (1536, D_FF))
FF_ROWS = 256
TC = 256
SB = 256
TQ = 256
VROWS = 80
BOUND_SLACK = 1.0 + 2.0 ** -6
MAX_BOUND_GAP = 60.0

Z_KB, Z_KC, Z_KA = 0, 256, 512
Z_W = 640
ZT_QA, ZT_QC, ZT_VC, ZT_QB, ZT_VB, ZT_OB, ZT_VA, ZT_G = 0, 512, 768, 1024, 1280, 1536, 1792, 1920
ZT_W = 1936

VMEM_LIMIT = 56 * 1024 * 1024


def _cparams(sem):
    return pltpu.CompilerParams(dimension_semantics=sem, vmem_limit_bytes=VMEM_LIMIT)


def _const_spec(shape, lead=()):
    nd = len(shape)
    return pl.BlockSpec((None,) * len(lead) + tuple(shape), lambda *_: tuple(lead) + (0,) * nd,
                        pipeline_mode=pl.Buffered(1))


def _mod_row(i):
    return jnp.maximum(i - CTX_TILES, 0) // LAT_TILES_PER_BATCH + (i >= CTX_TILES).astype(jnp.int32)


def _rms(x, g, axis=-1):
    ms = jnp.mean(x * x, axis=axis, keepdims=True)
    return x * lax.rsqrt(ms + EPS) * g


def _norm_mod(x, g, mod_ref, k):
    shift = mod_ref[0, k:k + 1, :]
    scale = mod_ref[0, k + 1:k + 2, :]
    return _rms(x, g) * (1.0 + scale) + shift


def _ds(c, n):
    if isinstance(c, int):
        return pl.ds(c * n, n)
    return pl.ds(pl.multiple_of(c * n, n), n)


def _log_sigmoid(x):
    return jnp.minimum(x, 0.0) - jnp.log1p(jnp.exp(-jnp.abs(x)))


def _ada_kernel(c_ref, w_ref, b_ref, o_ref):
    c = c_ref[...]
    a = (c * jax.nn.sigmoid(c)).astype(BF16)
    o_ref[0] = jnp.dot(a, w_ref[0].astype(BF16), preferred_element_type=F32) + b_ref[0]


def _ada(cc, w_ada, b_ada):
    tn = 1024
    n = 9 * D_MODEL
    return pl.pallas_call(
        _ada_kernel,
        grid=(DEPTH, n // tn),
        in_specs=[
            pl.BlockSpec((8, D_MODEL), lambda l, j: (0, 0)),
            pl.BlockSpec((1, D_MODEL, tn), lambda l, j: (l, 0, j)),
            pl.BlockSpec((1, 1, tn), lambda l, j: (l, 0, j)),
        ],
        out_specs=pl.BlockSpec((1, 8, tn), lambda l, j: (l, 0, j)),
        out_shape=jax.ShapeDtypeStruct((DEPTH, 8, n), F32),
        compiler_params=_cparams(("arbitrary", "arbitrary")),
        name="ada",
    )(cc, w_ada, b_ada.reshape(DEPTH, 1, n))


def _ffn_kernel(*refs, k, first, mix, final):
    is_ctx = pl.program_id(0) < CTX_TILES
    it = iter(refs)
    x_refs = (next(it), next(it)) if first else (next(it),)
    if mix:
        y_refs = [(next(it), next(it)) for _ in range(3)]
        wmix_ref = next(it)
    mod_ref, g_ref, wi_ref, wo_ref, gf_ref = (next(it) for _ in range(5))

    nrows = TM if mix else FF_ROWS
    ys = []
    for r in range(TM // nrows):
        rows = slice(r * nrows, (r + 1) * nrows)
        pick = lambda c_ref, l_ref: jnp.where(is_ctx, c_ref[rows, :], l_ref[rows, :])
        x = pick(*x_refs) if first else x_refs[0][rows, :]
        if mix:
            ya, yb, yc = (pick(*pair) for pair in y_refs)
            y = jnp.dot(ya, wmix_ref[0:512, :], preferred_element_type=F32)
            y = y + jnp.dot(yb, wmix_ref[512:768, :], preferred_element_type=F32)
            y = y + jnp.dot(yc, wmix_ref[768:1024, :], preferred_element_type=F32)
            x = x + mod_ref[0, 5:6, :] * y
        xn = _norm_mod(x, g_ref[...], mod_ref, k).astype(BF16)
        acc = jnp.zeros((nrows, D_MODEL), F32)
        for lo, hi in FF_CHUNKS:
            g = jnp.dot(xn, wi_ref[:, lo:hi], preferred_element_type=F32)
            u = jnp.dot(xn, wi_ref[:, D_FF + lo:D_FF + hi], preferred_element_type=F32)
            a = (g * jax.nn.sigmoid(g) * u).astype(BF16)
            acc = acc + jnp.dot(a, wo_ref[lo:hi, :], preferred_element_type=F32)
        y = x + 0.5 * mod_ref[0, k + 2:k + 3, :] * acc
        if final:
            ys.append(_rms(y, gf_ref[...]))
        else:
            refs[-1][rows, :] = y
    if final:
        yc_ref, yl_ref = refs[-2:]
        y = jnp.concatenate(ys, axis=0)

        @pl.when(is_ctx)
        def _():
            yc_ref[...] = y

        @pl.when(jnp.logical_not(is_ctx))
        def _():
            yl_ref[...] = y


def _ctx_lat_specs(width):
    return [
        pl.BlockSpec((TM, width), lambda i: (jnp.minimum(i, CTX_TILES - 1), 0)),
        pl.BlockSpec((TM, width), lambda i: (jnp.maximum(i - CTX_TILES, 0), 0)),
    ]


def _mod_spec(l):
    return pl.BlockSpec((None, 1, 9, D_MODEL), lambda i: (l, _mod_row(i), 0, 0))


def _ffn(h, mods, g, wi, wo, gf, *, l, half, first=False, mix=None, final=False):
    k = 6 * half
    flat_spec = pl.BlockSpec((TM, D_MODEL), lambda i: (i, 0))
    flat_shape = jax.ShapeDtypeStruct((N_TOK, D_MODEL), F32)
    mix_specs, mix_args = [], []
    if mix is not None:
        mix_specs = (_ctx_lat_specs(512) + _ctx_lat_specs(256) + _ctx_lat_specs(256)
                     + [_const_spec((D_MODEL, D_MODEL), (l,))])
        mix_args = [*mix[0], *mix[1], *mix[2], mix[3]]
    return pl.pallas_call(
        functools.partial(_ffn_kernel, k=k, first=first, mix=mix is not None, final=final),
        grid=(N_TOK // TM,),
        in_specs=(_ctx_lat_specs(D_MODEL) if first else [flat_spec]) + mix_specs + [
            _mod_spec(l),
            _const_spec((1, D_MODEL), (l, 2 * half)),
            _const_spec((D_MODEL, 2 * D_FF), (l, half)),
            _const_spec((D_FF, D_MODEL), (l, half)),
            _const_spec((1, D_MODEL)),
        ],
        out_specs=_ctx_lat_specs(D_MODEL) if final else flat_spec,
        out_shape=[jax.ShapeDtypeStruct((N_CTX, D_MODEL), F32),
                   jax.ShapeDtypeStruct((N_LAT, D_MODEL), F32)] if final else flat_shape,
        compiler_params=_cparams(("arbitrary",)),
        name="ffn_final" if final else ("ffn_first" if first else ("ffn_mix" if mix else "ffn")),
    )(*(h if first else (h,)), *mix_args, mods, g, wi, wo, gf)


def _proj_in_kernel(h_ref, mod_ref, g_ref, w_ref, wt_ref, bt_ref, z_ref, zt_ref):
    xn = _norm_mod(h_ref[...], g_ref[...], mod_ref, 3).astype(BF16)
    z_ref[...] = jnp.dot(xn, w_ref[...], preferred_element_type=F32)
    zt = lax.dot_general(wt_ref[...], xn, (((1,), (1,)), ((), ())), preferred_element_type=F32)
    zt = zt + bt_ref[...]
    for c in range(TM // TC):
        zt_ref[c] = zt[:, c * TC:(c + 1) * TC]


def _proj_in(h, mods, g, wz, wt, bt, *, l):
    return pl.pallas_call(
        _proj_in_kernel,
        grid=(N_TOK // TM,),
        in_specs=[
            pl.BlockSpec((TM, D_MODEL), lambda i: (i, 0)),
            _mod_spec(l),
            _const_spec((1, D_MODEL), (l, 1)),
            _const_spec((D_MODEL, Z_W), (l,)),
            _const_spec((ZT_W, D_MODEL), (l,)),
            _const_spec((ZT_W, 1), (l,)),
        ],
        out_specs=[
            pl.BlockSpec((TM, Z_W), lambda i: (i, 0)),
            pl.BlockSpec((TM // TC, ZT_W, TC), lambda i: (i, 0, 0)),
        ],
        out_shape=[
            jax.ShapeDtypeStruct((N_TOK, Z_W), F32),
            jax.ShapeDtypeStruct((N_TOK // TC, ZT_W, TC), F32),
        ],
        compiler_params=_cparams(("arbitrary",)),
        name="proj_in",
    )(h, mods, g, wz, wt, bt)


def _swap_halves(x, width, axis):
    n = x.shape[axis]
    half = width // 2
    parts = []
    for s in range(0, n, width):
        parts.append(lax.slice_in_dim(x, s + half, s + width, axis=axis))
        parts.append(lax.slice_in_dim(x, s, s + half, axis=axis))
    return jnp.concatenate(parts, axis=axis)


def _attn_kernel(*refs, mode, seq, has_ctx, rope, emit_cache, lam_init):
    it = iter(refs)
    qt_ref, k_ref, vt_ref = next(it), next(it), next(it)
    if has_ctx:
        ck_ref, cvt_ref = next(it), next(it)
    if rope:
        cqt_ref, sqt_ref, ck_tab, sk_tab = next(it), next(it), next(it), next(it)
    if mode == "A":
        gq_ref, gk_ref = next(it), next(it)
    else:
        lam_ref, gc_ref = next(it), next(it)
    y_ref = next(it)
    if emit_cache:
        kc_ref, vc_ref = next(it), next(it)
    kbuf, vbuf, p_ref, kmax_ref = next(it), next(it), next(it), next(it)

    n_kv = KV_A if mode == "A" else H_C
    per_kv = 8 // n_kv
    n_ctx = PAST_LEN if has_ctx else 0
    lk = n_ctx + seq
    width = HD_A if mode == "A" else D_C

    @pl.when(pl.program_id(1) == 0)
    def _fill():
        ones_rows = (lax.broadcasted_iota(jnp.int32, (VROWS - 64, lk), 0) == 0).astype(BF16)
        ones_cols = jnp.ones((64, 128), BF16)

        def max_sq_norm(kb):
            if lk == SB:
                return jnp.zeros((1, 1), F32)
            kf = kb.astype(F32)
            sq = jnp.dot((kf * kf).astype(BF16), ones_cols, preferred_element_type=F32)
            return jnp.max(sq, axis=0, keepdims=True)[:, 0:1]

        for h in range(n_kv):
            hs = slice(h * 64, (h + 1) * 64)
            vbuf[h, 64:VROWS, :] = ones_rows
            kmax2 = jnp.zeros((1, 1), F32)
            if has_ctx:
                kb = ck_ref[0, :, hs].astype(BF16)
                kmax2 = jnp.maximum(kmax2, max_sq_norm(kb))
                kbuf[h, 0:n_ctx, :] = kb
                vbuf[h, 0:64, 0:n_ctx] = cvt_ref[0, h].astype(BF16)
            for c in range(seq // TC):
                rows = slice(c * TC, (c + 1) * TC)
                k = k_ref[rows, hs]
                if mode == "A":
                    k = _rms(k, gk_ref[...])
                if emit_cache:
                    kc_ref[0, rows, hs] = k
                if rope:
                    k = k * ck_tab[rows, :] + _swap_halves(k, width, 1) * sk_tab[rows, :]
                kb = k.astype(BF16)
                kmax2 = jnp.maximum(kmax2, max_sq_norm(kb))
                kbuf[h, n_ctx + c * TC:n_ctx + (c + 1) * TC, :] = kb
                vbuf[h, 0:64, n_ctx + c * TC:n_ctx + (c + 1) * TC] = vt_ref[c, hs, :].astype(BF16)
            kmax_ref[h] = jnp.broadcast_to(jnp.sqrt(kmax2), (8, 128))
        if emit_cache:
            for c in range(seq // TC):
                vc_ref[0, c * TC:(c + 1) * TC, :] = vt_ref[c].T

    qts = []
    if mode == "A":
        scale = HD_A ** -0.5 * LOG2E
        for hh in range(H_A):
            x = _rms(qt_ref[0, hh * 64:(hh + 1) * 64, :], gq_ref[...], axis=0)
            if rope:
                x = x * cqt_ref[0] + _swap_halves(x, width, 0) * sqt_ref[0]
            qts.append((x * scale).astype(BF16))
    else:
        scale = D_C ** -0.5 * LOG2E
        first_map = lax.broadcasted_iota(jnp.int32, (64, TQ), 0) < D_C
        for h in range(H_C):
            x = qt_ref[0, h * 64:(h + 1) * 64, :]
            if rope:
                x = x * cqt_ref[0] + _swap_halves(x, width, 0) * sqt_ref[0]
            x = x * scale
            qts.append(jnp.where(first_map, x, 0.0).astype(BF16))
            qts.append(jnp.where(first_map, 0.0, x).astype(BF16))

    def value_matmul(u):
        ox = jnp.dot(vbuf[u // per_kv], p_ref[:, u * TQ:(u + 1) * TQ], preferred_element_type=F32)
        return ox[0:64] / ox[64:65]

    def emit(outs):
        if mode == "A":
            for pair in range(4):
                o = jnp.concatenate(outs[2 * pair:2 * pair + 2], axis=0)
                y_ref[:, pair * 128:(pair + 1) * 128] = o.T.astype(BF16)
        else:
            lp = lam_ref[...]
            lam = (jnp.exp(jnp.sum(lp[0:1] * lp[1:2], axis=1, keepdims=True))
                   - jnp.exp(jnp.sum(lp[2:3] * lp[3:4], axis=1, keepdims=True)) + lam_init)
            for pair in range(2):
                o = []
                for h in (2 * pair, 2 * pair + 1):
                    d = outs[2 * h] - lam * outs[2 * h + 1]
                    o.append(_rms(d, gc_ref[...], axis=0) * (1.0 - lam_init))
                y_ref[:, pair * 128:(pair + 1) * 128] = jnp.concatenate(o, axis=0).T.astype(BF16)

    outs = []
    if lk == SB:
        sts = [jnp.dot(kbuf[u // per_kv], qts[u], preferred_element_type=F32) for u in range(8)]
        ps = [jnp.exp2(st - jnp.max(st, axis=0, keepdims=True)).astype(BF16) for st in sts]
        for u in range(8):
            ox = jnp.dot(vbuf[u // per_kv], ps[u], preferred_element_type=F32)
            outs.append(ox[0:64] / ox[64:65])
        emit(outs)
        return

    mhats, ms = [], []
    for u in range(8):
        g = u // per_kv
        qn = jnp.sqrt(jnp.sum(jnp.square(qts[u].astype(F32)), axis=0, keepdims=True))
        mhat = qn * kmax_ref[g][0:1, 0:1] * BOUND_SLACK
        mrun = jnp.full((8, TQ), -jnp.inf, F32)
        for j in range(lk // SB):
            rows = slice(j * SB, (j + 1) * SB)
            st = jnp.dot(kbuf[g, rows, :], qts[u], preferred_element_type=F32)
            p_ref[rows, u * TQ:(u + 1) * TQ] = jnp.exp2(st - mhat).astype(BF16)
            for r in range(SB // 8):
                mrun = jnp.maximum(mrun, st[8 * r:8 * r + 8])
        mhats.append(mhat)
        ms.append(jnp.max(mrun, axis=0, keepdims=True))
        if u >= 1:
            outs.append(value_matmul(u - 1))
    outs.append(value_matmul(7))
    emit(outs)

    worst_gap = functools.reduce(jnp.maximum, [jnp.max(mh - m) for mh, m in zip(mhats, ms)])

    @pl.when(worst_gap > MAX_BOUND_GAP)
    def _redo():
        redone = []
        for u in range(8):
            def block(j, carry, u=u):
                rows = _ds(j, SB)
                st = jnp.dot(kbuf[u // per_kv, rows, :], qts[u], preferred_element_type=F32)
                p_ref[rows, u * TQ:(u + 1) * TQ] = jnp.exp2(st - ms[u]).astype(BF16)
                return carry

            lax.fori_loop(0, lk // SB, block, 0)
            redone.append(value_matmul(u))
        emit(redone)


def _attn(mode, z, zt, *, ctx, tables, params, l, lam_init, latent):
    if latent:
        nb, seq, row0 = DEC_BATCH, DEC_SEQ, N_CTX
    else:
        nb, seq, row0 = BATCH, SEQ, 0
    nq = seq // TQ
    n_kv = KV_A if mode == "A" else H_C
    n_ctx = PAST_LEN if ctx is not None else 0
    lk = n_ctx + seq
    qw = 512 if mode == "A" else 256
    vw = n_kv * 64
    q_blk = (ZT_QA if mode == "A" else ZT_QC) // qw
    k_blk = (Z_KA if mode == "A" else Z_KC) // vw
    v_blk = (ZT_VA if mode == "A" else ZT_VC) // vw
    emit_cache = not latent
    blk0 = row0 // seq

    in_specs = [
        pl.BlockSpec((1, qw, TC), lambda b, i: (row0 // TC + b * nq + i, q_blk, 0)),
        pl.BlockSpec((seq, vw), lambda b, i: (blk0 + b, k_blk)),
        pl.BlockSpec((seq // TC, vw, TC), lambda b, i: (blk0 + b, v_blk, 0)),
    ]
    args = [zt, z, zt]
    if ctx is not None:
        in_specs += [
            pl.BlockSpec((1, None, PAST_LEN, vw), lambda b, i: (b, l, 0, 0)),
            pl.BlockSpec((1, None, n_kv, 64, PAST_LEN), lambda b, i: (b, l, 0, 0, 0)),
        ]
        args += list(ctx)
    if tables is not None:
        cq, sq, cqt, sqt = tables
        in_specs += [
            pl.BlockSpec((1, 64, TC), lambda b, i: (i, 0, 0)),
            pl.BlockSpec((1, 64, TC), lambda b, i: (i, 0, 0)),
            _const_spec((seq, 64)),
            _const_spec((seq, 64)),
        ]
        args += [cqt, sqt, cq, sq]
    for p in params:
        in_specs.append(_const_spec(p.shape[1:], (l,)))
        args.append(p)

    out_specs = [pl.BlockSpec((TQ, qw), lambda b, i: (b * nq + i, 0))]
    out_shape = [jax.ShapeDtypeStruct((nb * seq, qw), BF16)]
    if emit_cache:
        out_specs += [pl.BlockSpec((1, seq, vw), lambda b, i: (b, 0, 0))] * 2
        out_shape += [jax.ShapeDtypeStruct((nb, seq, vw), F32)] * 2

    return pl.pallas_call(
        functools.partial(_attn_kernel, mode=mode, seq=seq, has_ctx=ctx is not None,
                          rope=tables is not None, emit_cache=emit_cache, lam_init=lam_init),
        grid=(nb, nq),
        in_specs=in_specs,
        out_specs=out_specs,
        out_shape=out_shape,
        scratch_shapes=[
            pltpu.VMEM((n_kv, lk, 64), BF16),
            pltpu.VMEM((n_kv, VROWS, lk), BF16),
            pltpu.VMEM((lk, 8 * TQ), BF16),
            pltpu.VMEM((n_kv, 8, 128), F32),
        ],
        compiler_params=_cparams(("arbitrary", "arbitrary")),
        name=f"attn_{mode}_{'lat' if latent else 'ctx'}",
    )(*args)


N_CHUNKS = N_TOK // TC
GS_A, GS_B, GS_LM, GS_BLAST, GS_ALAST, GS_ROWS = 0, 8, 16, 24, 32, 40


def _scan_lanes(x, op, fill, reverse):
    axis = x.ndim - 1
    lane = lax.broadcasted_iota(jnp.int32, x.shape, axis)
    k = 1
    while k < TC:
        if reverse:
            x = op(x, jnp.where(lane < TC - k, pltpu.roll(x, TC - k, axis), fill))
        else:
            x = op(x, jnp.where(lane >= k, pltpu.roll(x, k, axis), fill))
        k *= 2
    return x


def _gate_stats_kernel(g_ref, st_ref, ac_ref):
    is_fwd = lax.broadcasted_iota(jnp.int32, (N_CHUNKS, 8, TC), 1) < H_B
    g = g_ref[...]
    gi = jnp.concatenate([g[:, 0:4], g[:, 8:12]], axis=1)
    f = _log_sigmoid(jnp.concatenate([g[:, 4:8], g[:, 12:16]], axis=1))
    b = jnp.where(is_fwd, _scan_lanes(f, jnp.add, 0.0, False), _scan_lanes(f, jnp.add, 0.0, True))
    a = gi - b
    lm = jnp.where(is_fwd, _scan_lanes(a, jnp.maximum, -jnp.inf, False),
                   _scan_lanes(a, jnp.maximum, -jnp.inf, True))
    st_ref[:, GS_A:GS_A + 8, :] = a
    st_ref[:, GS_B:GS_B + 8, :] = b
    st_ref[:, GS_LM:GS_LM + 8, :] = lm
    st_ref[:, GS_BLAST:GS_BLAST + 8, :] = jnp.broadcast_to(jnp.sum(f, axis=2, keepdims=True), a.shape)
    st_ref[:, GS_ALAST:GS_ALAST + 8, :] = jnp.broadcast_to(jnp.max(a, axis=2, keepdims=True), a.shape)

    def transpose_chunk(c, carry):
        ac_ref[c] = jnp.concatenate([st_ref[c, GS_A:GS_A + 8, :], jnp.zeros((120, TC), F32)], axis=0).T
        return carry

    lax.fori_loop(0, N_CHUNKS, transpose_chunk, 0, unroll=4)


def _gate_stats(zt):
    return pl.pallas_call(
        _gate_stats_kernel,
        grid=(1,),
        in_specs=[pl.BlockSpec((N_CHUNKS, 16, TC), lambda i: (0, ZT_G // 16, 0))],
        out_specs=[
            pl.BlockSpec((N_CHUNKS, GS_ROWS, TC), lambda i: (0, 0, 0)),
            pl.BlockSpec((N_CHUNKS, TC, 128), lambda i: (0, 0, 0)),
        ],
        out_shape=[
            jax.ShapeDtypeStruct((N_CHUNKS, GS_ROWS, TC), F32),
            jax.ShapeDtypeStruct((N_CHUNKS, TC, 128), F32),
        ],
        compiler_params=_cparams(("arbitrary",)),
        name="gate_stats",
    )(zt)


def _mlstm_kernel(k_ref, qt_ref, vt_ref, ot_ref, st_ref, ac_ref, s0_ref, m0_ref, gb_ref,
                  y_ref, s_ref, m_ref, hf_ref, hr_ref, *, seq):
    nc = seq // TC
    s_idx = lax.broadcasted_iota(jnp.int32, (TC, TC), 0)
    t_idx = lax.broadcasted_iota(jnp.int32, (TC, TC), 1)
    seen = (s_idx <= t_idx, s_idx >= t_idx)
    ones_rows = (lax.broadcasted_iota(jnp.int32, (64, TC), 0) == 0).astype(F32)
    units = [(d, h) for d in range(2) for h in range(H_B)]

    def chunk(j, carry):
        states, m = carry
        cs = (j, nc - 1 - j)
        stat = lambda r: jnp.concatenate([st_ref[cs[0], r:r + 4, :], st_ref[cs[1], r + 4:r + 8, :]], axis=0)
        a, b, lm = stat(GS_A), stat(GS_B), stat(GS_LM)
        mx = jnp.maximum(m, lm)
        w = jnp.exp(m - mx)
        floor = jnp.exp(-(b + mx))
        mxl = jnp.maximum(m, stat(GS_ALAST)[:, 0:1])
        decay = jnp.exp(m - mxl)
        m_new = stat(GS_BLAST)[:, 0:1] + mxl

        first = []
        for (d, h), st in zip(units, states):
            hs = slice(h * 64, (h + 1) * 64)
            kh = k_ref[_ds(cs[d], TC), hs] * (DK_B ** -0.5)
            qt = qt_ref[cs[d], hs, :].astype(BF16)
            lhs = jnp.concatenate([kh.astype(BF16), st.astype(BF16)], axis=0)
            first.append((kh, jnp.dot(lhs, qt, preferred_element_type=F32)))
        second = []
        for u, ((d, h), (kh, both)) in enumerate(zip(units, first)):
            a_col = ac_ref[cs[d], :, u:u + 1]
            dt = jnp.where(seen[d], a_col - mx[u:u + 1, :], -jnp.inf)
            sp = both[0:TC] * jnp.exp(dt)
            den = jnp.sum(sp, axis=0, keepdims=True) + w[u:u + 1, :] * both[TC + 64:TC + 65]
            wk = jnp.exp(a_col - mxl[u:u + 1, :])
            second.append((sp.astype(BF16), (kh * wk).astype(BF16), den))
        new_states = []
        for u, ((d, h), st, (kh, both), (spb, kw, den)) in enumerate(zip(units, states, first, second)):
            hs = slice(h * 64, (h + 1) * 64)
            vx = jnp.concatenate([vt_ref[cs[d], hs, :], ones_rows], axis=0).astype(BF16)
            num = jnp.dot(vx, spb, preferred_element_type=F32)[0:64] + w[u:u + 1, :] * both[TC:TC + 64]
            hv = num / jnp.maximum(jnp.abs(den), floor[u:u + 1, :])
            if d == 0:
                hf_ref[cs[d], hs, :] = hv
            else:
                hr_ref[cs[d], hs, :] = hv
            new_states.append(decay[u:u + 1, :] * st + jnp.dot(vx, kw, preferred_element_type=F32))
        return tuple(new_states), m_new

    init = (tuple(s0_ref[0, u] for u in range(2 * H_B)), m0_ref[0][:, 0:1])
    if nc == 1:
        states, m = chunk(0, init)
    else:
        states, m = lax.fori_loop(0, nc, chunk, init)
    for u, st in enumerate(states):
        s_ref[0, u] = st
    m_ref[0] = jnp.broadcast_to(m, (2 * H_B, 128))

    def epilogue(c, carry):
        ys = []
        for h in range(H_B):
            hs = slice(h * 64, (h + 1) * 64)
            x = hf_ref[c, hs, :] + hr_ref[c, hs, :]
            ys.append(_rms(x, gb_ref[...], axis=0) * jax.nn.sigmoid(ot_ref[c, hs, :]))
        y_ref[_ds(c, TC), :] = jnp.concatenate(ys, axis=0).T.astype(BF16)
        return carry

    if nc == 1:
        epilogue(0, 0)
    else:
        lax.fori_loop(0, nc, epilogue, 0)


def _mlstm(z, zt, stats, acols, s0, m0, gb, *, l, latent):
    ls = min(l, s0.shape[1] - 1)
    if latent:
        nb, seq, row0 = DEC_BATCH, DEC_SEQ, N_CTX
    else:
        nb, seq, row0 = BATCH, SEQ, 0
    blk0 = row0 // seq
    nc = seq // TC
    ztspec = lambda rb: pl.BlockSpec((nc, 256, TC), lambda b: (blk0 + b, rb, 0))
    return pl.pallas_call(
        functools.partial(_mlstm_kernel, seq=seq),
        grid=(nb,),
        in_specs=[
            pl.BlockSpec((seq, 256), lambda b: (blk0 + b, Z_KB // 256)),
            ztspec(ZT_QB // 256), ztspec(ZT_VB // 256), ztspec(ZT_OB // 256),
            pl.BlockSpec((nc, GS_ROWS, TC), lambda b: (blk0 + b, 0, 0)),
            pl.BlockSpec((nc, TC, 128), lambda b: (blk0 + b, 0, 0)),
            pl.BlockSpec((1, None, 2 * H_B, 128, 64), lambda b: (b, ls, 0, 0, 0)),
            pl.BlockSpec((1, None, 2 * H_B, 128), lambda b: (b, ls, 0, 0)),
            _const_spec((64, 1), (l,)),
        ],
        out_specs=[
            pl.BlockSpec((seq, 256), lambda b: (b, 0)),
            pl.BlockSpec((1, 2 * H_B, 128, 64), lambda b: (b, 0, 0, 0)),
            pl.BlockSpec((1, 2 * H_B, 128), lambda b: (b, 0, 0)),
        ],
        out_shape=[
            jax.ShapeDtypeStruct((nb * seq, 256), BF16),
            jax.ShapeDtypeStruct((nb, 2 * H_B, 128, 64), F32),
            jax.ShapeDtypeStruct((nb, 2 * H_B, 128), F32),
        ],
        scratch_shapes=[pltpu.VMEM((nc, 256, TC), F32), pltpu.VMEM((nc, 256, TC), F32)],
        compiler_params=_cparams(("arbitrary",)),
        name=f"mlstm_{'lat' if latent else 'ctx'}",
    )(z, zt, zt, zt, stats, acols, s0, m0, gb)


def _rope_tables(dim):
    t = jnp.arange(DEC_SEQ)
    row = (t // GRID_W).astype(F32)
    colp = (t % GRID_W).astype(F32)
    axis_dim = dim // 2
    freqs = ROPE_THETA ** (-jnp.arange(0, axis_dim, 2, dtype=F32) / axis_dim)
    ang = jnp.concatenate([row[:, None] * freqs, colp[:, None] * freqs], axis=-1)
    cos, sin = jnp.cos(ang), jnp.sin(ang)
    reps = 64 // dim
    cq = jnp.concatenate([cos, cos] * reps, axis=1)
    sq = jnp.concatenate([-sin, sin] * reps, axis=1)
    to_chunks = lambda x: x.T.reshape(64, DEC_SEQ // TC, TC).transpose(1, 0, 2)
    return cq, sq, to_chunks(cq), to_chunks(sq)


def _split_w_in(w, b_gates):
    a0, b0, c0 = 0, 768, 1808
    qa, ka, va = w[..., a0:a0 + 512], w[..., a0 + 512:a0 + 640], w[..., a0 + 640:a0 + 768]
    qb, kb = w[..., b0:b0 + 256], w[..., b0 + 256:b0 + 512]
    vb, ob = w[..., b0 + 512:b0 + 768], w[..., b0 + 768:b0 + 1024]
    gb = w[..., b0 + 1024:b0 + 1040]
    qc, kc, vc = w[..., c0:c0 + 256], w[..., c0 + 256:c0 + 512], w[..., c0 + 512:c0 + 768]
    wz = jnp.concatenate([kb, kc, ka], axis=-1)
    wt = jnp.swapaxes(jnp.concatenate([qa, qc, vc, qb, vb, ob, va, gb], axis=-1), -1, -2)
    bt = jnp.zeros((DEPTH, ZT_W, 1), F32).at[:, ZT_G:ZT_G + 16, 0].set(b_gates)
    return wz.astype(BF16), wt.astype(BF16), bt


def kernel(x_prompt, x_sample, cache_a_k, cache_a_v, cache_c_k, cache_c_v, state_b_C, state_b_n,
           state_b_m, c, c_ctx, w_ada, b_ada, g_norm, w_ff_in, w_ff_out, w_in, w_out, g_qa, g_ka,
           b_gates, g_b, lam_q1, lam_k1, lam_q2, lam_k2, g_c, g_final):
    h = (x_prompt.reshape(N_CTX, D_MODEL), x_sample.reshape(N_LAT, D_MODEL))
    cc = jnp.concatenate([c_ctx[None, :], c, jnp.zeros((3, D_MODEL), F32)], axis=0)
    mods = _ada(cc, w_ada, b_ada).reshape(DEPTH, 8, 9, D_MODEL)
    tab_a = _rope_tables(HD_A)
    tab_c = _rope_tables(D_C)

    gf = g_final.reshape(1, D_MODEL)
    gn = g_norm.reshape(DEPTH, 3, 1, D_MODEL)
    wi, wo, wmix = w_ff_in.astype(BF16), w_ff_out.astype(BF16), w_out.astype(BF16)
    wz, wt, bt = _split_w_in(w_in, b_gates)
    par_a = (g_qa.reshape(DEPTH, 64, 1), g_ka.reshape(DEPTH, 1, 64))
    par_c = (jnp.stack([lam_q1, lam_k1, lam_q2, lam_k2], axis=1), g_c.reshape(DEPTH, 64, 1))
    gb = g_b.reshape(DEPTH, 64, 1)
    ctx_a = (cache_a_k.reshape(DEC_BATCH, DEPTH, PAST_LEN, KV_A * HD_A), cache_a_v.transpose(0, 1, 3, 4, 2))
    ctx_c = (cache_c_k.reshape(DEC_BATCH, DEPTH, PAST_LEN, H_C * 2 * D_C), cache_c_v.transpose(0, 1, 3, 4, 2))
    s0 = jnp.concatenate([jnp.swapaxes(state_b_C, -1, -2), state_b_n[..., None, :],
                          jnp.zeros((DEC_BATCH, DEPTH, 2, H_B, 63, DK_B), F32)], axis=-2)
    s0 = s0.reshape(DEC_BATCH, DEPTH, 2 * H_B, 128, DK_B)
    m0 = jnp.broadcast_to(state_b_m.reshape(DEC_BATCH, DEPTH, 2 * H_B, 1), (DEC_BATCH, DEPTH, 2 * H_B, 128))
    zeros_s = jnp.zeros((BATCH, 1, 2 * H_B, 128, DK_B), F32)
    zeros_m = jnp.zeros((BATCH, 1, 2 * H_B, 128), F32)

    outs = {k: [] for k in ("ak", "av", "ck", "cv", "bC", "bn", "bm")}
    for l in range(DEPTH):
        lam_init = 0.8 - 0.6 * math.exp(-0.3 * l)
        h = _ffn(h, mods, gn, wi, wo, gf, l=l, half=0, first=(l == 0))
        z, zt = _proj_in(h, mods, gn, wz, wt, bt, l=l)

        ya_c, ak, av = _attn("A", z, zt, ctx=None, tables=None, params=par_a, l=l, lam_init=lam_init,
                             latent=False)
        yc_c, ck, cv = _attn("C", z, zt, ctx=None, tables=None, params=par_c, l=l, lam_init=lam_init,
                             latent=False)
        stats, acols = _gate_stats(zt)
        yb_c, s_fin, m_fin = _mlstm(z, zt, stats, acols, zeros_s, zeros_m, gb, l=l, latent=False)

        (ya_l,) = _attn("A", z, zt, ctx=ctx_a, tables=tab_a, params=par_a, l=l, lam_init=lam_init,
                        latent=True)
        (yc_l,) = _attn("C", z, zt, ctx=ctx_c, tables=tab_c, params=par_c, l=l, lam_init=lam_init,
                        latent=True)
        yb_l, _, _ = _mlstm(z, zt, stats, acols, s0, m0, gb, l=l, latent=True)

        h = _ffn(h, mods, gn, wi, wo, gf, l=l, half=1,
                 mix=((ya_c, ya_l), (yb_c, yb_l), (yc_c, yc_l), wmix), final=(l == DEPTH - 1))

        outs["ak"].append(ak.reshape(BATCH, SEQ, KV_A, HD_A))
        outs["av"].append(av.reshape(BATCH, SEQ, KV_A, HD_A))
        outs["ck"].append(ck.reshape(BATCH, SEQ, H_C, 2, D_C))
        outs["cv"].append(cv.reshape(BATCH, SEQ, H_C, 2 * D_C))
        s_fin = s_fin.reshape(BATCH, 2, H_B, 128, DK_B)
        outs["bC"].append(jnp.swapaxes(s_fin[..., 0:64, :], -1, -2))
        outs["bn"].append(s_fin[..., 64, :])
        outs["bm"].append(m_fin[:, :, 0].reshape(BATCH, 2, H_B))

    y_prompt = h[0].reshape(BATCH, SEQ, D_MODEL)
    y_sample = h[1].reshape(DEC_BATCH, DEC_SEQ, D_MODEL)
    st = lambda k: jnp.stack(outs[k], axis=1)
    return (y_prompt, y_sample, st("ak"), st("av"), st("ck"), st("cv"), st("bC"), st("bn"), st("bm"))
```

```python
import functools
import math

import jax
import jax.numpy as jnp
from jax import lax
from jax.experimental import pallas as pl
from jax.experimental.pallas import tpu as pltpu

F32 = jnp.float32
BF16 = jnp.bfloat16

D_MODEL = 1024
BATCH = 16
SEQ = 256
DEPTH = 2
DEC_BATCH = 4
DEC_SEQ = 2048
PAST_LEN = 512
GRID_W = 64
ROPE_THETA = 10000.0
EPS = 1e-6
HD_A = 64
H_A = 8
KV_A = 2
DK_B = 64
H_B = 4
D_C = 32
H_C = 4
D_FF = 2816
LOG2E = 1.4426950408889634

N_CTX = BATCH * SEQ
N_LAT = DEC_BATCH * DEC_SEQ
N_TOK = N_CTX + N_LAT

TM = 512
CTX_TILES = N_CTX // TM
LAT_TILES_PER_BATCH = DEC_SEQ // TM
FF_CHUNKS = ((0, 1536), (1536, D_FF))
FF_ROWS = 256
TC = 256
SB = 256
TQ = 256
VROWS = 80
BOUND_SLACK = 1.0 + 2.0 ** -6
MAX_BOUND_GAP = 60.0

Z_KB, Z_KC, Z_KA = 0, 256, 512
Z_W = 640
ZT_QA, ZT_QC, ZT_VC, ZT_QB, ZT_VB, ZT_OB, ZT_VA, ZT_G = 0, 512, 768, 1024, 1280, 1536, 1792, 1920
ZT_W = 1936

VMEM_LIMIT = 56 * 1024 * 1024


def _cparams(sem):
    return pltpu.CompilerParams(dimension_semantics=sem, vmem_limit_bytes=VMEM_LIMIT)


def _const_spec(shape, lead=()):
    nd = len(shape)
    return pl.BlockSpec((None,) * len(lead) + tuple(shape), lambda *_: tuple(lead) + (0,) * nd,
                        pipeline_mode=pl.Buffered(1))


def _mod_row(i):
    return jnp.maximum(i - CTX_TILES, 0) // LAT_TILES_PER_BATCH + (i >= CTX_TILES).astype(jnp.int32)


def _rms(x, g, axis=-1):
    ms = jnp.mean(x * x, axis=axis, keepdims=True)
    return x * lax.rsqrt(ms + EPS) * g


def _norm_mod(x, g, mod_ref, k):
    shift = mod_ref[0, k:k + 1, :]
    scale = mod_ref[0, k + 1:k + 2, :]
    return _rms(x, g) * (1.0 + scale) + shift


def _ds(c, n):
    if isinstance(c, int):
        return pl.ds(c * n, n)
    return pl.ds(pl.multiple_of(c * n, n), n)


def _log_sigmoid(x):
    return jnp.minimum(x, 0.0) - jnp.log1p(jnp.exp(-jnp.abs(x)))


def _ada_kernel(c_ref, w_ref, b_ref, o_ref):
    c = c_ref[...]
    a = (c * jax.nn.sigmoid(c)).astype(BF16)
    o_ref[0] = jnp.dot(a, w_ref[0].astype(BF16), preferred_element_type=F32) + b_ref[0]


def _ada(cc, w_ada, b_ada):
    tn = 1024
    n = 9 * D_MODEL
    return pl.pallas_call(
        _ada_kernel,
        grid=(DEPTH, n // tn),
        in_specs=[
            pl.BlockSpec((8, D_MODEL), lambda l, j: (0, 0)),
            pl.BlockSpec((1, D_MODEL, tn), lambda l, j: (l, 0, j)),
            pl.BlockSpec((1, 1, tn), lambda l, j: (l, 0, j)),
        ],
        out_specs=pl.BlockSpec((1, 8, tn), lambda l, j: (l, 0, j)),
        out_shape=jax.ShapeDtypeStruct((DEPTH, 8, n), F32),
        compiler_params=_cparams(("arbitrary", "arbitrary")),
        name="ada",
    )(cc, w_ada, b_ada.reshape(DEPTH, 1, n))


def _ffn_kernel(*refs, k, first, mix, final):
    is_ctx = pl.program_id(0) < CTX_TILES
    it = iter(refs)
    x_refs = (next(it), next(it)) if first else (next(it),)
    if mix:
        y_refs = [(next(it), next(it)) for _ in range(3)]
        wmix_ref = next(it)
    mod_ref, g_ref, wi_ref, wo_ref, gf_ref = (next(it) for _ in range(5))

    nrows = TM if mix else FF_ROWS
    ys = []
    for r in range(TM // nrows):
        rows = slice(r * nrows, (r + 1) * nrows)
        pick = lambda c_ref, l_ref: jnp.where(is_ctx, c_ref[rows, :], l_ref[rows, :])
        x = pick(*x_refs) if first else x_refs[0][rows, :]
        if mix:
            ya, yb, yc = (pick(*pair) for pair in y_refs)
            y = jnp.dot(ya, wmix_ref[0:512, :], preferred_element_type=F32)
            y = y + jnp.dot(yb, wmix_ref[512:768, :], preferred_element_type=F32)
            y = y + jnp.dot(yc, wmix_ref[768:1024, :], preferred_element_type=F32)
            x = x + mod_ref[0, 5:6, :] * y
        xn = _norm_mod(x, g_ref[...], mod_ref, k).astype(BF16)
        acc = jnp.zeros((nrows, D_MODEL), F32)
        for lo, hi in FF_CHUNKS:
            g = jnp.dot(xn, wi_ref[:, lo:hi], preferred_element_type=F32)
            u = jnp.dot(xn, wi_ref[:, D_FF + lo:D_FF + hi], preferred_element_type=F32)
            a = (g * jax.nn.sigmoid(g) * u).astype(BF16)
            acc = acc + jnp.dot(a, wo_ref[lo:hi, :], preferred_element_type=F32)
        y = x + 0.5 * mod_ref[0, k + 2:k + 3, :] * acc
        if final:
            ys.append(_rms(y, gf_ref[...]))
        else:
            refs[-1][rows, :] = y
    if final:
        yc_ref, yl_ref = refs[-2:]
        y = jnp.concatenate(ys, axis=0)

        @pl.when(is_ctx)
        def _():
            yc_ref[...] = y

        @pl.when(jnp.logical_not(is_ctx))
        def _():
            yl_ref[...] = y


def _ctx_lat_specs(width):
    return [
        pl.BlockSpec((TM, width), lambda i: (jnp.minimum(i, CTX_TILES - 1), 0)),
        pl.BlockSpec((TM, width), lambda i: (jnp.maximum(i - CTX_TILES, 0), 0)),
    ]


def _mod_spec(l):
    return pl.BlockSpec((None, 1, 9, D_MODEL), lambda i: (l, _mod_row(i), 0, 0))


def _ffn(h, mods, g, wi, wo, gf, *, l, half, first=False, mix=None, final=False):
    k = 6 * half
    flat_spec = pl.BlockSpec((TM, D_MODEL), lambda i: (i, 0))
    flat_shape = jax.ShapeDtypeStruct((N_TOK, D_MODEL), F32)
    mix_specs, mix_args = [], []
    if mix is not None:
        mix_specs = (_ctx_lat_specs(512) + _ctx_lat_specs(256) + _ctx_lat_specs(256)
                     + [_const_spec((D_MODEL, D_MODEL), (l,))])
        mix_args = [*mix[0], *mix[1], *mix[2], mix[3]]
    return pl.pallas_call(
        functools.partial(_ffn_kernel, k=k, first=first, mix=mix is not None, final=final),
        grid=(N_TOK // TM,),
        in_specs=(_ctx_lat_specs(D_MODEL) if first else [flat_spec]) + mix_specs + [
            _mod_spec(l),
            _const_spec((1, D_MODEL), (l, 2 * half)),
            _const_spec((D_MODEL, 2 * D_FF), (l, half)),
            _const_spec((D_FF, D_MODEL), (l, half)),
            _const_spec((1, D_MODEL)),
        ],
        out_specs=_ctx_lat_specs(D_MODEL) if final else flat_spec,
        out_shape=[jax.ShapeDtypeStruct((N_CTX, D_MODEL), F32),
                   jax.ShapeDtypeStruct((N_LAT, D_MODEL), F32)] if final else flat_shape,
        compiler_params=_cparams(("arbitrary",)),
        name="ffn_final" if final else ("ffn_first" if first else ("ffn_mix" if mix else "ffn")),
    )(*(h if first else (h,)), *mix_args, mods, g, wi, wo, gf)


def _proj_in_kernel(h_ref, mod_ref, g_ref, w_ref, wt_ref, bt_ref, z_ref, zt_ref):
    xn = _norm_mod(h_ref[...], g_ref[...], mod_ref, 3).astype(BF16)
    z_ref[...] = jnp.dot(xn, w_ref[...], preferred_element_type=F32)
    zt = lax.dot_general(wt_ref[...], xn, (((1,), (1,)), ((), ())), preferred_element_type=F32)
    zt = zt + bt_ref[...]
    for c in range(TM // TC):
        zt_ref[c] = zt[:, c * TC:(c + 1) * TC]


def _proj_in(h, mods, g, wz, wt, bt, *, l):
    return pl.pallas_call(
        _proj_in_kernel,
        grid=(N_TOK // TM,),
        in_specs=[
            pl.BlockSpec((TM, D_MODEL), lambda i: (i, 0)),
            _mod_spec(l),
            _const_spec((1, D_MODEL), (l, 1)),
            _const_spec((D_MODEL, Z_W), (l,)),
            _const_spec((ZT_W, D_MODEL), (l,)),
            _const_spec((ZT_W, 1), (l,)),
        ],
        out_specs=[
            pl.BlockSpec((TM, Z_W), lambda i: (i, 0)),
            pl.BlockSpec((TM // TC, ZT_W, TC), lambda i: (i, 0, 0)),
        ],
        out_shape=[
            jax.ShapeDtypeStruct((N_TOK, Z_W), F32),
            jax.ShapeDtypeStruct((N_TOK // TC, ZT_W, TC), F32),
        ],
        compiler_params=_cparams(("arbitrary",)),
        name="proj_in",
    )(h, mods, g, wz, wt, bt)


def _swap_halves(x, width, axis):
    n = x.shape[axis]
    half = width // 2
    parts = []
    for s in range(0, n, width):
        parts.append(lax.slice_in_dim(x, s + half, s + width, axis=axis))
        parts.append(lax.slice_in_dim(x, s, s + half, axis=axis))
    return jnp.concatenate(parts, axis=axis)


def _attn_kernel(*refs, mode, seq, has_ctx, rope, emit_cache, lam_init):
    it = iter(refs)
    qt_ref, k_ref, vt_ref = next(it), next(it), next(it)
    if has_ctx:
        ck_ref, cvt_ref = next(it), next(it)
    if rope:
        cqt_ref, sqt_ref, ck_tab, sk_tab = next(it), next(it), next(it), next(it)
    if mode == "A":
        gq_ref, gk_ref = next(it), next(it)
    else:
        lam_ref, gc_ref = next(it), next(it)
    y_ref = next(it)
    if emit_cache:
        kc_ref, vc_ref = next(it), next(it)
    kbuf, vbuf, p_ref, kmax_ref = next(it), next(it), next(it), next(it)

    n_kv = KV_A if mode == "A" else H_C
    per_kv = 8 // n_kv
    n_ctx = PAST_LEN if has_ctx else 0
    lk = n_ctx + seq
    width = HD_A if mode == "A" else D_C

    @pl.when(pl.program_id(1) == 0)
    def _fill():
        ones_rows = (lax.broadcasted_iota(jnp.int32, (VROWS - 64, lk), 0) == 0).astype(BF16)
        ones_cols = jnp.ones((64, 128), BF16)

        def max_sq_norm(kb):
            if lk == SB:
                return jnp.zeros((1, 1), F32)
            kf = kb.astype(F32)
            sq = jnp.dot((kf * kf).astype(BF16), ones_cols, preferred_element_type=F32)
            return jnp.max(sq, axis=0, keepdims=True)[:, 0:1]

        for h in range(n_kv):
            hs = slice(h * 64, (h + 1) * 64)
            vbuf[h, 64:VROWS, :] = ones_rows
            kmax2 = jnp.zeros((1, 1), F32)
            if has_ctx:
                kb = ck_ref[0, :, hs].astype(BF16)
                kmax2 = jnp.maximum(kmax2, max_sq_norm(kb))
                kbuf[h, 0:n_ctx, :] = kb
                vbuf[h, 0:64, 0:n_ctx] = cvt_ref[0, h].astype(BF16)
            for c in range(seq // TC):
                rows = slice(c * TC, (c + 1) * TC)
                k = k_ref[rows, hs]
                if mode == "A":
                    k = _rms(k, gk_ref[...])
                if emit_cache:
                    kc_ref[0, rows, hs] = k
                if rope:
                    k = k * ck_tab[rows, :] + _swap_halves(k, width, 1) * sk_tab[rows, :]
                kb = k.astype(BF16)
                kmax2 = jnp.maximum(kmax2, max_sq_norm(kb))
                kbuf[h, n_ctx + c * TC:n_ctx + (c + 1) * TC, :] = kb
                vbuf[h, 0:64, n_ctx + c * TC:n_ctx + (c + 1) * TC] = vt_ref[c, hs, :].astype(BF16)
            kmax_ref[h] = jnp.broadcast_to(jnp.sqrt(kmax2), (8, 128))
        if emit_cache:
            for c in range(seq // TC):
                vc_ref[0, c * TC:(c + 1) * TC, :] = vt_ref[c].T

    qts = []
    if mode == "A":
        scale = HD_A ** -0.5 * LOG2E
        for hh in range(H_A):
            x = _rms(qt_ref[0, hh * 64:(hh + 1) * 64, :], gq_ref[...], axis=0)
            if rope:
                x = x * cqt_ref[0] + _swap_halves(x, width, 0) * sqt_ref[0]
            qts.append((x * scale).astype(BF16))
    else:
        scale = D_C ** -0.5 * LOG2E
        first_map = lax.broadcasted_iota(jnp.int32, (64, TQ), 0) < D_C
        for h in range(H_C):
            x = qt_ref[0, h * 64:(h + 1) * 64, :]
            if rope:
                x = x * cqt_ref[0] + _swap_halves(x, width, 0) * sqt_ref[0]
            x = x * scale
            qts.append(jnp.where(first_map, x, 0.0).astype(BF16))
            qts.append(jnp.where(first_map, 0.0, x).astype(BF16))

    def value_matmul(u):
        ox = jnp.dot(vbuf[u // per_kv], p_ref[:, u * TQ:(u + 1) * TQ], preferred_element_type=F32)
        return ox[0:64] / ox[64:65]

    def emit(outs):
        if mode == "A":
            for pair in range(4):
                o = jnp.concatenate(outs[2 * pair:2 * pair + 2], axis=0)
                y_ref[:, pair * 128:(pair + 1) * 128] = o.T.astype(BF16)
        else:
            lp = lam_ref[...]
            lam = (jnp.exp(jnp.sum(lp[0:1] * lp[1:2], axis=1, keepdims=True))
                   - jnp.exp(jnp.sum(lp[2:3] * lp[3:4], axis=1, keepdims=True)) + lam_init)
            for pair in range(2):
                o = []
                for h in (2 * pair, 2 * pair + 1):
                    d = outs[2 * h] - lam * outs[2 * h + 1]
                    o.append(_rms(d, gc_ref[...], axis=0) * (1.0 - lam_init))
                y_ref[:, pair * 128:(pair + 1) * 128] = jnp.concatenate(o, axis=0).T.astype(BF16)

    outs = []
    if lk == SB:
        sts = [jnp.dot(kbuf[u // per_kv], qts[u], preferred_element_type=F32) for u in range(8)]
        ps = [jnp.exp2(st - jnp.max(st, axis=0, keepdims=True)).astype(BF16) for st in sts]
        for u in range(8):
            ox = jnp.dot(vbuf[u // per_kv], ps[u], preferred_element_type=F32)
            outs.append(ox[0:64] / ox[64:65])
        emit(outs)
        return

    mhats, ms = [], []
    for u in range(8):
        g = u // per_kv
        qn = jnp.sqrt(jnp.sum(jnp.square(qts[u].astype(F32)), axis=0, keepdims=True))
        mhat = qn * kmax_ref[g][0:1, 0:1] * BOUND_SLACK
        mrun = jnp.full((8, TQ), -jnp.inf, F32)
        for j in range(lk // SB):
            rows = slice(j * SB, (j + 1) * SB)
            st = jnp.dot(kbuf[g, rows, :], qts[u], preferred_element_type=F32)
            p_ref[rows, u * TQ:(u + 1) * TQ] = jnp.exp2(st - mhat).astype(BF16)
            for r in range(SB // 8):
                mrun = jnp.maximum(mrun, st[8 * r:8 * r + 8])
        mhats.append(mhat)
        ms.append(jnp.max(mrun, axis=0, keepdims=True))
        if u >= 1:
            outs.append(value_matmul(u - 1))
    outs.append(value_matmul(7))
    emit(outs)

    worst_gap = functools.reduce(jnp.maximum, [jnp.max(mh - m) for mh, m in zip(mhats, ms)])

    @pl.when(worst_gap > MAX_BOUND_GAP)
    def _redo():
        redone = []
        for u in range(8):
            def block(j, carry, u=u):
                rows = _ds(j, SB)
                st = jnp.dot(kbuf[u // per_kv, rows, :], qts[u], preferred_element_type=F32)
                p_ref[rows, u * TQ:(u + 1) * TQ] = jnp.exp2(st - ms[u]).astype(BF16)
                return carry

            lax.fori_loop(0, lk // SB, block, 0)
            redone.append(value_matmul(u))
        emit(redone)


def _attn(mode, z, zt, *, ctx, tables, params, l, lam_init, latent):
    if latent:
        nb, seq, row0 = DEC_BATCH, DEC_SEQ, N_CTX
    else:
        nb, seq, row0 = BATCH, SEQ, 0
    nq = seq // TQ
    n_kv = KV_A if mode == "A" else H_C
    n_ctx = PAST_LEN if ctx is not None else 0
    lk = n_ctx + seq
    qw = 512 if mode == "A" else 256
    vw = n_kv * 64
    q_blk = (ZT_QA if mode == "A" else ZT_QC) // qw
    k_blk = (Z_KA if mode == "A" else Z_KC) // vw
    v_blk = (ZT_VA if mode == "A" else ZT_VC) // vw
    emit_cache = not latent
    blk0 = row0 // seq

    in_specs = [
        pl.BlockSpec((1, qw, TC), lambda b, i: (row0 // TC + b * nq + i, q_blk, 0)),
        pl.BlockSpec((seq, vw), lambda b, i: (blk0 + b, k_blk)),
        pl.BlockSpec((seq // TC, vw, TC), lambda b, i: (blk0 + b, v_blk, 0)),
    ]
    args = [zt, z, zt]
    if ctx is not None:
        in_specs += [
            pl.BlockSpec((1, None, PAST_LEN, vw), lambda b, i: (b, l, 0, 0)),
            pl.BlockSpec((1, None, n_kv, 64, PAST_LEN), lambda b, i: (b, l, 0, 0, 0)),
        ]
        args += list(ctx)
    if tables is not None:
        cq, sq, cqt, sqt = tables
        in_specs += [
            pl.BlockSpec((1, 64, TC), lambda b, i: (i, 0, 0)),
            pl.BlockSpec((1, 64, TC), lambda b, i: (i, 0, 0)),
            _const_spec((seq, 64)),
            _const_spec((seq, 64)),
        ]
        args += [cqt, sqt, cq, sq]
    for p in params:
        in_specs.append(_const_spec(p.shape[1:], (l,)))
        args.append(p)

    out_specs = [pl.BlockSpec((TQ, qw), lambda b, i: (b * nq + i, 0))]
    out_shape = [jax.ShapeDtypeStruct((nb * seq, qw), BF16)]
    if emit_cache:
        out_specs += [pl.BlockSpec((1, seq, vw), lambda b, i: (b, 0, 0))] * 2
        out_shape += [jax.ShapeDtypeStruct((nb, seq, vw), F32)] * 2

    return pl.pallas_call(
        functools.partial(_attn_kernel, mode=mode, seq=seq, has_ctx=ctx is not None,
                          rope=tables is not None, emit_cache=emit_cache, lam_init=lam_init),
        grid=(nb, nq),
        in_specs=in_specs,
        out_specs=out_specs,
        out_shape=out_shape,
        scratch_shapes=[
            pltpu.VMEM((n_kv, lk, 64), BF16),
            pltpu.VMEM((n_kv, VROWS, lk), BF16),
            pltpu.VMEM((lk, 8 * TQ), BF16),
            pltpu.VMEM((n_kv, 8, 128), F32),
        ],
        compiler_params=_cparams(("arbitrary", "arbitrary")),
        name=f"attn_{mode}_{'lat' if latent else 'ctx'}",
    )(*args)


N_CHUNKS = N_TOK // TC
GS_A, GS_B, GS_LM, GS_BLAST, GS_ALAST, GS_ROWS = 0, 8, 16, 24, 32, 40


def _scan_lanes(x, op, fill, reverse):
    axis = x.ndim - 1
    lane = lax.broadcasted_iota(jnp.int32, x.shape, axis)
    k = 1
    while k < TC:
        if reverse:
            x = op(x, jnp.where(lane < TC - k, pltpu.roll(x, TC - k, axis), fill))
        else:
            x = op(x, jnp.where(lane >= k, pltpu.roll(x, k, axis), fill))
        k *= 2
    return x


def _gate_stats_kernel(g_ref, st_ref, ac_ref):
    is_fwd = lax.broadcasted_iota(jnp.int32, (N_CHUNKS, 8, TC), 1) < H_B
    g = g_ref[...]
    gi = jnp.concatenate([g[:, 0:4], g[:, 8:12]], axis=1)
    f = _log_sigmoid(jnp.concatenate([g[:, 4:8], g[:, 12:16]], axis=1))
    b = jnp.where(is_fwd, _scan_lanes(f, jnp.add, 0.0, False), _scan_lanes(f, jnp.add, 0.0, True))
    a = gi - b
    lm = jnp.where(is_fwd, _scan_lanes(a, jnp.maximum, -jnp.inf, False),
                   _scan_lanes(a, jnp.maximum, -jnp.inf, True))
    st_ref[:, GS_A:GS_A + 8, :] = a
    st_ref[:, GS_B:GS_B + 8, :] = b
    st_ref[:, GS_LM:GS_LM + 8, :] = lm
    st_ref[:, GS_BLAST:GS_BLAST + 8, :] = jnp.broadcast_to(jnp.sum(f, axis=2, keepdims=True), a.shape)
    st_ref[:, GS_ALAST:GS_ALAST + 8, :] = jnp.broadcast_to(jnp.max(a, axis=2, keepdims=True), a.shape)

    def transpose_chunk(c, carry):
        ac_ref[c] = jnp.concatenate([st_ref[c, GS_A:GS_A + 8, :], jnp.zeros((120, TC), F32)], axis=0).T
        return carry

    lax.fori_loop(0, N_CHUNKS, transpose_chunk, 0, unroll=4)


def _gate_stats(zt):
    return pl.pallas_call(
        _gate_stats_kernel,
        grid=(1,),
        in_specs=[pl.BlockSpec((N_CHUNKS, 16, TC), lambda i: (0, ZT_G // 16, 0))],
        out_specs=[
            pl.BlockSpec((N_CHUNKS, GS_ROWS, TC), lambda i: (0, 0, 0)),
            pl.BlockSpec((N_CHUNKS, TC, 128), lambda i: (0, 0, 0)),
        ],
        out_shape=[
            jax.ShapeDtypeStruct((N_CHUNKS, GS_ROWS, TC), F32),
            jax.ShapeDtypeStruct((N_CHUNKS, TC, 128), F32),
        ],
        compiler_params=_cparams(("arbitrary",)),
        name="gate_stats",
    )(zt)


def _mlstm_kernel(k_ref, qt_ref, vt_ref, ot_ref, st_ref, ac_ref, s0_ref, m0_ref, gb_ref,
                  y_ref, s_ref, m_ref, hf_ref, hr_ref, *, seq):
    nc = seq // TC
    s_idx = lax.broadcasted_iota(jnp.int32, (TC, TC), 0)
    t_idx = lax.broadcasted_iota(jnp.int32, (TC, TC), 1)
    seen = (s_idx <= t_idx, s_idx >= t_idx)
    ones_rows = (lax.broadcasted_iota(jnp.int32, (64, TC), 0) == 0).astype(F32)
    units = [(d, h) for d in range(2) for h in range(H_B)]

    def chunk(j, carry):
        states, m = carry
        cs = (j, nc - 1 - j)
        stat = lambda r: jnp.concatenate([st_ref[cs[0], r:r + 4, :], st_ref[cs[1], r + 4:r + 8, :]], axis=0)
        a, b, lm = stat(GS_A), stat(GS_B), stat(GS_LM)
        mx = jnp.maximum(m, lm)
        w = jnp.exp(m - mx)
        floor = jnp.exp(-(b + mx))
        mxl = jnp.maximum(m, stat(GS_ALAST)[:, 0:1])
        decay = jnp.exp(m - mxl)
        wk = jnp.exp(a - mxl)
        m_new = stat(GS_BLAST)[:, 0:1] + mxl

        first = []
        for (d, h), st in zip(units, states):
            hs = slice(h * 64, (h + 1) * 64)
            kh = (k_ref[_ds(cs[d], TC), hs] * (DK_B ** -0.5)).astype(BF16)
            qt = qt_ref[cs[d], hs, :].astype(BF16)
            lhs = jnp.concatenate([kh, st.astype(BF16)], axis=0)
            first.append((kh, jnp.dot(lhs, qt, preferred_element_type=F32)))
        second = []
        for u, ((d, h), (kh, both)) in enumerate(zip(units, first)):
            a_col = ac_ref[cs[d], :, u:u + 1]
            dt = jnp.where(seen[d], a_col - mx[u:u + 1, :], -jnp.inf)
            sp = both[0:TC] * jnp.exp(dt)
            den = jnp.sum(sp, axis=0, keepdims=True) + w[u:u + 1, :] * both[TC + 64:TC + 65]
            second.append((sp.astype(BF16), den))
        new_states = []
        for u, ((d, h), st, (kh, both), (spb, den)) in enumerate(zip(units, states, first, second)):
            hs = slice(h * 64, (h + 1) * 64)
            vx = jnp.concatenate([vt_ref[cs[d], hs, :], ones_rows], axis=0)
            num = (jnp.dot(vx.astype(BF16), spb, preferred_element_type=F32)[0:64]
                   + w[u:u + 1, :] * both[TC:TC + 64])
            hv = num / jnp.maximum(jnp.abs(den), floor[u:u + 1, :])
            if d == 0:
                hf_ref[cs[d], hs, :] = hv
            else:
                hr_ref[cs[d], hs, :] = hv
            upd = jnp.dot((vx * wk[u:u + 1, :]).astype(BF16), kh, preferred_element_type=F32)
            new_states.append(decay[u:u + 1, :] * st + upd)
        return tuple(new_states), m_new

    init = (tuple(s0_ref[0, u] for u in range(2 * H_B)), m0_ref[0][:, 0:1])
    if nc == 1:
        states, m = chunk(0, init)
    else:
        states, m = lax.fori_loop(0, nc, chunk, init)
    for u, st in enumerate(states):
        s_ref[0, u] = st
    m_ref[0] = jnp.broadcast_to(m, (2 * H_B, 128))

    def epilogue(c, carry):
        ys = []
        for h in range(H_B):
            hs = slice(h * 64, (h + 1) * 64)
            x = hf_ref[c, hs, :] + hr_ref[c, hs, :]
            ys.append(_rms(x, gb_ref[...], axis=0) * jax.nn.sigmoid(ot_ref[c, hs, :]))
        y_ref[_ds(c, TC), :] = jnp.concatenate(ys, axis=0).T.astype(BF16)
        return carry

    if nc == 1:
        epilogue(0, 0)
    else:
        lax.fori_loop(0, nc, epilogue, 0)


def _mlstm(z, zt, stats, acols, s0, m0, gb, *, l, latent):
    ls = min(l, s0.shape[1] - 1)
    if latent:
        nb, seq, row0 = DEC_BATCH, DEC_SEQ, N_CTX
    else:
        nb, seq, row0 = BATCH, SEQ, 0
    blk0 = row0 // seq
    nc = seq // TC
    ztspec = lambda rb: pl.BlockSpec((nc, 256, TC), lambda b: (blk0 + b, rb, 0))
    return pl.pallas_call(
        functools.partial(_mlstm_kernel, seq=seq),
        grid=(nb,),
        in_specs=[
            pl.BlockSpec((seq, 256), lambda b: (blk0 + b, Z_KB // 256)),
            ztspec(ZT_QB // 256), ztspec(ZT_VB // 256), ztspec(ZT_OB // 256),
            pl.BlockSpec((nc, GS_ROWS, TC), lambda b: (blk0 + b, 0, 0)),
            pl.BlockSpec((nc, TC, 128), lambda b: (blk0 + b, 0, 0)),
            pl.BlockSpec((1, None, 2 * H_B, 128, 64), lambda b: (b, ls, 0, 0, 0)),
            pl.BlockSpec((1, None, 2 * H_B, 128), lambda b: (b, ls, 0, 0)),
            _const_spec((64, 1), (l,)),
        ],
        out_specs=[
            pl.BlockSpec((seq, 256), lambda b: (b, 0)),
            pl.BlockSpec((1, 2 * H_B, 128, 64), lambda b: (b, 0, 0, 0)),
            pl.BlockSpec((1, 2 * H_B, 128), lambda b: (b, 0, 0)),
        ],
        out_shape=[
            jax.ShapeDtypeStruct((nb * seq, 256), BF16),
            jax.ShapeDtypeStruct((nb, 2 * H_B, 128, 64), F32),
            jax.ShapeDtypeStruct((nb, 2 * H_B, 128), F32),
        ],
        scratch_shapes=[pltpu.VMEM((nc, 256, TC), F32), pltpu.VMEM((nc, 256, TC), F32)],
        compiler_params=_cparams(("arbitrary",)),
        name=f"mlstm_{'lat' if latent else 'ctx'}",
    )(z, zt, zt, zt, stats, acols, s0, m0, gb)


def _rope_tables(dim):
    t = jnp.arange(DEC_SEQ)
    row = (t // GRID_W).astype(F32)
    colp = (t % GRID_W).astype(F32)
    axis_dim = dim // 2
    freqs = ROPE_THETA ** (-jnp.arange(0, axis_dim, 2, dtype=F32) / axis_dim)
    ang = jnp.concatenate([row[:, None] * freqs, colp[:, None] * freqs], axis=-1)
    cos, sin = jnp.cos(ang), jnp.sin(ang)
    reps = 64 // dim
    cq = jnp.concatenate([cos, cos] * reps, axis=1)
    sq = jnp.concatenate([-sin, sin] * reps, axis=1)
    to_chunks = lambda x: x.T.reshape(64, DEC_SEQ // TC, TC).transpose(1, 0, 2)
    return cq, sq, to_chunks(cq), to_chunks(sq)


def _split_w_in(w, b_gates):
    a0, b0, c0 = 0, 768, 1808
    qa, ka, va = w[..., a0:a0 + 512], w[..., a0 + 512:a0 + 640], w[..., a0 + 640:a0 + 768]
    qb, kb = w[..., b0:b0 + 256], w[..., b0 + 256:b0 + 512]
    vb, ob = w[..., b0 + 512:b0 + 768], w[..., b0 + 768:b0 + 1024]
    gb = w[..., b0 + 1024:b0 + 1040]
    qc, kc, vc = w[..., c0:c0 + 256], w[..., c0 + 256:c0 + 512], w[..., c0 + 512:c0 + 768]
    wz = jnp.concatenate([kb, kc, ka], axis=-1)
    wt = jnp.swapaxes(jnp.concatenate([qa, qc, vc, qb, vb, ob, va, gb], axis=-1), -1, -2)
    bt = jnp.zeros((DEPTH, ZT_W, 1), F32).at[:, ZT_G:ZT_G + 16, 0].set(b_gates)
    return wz.astype(BF16), wt.astype(BF16), bt


def kernel(x_prompt, x_sample, cache_a_k, cache_a_v, cache_c_k, cache_c_v, state_b_C, state_b_n,
           state_b_m, c, c_ctx, w_ada, b_ada, g_norm, w_ff_in, w_ff_out, w_in, w_out, g_qa, g_ka,
           b_gates, g_b, lam_q1, lam_k1, lam_q2, lam_k2, g_c, g_final):
    h = (x_prompt.reshape(N_CTX, D_MODEL), x_sample.reshape(N_LAT, D_MODEL))
    cc = jnp.concatenate([c_ctx[None, :], c, jnp.zeros((3, D_MODEL), F32)], axis=0)
    mods = _ada(cc, w_ada, b_ada).reshape(DEPTH, 8, 9, D_MODEL)
    tab_a = _rope_tables(HD_A)
    tab_c = _rope_tables(D_C)

    gf = g_final.reshape(1, D_MODEL)
    gn = g_norm.reshape(DEPTH, 3, 1, D_MODEL)
    wi, wo, wmix = w_ff_in.astype(BF16), w_ff_out.astype(BF16), w_out.astype(BF16)
    wz, wt, bt = _split_w_in(w_in, b_gates)
    par_a = (g_qa.reshape(DEPTH, 64, 1), g_ka.reshape(DEPTH, 1, 64))
    par_c = (jnp.stack([lam_q1, lam_k1, lam_q2, lam_k2], axis=1), g_c.reshape(DEPTH, 64, 1))
    gb = g_b.reshape(DEPTH, 64, 1)
    ctx_a = (cache_a_k.reshape(DEC_BATCH, DEPTH, PAST_LEN, KV_A * HD_A), cache_a_v.transpose(0, 1, 3, 4, 2))
    ctx_c = (cache_c_k.reshape(DEC_BATCH, DEPTH, PAST_LEN, H_C * 2 * D_C), cache_c_v.transpose(0, 1, 3, 4, 2))
    s0 = jnp.concatenate([jnp.swapaxes(state_b_C, -1, -2), state_b_n[..., None, :],
                          jnp.zeros((DEC_BATCH, DEPTH, 2, H_B, 63, DK_B), F32)], axis=-2)
    s0 = s0.reshape(DEC_BATCH, DEPTH, 2 * H_B, 128, DK_B)
    m0 = jnp.broadcast_to(state_b_m.reshape(DEC_BATCH, DEPTH, 2 * H_B, 1), (DEC_BATCH, DEPTH, 2 * H_B, 128))
    zeros_s = jnp.zeros((BATCH, 1, 2 * H_B, 128, DK_B), F32)
    zeros_m = jnp.zeros((BATCH, 1, 2 * H_B, 128), F32)

    outs = {k: [] for k in ("ak", "av", "ck", "cv", "bC", "bn", "bm")}
    for l in range(DEPTH):
        lam_init = 0.8 - 0.6 * math.exp(-0.3 * l)
        h = _ffn(h, mods, gn, wi, wo, gf, l=l, half=0, first=(l == 0))
        z, zt = _proj_in(h, mods, gn, wz, wt, bt, l=l)

        ya_c, ak, av = _attn("A", z, zt, ctx=None, tables=None, params=par_a, l=l, lam_init=lam_init,
                             latent=False)
        yc_c, ck, cv = _attn("C", z, zt, ctx=None, tables=None, params=par_c, l=l, lam_init=lam_init,
                             latent=False)
        stats, acols = _gate_stats(zt)
        yb_c, s_fin, m_fin = _mlstm(z, zt, stats, acols, zeros_s, zeros_m, gb, l=l, latent=False)

        (ya_l,) = _attn("A", z, zt, ctx=ctx_a, tables=tab_a, params=par_a, l=l, lam_init=lam_init,
                        latent=True)
        (yc_l,) = _attn("C", z, zt, ctx=ctx_c, tables=tab_c, params=par_c, l=l, lam_init=lam_init,
                        latent=True)
        yb_l, _, _ = _mlstm(z, zt, stats, acols, s0, m0, gb, l=l, latent=True)

        h = _ffn(h, mods, gn, wi, wo, gf, l=l, half=1,
                 mix=((ya_c, ya_l), (yb_c, yb_l), (yc_c, yc_l), wmix), final=(l == DEPTH - 1))

        outs["ak"].append(ak.reshape(BATCH, SEQ, KV_A, HD_A))
        outs["av"].append(av.reshape(BATCH, SEQ, KV_A, HD_A))
        outs["ck"].append(ck.reshape(BATCH, SEQ, H_C, 2, D_C))
        outs["cv"].append(cv.reshape(BATCH, SEQ, H_C, 2 * D_C))
        s_fin = s_fin.reshape(BATCH, 2, H_B, 128, DK_B)
        outs["bC"].append(jnp.swapaxes(s_fin[..., 0:64, :], -1, -2))
        outs["bn"].append(s_fin[..., 64, :])
        outs["bm"].append(m_fin[:, :, 0].reshape(BATCH, 2, H_B))

    y_prompt = h[0].reshape(BATCH, SEQ, D_MODEL)
    y_sample = h[1].reshape(DEC_BATCH, DEC_SEQ, D_MODEL)
    st = lambda k: jnp.stack(outs[k], axis=1)
    return (y_prompt, y_sample, st("ak"), st("av"), st("ck"), st("cv"), st("bC"), st("bn"), st("bm"))
```

```python
import functools
import math

import jax
import jax.numpy as jnp
from jax import lax
from jax.experimental import pallas as pl
from jax.experimental.pallas import tpu as pltpu

F32 = jnp.float32
BF16 = jnp.bfloat16

D_MODEL = 1024
BATCH = 16
SEQ = 256
DEPTH = 2
DEC_BATCH = 4
DEC_SEQ = 2048
PAST_LEN = 512
GRID_W = 64
ROPE_THETA = 10000.0
EPS = 1e-6
HD_A = 64
H_A = 8
KV_A = 2
DK_B = 64
H_B = 4
D_C = 32
H_C = 4
D_FF = 2816
LOG2E = 1.4426950408889634

N_CTX = BATCH * SEQ
N_LAT = DEC_BATCH * DEC_SEQ
N_TOK = N_CTX + N_LAT

TM = 512
CTX_TILES = N_CTX // TM
LAT_TILES_PER_BATCH = DEC_SEQ // TM
FF_CHUNKS = ((0, 1536), (1536, D_FF))
FF_ROWS = 256
TC = 256
SB = 256
TQ = 256
VROWS = 80
BOUND_SLACK = 1.0 + 2.0 ** -6
MAX_BOUND_GAP = 60.0

Z_KB, Z_KC, Z_KA = 0, 256, 512
Z_W = 640
ZT_QA, ZT_QC, ZT_VC, ZT_QB, ZT_VB, ZT_OB, ZT_VA, ZT_G = 0, 512, 768, 1024, 1280, 1536, 1792, 1920
ZT_W = 1936

VMEM_LIMIT = 56 * 1024 * 1024


def _cparams(sem):
    return pltpu.CompilerParams(dimension_semantics=sem, vmem_limit_bytes=VMEM_LIMIT)


def _const_spec(shape, lead=()):
    nd = len(shape)
    return pl.BlockSpec((None,) * len(lead) + tuple(shape), lambda *_: tuple(lead) + (0,) * nd,
                        pipeline_mode=pl.Buffered(1))


def _mod_row(i):
    return jnp.maximum(i - CTX_TILES, 0) // LAT_TILES_PER_BATCH + (i >= CTX_TILES).astype(jnp.int32)


def _rms(x, g, axis=-1):
    ms = jnp.mean(x * x, axis=axis, keepdims=True)
    return x * lax.rsqrt(ms + EPS) * g


def _norm_mod(x, g, mod_ref, k):
    shift = mod_ref[0, k:k + 1, :]
    scale = mod_ref[0, k + 1:k + 2, :]
    return _rms(x, g) * (1.0 + scale) + shift


def _ds(c, n):
    if isinstance(c, int):
        return pl.ds(c * n, n)
    return pl.ds(pl.multiple_of(c * n, n), n)


def _log_sigmoid(x):
    return jnp.minimum(x, 0.0) - jnp.log1p(jnp.exp(-jnp.abs(x)))


def _ada_kernel(c_ref, w_ref, b_ref, o_ref):
    c = c_ref[...]
    a = (c * jax.nn.sigmoid(c)).astype(BF16)
    o_ref[0] = jnp.dot(a, w_ref[0].astype(BF16), preferred_element_type=F32) + b_ref[0]


def _ada(cc, w_ada, b_ada):
    tn = 1024
    n = 9 * D_MODEL
    return pl.pallas_call(
        _ada_kernel,
        grid=(DEPTH, n // tn),
        in_specs=[
            pl.BlockSpec((8, D_MODEL), lambda l, j: (0, 0)),
            pl.BlockSpec((1, D_MODEL, tn), lambda l, j: (l, 0, j)),
            pl.BlockSpec((1, 1, tn), lambda l, j: (l, 0, j)),
        ],
        out_specs=pl.BlockSpec((1, 8, tn), lambda l, j: (l, 0, j)),
        out_shape=jax.ShapeDtypeStruct((DEPTH, 8, n), F32),
        compiler_params=_cparams(("arbitrary", "arbitrary")),
        name="ada",
    )(cc, w_ada, b_ada.reshape(DEPTH, 1, n))


def _ffn_kernel(*refs, k, first, mix, final):
    is_ctx = pl.program_id(0) < CTX_TILES
    it = iter(refs)
    x_refs = (next(it), next(it)) if first else (next(it),)
    if mix:
        y_refs = [(next(it), next(it)) for _ in range(3)]
        wmix_ref = next(it)
    mod_ref, g_ref, wi_ref, wo_ref, gf_ref = (next(it) for _ in range(5))

    nrows = TM if mix else FF_ROWS
    ys = []
    for r in range(TM // nrows):
        rows = slice(r * nrows, (r + 1) * nrows)
        pick = lambda c_ref, l_ref: jnp.where(is_ctx, c_ref[rows, :], l_ref[rows, :])
        x = pick(*x_refs) if first else x_refs[0][rows, :]
        if mix:
            ya, yb, yc = (pick(*pair) for pair in y_refs)
            y = jnp.dot(ya, wmix_ref[0:512, :], preferred_element_type=F32)
            y = y + jnp.dot(yb, wmix_ref[512:768, :], preferred_element_type=F32)
            y = y + jnp.dot(yc, wmix_ref[768:1024, :], preferred_element_type=F32)
            x = x + mod_ref[0, 5:6, :] * y
        xn = _norm_mod(x, g_ref[...], mod_ref, k).astype(BF16)
        acc = jnp.zeros((nrows, D_MODEL), F32)
        for lo, hi in FF_CHUNKS:
            g = jnp.dot(xn, wi_ref[:, lo:hi], preferred_element_type=F32)
            u = jnp.dot(xn, wi_ref[:, D_FF + lo:D_FF + hi], preferred_element_type=F32)
            a = (g * jax.nn.sigmoid(g) * u).astype(BF16)
            acc = acc + jnp.dot(a, wo_ref[lo:hi, :], preferred_element_type=F32)
        y = x + 0.5 * mod_ref[0, k + 2:k + 3, :] * acc
        if final:
            ys.append(_rms(y, gf_ref[...]))
        else:
            refs[-1][rows, :] = y
    if final:
        yc_ref, yl_ref = refs[-2:]
        y = jnp.concatenate(ys, axis=0)

        @pl.when(is_ctx)
        def _():
            yc_ref[...] = y

        @pl.when(jnp.logical_not(is_ctx))
        def _():
            yl_ref[...] = y


def _ctx_lat_specs(width):
    return [
        pl.BlockSpec((TM, width), lambda i: (jnp.minimum(i, CTX_TILES - 1), 0)),
        pl.BlockSpec((TM, width), lambda i: (jnp.maximum(i - CTX_TILES, 0), 0)),
    ]


def _mod_spec(l):
    return pl.BlockSpec((None, 1, 9, D_MODEL), lambda i: (l, _mod_row(i), 0, 0))


def _ffn(h, mods, g, wi, wo, gf, *, l, half, first=False, mix=None, final=False):
    k = 6 * half
    flat_spec = pl.BlockSpec((TM, D_MODEL), lambda i: (i, 0))
    flat_shape = jax.ShapeDtypeStruct((N_TOK, D_MODEL), F32)
    mix_specs, mix_args = [], []
    if mix is not None:
        mix_specs = (_ctx_lat_specs(512) + _ctx_lat_specs(256) + _ctx_lat_specs(256)
                     + [_const_spec((D_MODEL, D_MODEL), (l,))])
        mix_args = [*mix[0], *mix[1], *mix[2], mix[3]]
    return pl.pallas_call(
        functools.partial(_ffn_kernel, k=k, first=first, mix=mix is not None, final=final),
        grid=(N_TOK // TM,),
        in_specs=(_ctx_lat_specs(D_MODEL) if first else [flat_spec]) + mix_specs + [
            _mod_spec(l),
            _const_spec((1, D_MODEL), (l, 2 * half)),
            _const_spec((D_MODEL, 2 * D_FF), (l, half)),
            _const_spec((D_FF, D_MODEL), (l, half)),
            _const_spec((1, D_MODEL)),
        ],
        out_specs=_ctx_lat_specs(D_MODEL) if final else flat_spec,
        out_shape=[jax.ShapeDtypeStruct((N_CTX, D_MODEL), F32),
                   jax.ShapeDtypeStruct((N_LAT, D_MODEL), F32)] if final else flat_shape,
        compiler_params=_cparams(("arbitrary",)),
        name="ffn_final" if final else ("ffn_first" if first else ("ffn_mix" if mix else "ffn")),
    )(*(h if first else (h,)), *mix_args, mods, g, wi, wo, gf)


def _proj_in_kernel(h_ref, mod_ref, g_ref, w_ref, wt_ref, bt_ref, z_ref, zt_ref):
    xn = _norm_mod(h_ref[...], g_ref[...], mod_ref, 3).astype(BF16)
    z_ref[...] = jnp.dot(xn, w_ref[...], preferred_element_type=F32)
    zt = lax.dot_general(wt_ref[...], xn, (((1,), (1,)), ((), ())), preferred_element_type=F32)
    zt = zt + bt_ref[...]
    for c in range(TM // TC):
        zt_ref[c] = zt[:, c * TC:(c + 1) * TC]


def _proj_in(h, mods, g, wz, wt, bt, *, l):
    return pl.pallas_call(
        _proj_in_kernel,
        grid=(N_TOK // TM,),
        in_specs=[
            pl.BlockSpec((TM, D_MODEL), lambda i: (i, 0)),
            _mod_spec(l),
            _const_spec((1, D_MODEL), (l, 1)),
            _const_spec((D_MODEL, Z_W), (l,)),
            _const_spec((ZT_W, D_MODEL), (l,)),
            _const_spec((ZT_W, 1), (l,)),
        ],
        out_specs=[
            pl.BlockSpec((TM, Z_W), lambda i: (i, 0)),
            pl.BlockSpec((TM // TC, ZT_W, TC), lambda i: (i, 0, 0)),
        ],
        out_shape=[
            jax.ShapeDtypeStruct((N_TOK, Z_W), F32),
            jax.ShapeDtypeStruct((N_TOK // TC, ZT_W, TC), F32),
        ],
        compiler_params=_cparams(("arbitrary",)),
        name="proj_in",
    )(h, mods, g, wz, wt, bt)


def _swap_halves(x, width, axis):
    n = x.shape[axis]
    half = width // 2
    parts = []
    for s in range(0, n, width):
        parts.append(lax.slice_in_dim(x, s + half, s + width, axis=axis))
        parts.append(lax.slice_in_dim(x, s, s + half, axis=axis))
    return jnp.concatenate(parts, axis=axis)


def _attn_kernel(*refs, mode, seq, has_ctx, rope, emit_cache, lam_init):
    it = iter(refs)
    qt_ref, k_ref, vt_ref = next(it), next(it), next(it)
    if has_ctx:
        ck_ref, cv_ref = next(it), next(it)
    if rope:
        cqt_ref, sqt_ref, ck_tab, sk_tab = next(it), next(it), next(it), next(it)
    if mode == "A":
        gq_ref, gk_ref = next(it), next(it)
    else:
        lam_ref, gc_ref = next(it), next(it)
    y_ref = next(it)
    if emit_cache:
        kc_ref, vc_ref = next(it), next(it)
    kbuf, vbuf, p_ref, kmax_ref = next(it), next(it), next(it), next(it)

    n_kv = KV_A if mode == "A" else H_C
    per_kv = 8 // n_kv
    n_ctx = PAST_LEN if has_ctx else 0
    lk = n_ctx + seq
    width = HD_A if mode == "A" else D_C

    @pl.when(pl.program_id(1) == 0)
    def _fill():
        ones_rows = (lax.broadcasted_iota(jnp.int32, (VROWS - 64, lk), 0) == 0).astype(BF16)
        ones_cols = jnp.ones((64, 128), BF16)

        def max_sq_norm(kb):
            if lk == SB:
                return jnp.zeros((1, 1), F32)
            kf = kb.astype(F32)
            sq = jnp.dot((kf * kf).astype(BF16), ones_cols, preferred_element_type=F32)
            return jnp.max(sq, axis=0, keepdims=True)[:, 0:1]

        if has_ctx:
            cvt = cv_ref[0].T
        for h in range(n_kv):
            hs = slice(h * 64, (h + 1) * 64)
            vbuf[h, 64:VROWS, :] = ones_rows
            kmax2 = jnp.zeros((1, 1), F32)
            if has_ctx:
                kb = ck_ref[0, :, hs].astype(BF16)
                kmax2 = jnp.maximum(kmax2, max_sq_norm(kb))
                kbuf[h, 0:n_ctx, :] = kb
                vbuf[h, 0:64, 0:n_ctx] = cvt[hs, :].astype(BF16)
            for c in range(seq // TC):
                rows = slice(c * TC, (c + 1) * TC)
                k = k_ref[rows, hs]
                if mode == "A":
                    k = _rms(k, gk_ref[...])
                if emit_cache:
                    kc_ref[0, rows, hs] = k
                if rope:
                    k = k * ck_tab[rows, :] + _swap_halves(k, width, 1) * sk_tab[rows, :]
                kb = k.astype(BF16)
                kmax2 = jnp.maximum(kmax2, max_sq_norm(kb))
                kbuf[h, n_ctx + c * TC:n_ctx + (c + 1) * TC, :] = kb
                vbuf[h, 0:64, n_ctx + c * TC:n_ctx + (c + 1) * TC] = vt_ref[c, hs, :].astype(BF16)
            kmax_ref[h] = jnp.broadcast_to(jnp.sqrt(kmax2), (8, 128))
        if emit_cache:
            for c in range(seq // TC):
                vc_ref[0, c * TC:(c + 1) * TC, :] = vt_ref[c].T

    qts = []
    if mode == "A":
        scale = HD_A ** -0.5 * LOG2E
        for hh in range(H_A):
            x = _rms(qt_ref[0, hh * 64:(hh + 1) * 64, :], gq_ref[...], axis=0)
            if rope:
                x = x * cqt_ref[0] + _swap_halves(x, width, 0) * sqt_ref[0]
            qts.append((x * scale).astype(BF16))
    else:
        scale = D_C ** -0.5 * LOG2E
        first_map = lax.broadcasted_iota(jnp.int32, (64, TQ), 0) < D_C
        for h in range(H_C):
            x = qt_ref[0, h * 64:(h + 1) * 64, :]
            if rope:
                x = x * cqt_ref[0] + _swap_halves(x, width, 0) * sqt_ref[0]
            x = x * scale
            qts.append(jnp.where(first_map, x, 0.0).astype(BF16))
            qts.append(jnp.where(first_map, 0.0, x).astype(BF16))

    def value_matmul(u):
        ox = jnp.dot(vbuf[u // per_kv], p_ref[:, u * TQ:(u + 1) * TQ], preferred_element_type=F32)
        return ox[0:64] / ox[64:65]

    def emit(outs):
        if mode == "A":
            for pair in range(4):
                o = jnp.concatenate(outs[2 * pair:2 * pair + 2], axis=0)
                y_ref[:, pair * 128:(pair + 1) * 128] = o.T.astype(BF16)
        else:
            lp = lam_ref[...]
            lam = (jnp.exp(jnp.sum(lp[0:1] * lp[1:2], axis=1, keepdims=True))
                   - jnp.exp(jnp.sum(lp[2:3] * lp[3:4], axis=1, keepdims=True)) + lam_init)
            for pair in range(2):
                o = []
                for h in (2 * pair, 2 * pair + 1):
                    d = outs[2 * h] - lam * outs[2 * h + 1]
                    o.append(_rms(d, gc_ref[...], axis=0) * (1.0 - lam_init))
                y_ref[:, pair * 128:(pair + 1) * 128] = jnp.concatenate(o, axis=0).T.astype(BF16)

    outs = []
    if lk == SB:
        sts = [jnp.dot(kbuf[u // per_kv], qts[u], preferred_element_type=F32) for u in range(8)]
        ps = [jnp.exp2(st - jnp.max(st, axis=0, keepdims=True)).astype(BF16) for st in sts]
        for u in range(8):
            ox = jnp.dot(vbuf[u // per_kv], ps[u], preferred_element_type=F32)
            outs.append(ox[0:64] / ox[64:65])
        emit(outs)
        return

    mhats, ms = [], []
    for u in range(8):
        g = u // per_kv
        qn = jnp.sqrt(jnp.sum(jnp.square(qts[u].astype(F32)), axis=0, keepdims=True))
        mhat = qn * kmax_ref[g][0:1, 0:1] * BOUND_SLACK
        mrun = jnp.full((8, TQ), -jnp.inf, F32)
        for j in range(lk // SB):
            rows = slice(j * SB, (j + 1) * SB)
            st = jnp.dot(kbuf[g, rows, :], qts[u], preferred_element_type=F32)
            p_ref[rows, u * TQ:(u + 1) * TQ] = jnp.exp2(st - mhat).astype(BF16)
            for r in range(SB // 8):
                mrun = jnp.maximum(mrun, st[8 * r:8 * r + 8])
        mhats.append(mhat)
        ms.append(jnp.max(mrun, axis=0, keepdims=True))
        if u >= 1:
            outs.append(value_matmul(u - 1))
    outs.append(value_matmul(7))
    emit(outs)

    worst_gap = functools.reduce(jnp.maximum, [jnp.max(mh - m) for mh, m in zip(mhats, ms)])

    @pl.when(worst_gap > MAX_BOUND_GAP)
    def _redo():
        redone = []
        for u in range(8):
            def block(j, carry, u=u):
                rows = _ds(j, SB)
                st = jnp.dot(kbuf[u // per_kv, rows, :], qts[u], preferred_element_type=F32)
                p_ref[rows, u * TQ:(u + 1) * TQ] = jnp.exp2(st - ms[u]).astype(BF16)
                return carry

            lax.fori_loop(0, lk // SB, block, 0)
            redone.append(value_matmul(u))
        emit(redone)


def _attn(mode, z, zt, *, ctx, tables, params, l, lam_init, latent):
    if latent:
        nb, seq, row0 = DEC_BATCH, DEC_SEQ, N_CTX
    else:
        nb, seq, row0 = BATCH, SEQ, 0
    nq = seq // TQ
    n_kv = KV_A if mode == "A" else H_C
    n_ctx = PAST_LEN if ctx is not None else 0
    lk = n_ctx + seq
    qw = 512 if mode == "A" else 256
    vw = n_kv * 64
    q_blk = (ZT_QA if mode == "A" else ZT_QC) // qw
    k_blk = (Z_KA if mode == "A" else Z_KC) // vw
    v_blk = (ZT_VA if mode == "A" else ZT_VC) // vw
    emit_cache = not latent
    blk0 = row0 // seq

    in_specs = [
        pl.BlockSpec((1, qw, TC), lambda b, i: (row0 // TC + b * nq + i, q_blk, 0)),
        pl.BlockSpec((seq, vw), lambda b, i: (blk0 + b, k_blk)),
        pl.BlockSpec((seq // TC, vw, TC), lambda b, i: (blk0 + b, v_blk, 0)),
    ]
    args = [zt, z, zt]
    if ctx is not None:
        in_specs += [
            pl.BlockSpec((1, None, PAST_LEN, vw), lambda b, i: (b, l, 0, 0)),
            pl.BlockSpec((1, None, PAST_LEN, vw), lambda b, i: (b, l, 0, 0)),
        ]
        args += list(ctx)
    if tables is not None:
        cq, sq, cqt, sqt = tables
        in_specs += [
            pl.BlockSpec((1, 64, TC), lambda b, i: (i, 0, 0)),
            pl.BlockSpec((1, 64, TC), lambda b, i: (i, 0, 0)),
            _const_spec((seq, 64)),
            _const_spec((seq, 64)),
        ]
        args += [cqt, sqt, cq, sq]
    for p in params:
        in_specs.append(_const_spec(p.shape[1:], (l,)))
        args.append(p)

    out_specs = [pl.BlockSpec((TQ, qw), lambda b, i: (b * nq + i, 0))]
    out_shape = [jax.ShapeDtypeStruct((nb * seq, qw), BF16)]
    if emit_cache:
        out_specs += [pl.BlockSpec((1, seq, vw), lambda b, i: (b, 0, 0))] * 2
        out_shape += [jax.ShapeDtypeStruct((nb, seq, vw), F32)] * 2

    return pl.pallas_call(
        functools.partial(_attn_kernel, mode=mode, seq=seq, has_ctx=ctx is not None,
                          rope=tables is not None, emit_cache=emit_cache, lam_init=lam_init),
        grid=(nb, nq),
        in_specs=in_specs,
        out_specs=out_specs,
        out_shape=out_shape,
        scratch_shapes=[
            pltpu.VMEM((n_kv, lk, 64), BF16),
            pltpu.VMEM((n_kv, VROWS, lk), BF16),
            pltpu.VMEM((lk, 8 * TQ), BF16),
            pltpu.VMEM((n_kv, 8, 128), F32),
        ],
        compiler_params=_cparams(("arbitrary", "arbitrary")),
        name=f"attn_{mode}_{'lat' if latent else 'ctx'}",
    )(*args)


N_CHUNKS = N_TOK // TC
GS_A, GS_B, GS_LM, GS_BLAST, GS_ALAST, GS_ROWS = 0, 8, 16, 24, 32, 40


def _scan_lanes(x, op, fill, reverse):
    axis = x.ndim - 1
    lane = lax.broadcasted_iota(jnp.int32, x.shape, axis)
    k = 1
    while k < TC:
        if reverse:
            x = op(x, jnp.where(lane < TC - k, pltpu.roll(x, TC - k, axis), fill))
        else:
            x = op(x, jnp.where(lane >= k, pltpu.roll(x, k, axis), fill))
        k *= 2
    return x


def _gate_stats_kernel(g_ref, st_ref, ac_ref):
    is_fwd = lax.broadcasted_iota(jnp.int32, (N_CHUNKS, 8, TC), 1) < H_B
    g = g_ref[...]
    gi = jnp.concatenate([g[:, 0:4], g[:, 8:12]], axis=1)
    f = _log_sigmoid(jnp.concatenate([g[:, 4:8], g[:, 12:16]], axis=1))
    b = jnp.where(is_fwd, _scan_lanes(f, jnp.add, 0.0, False), _scan_lanes(f, jnp.add, 0.0, True))
    a = gi - b
    lm = jnp.where(is_fwd, _scan_lanes(a, jnp.maximum, -jnp.inf, False),
                   _scan_lanes(a, jnp.maximum, -jnp.inf, True))
    st_ref[:, GS_A:GS_A + 8, :] = a
    st_ref[:, GS_B:GS_B + 8, :] = b
    st_ref[:, GS_LM:GS_LM + 8, :] = lm
    st_ref[:, GS_BLAST:GS_BLAST + 8, :] = jnp.broadcast_to(jnp.sum(f, axis=2, keepdims=True), a.shape)
    st_ref[:, GS_ALAST:GS_ALAST + 8, :] = jnp.broadcast_to(jnp.max(a, axis=2, keepdims=True), a.shape)

    def transpose_chunk(c, carry):
        ac_ref[c] = jnp.concatenate([st_ref[c, GS_A:GS_A + 8, :], jnp.zeros((120, TC), F32)], axis=0).T
        return carry

    lax.fori_loop(0, N_CHUNKS, transpose_chunk, 0, unroll=4)


def _gate_stats(zt):
    return pl.pallas_call(
        _gate_stats_kernel,
        grid=(1,),
        in_specs=[pl.BlockSpec((N_CHUNKS, 16, TC), lambda i: (0, ZT_G // 16, 0))],
        out_specs=[
            pl.BlockSpec((N_CHUNKS, GS_ROWS, TC), lambda i: (0, 0, 0)),
            pl.BlockSpec((N_CHUNKS, TC, 128), lambda i: (0, 0, 0)),
        ],
        out_shape=[
            jax.ShapeDtypeStruct((N_CHUNKS, GS_ROWS, TC), F32),
            jax.ShapeDtypeStruct((N_CHUNKS, TC, 128), F32),
        ],
        compiler_params=_cparams(("arbitrary",)),
        name="gate_stats",
    )(zt)


def _mlstm_kernel(k_ref, qt_ref, vt_ref, ot_ref, st_ref, ac_ref, s0_ref, m0_ref, gb_ref,
                  y_ref, s_ref, m_ref, hf_ref, hr_ref, *, seq):
    nc = seq // TC
    s_idx = lax.broadcasted_iota(jnp.int32, (TC, TC), 0)
    t_idx = lax.broadcasted_iota(jnp.int32, (TC, TC), 1)
    seen = (s_idx <= t_idx, s_idx >= t_idx)
    ones_rows = (lax.broadcasted_iota(jnp.int32, (64, TC), 0) == 0).astype(F32)
    units = [(d, h) for d in range(2) for h in range(H_B)]

    def chunk(j, carry):
        states, m = carry
        cs = (j, nc - 1 - j)
        stat = lambda r: jnp.concatenate([st_ref[cs[0], r:r + 4, :], st_ref[cs[1], r + 4:r + 8, :]], axis=0)
        a, b, lm = stat(GS_A), stat(GS_B), stat(GS_LM)
        mx = jnp.maximum(m, lm)
        w = jnp.exp(m - mx)
        floor = jnp.exp(-(b + mx))
        mxl = jnp.maximum(m, stat(GS_ALAST)[:, 0:1])
        decay = jnp.exp(m - mxl)
        wk = jnp.exp(a - mxl)
        m_new = stat(GS_BLAST)[:, 0:1] + mxl

        first = []
        for (d, h), st in zip(units, states):
            hs = slice(h * 64, (h + 1) * 64)
            kh = (k_ref[_ds(cs[d], TC), hs] * (DK_B ** -0.5)).astype(BF16)
            qt = qt_ref[cs[d], hs, :].astype(BF16)
            lhs = jnp.concatenate([kh, st.astype(BF16)], axis=0)
            first.append((kh, jnp.dot(lhs, qt, preferred_element_type=F32)))
        second = []
        for u, ((d, h), (kh, both)) in enumerate(zip(units, first)):
            a_col = ac_ref[cs[d], :, u:u + 1]
            dt = jnp.where(seen[d], a_col - mx[u:u + 1, :], -jnp.inf)
            sp = both[0:TC] * jnp.exp(dt)
            den = jnp.sum(sp, axis=0, keepdims=True) + w[u:u + 1, :] * both[TC + 64:TC + 65]
            second.append((sp.astype(BF16), den))
        new_states = []
        for u, ((d, h), st, (kh, both), (spb, den)) in enumerate(zip(units, states, first, second)):
            hs = slice(h * 64, (h + 1) * 64)
            vx = jnp.concatenate([vt_ref[cs[d], hs, :], ones_rows], axis=0)
            num = (jnp.dot(vx.astype(BF16), spb, preferred_element_type=F32)[0:64]
                   + w[u:u + 1, :] * both[TC:TC + 64])
            hv = num / jnp.maximum(jnp.abs(den), floor[u:u + 1, :])
            if d == 0:
                hf_ref[cs[d], hs, :] = hv
            else:
                hr_ref[cs[d], hs, :] = hv
            upd = jnp.dot((vx * wk[u:u + 1, :]).astype(BF16), kh, preferred_element_type=F32)
            new_states.append(decay[u:u + 1, :] * st + upd)
        return tuple(new_states), m_new

    init = (tuple(s0_ref[0, u] for u in range(2 * H_B)), m0_ref[0][:, 0:1])
    if nc == 1:
        states, m = chunk(0, init)
    else:
        states, m = lax.fori_loop(0, nc, chunk, init)
    for u, st in enumerate(states):
        s_ref[0, u] = st
    m_ref[0] = jnp.broadcast_to(m, (2 * H_B, 128))

    def epilogue(c, carry):
        ys = []
        for h in range(H_B):
            hs = slice(h * 64, (h + 1) * 64)
            x = hf_ref[c, hs, :] + hr_ref[c, hs, :]
            ys.append(_rms(x, gb_ref[...], axis=0) * jax.nn.sigmoid(ot_ref[c, hs, :]))
        y_ref[_ds(c, TC), :] = jnp.concatenate(ys, axis=0).T.astype(BF16)
        return carry

    if nc == 1:
        epilogue(0, 0)
    else:
        lax.fori_loop(0, nc, epilogue, 0)


def _mlstm(z, zt, stats, acols, s0, m0, gb, *, l, latent):
    ls = min(l, s0.shape[1] - 1)
    if latent:
        nb, seq, row0 = DEC_BATCH, DEC_SEQ, N_CTX
    else:
        nb, seq, row0 = BATCH, SEQ, 0
    blk0 = row0 // seq
    nc = seq // TC
    ztspec = lambda rb: pl.BlockSpec((nc, 256, TC), lambda b: (blk0 + b, rb, 0))
    return pl.pallas_call(
        functools.partial(_mlstm_kernel, seq=seq),
        grid=(nb,),
        in_specs=[
            pl.BlockSpec((seq, 256), lambda b: (blk0 + b, Z_KB // 256)),
            ztspec(ZT_QB // 256), ztspec(ZT_VB // 256), ztspec(ZT_OB // 256),
            pl.BlockSpec((nc, GS_ROWS, TC), lambda b: (blk0 + b, 0, 0)),
            pl.BlockSpec((nc, TC, 128), lambda b: (blk0 + b, 0, 0)),
            pl.BlockSpec((1, None, 2 * H_B, 128, 64), lambda b: (b, ls, 0, 0, 0)),
            pl.BlockSpec((1, None, 2 * H_B, 128), lambda b: (b, ls, 0, 0)),
            _const_spec((64, 1), (l,)),
        ],
        out_specs=[
            pl.BlockSpec((seq, 256), lambda b: (b, 0)),
            pl.BlockSpec((1, 2 * H_B, 128, 64), lambda b: (b, 0, 0, 0)),
            pl.BlockSpec((1, 2 * H_B, 128), lambda b: (b, 0, 0)),
        ],
        out_shape=[
            jax.ShapeDtypeStruct((nb * seq, 256), BF16),
            jax.ShapeDtypeStruct((nb, 2 * H_B, 128, 64), F32),
            jax.ShapeDtypeStruct((nb, 2 * H_B, 128), F32),
        ],
        scratch_shapes=[pltpu.VMEM((nc, 256, TC), F32), pltpu.VMEM((nc, 256, TC), F32)],
        compiler_params=_cparams(("arbitrary",)),
        name=f"mlstm_{'lat' if latent else 'ctx'}",
    )(z, zt, zt, zt, stats, acols, s0, m0, gb)


def _rope_tables(dim):
    t = jnp.arange(DEC_SEQ)
    row = (t // GRID_W).astype(F32)
    colp = (t % GRID_W).astype(F32)
    axis_dim = dim // 2
    freqs = ROPE_THETA ** (-jnp.arange(0, axis_dim, 2, dtype=F32) / axis_dim)
    ang = jnp.concatenate([row[:, None] * freqs, colp[:, None] * freqs], axis=-1)
    cos, sin = jnp.cos(ang), jnp.sin(ang)
    reps = 64 // dim
    cq = jnp.concatenate([cos, cos] * reps, axis=1)
    sq = jnp.concatenate([-sin, sin] * reps, axis=1)
    to_chunks = lambda x: x.T.reshape(64, DEC_SEQ // TC, TC).transpose(1, 0, 2)
    return cq, sq, to_chunks(cq), to_chunks(sq)


def _split_w_in(w, b_gates):
    a0, b0, c0 = 0, 768, 1808
    qa, ka, va = w[..., a0:a0 + 512], w[..., a0 + 512:a0 + 640], w[..., a0 + 640:a0 + 768]
    qb, kb = w[..., b0:b0 + 256], w[..., b0 + 256:b0 + 512]
    vb, ob = w[..., b0 + 512:b0 + 768], w[..., b0 + 768:b0 + 1024]
    gb = w[..., b0 + 1024:b0 + 1040]
    qc, kc, vc = w[..., c0:c0 + 256], w[..., c0 + 256:c0 + 512], w[..., c0 + 512:c0 + 768]
    wz = jnp.concatenate([kb, kc, ka], axis=-1)
    wt = jnp.swapaxes(jnp.concatenate([qa, qc, vc, qb, vb, ob, va, gb], axis=-1), -1, -2)
    bt = jnp.zeros((DEPTH, ZT_W, 1), F32).at[:, ZT_G:ZT_G + 16, 0].set(b_gates)
    return wz.astype(BF16), wt.astype(BF16), bt


def kernel(x_prompt, x_sample, cache_a_k, cache_a_v, cache_c_k, cache_c_v, state_b_C, state_b_n,
           state_b_m, c, c_ctx, w_ada, b_ada, g_norm, w_ff_in, w_ff_out, w_in, w_out, g_qa, g_ka,
           b_gates, g_b, lam_q1, lam_k1, lam_q2, lam_k2, g_c, g_final):
    h = (x_prompt.reshape(N_CTX, D_MODEL), x_sample.reshape(N_LAT, D_MODEL))
    cc = jnp.concatenate([c_ctx[None, :], c, jnp.zeros((3, D_MODEL), F32)], axis=0)
    mods = _ada(cc, w_ada, b_ada).reshape(DEPTH, 8, 9, D_MODEL)
    tab_a = _rope_tables(HD_A)
    tab_c = _rope_tables(D_C)

    gf = g_final.reshape(1, D_MODEL)
    gn = g_norm.reshape(DEPTH, 3, 1, D_MODEL)
    wi, wo, wmix = w_ff_in.astype(BF16), w_ff_out.astype(BF16), w_out.astype(BF16)
    wz, wt, bt = _split_w_in(w_in, b_gates)
    par_a = (g_qa.reshape(DEPTH, 64, 1), g_ka.reshape(DEPTH, 1, 64))
    par_c = (jnp.stack([lam_q1, lam_k1, lam_q2, lam_k2], axis=1), g_c.reshape(DEPTH, 64, 1))
    gb = g_b.reshape(DEPTH, 64, 1)
    ctx_a = tuple(x.reshape(DEC_BATCH, DEPTH, PAST_LEN, KV_A * HD_A) for x in (cache_a_k, cache_a_v))
    ctx_c = tuple(x.reshape(DEC_BATCH, DEPTH, PAST_LEN, H_C * 2 * D_C) for x in (cache_c_k, cache_c_v))
    s0 = jnp.concatenate([jnp.swapaxes(state_b_C, -1, -2), state_b_n[..., None, :],
                          jnp.zeros((DEC_BATCH, DEPTH, 2, H_B, 63, DK_B), F32)], axis=-2)
    s0 = s0.reshape(DEC_BATCH, DEPTH, 2 * H_B, 128, DK_B)
    m0 = jnp.broadcast_to(state_b_m.reshape(DEC_BATCH, DEPTH, 2 * H_B, 1), (DEC_BATCH, DEPTH, 2 * H_B, 128))
    zeros_s = jnp.zeros((BATCH, 1, 2 * H_B, 128, DK_B), F32)
    zeros_m = jnp.zeros((BATCH, 1, 2 * H_B, 128), F32)

    outs = {k: [] for k in ("ak", "av", "ck", "cv", "bC", "bn", "bm")}
    for l in range(DEPTH):
        lam_init = 0.8 - 0.6 * math.exp(-0.3 * l)
        h = _ffn(h, mods, gn, wi, wo, gf, l=l, half=0, first=(l == 0))
        z, zt = _proj_in(h, mods, gn, wz, wt, bt, l=l)

        ya_c, ak, av = _attn("A", z, zt, ctx=None, tables=None, params=par_a, l=l, lam_init=lam_init,
                             latent=False)
        yc_c, ck, cv = _attn("C", z, zt, ctx=None, tables=None, params=par_c, l=l, lam_init=lam_init,
                             latent=False)
        stats, acols = _gate_stats(zt)
        yb_c, s_fin, m_fin = _mlstm(z, zt, stats, acols, zeros_s, zeros_m, gb, l=l, latent=False)

        (ya_l,) = _attn("A", z, zt, ctx=ctx_a, tables=tab_a, params=par_a, l=l, lam_init=lam_init,
                        latent=True)
        (yc_l,) = _attn("C", z, zt, ctx=ctx_c, tables=tab_c, params=par_c, l=l, lam_init=lam_init,
                        latent=True)
        yb_l, _, _ = _mlstm(z, zt, stats, acols, s0, m0, gb, l=l, latent=True)

        h = _ffn(h, mods, gn, wi, wo, gf, l=l, half=1,
                 mix=((ya_c, ya_l), (yb_c, yb_l), (yc_c, yc_l), wmix), final=(l == DEPTH - 1))

        outs["ak"].append(ak.reshape(BATCH, SEQ, KV_A, HD_A))
        outs["av"].append(av.reshape(BATCH, SEQ, KV_A, HD_A))
        outs["ck"].append(ck.reshape(BATCH, SEQ, H_C, 2, D_C))
        outs["cv"].append(cv.reshape(BATCH, SEQ, H_C, 2 * D_C))
        s_fin = s_fin.reshape(BATCH, 2, H_B, 128, DK_B)
        outs["bC"].append(jnp.swapaxes(s_fin[..., 0:64, :], -1, -2))
        outs["bn"].append(s_fin[..., 64, :])
        outs["bm"].append(m_fin[:, :, 0].reshape(BATCH, 2, H_B))

    y_prompt = h[0].reshape(BATCH, SEQ, D_MODEL)
    y_sample = h[1].reshape(DEC_BATCH, DEC_SEQ, D_MODEL)
    st = lambda k: jnp.stack(outs[k], axis=1)
    return (y_prompt, y_sample, st("ak"), st("av"), st("ck"), st("cv"), st("bC"), st("bn"), st("bm"))
```

```python
import functools
import math

import jax
import jax.numpy as jnp
from jax import lax
from jax.experimental import pallas as pl
from jax.experimental.pallas import tpu as pltpu

F32 = jnp.float32
BF16 = jnp.bfloat16

D_MODEL = 1024
BATCH = 16
SEQ = 256
DEPTH = 2
DEC_BATCH = 4
DEC_SEQ = 2048
PAST_LEN = 512
GRID_W = 64
ROPE_THETA = 10000.0
EPS = 1e-6
HD_A = 64
H_A = 8
KV_A = 2
DK_B = 64
H_B = 4
D_C = 32
H_C = 4
D_FF = 2816
LOG2E = 1.4426950408889634

N_CTX = BATCH * SEQ
N_LAT = DEC_BATCH * DEC_SEQ
N_TOK = N_CTX + N_LAT

TM = 512
CTX_TILES = N_CTX // TM
LAT_TILES_PER_BATCH = DEC_SEQ // TM
FF_CHUNKS = ((0, 1536), (1536, D_FF))
FF_ROWS = 256
TC = 256
SB = 256
TQ = 256
VROWS = 80
BOUND_SLACK = 1.0 + 2.0 ** -6
MAX_BOUND_GAP = 60.0

Z_KB, Z_KC, Z_KA = 0, 256, 512
Z_W = 640
ZT_QA, ZT_QC, ZT_VC, ZT_QB, ZT_VB, ZT_OB, ZT_VA, ZT_G = 0, 512, 768, 1024, 1280, 1536, 1792, 1920
ZT_W = 1936

VMEM_LIMIT = 56 * 1024 * 1024


def _cparams(sem):
    return pltpu.CompilerParams(dimension_semantics=sem, vmem_limit_bytes=VMEM_LIMIT)


def _const_spec(shape, lead=()):
    nd = len(shape)
    return pl.BlockSpec((None,) * len(lead) + tuple(shape), lambda *_: tuple(lead) + (0,) * nd,
                        pipeline_mode=pl.Buffered(1))


def _mod_row(i):
    return jnp.maximum(i - CTX_TILES, 0) // LAT_TILES_PER_BATCH + (i >= CTX_TILES).astype(jnp.int32)


def _rms(x, g, axis=-1):
    ms = jnp.mean(x * x, axis=axis, keepdims=True)
    return x * lax.rsqrt(ms + EPS) * g


def _norm_mod(x, g, mod_ref, k):
    shift = mod_ref[0, k:k + 1, :]
    scale = mod_ref[0, k + 1:k + 2, :]
    return _rms(x, g) * (1.0 + scale) + shift


def _ds(c, n):
    if isinstance(c, int):
        return pl.ds(c * n, n)
    return pl.ds(pl.multiple_of(c * n, n), n)


def _log_sigmoid(x):
    return jnp.minimum(x, 0.0) - jnp.log1p(jnp.exp(-jnp.abs(x)))


def _ada_kernel(c_ref, w_ref, b_ref, o_ref):
    c = c_ref[...]
    a = (c * jax.nn.sigmoid(c)).astype(BF16)
    o_ref[0] = jnp.dot(a, w_ref[0].astype(BF16), preferred_element_type=F32) + b_ref[0]


def _ada(cc, w_ada, b_ada):
    tn = 1024
    n = 9 * D_MODEL
    return pl.pallas_call(
        _ada_kernel,
        grid=(DEPTH, n // tn),
        in_specs=[
            pl.BlockSpec((8, D_MODEL), lambda l, j: (0, 0)),
            pl.BlockSpec((1, D_MODEL, tn), lambda l, j: (l, 0, j)),
            pl.BlockSpec((1, 1, tn), lambda l, j: (l, 0, j)),
        ],
        out_specs=pl.BlockSpec((1, 8, tn), lambda l, j: (l, 0, j)),
        out_shape=jax.ShapeDtypeStruct((DEPTH, 8, n), F32),
        compiler_params=_cparams(("arbitrary", "arbitrary")),
        name="ada",
    )(cc, w_ada, b_ada.reshape(DEPTH, 1, n))


def _ffn_kernel(*refs, k, first, mix, final):
    is_ctx = pl.program_id(0) < CTX_TILES
    it = iter(refs)
    x_refs = (next(it), next(it)) if first else (next(it),)
    if mix:
        y_refs = [(next(it), next(it)) for _ in range(3)]
        wmix_ref = next(it)
    mod_ref, g_ref, wi_ref, wo_ref, gf_ref = (next(it) for _ in range(5))

    nrows = TM if mix else FF_ROWS
    ys = []
    for r in range(TM // nrows):
        rows = slice(r * nrows, (r + 1) * nrows)
        pick = lambda c_ref, l_ref: jnp.where(is_ctx, c_ref[rows, :], l_ref[rows, :])
        x = pick(*x_refs) if first else x_refs[0][rows, :]
        if mix:
            ya, yb, yc = (pick(*pair) for pair in y_refs)
            y = jnp.dot(ya, wmix_ref[0:512, :], preferred_element_type=F32)
            y = y + jnp.dot(yb, wmix_ref[512:768, :], preferred_element_type=F32)
            y = y + jnp.dot(yc, wmix_ref[768:1024, :], preferred_element_type=F32)
            x = x + mod_ref[0, 5:6, :] * y
        xn = _norm_mod(x, g_ref[...], mod_ref, k).astype(BF16)
        acc = jnp.zeros((nrows, D_MODEL), F32)
        for lo, hi in FF_CHUNKS:
            g = jnp.dot(xn, wi_ref[:, lo:hi], preferred_element_type=F32)
            u = jnp.dot(xn, wi_ref[:, D_FF + lo:D_FF + hi], preferred_element_type=F32)
            a = (g * jax.nn.sigmoid(g) * u).astype(BF16)
            acc = acc + jnp.dot(a, wo_ref[lo:hi, :], preferred_element_type=F32)
        y = x + 0.5 * mod_ref[0, k + 2:k + 3, :] * acc
        if final:
            ys.append(_rms(y, gf_ref[...]))
        else:
            refs[-1][rows, :] = y
    if final:
        yc_ref, yl_ref = refs[-2:]
        y = jnp.concatenate(ys, axis=0)

        @pl.when(is_ctx)
        def _():
            yc_ref[...] = y

        @pl.when(jnp.logical_not(is_ctx))
        def _():
            yl_ref[...] = y


def _ctx_lat_specs(width):
    return [
        pl.BlockSpec((TM, width), lambda i: (jnp.minimum(i, CTX_TILES - 1), 0)),
        pl.BlockSpec((TM, width), lambda i: (jnp.maximum(i - CTX_TILES, 0), 0)),
    ]


def _mod_spec(l):
    return pl.BlockSpec((None, 1, 9, D_MODEL), lambda i: (l, _mod_row(i), 0, 0))


def _ffn(h, mods, g, wi, wo, gf, *, l, half, first=False, mix=None, final=False):
    k = 6 * half
    flat_spec = pl.BlockSpec((TM, D_MODEL), lambda i: (i, 0))
    flat_shape = jax.ShapeDtypeStruct((N_TOK, D_MODEL), F32)
    mix_specs, mix_args = [], []
    if mix is not None:
        mix_specs = (_ctx_lat_specs(512) + _ctx_lat_specs(256) + _ctx_lat_specs(256)
                     + [_const_spec((D_MODEL, D_MODEL), (l,))])
        mix_args = [*mix[0], *mix[1], *mix[2], mix[3]]
    return pl.pallas_call(
        functools.partial(_ffn_kernel, k=k, first=first, mix=mix is not None, final=final),
        grid=(N_TOK // TM,),
        in_specs=(_ctx_lat_specs(D_MODEL) if first else [flat_spec]) + mix_specs + [
            _mod_spec(l),
            _const_spec((1, D_MODEL), (l, 2 * half)),
            _const_spec((D_MODEL, 2 * D_FF), (l, half)),
            _const_spec((D_FF, D_MODEL), (l, half)),
            _const_spec((1, D_MODEL)),
        ],
        out_specs=_ctx_lat_specs(D_MODEL) if final else flat_spec,
        out_shape=[jax.ShapeDtypeStruct((N_CTX, D_MODEL), F32),
                   jax.ShapeDtypeStruct((N_LAT, D_MODEL), F32)] if final else flat_shape,
        compiler_params=_cparams(("arbitrary",)),
        name="ffn_final" if final else ("ffn_first" if first else ("ffn_mix" if mix else "ffn")),
    )(*(h if first else (h,)), *mix_args, mods, g, wi, wo, gf)


def _proj_in_kernel(h_ref, mod_ref, g_ref, w_ref, wt_ref, bt_ref, z_ref, zt_ref):
    xn = _norm_mod(h_ref[...], g_ref[...], mod_ref, 3).astype(BF16)
    z_ref[...] = jnp.dot(xn, w_ref[...], preferred_element_type=F32)
    zt = lax.dot_general(wt_ref[...], xn, (((1,), (1,)), ((), ())), preferred_element_type=F32)
    zt = zt + bt_ref[...]
    for c in range(TM // TC):
        zt_ref[c] = zt[:, c * TC:(c + 1) * TC]


def _proj_in(h, mods, g, wz, wt, bt, *, l):
    return pl.pallas_call(
        _proj_in_kernel,
        grid=(N_TOK // TM,),
        in_specs=[
            pl.BlockSpec((TM, D_MODEL), lambda i: (i, 0)),
            _mod_spec(l),
            _const_spec((1, D_MODEL), (l, 1)),
            _const_spec((D_MODEL, Z_W), (l,)),
            _const_spec((ZT_W, D_MODEL), (l,)),
            _const_spec((ZT_W, 1), (l,)),
        ],
        out_specs=[
            pl.BlockSpec((TM, Z_W), lambda i: (i, 0)),
            pl.BlockSpec((TM // TC, ZT_W, TC), lambda i: (i, 0, 0)),
        ],
        out_shape=[
            jax.ShapeDtypeStruct((N_TOK, Z_W), F32),
            jax.ShapeDtypeStruct((N_TOK // TC, ZT_W, TC), F32),
        ],
        compiler_params=_cparams(("arbitrary",)),
        name="proj_in",
    )(h, mods, g, wz, wt, bt)


def _swap_halves(x, width, axis):
    n = x.shape[axis]
    half = width // 2
    parts = []
    for s in range(0, n, width):
        parts.append(lax.slice_in_dim(x, s + half, s + width, axis=axis))
        parts.append(lax.slice_in_dim(x, s, s + half, axis=axis))
    return jnp.concatenate(parts, axis=axis)


def _attn_kernel(*refs, mode, seq, has_ctx, rope, emit_cache, lam_init):
    it = iter(refs)
    qt_ref, k_ref, vt_ref = next(it), next(it), next(it)
    if has_ctx:
        ck_ref, cvt_ref = next(it), next(it)
    if rope:
        cqt_ref, sqt_ref, ck_tab, sk_tab = next(it), next(it), next(it), next(it)
    if mode == "A":
        gq_ref, gk_ref = next(it), next(it)
    else:
        lam_ref, gc_ref = next(it), next(it)
    y_ref = next(it)
    if emit_cache:
        kc_ref, vc_ref = next(it), next(it)
    kbuf, vbuf, p_ref, kmax_ref = next(it), next(it), next(it), next(it)

    n_kv = KV_A if mode == "A" else H_C
    per_kv = 8 // n_kv
    n_ctx = PAST_LEN if has_ctx else 0
    lk = n_ctx + seq
    width = HD_A if mode == "A" else D_C

    @pl.when(pl.program_id(1) == 0)
    def _fill():
        ones_rows = (lax.broadcasted_iota(jnp.int32, (VROWS - 64, lk), 0) == 0).astype(BF16)
        ones_cols = jnp.ones((64, 128), BF16)

        def max_sq_norm(kb):
            if lk == SB:
                return jnp.zeros((1, 1), F32)
            kf = kb.astype(F32)
            sq = jnp.dot((kf * kf).astype(BF16), ones_cols, preferred_element_type=F32)
            return jnp.max(sq, axis=0, keepdims=True)[:, 0:1]

        for h in range(n_kv):
            hs = slice(h * 64, (h + 1) * 64)
            vbuf[h, 64:VROWS, :] = ones_rows
            kmax2 = jnp.zeros((1, 1), F32)
            if has_ctx:
                kb = ck_ref[0, :, hs].astype(BF16)
                kmax2 = jnp.maximum(kmax2, max_sq_norm(kb))
                kbuf[h, 0:n_ctx, :] = kb
                vbuf[h, 0:64, 0:n_ctx] = cvt_ref[0, h].astype(BF16)
            for c in range(seq // TC):
                rows = slice(c * TC, (c + 1) * TC)
                k = k_ref[rows, hs]
                if mode == "A":
                    k = _rms(k, gk_ref[...])
                if emit_cache:
                    kc_ref[0, rows, hs] = k
                if rope:
                    k = k * ck_tab[rows, :] + _swap_halves(k, width, 1) * sk_tab[rows, :]
                kb = k.astype(BF16)
                kmax2 = jnp.maximum(kmax2, max_sq_norm(kb))
                kbuf[h, n_ctx + c * TC:n_ctx + (c + 1) * TC, :] = kb
                vbuf[h, 0:64, n_ctx + c * TC:n_ctx + (c + 1) * TC] = vt_ref[c, hs, :].astype(BF16)
            kmax_ref[h] = jnp.broadcast_to(jnp.sqrt(kmax2), (8, 128))
        if emit_cache:
            for c in range(seq // TC):
                vc_ref[0, c * TC:(c + 1) * TC, :] = vt_ref[c].T

    qts = []
    if mode == "A":
        scale = HD_A ** -0.5 * LOG2E
        for hh in range(H_A):
            x = _rms(qt_ref[0, hh * 64:(hh + 1) * 64, :], gq_ref[...], axis=0)
            if rope:
                x = x * cqt_ref[0] + _swap_halves(x, width, 0) * sqt_ref[0]
            qts.append((x * scale).astype(BF16))
    else:
        scale = D_C ** -0.5 * LOG2E
        first_map = lax.broadcasted_iota(jnp.int32, (64, TQ), 0) < D_C
        for h in range(H_C):
            x = qt_ref[0, h * 64:(h + 1) * 64, :]
            if rope:
                x = x * cqt_ref[0] + _swap_halves(x, width, 0) * sqt_ref[0]
            x = x * scale
            qts.append(jnp.where(first_map, x, 0.0).astype(BF16))
            qts.append(jnp.where(first_map, 0.0, x).astype(BF16))

    def value_matmul(u):
        ox = jnp.dot(vbuf[u // per_kv], p_ref[:, u * TQ:(u + 1) * TQ], preferred_element_type=F32)
        return ox[0:64] / ox[64:65]

    def emit(outs):
        if mode == "A":
            for pair in range(4):
                o = jnp.concatenate(outs[2 * pair:2 * pair + 2], axis=0)
                y_ref[:, pair * 128:(pair + 1) * 128] = o.T.astype(BF16)
        else:
            lp = lam_ref[...]
            lam = (jnp.exp(jnp.sum(lp[0:1] * lp[1:2], axis=1, keepdims=True))
                   - jnp.exp(jnp.sum(lp[2:3] * lp[3:4], axis=1, keepdims=True)) + lam_init)
            for pair in range(2):
                o = []
                for h in (2 * pair, 2 * pair + 1):
                    d = outs[2 * h] - lam * outs[2 * h + 1]
                    o.append(_rms(d, gc_ref[...], axis=0) * (1.0 - lam_init))
                y_ref[:, pair * 128:(pair + 1) * 128] = jnp.concatenate(o, axis=0).T.astype(BF16)

    outs = []
    if lk == SB:
        sts = [jnp.dot(kbuf[u // per_kv], qts[u], preferred_element_type=F32) for u in range(8)]
        ps = [jnp.exp2(st - jnp.max(st, axis=0, keepdims=True)).astype(BF16) for st in sts]
        for u in range(8):
            ox = jnp.dot(vbuf[u // per_kv], ps[u], preferred_element_type=F32)
            outs.append(ox[0:64] / ox[64:65])
        emit(outs)
        return

    mhats, ms = [], []
    for u in range(8):
        g = u // per_kv
        qn = jnp.sqrt(jnp.sum(jnp.square(qts[u].astype(F32)), axis=0, keepdims=True))
        mhat = qn * kmax_ref[g][0:1, 0:1] * BOUND_SLACK
        mrun = jnp.full((8, TQ), -jnp.inf, F32)
        for j in range(lk // SB):
            rows = slice(j * SB, (j + 1) * SB)
            st = jnp.dot(kbuf[g, rows, :], qts[u], preferred_element_type=F32)
            p_ref[rows, u * TQ:(u + 1) * TQ] = jnp.exp2(st - mhat).astype(BF16)
            for r in range(SB // 8):
                mrun = jnp.maximum(mrun, st[8 * r:8 * r + 8])
        mhats.append(mhat)
        ms.append(jnp.max(mrun, axis=0, keepdims=True))
        if u >= 1:
            outs.append(value_matmul(u - 1))
    outs.append(value_matmul(7))
    emit(outs)

    worst_gap = functools.reduce(jnp.maximum, [jnp.max(mh - m) for mh, m in zip(mhats, ms)])

    @pl.when(worst_gap > MAX_BOUND_GAP)
    def _redo():
        redone = []
        for u in range(8):
            def block(j, carry, u=u):
                rows = _ds(j, SB)
                st = jnp.dot(kbuf[u // per_kv, rows, :], qts[u], preferred_element_type=F32)
                p_ref[rows, u * TQ:(u + 1) * TQ] = jnp.exp2(st - ms[u]).astype(BF16)
                return carry

            lax.fori_loop(0, lk // SB, block, 0)
            redone.append(value_matmul(u))
        emit(redone)


def _attn(mode, z, zt, *, ctx, tables, params, l, lam_init, latent):
    if latent:
        nb, seq, row0 = DEC_BATCH, DEC_SEQ, N_CTX
    else:
        nb, seq, row0 = BATCH, SEQ, 0
    nq = seq // TQ
    n_kv = KV_A if mode == "A" else H_C
    n_ctx = PAST_LEN if ctx is not None else 0
    lk = n_ctx + seq
    qw = 512 if mode == "A" else 256
    vw = n_kv * 64
    q_blk = (ZT_QA if mode == "A" else ZT_QC) // qw
    k_blk = (Z_KA if mode == "A" else Z_KC) // vw
    v_blk = (ZT_VA if mode == "A" else ZT_VC) // vw
    emit_cache = not latent
    blk0 = row0 // seq

    in_specs = [
        pl.BlockSpec((1, qw, TC), lambda b, i: (row0 // TC + b * nq + i, q_blk, 0)),
        pl.BlockSpec((seq, vw), lambda b, i: (blk0 + b, k_blk)),
        pl.BlockSpec((seq // TC, vw, TC), lambda b, i: (blk0 + b, v_blk, 0)),
    ]
    args = [zt, z, zt]
    if ctx is not None:
        in_specs += [
            pl.BlockSpec((1, None, PAST_LEN, vw), lambda b, i: (b, l, 0, 0)),
            pl.BlockSpec((1, None, n_kv, 64, PAST_LEN), lambda b, i: (b, l, 0, 0, 0)),
        ]
        args += list(ctx)
    if tables is not None:
        cq, sq, cqt, sqt = tables
        in_specs += [
            pl.BlockSpec((1, 64, TC), lambda b, i: (i, 0, 0)),
            pl.BlockSpec((1, 64, TC), lambda b, i: (i, 0, 0)),
            _const_spec((seq, 64)),
            _const_spec((seq, 64)),
        ]
        args += [cqt, sqt, cq, sq]
    for p in params:
        in_specs.append(_const_spec(p.shape[1:], (l,)))
        args.append(p)

    out_specs = [pl.BlockSpec((TQ, qw), lambda b, i: (b * nq + i, 0))]
    out_shape = [jax.ShapeDtypeStruct((nb * seq, qw), BF16)]
    if emit_cache:
        out_specs += [pl.BlockSpec((1, seq, vw), lambda b, i: (b, 0, 0))] * 2
        out_shape += [jax.ShapeDtypeStruct((nb, seq, vw), F32)] * 2

    return pl.pallas_call(
        functools.partial(_attn_kernel, mode=mode, seq=seq, has_ctx=ctx is not None,
                          rope=tables is not None, emit_cache=emit_cache, lam_init=lam_init),
        grid=(nb, nq),
        in_specs=in_specs,
        out_specs=out_specs,
        out_shape=out_shape,
        scratch_shapes=[
            pltpu.VMEM((n_kv, lk, 64), BF16),
            pltpu.VMEM((n_kv, VROWS, lk), BF16),
            pltpu.VMEM((lk, 8 * TQ), BF16),
            pltpu.VMEM((n_kv, 8, 128), F32),
        ],
        compiler_params=_cparams(("arbitrary", "arbitrary")),
        name=f"attn_{mode}_{'lat' if latent else 'ctx'}",
    )(*args)


N_CHUNKS = N_TOK // TC
GS_A, GS_B, GS_LM, GS_BLAST, GS_ALAST, GS_ROWS = 0, 8, 16, 24, 32, 40


def _scan_lanes(x, op, fill, reverse):
    axis = x.ndim - 1
    lane = lax.broadcasted_iota(jnp.int32, x.shape, axis)
    k = 1
    while k < TC:
        if reverse:
            x = op(x, jnp.where(lane < TC - k, pltpu.roll(x, TC - k, axis), fill))
        else:
            x = op(x, jnp.where(lane >= k, pltpu.roll(x, k, axis), fill))
        k *= 2
    return x


def _gate_stats_kernel(g_ref, st_ref, ac_ref):
    is_fwd = lax.broadcasted_iota(jnp.int32, (N_CHUNKS, 8, TC), 1) < H_B
    g = g_ref[...]
    gi = jnp.concatenate([g[:, 0:4], g[:, 8:12]], axis=1)
    f = _log_sigmoid(jnp.concatenate([g[:, 4:8], g[:, 12:16]], axis=1))
    b = jnp.where(is_fwd, _scan_lanes(f, jnp.add, 0.0, False), _scan_lanes(f, jnp.add, 0.0, True))
    a = gi - b
    lm = jnp.where(is_fwd, _scan_lanes(a, jnp.maximum, -jnp.inf, False),
                   _scan_lanes(a, jnp.maximum, -jnp.inf, True))
    st_ref[:, GS_A:GS_A + 8, :] = a
    st_ref[:, GS_B:GS_B + 8, :] = b
    st_ref[:, GS_LM:GS_LM + 8, :] = lm
    st_ref[:, GS_BLAST:GS_BLAST + 8, :] = jnp.broadcast_to(jnp.sum(f, axis=2, keepdims=True), a.shape)
    st_ref[:, GS_ALAST:GS_ALAST + 8, :] = jnp.broadcast_to(jnp.max(a, axis=2, keepdims=True), a.shape)

    def transpose_chunk(c, carry):
        ac_ref[c] = jnp.concatenate([st_ref[c, GS_A:GS_A + 8, :], jnp.zeros((120, TC), F32)], axis=0).T
        return carry

    lax.fori_loop(0, N_CHUNKS, transpose_chunk, 0, unroll=4)


def _gate_stats(zt):
    return pl.pallas_call(
        _gate_stats_kernel,
        grid=(1,),
        in_specs=[pl.BlockSpec((N_CHUNKS, 16, TC), lambda i: (0, ZT_G // 16, 0))],
        out_specs=[
            pl.BlockSpec((N_CHUNKS, GS_ROWS, TC), lambda i: (0, 0, 0)),
            pl.BlockSpec((N_CHUNKS, TC, 128), lambda i: (0, 0, 0)),
        ],
        out_shape=[
            jax.ShapeDtypeStruct((N_CHUNKS, GS_ROWS, TC), F32),
            jax.ShapeDtypeStruct((N_CHUNKS, TC, 128), F32),
        ],
        compiler_params=_cparams(("arbitrary",)),
        name="gate_stats",
    )(zt)


def _mlstm_kernel(k_ref, qt_ref, vt_ref, ot_ref, st_ref, ac_ref, s0_ref, m0_ref, gb_ref,
                  y_ref, s_ref, m_ref, hf_ref, hr_ref, *, seq):
    nc = seq // TC
    s_idx = lax.broadcasted_iota(jnp.int32, (TC, TC), 0)
    t_idx = lax.broadcasted_iota(jnp.int32, (TC, TC), 1)
    seen = (s_idx <= t_idx, s_idx >= t_idx)
    ones_rows = (lax.broadcasted_iota(jnp.int32, (VROWS - 64, TC), 0) == 0).astype(F32)
    units = [(d, h) for d in range(2) for h in range(H_B)]

    def chunk(j, carry):
        states, m = carry
        cs = (j, nc - 1 - j)
        stat = lambda r: jnp.concatenate([st_ref[cs[0], r:r + 4, :], st_ref[cs[1], r + 4:r + 8, :]], axis=0)
        a, b, lm = stat(GS_A), stat(GS_B), stat(GS_LM)
        mx = jnp.maximum(m, lm)
        w = jnp.exp(m - mx)
        floor = jnp.exp(-(b + mx))
        mxl = jnp.maximum(m, stat(GS_ALAST)[:, 0:1])
        decay = jnp.exp(m - mxl)
        wk = jnp.exp(a - mxl)
        m_new = stat(GS_BLAST)[:, 0:1] + mxl

        first = []
        for (d, h), st in zip(units, states):
            hs = slice(h * 64, (h + 1) * 64)
            kh = (k_ref[_ds(cs[d], TC), hs] * (DK_B ** -0.5)).astype(BF16)
            qt = qt_ref[cs[d], hs, :].astype(BF16)
            lhs = jnp.concatenate([kh, st.astype(BF16)], axis=0)
            first.append((kh, jnp.dot(lhs, qt, preferred_element_type=F32)))
        second = []
        for u, ((d, h), (kh, both)) in enumerate(zip(units, first)):
            a_col = ac_ref[cs[d], :, u:u + 1]
            dt = jnp.where(seen[d], a_col - mx[u:u + 1, :], -jnp.inf)
            sp = both[0:TC] * jnp.exp(dt)
            den = jnp.sum(sp, axis=0, keepdims=True) + w[u:u + 1, :] * both[TC + 64:TC + 65]
            second.append((sp.astype(BF16), den))
        new_states = []
        for u, ((d, h), st, (kh, both), (spb, den)) in enumerate(zip(units, states, first, second)):
            hs = slice(h * 64, (h + 1) * 64)
            vx = jnp.concatenate([vt_ref[cs[d], hs, :], ones_rows], axis=0)
            num = (jnp.dot(vx.astype(BF16), spb, preferred_element_type=F32)[0:64]
                   + w[u:u + 1, :] * both[TC:TC + 64])
            hv = num / jnp.maximum(jnp.abs(den), floor[u:u + 1, :])
            if d == 0:
                hf_ref[cs[d], hs, :] = hv
            else:
                hr_ref[cs[d], hs, :] = hv
            upd = jnp.dot((vx * wk[u:u + 1, :]).astype(BF16), kh, preferred_element_type=F32)
            new_states.append(decay[u:u + 1, :] * st + upd)
        return tuple(new_states), m_new

    init = (tuple(s0_ref[0, u] for u in range(2 * H_B)), m0_ref[0][:, 0:1])
    if nc == 1:
        states, m = chunk(0, init)
    else:
        states, m = lax.fori_loop(0, nc, chunk, init)
    for u, st in enumerate(states):
        s_ref[0, u] = st
    m_ref[0] = jnp.broadcast_to(m, (2 * H_B, 128))

    def epilogue(c, carry):
        ys = []
        for h in range(H_B):
            hs = slice(h * 64, (h + 1) * 64)
            x = hf_ref[c, hs, :] + hr_ref[c, hs, :]
            ys.append(_rms(x, gb_ref[...], axis=0) * jax.nn.sigmoid(ot_ref[c, hs, :]))
        y_ref[_ds(c, TC), :] = jnp.concatenate(ys, axis=0).T.astype(BF16)
        return carry

    if nc == 1:
        epilogue(0, 0)
    else:
        lax.fori_loop(0, nc, epilogue, 0)


def _mlstm(z, zt, stats, acols, s0, m0, gb, *, l, latent):
    ls = min(l, s0.shape[1] - 1)
    if latent:
        nb, seq, row0 = DEC_BATCH, DEC_SEQ, N_CTX
    else:
        nb, seq, row0 = BATCH, SEQ, 0
    blk0 = row0 // seq
    nc = seq // TC
    ztspec = lambda rb: pl.BlockSpec((nc, 256, TC), lambda b: (blk0 + b, rb, 0))
    return pl.pallas_call(
        functools.partial(_mlstm_kernel, seq=seq),
        grid=(nb,),
        in_specs=[
            pl.BlockSpec((seq, 256), lambda b: (blk0 + b, Z_KB // 256)),
            ztspec(ZT_QB // 256), ztspec(ZT_VB // 256), ztspec(ZT_OB // 256),
            pl.BlockSpec((nc, GS_ROWS, TC), lambda b: (blk0 + b, 0, 0)),
            pl.BlockSpec((nc, TC, 128), lambda b: (blk0 + b, 0, 0)),
            pl.BlockSpec((1, None, 2 * H_B, VROWS, 64), lambda b: (b, ls, 0, 0, 0)),
            pl.BlockSpec((1, None, 2 * H_B, 128), lambda b: (b, ls, 0, 0)),
            _const_spec((64, 1), (l,)),
        ],
        out_specs=[
            pl.BlockSpec((seq, 256), lambda b: (b, 0)),
            pl.BlockSpec((1, 2 * H_B, VROWS, 64), lambda b: (b, 0, 0, 0)),
            pl.BlockSpec((1, 2 * H_B, 128), lambda b: (b, 0, 0)),
        ],
        out_shape=[
            jax.ShapeDtypeStruct((nb * seq, 256), BF16),
            jax.ShapeDtypeStruct((nb, 2 * H_B, VROWS, 64), F32),
            jax.ShapeDtypeStruct((nb, 2 * H_B, 128), F32),
        ],
        scratch_shapes=[pltpu.VMEM((nc, 256, TC), F32), pltpu.VMEM((nc, 256, TC), F32)],
        compiler_params=_cparams(("arbitrary",)),
        name=f"mlstm_{'lat' if latent else 'ctx'}",
    )(z, zt, zt, zt, stats, acols, s0, m0, gb)


def _rope_tables(dim):
    t = jnp.arange(DEC_SEQ)
    row = (t // GRID_W).astype(F32)
    colp = (t % GRID_W).astype(F32)
    axis_dim = dim // 2
    freqs = ROPE_THETA ** (-jnp.arange(0, axis_dim, 2, dtype=F32) / axis_dim)
    ang = jnp.concatenate([row[:, None] * freqs, colp[:, None] * freqs], axis=-1)
    cos, sin = jnp.cos(ang), jnp.sin(ang)
    reps = 64 // dim
    cq = jnp.concatenate([cos, cos] * reps, axis=1)
    sq = jnp.concatenate([-sin, sin] * reps, axis=1)
    to_chunks = lambda x: x.T.reshape(64, DEC_SEQ // TC, TC).transpose(1, 0, 2)
    return cq, sq, to_chunks(cq), to_chunks(sq)


def _split_w_in(w, b_gates):
    a0, b0, c0 = 0, 768, 1808
    qa, ka, va = w[..., a0:a0 + 512], w[..., a0 + 512:a0 + 640], w[..., a0 + 640:a0 + 768]
    qb, kb = w[..., b0:b0 + 256], w[..., b0 + 256:b0 + 512]
    vb, ob = w[..., b0 + 512:b0 + 768], w[..., b0 + 768:b0 + 1024]
    gb = w[..., b0 + 1024:b0 + 1040]
    qc, kc, vc = w[..., c0:c0 + 256], w[..., c0 + 256:c0 + 512], w[..., c0 + 512:c0 + 768]
    wz = jnp.concatenate([kb, kc, ka], axis=-1)
    wt = jnp.swapaxes(jnp.concatenate([qa, qc, vc, qb, vb, ob, va, gb], axis=-1), -1, -2)
    bt = jnp.zeros((DEPTH, ZT_W, 1), F32).at[:, ZT_G:ZT_G + 16, 0].set(b_gates)
    return wz.astype(BF16), wt.astype(BF16), bt


def kernel(x_prompt, x_sample, cache_a_k, cache_a_v, cache_c_k, cache_c_v, state_b_C, state_b_n,
           state_b_m, c, c_ctx, w_ada, b_ada, g_norm, w_ff_in, w_ff_out, w_in, w_out, g_qa, g_ka,
           b_gates, g_b, lam_q1, lam_k1, lam_q2, lam_k2, g_c, g_final):
    h = (x_prompt.reshape(N_CTX, D_MODEL), x_sample.reshape(N_LAT, D_MODEL))
    cc = jnp.concatenate([c_ctx[None, :], c, jnp.zeros((3, D_MODEL), F32)], axis=0)
    mods = _ada(cc, w_ada, b_ada).reshape(DEPTH, 8, 9, D_MODEL)
    tab_a = _rope_tables(HD_A)
    tab_c = _rope_tables(D_C)

    gf = g_final.reshape(1, D_MODEL)
    gn = g_norm.reshape(DEPTH, 3, 1, D_MODEL)
    wi, wo, wmix = w_ff_in.astype(BF16), w_ff_out.astype(BF16), w_out.astype(BF16)
    wz, wt, bt = _split_w_in(w_in, b_gates)
    par_a = (g_qa.reshape(DEPTH, 64, 1), g_ka.reshape(DEPTH, 1, 64))
    par_c = (jnp.stack([lam_q1, lam_k1, lam_q2, lam_k2], axis=1), g_c.reshape(DEPTH, 64, 1))
    gb = g_b.reshape(DEPTH, 64, 1)
    ctx_a = (cache_a_k.reshape(DEC_BATCH, DEPTH, PAST_LEN, KV_A * HD_A), cache_a_v.transpose(0, 1, 3, 4, 2))
    ctx_c = (cache_c_k.reshape(DEC_BATCH, DEPTH, PAST_LEN, H_C * 2 * D_C), cache_c_v.transpose(0, 1, 3, 4, 2))
    s0 = jnp.concatenate([jnp.swapaxes(state_b_C, -1, -2), state_b_n[..., None, :],
                          jnp.zeros((DEC_BATCH, DEPTH, 2, H_B, VROWS - 65, DK_B), F32)], axis=-2)
    s0 = s0.reshape(DEC_BATCH, DEPTH, 2 * H_B, VROWS, DK_B)
    m0 = jnp.broadcast_to(state_b_m.reshape(DEC_BATCH, DEPTH, 2 * H_B, 1), (DEC_BATCH, DEPTH, 2 * H_B, 128))
    zeros_s = jnp.zeros((BATCH, 1, 2 * H_B, VROWS, DK_B), F32)
    zeros_m = jnp.zeros((BATCH, 1, 2 * H_B, 128), F32)

    outs = {k: [] for k in ("ak", "av", "ck", "cv", "bC", "bn", "bm")}
    for l in range(DEPTH):
        lam_init = 0.8 - 0.6 * math.exp(-0.3 * l)
        h = _ffn(h, mods, gn, wi, wo, gf, l=l, half=0, first=(l == 0))
        z, zt = _proj_in(h, mods, gn, wz, wt, bt, l=l)

        ya_c, ak, av = _attn("A", z, zt, ctx=None, tables=None, params=par_a, l=l, lam_init=lam_init,
                             latent=False)
        yc_c, ck, cv = _attn("C", z, zt, ctx=None, tables=None, params=par_c, l=l, lam_init=lam_init,
                             latent=False)
        stats, acols = _gate_stats(zt)
        yb_c, s_fin, m_fin = _mlstm(z, zt, stats, acols, zeros_s, zeros_m, gb, l=l, latent=False)

        (ya_l,) = _attn("A", z, zt, ctx=ctx_a, tables=tab_a, params=par_a, l=l, lam_init=lam_init,
                        latent=True)
        (yc_l,) = _attn("C", z, zt, ctx=ctx_c, tables=tab_c, params=par_c, l=l, lam_init=lam_init,
                        latent=True)
        yb_l, _, _ = _mlstm(z, zt, stats, acols, s0, m0, gb, l=l, latent=True)

        h = _ffn(h, mods, gn, wi, wo, gf, l=l, half=1,
                 mix=((ya_c, ya_l), (yb_c, yb_l), (yc_c, yc_l), wmix), final=(l == DEPTH - 1))

        outs["ak"].append(ak.reshape(BATCH, SEQ, KV_A, HD_A))
        outs["av"].append(av.reshape(BATCH, SEQ, KV_A, HD_A))
        outs["ck"].append(ck.reshape(BATCH, SEQ, H_C, 2, D_C))
        outs["cv"].append(cv.reshape(BATCH, SEQ, H_C, 2 * D_C))
        s_fin = s_fin.reshape(BATCH, 2, H_B, VROWS, DK_B)
        outs["bC"].append(jnp.swapaxes(s_fin[..., 0:64, :], -1, -2))
        outs["bn"].append(s_fin[..., 64, :])
        outs["bm"].append(m_fin[:, :, 0].reshape(BATCH, 2, H_B))

    y_prompt = h[0].reshape(BATCH, SEQ, D_MODEL)
    y_sample = h[1].reshape(DEC_BATCH, DEC_SEQ, D_MODEL)
    st = lambda k: jnp.stack(outs[k], axis=1)
    return (y_prompt, y_sample, st("ak"), st("av"), st("ck"), st("cv"), st("bC"), st("bn"), st("bm"))
```

```python
import functools
import math

import jax
import jax.numpy as jnp
from jax import lax
from jax.experimental import pallas as pl
from jax.experimental.pallas import tpu as pltpu

F32 = jnp.float32
BF16 = jnp.bfloat16

D_MODEL = 1024
BATCH = 16
SEQ = 256
DEPTH = 2
DEC_BATCH = 4
DEC_SEQ = 2048
PAST_LEN = 512
GRID_W = 64
ROPE_THETA = 10000.0
EPS = 1e-6
HD_A = 64
H_A = 8
KV_A = 2
DK_B = 64
H_B = 4
D_C = 32
H_C = 4
D_FF = 2816
LOG2E = 1.4426950408889634

N_CTX = BATCH * SEQ
N_LAT = DEC_BATCH * DEC_SEQ
N_TOK = N_CTX + N_LAT

TM = 512
CTX_TILES = N_CTX // TM
LAT_TILES_PER_BATCH = DEC_SEQ // TM
FF_CHUNKS = ((0, 1536), (1536, D_FF))
FF_ROWS = 256
TC = 256
SB = 256
TQ = 256
VROWS = 80
BOUND_SLACK = 1.0 + 2.0 ** -6
MAX_BOUND_GAP = 60.0

Z_KB, Z_KC, Z_KA = 0, 256, 512
Z_W = 640
ZT_QA, ZT_QC, ZT_VC, ZT_QB, ZT_VB, ZT_OB, ZT_VA, ZT_G = 0, 512, 768, 1024, 1280, 1536, 1792, 1920
ZT_W = 1936

VMEM_LIMIT = 56 * 1024 * 1024


def _cparams(sem):
    return pltpu.CompilerParams(dimension_semantics=sem, vmem_limit_bytes=VMEM_LIMIT)


def _const_spec(shape, lead=()):
    nd = len(shape)
    return pl.BlockSpec((None,) * len(lead) + tuple(shape), lambda *_: tuple(lead) + (0,) * nd,
                        pipeline_mode=pl.Buffered(1))


def _mod_row(i):
    return jnp.maximum(i - CTX_TILES, 0) // LAT_TILES_PER_BATCH + (i >= CTX_TILES).astype(jnp.int32)


def _rms(x, g, axis=-1):
    ms = jnp.mean(x * x, axis=axis, keepdims=True)
    return x * lax.rsqrt(ms + EPS) * g


def _norm_mod(x, g, mod_ref, k):
    shift = mod_ref[0, k:k + 1, :]
    scale = mod_ref[0, k + 1:k + 2, :]
    return _rms(x, g) * (1.0 + scale) + shift


def _ds(c, n):
    if isinstance(c, int):
        return pl.ds(c * n, n)
    return pl.ds(pl.multiple_of(c * n, n), n)


def _log_sigmoid(x):
    return jnp.minimum(x, 0.0) - jnp.log1p(jnp.exp(-jnp.abs(x)))


def _ada_kernel(c_ref, w_ref, b_ref, o_ref):
    c = c_ref[...]
    a = (c * jax.nn.sigmoid(c)).astype(BF16)
    o_ref[0] = jnp.dot(a, w_ref[0].astype(BF16), preferred_element_type=F32) + b_ref[0]


def _ada(cc, w_ada, b_ada):
    tn = 2304
    n = 9 * D_MODEL
    return pl.pallas_call(
        _ada_kernel,
        grid=(DEPTH, n // tn),
        in_specs=[
            pl.BlockSpec((8, D_MODEL), lambda l, j: (0, 0)),
            pl.BlockSpec((1, D_MODEL, tn), lambda l, j: (l, 0, j)),
            pl.BlockSpec((1, 1, tn), lambda l, j: (l, 0, j)),
        ],
        out_specs=pl.BlockSpec((1, 8, tn), lambda l, j: (l, 0, j)),
        out_shape=jax.ShapeDtypeStruct((DEPTH, 8, n), F32),
        compiler_params=_cparams(("arbitrary", "arbitrary")),
        name="ada",
    )(cc, w_ada, b_ada.reshape(DEPTH, 1, n))


def _ffn_kernel(*refs, k, first, mix, final):
    is_ctx = pl.program_id(0) < CTX_TILES
    it = iter(refs)
    x_refs = (next(it), next(it)) if first else (next(it),)
    if mix:
        y_refs = [(next(it), next(it)) for _ in range(3)]
        wmix_ref = next(it)
    mod_ref, g_ref, wi_ref, wo_ref, gf_ref = (next(it) for _ in range(5))

    nrows = TM if mix else FF_ROWS
    ys = []
    for r in range(TM // nrows):
        rows = slice(r * nrows, (r + 1) * nrows)
        pick = lambda c_ref, l_ref: jnp.where(is_ctx, c_ref[rows, :], l_ref[rows, :])
        x = pick(*x_refs) if first else x_refs[0][rows, :]
        if mix:
            ya, yb, yc = (pick(*pair) for pair in y_refs)
            y = jnp.dot(ya, wmix_ref[0:512, :], preferred_element_type=F32)
            y = y + jnp.dot(yb, wmix_ref[512:768, :], preferred_element_type=F32)
            y = y + jnp.dot(yc, wmix_ref[768:1024, :], preferred_element_type=F32)
            x = x + mod_ref[0, 5:6, :] * y
        xn = _norm_mod(x, g_ref[...], mod_ref, k).astype(BF16)
        acc = jnp.zeros((nrows, D_MODEL), F32)
        for lo, hi in FF_CHUNKS:
            g = jnp.dot(xn, wi_ref[:, lo:hi], preferred_element_type=F32)
            u = jnp.dot(xn, wi_ref[:, D_FF + lo:D_FF + hi], preferred_element_type=F32)
            a = (g * jax.nn.sigmoid(g) * u).astype(BF16)
            acc = acc + jnp.dot(a, wo_ref[lo:hi, :], preferred_element_type=F32)
        y = x + 0.5 * mod_ref[0, k + 2:k + 3, :] * acc
        if final:
            ys.append(_rms(y, gf_ref[...]))
        else:
            refs[-1][rows, :] = y
    if final:
        yc_ref, yl_ref = refs[-2:]
        y = jnp.concatenate(ys, axis=0)

        @pl.when(is_ctx)
        def _():
            yc_ref[...] = y

        @pl.when(jnp.logical_not(is_ctx))
        def _():
            yl_ref[...] = y


def _ctx_lat_specs(width):
    return [
        pl.BlockSpec((TM, width), lambda i: (jnp.minimum(i, CTX_TILES - 1), 0)),
        pl.BlockSpec((TM, width), lambda i: (jnp.maximum(i - CTX_TILES, 0), 0)),
    ]


def _mod_spec(l):
    return pl.BlockSpec((None, 1, 9, D_MODEL), lambda i: (l, _mod_row(i), 0, 0))


def _ffn(h, mods, g, wi, wo, gf, *, l, half, first=False, mix=None, final=False):
    k = 6 * half
    flat_spec = pl.BlockSpec((TM, D_MODEL), lambda i: (i, 0))
    flat_shape = jax.ShapeDtypeStruct((N_TOK, D_MODEL), F32)
    mix_specs, mix_args = [], []
    if mix is not None:
        mix_specs = (_ctx_lat_specs(512) + _ctx_lat_specs(256) + _ctx_lat_specs(256)
                     + [_const_spec((D_MODEL, D_MODEL), (l,))])
        mix_args = [*mix[0], *mix[1], *mix[2], mix[3]]
    return pl.pallas_call(
        functools.partial(_ffn_kernel, k=k, first=first, mix=mix is not None, final=final),
        grid=(N_TOK // TM,),
        in_specs=(_ctx_lat_specs(D_MODEL) if first else [flat_spec]) + mix_specs + [
            _mod_spec(l),
            _const_spec((1, D_MODEL), (l, 2 * half)),
            _const_spec((D_MODEL, 2 * D_FF), (l, half)),
            _const_spec((D_FF, D_MODEL), (l, half)),
            _const_spec((1, D_MODEL)),
        ],
        out_specs=_ctx_lat_specs(D_MODEL) if final else flat_spec,
        out_shape=[jax.ShapeDtypeStruct((N_CTX, D_MODEL), F32),
                   jax.ShapeDtypeStruct((N_LAT, D_MODEL), F32)] if final else flat_shape,
        compiler_params=_cparams(("arbitrary",)),
        name="ffn_final" if final else ("ffn_first" if first else ("ffn_mix" if mix else "ffn")),
    )(*(h if first else (h,)), *mix_args, mods, g, wi, wo, gf)


def _proj_in_kernel(h_ref, mod_ref, g_ref, w_ref, wt_ref, bt_ref, z_ref, zt_ref):
    xn = _norm_mod(h_ref[...], g_ref[...], mod_ref, 3).astype(BF16)
    z_ref[...] = jnp.dot(xn, w_ref[...], preferred_element_type=F32)
    zt = lax.dot_general(wt_ref[...], xn, (((1,), (1,)), ((), ())), preferred_element_type=F32)
    zt = zt + bt_ref[...]
    for c in range(TM // TC):
        zt_ref[c] = zt[:, c * TC:(c + 1) * TC]


def _proj_in(h, mods, g, wz, wt, bt, *, l):
    return pl.pallas_call(
        _proj_in_kernel,
        grid=(N_TOK // TM,),
        in_specs=[
            pl.BlockSpec((TM, D_MODEL), lambda i: (i, 0)),
            _mod_spec(l),
            _const_spec((1, D_MODEL), (l, 1)),
            _const_spec((D_MODEL, Z_W), (l,)),
            _const_spec((ZT_W, D_MODEL), (l,)),
            _const_spec((ZT_W, 1), (l,)),
        ],
        out_specs=[
            pl.BlockSpec((TM, Z_W), lambda i: (i, 0)),
            pl.BlockSpec((TM // TC, ZT_W, TC), lambda i: (i, 0, 0)),
        ],
        out_shape=[
            jax.ShapeDtypeStruct((N_TOK, Z_W), F32),
            jax.ShapeDtypeStruct((N_TOK // TC, ZT_W, TC), F32),
        ],
        compiler_params=_cparams(("arbitrary",)),
        name="proj_in",
    )(h, mods, g, wz, wt, bt)


def _swap_halves(x, width, axis):
    n = x.shape[axis]
    half = width // 2
    parts = []
    for s in range(0, n, width):
        parts.append(lax.slice_in_dim(x, s + half, s + width, axis=axis))
        parts.append(lax.slice_in_dim(x, s, s + half, axis=axis))
    return jnp.concatenate(parts, axis=axis)


def _attn_kernel(*refs, mode, seq, has_ctx, rope, emit_cache, lam_init):
    it = iter(refs)
    qt_ref, k_ref, vt_ref = next(it), next(it), next(it)
    if has_ctx:
        ck_ref, cvt_ref = next(it), next(it)
    if rope:
        cqt_ref, sqt_ref, ck_tab, sk_tab = next(it), next(it), next(it), next(it)
    if mode == "A":
        gq_ref, gk_ref = next(it), next(it)
    else:
        lam_ref, gc_ref = next(it), next(it)
    y_ref = next(it)
    if emit_cache:
        kc_ref, vc_ref = next(it), next(it)
    kbuf, vbuf, p_ref, kmax_ref = next(it), next(it), next(it), next(it)

    n_kv = KV_A if mode == "A" else H_C
    per_kv = 8 // n_kv
    n_ctx = PAST_LEN if has_ctx else 0
    lk = n_ctx + seq
    width = HD_A if mode == "A" else D_C

    @pl.when(pl.program_id(1) == 0)
    def _fill():
        ones_rows = (lax.broadcasted_iota(jnp.int32, (VROWS - 64, lk), 0) == 0).astype(BF16)
        ones_cols = jnp.ones((64, 128), BF16)

        def max_sq_norm(kb):
            if lk == SB:
                return jnp.zeros((1, 1), F32)
            kf = kb.astype(F32)
            sq = jnp.dot((kf * kf).astype(BF16), ones_cols, preferred_element_type=F32)
            return jnp.max(sq, axis=0, keepdims=True)[:, 0:1]

        for h in range(n_kv):
            hs = slice(h * 64, (h + 1) * 64)
            vbuf[h, 64:VROWS, :] = ones_rows
            kmax2 = jnp.zeros((1, 1), F32)
            if has_ctx:
                kb = ck_ref[0, :, hs].astype(BF16)
                kmax2 = jnp.maximum(kmax2, max_sq_norm(kb))
                kbuf[h, 0:n_ctx, :] = kb
                vbuf[h, 0:64, 0:n_ctx] = cvt_ref[0, h].astype(BF16)
            for c in range(seq // TC):
                rows = slice(c * TC, (c + 1) * TC)
                k = k_ref[rows, hs]
                if mode == "A":
                    k = _rms(k, gk_ref[...])
                if emit_cache:
                    kc_ref[0, rows, hs] = k
                if rope:
                    k = k * ck_tab[rows, :] + _swap_halves(k, width, 1) * sk_tab[rows, :]
                kb = k.astype(BF16)
                kmax2 = jnp.maximum(kmax2, max_sq_norm(kb))
                kbuf[h, n_ctx + c * TC:n_ctx + (c + 1) * TC, :] = kb
                vbuf[h, 0:64, n_ctx + c * TC:n_ctx + (c + 1) * TC] = vt_ref[c, hs, :].astype(BF16)
            kmax_ref[h] = jnp.broadcast_to(jnp.sqrt(kmax2), (8, 128))
        if emit_cache:
            for c in range(seq // TC):
                vc_ref[0, c * TC:(c + 1) * TC, :] = vt_ref[c].T

    qts = []
    if mode == "A":
        scale = HD_A ** -0.5 * LOG2E
        for hh in range(H_A):
            x = _rms(qt_ref[0, hh * 64:(hh + 1) * 64, :], gq_ref[...], axis=0)
            if rope:
                x = x * cqt_ref[0] + _swap_halves(x, width, 0) * sqt_ref[0]
            qts.append((x * scale).astype(BF16))
    else:
        scale = D_C ** -0.5 * LOG2E
        first_map = lax.broadcasted_iota(jnp.int32, (64, TQ), 0) < D_C
        for h in range(H_C):
            x = qt_ref[0, h * 64:(h + 1) * 64, :]
            if rope:
                x = x * cqt_ref[0] + _swap_halves(x, width, 0) * sqt_ref[0]
            x = x * scale
            qts.append(jnp.where(first_map, x, 0.0).astype(BF16))
            qts.append(jnp.where(first_map, 0.0, x).astype(BF16))

    def value_matmul(u):
        ox = jnp.dot(vbuf[u // per_kv], p_ref[:, u * TQ:(u + 1) * TQ], preferred_element_type=F32)
        return ox[0:64] / ox[64:65]

    def emit(outs):
        if mode == "A":
            for pair in range(4):
                o = jnp.concatenate(outs[2 * pair:2 * pair + 2], axis=0)
                y_ref[:, pair * 128:(pair + 1) * 128] = o.T.astype(BF16)
        else:
            lp = lam_ref[...]
            lam = (jnp.exp(jnp.sum(lp[0:1] * lp[1:2], axis=1, keepdims=True))
                   - jnp.exp(jnp.sum(lp[2:3] * lp[3:4], axis=1, keepdims=True)) + lam_init)
            for pair in range(2):
                o = []
                for h in (2 * pair, 2 * pair + 1):
                    d = outs[2 * h] - lam * outs[2 * h + 1]
                    o.append(_rms(d, gc_ref[...], axis=0) * (1.0 - lam_init))
                y_ref[:, pair * 128:(pair + 1) * 128] = jnp.concatenate(o, axis=0).T.astype(BF16)

    outs = []
    if lk == SB:
        sts = [jnp.dot(kbuf[u // per_kv], qts[u], preferred_element_type=F32) for u in range(8)]
        ps = [jnp.exp2(st - jnp.max(st, axis=0, keepdims=True)).astype(BF16) for st in sts]
        for u in range(8):
            ox = jnp.dot(vbuf[u // per_kv], ps[u], preferred_element_type=F32)
            outs.append(ox[0:64] / ox[64:65])
        emit(outs)
        return

    mhats, ms = [], []
    for u in range(8):
        g = u // per_kv
        qn = jnp.sqrt(jnp.sum(jnp.square(qts[u].astype(F32)), axis=0, keepdims=True))
        mhat = qn * kmax_ref[g][0:1, 0:1] * BOUND_SLACK
        mrun = jnp.full((8, TQ), -jnp.inf, F32)
        for j in range(lk // SB):
            rows = slice(j * SB, (j + 1) * SB)
            st = jnp.dot(kbuf[g, rows, :], qts[u], preferred_element_type=F32)
            p_ref[rows, u * TQ:(u + 1) * TQ] = jnp.exp2(st - mhat).astype(BF16)
            for r in range(SB // 8):
                mrun = jnp.maximum(mrun, st[8 * r:8 * r + 8])
        mhats.append(mhat)
        ms.append(jnp.max(mrun, axis=0, keepdims=True))
        if u >= 1:
            outs.append(value_matmul(u - 1))
    outs.append(value_matmul(7))
    emit(outs)

    worst_gap = functools.reduce(jnp.maximum, [jnp.max(mh - m) for mh, m in zip(mhats, ms)])

    @pl.when(worst_gap > MAX_BOUND_GAP)
    def _redo():
        redone = []
        for u in range(8):
            def block(j, carry, u=u):
                rows = _ds(j, SB)
                st = jnp.dot(kbuf[u // per_kv, rows, :], qts[u], preferred_element_type=F32)
                p_ref[rows, u * TQ:(u + 1) * TQ] = jnp.exp2(st - ms[u]).astype(BF16)
                return carry

            lax.fori_loop(0, lk // SB, block, 0)
            redone.append(value_matmul(u))
        emit(redone)


def _attn(mode, z, zt, *, ctx, tables, params, l, lam_init, latent):
    if latent:
        nb, seq, row0 = DEC_BATCH, DEC_SEQ, N_CTX
    else:
        nb, seq, row0 = BATCH, SEQ, 0
    nq = seq // TQ
    n_kv = KV_A if mode == "A" else H_C
    n_ctx = PAST_LEN if ctx is not None else 0
    lk = n_ctx + seq
    qw = 512 if mode == "A" else 256
    vw = n_kv * 64
    q_blk = (ZT_QA if mode == "A" else ZT_QC) // qw
    k_blk = (Z_KA if mode == "A" else Z_KC) // vw
    v_blk = (ZT_VA if mode == "A" else ZT_VC) // vw
    emit_cache = not latent
    blk0 = row0 // seq

    in_specs = [
        pl.BlockSpec((1, qw, TC), lambda b, i: (row0 // TC + b * nq + i, q_blk, 0)),
        pl.BlockSpec((seq, vw), lambda b, i: (blk0 + b, k_blk)),
        pl.BlockSpec((seq // TC, vw, TC), lambda b, i: (blk0 + b, v_blk, 0)),
    ]
    args = [zt, z, zt]
    if ctx is not None:
        in_specs += [
            pl.BlockSpec((1, None, PAST_LEN, vw), lambda b, i: (b, l, 0, 0)),
            pl.BlockSpec((1, None, n_kv, 64, PAST_LEN), lambda b, i: (b, l, 0, 0, 0)),
        ]
        args += list(ctx)
    if tables is not None:
        cq, sq, cqt, sqt = tables
        in_specs += [
            pl.BlockSpec((1, 64, TC), lambda b, i: (i, 0, 0)),
            pl.BlockSpec((1, 64, TC), lambda b, i: (i, 0, 0)),
            _const_spec((seq, 64)),
            _const_spec((seq, 64)),
        ]
        args += [cqt, sqt, cq, sq]
    for p in params:
        in_specs.append(_const_spec(p.shape[1:], (l,)))
        args.append(p)

    out_specs = [pl.BlockSpec((TQ, qw), lambda b, i: (b * nq + i, 0))]
    out_shape = [jax.ShapeDtypeStruct((nb * seq, qw), BF16)]
    if emit_cache:
        out_specs += [pl.BlockSpec((1, seq, vw), lambda b, i: (b, 0, 0))] * 2
        out_shape += [jax.ShapeDtypeStruct((nb, seq, vw), F32)] * 2

    return pl.pallas_call(
        functools.partial(_attn_kernel, mode=mode, seq=seq, has_ctx=ctx is not None,
                          rope=tables is not None, emit_cache=emit_cache, lam_init=lam_init),
        grid=(nb, nq),
        in_specs=in_specs,
        out_specs=out_specs,
        out_shape=out_shape,
        scratch_shapes=[
            pltpu.VMEM((n_kv, lk, 64), BF16),
            pltpu.VMEM((n_kv, VROWS, lk), BF16),
            pltpu.VMEM((lk, 8 * TQ), BF16),
            pltpu.VMEM((n_kv, 8, 128), F32),
        ],
        compiler_params=_cparams(("arbitrary", "arbitrary")),
        name=f"attn_{mode}_{'lat' if latent else 'ctx'}",
    )(*args)


N_CHUNKS = N_TOK // TC
GS_STEPS = 4
GS_A, GS_B, GS_LM, GS_BLAST, GS_ALAST, GS_ROWS = 0, 8, 16, 24, 32, 40


def _scan_lanes(x, op, fill, reverse):
    axis = x.ndim - 1
    lane = lax.broadcasted_iota(jnp.int32, x.shape, axis)
    k = 1
    while k < TC:
        if reverse:
            x = op(x, jnp.where(lane < TC - k, pltpu.roll(x, TC - k, axis), fill))
        else:
            x = op(x, jnp.where(lane >= k, pltpu.roll(x, k, axis), fill))
        k *= 2
    return x


def _gate_stats_kernel(g_ref, st_ref, ac_ref):
    g = g_ref[...]
    is_fwd = lax.broadcasted_iota(jnp.int32, (g.shape[0], 8, TC), 1) < H_B
    gi = jnp.concatenate([g[:, 0:4], g[:, 8:12]], axis=1)
    f = _log_sigmoid(jnp.concatenate([g[:, 4:8], g[:, 12:16]], axis=1))
    b = jnp.where(is_fwd, _scan_lanes(f, jnp.add, 0.0, False), _scan_lanes(f, jnp.add, 0.0, True))
    a = gi - b
    lm = jnp.where(is_fwd, _scan_lanes(a, jnp.maximum, -jnp.inf, False),
                   _scan_lanes(a, jnp.maximum, -jnp.inf, True))
    st_ref[:, GS_A:GS_A + 8, :] = a
    st_ref[:, GS_B:GS_B + 8, :] = b
    st_ref[:, GS_LM:GS_LM + 8, :] = lm
    st_ref[:, GS_BLAST:GS_BLAST + 8, :] = jnp.broadcast_to(jnp.sum(f, axis=2, keepdims=True), a.shape)
    st_ref[:, GS_ALAST:GS_ALAST + 8, :] = jnp.broadcast_to(jnp.max(a, axis=2, keepdims=True), a.shape)

    def transpose_chunk(c, carry):
        ac_ref[c] = jnp.concatenate([st_ref[c, GS_A:GS_A + 8, :], jnp.zeros((120, TC), F32)], axis=0).T
        return carry

    lax.fori_loop(0, g.shape[0], transpose_chunk, 0, unroll=4)


def _gate_stats(zt):
    return pl.pallas_call(
        _gate_stats_kernel,
        grid=(GS_STEPS,),
        in_specs=[pl.BlockSpec((N_CHUNKS // GS_STEPS, 16, TC), lambda i: (i, ZT_G // 16, 0))],
        out_specs=[
            pl.BlockSpec((N_CHUNKS // GS_STEPS, GS_ROWS, TC), lambda i: (i, 0, 0)),
            pl.BlockSpec((N_CHUNKS // GS_STEPS, TC, 128), lambda i: (i, 0, 0)),
        ],
        out_shape=[
            jax.ShapeDtypeStruct((N_CHUNKS, GS_ROWS, TC), F32),
            jax.ShapeDtypeStruct((N_CHUNKS, TC, 128), F32),
        ],
        compiler_params=_cparams(("arbitrary",)),
        name="gate_stats",
    )(zt)


def _mlstm_kernel(k_ref, qt_ref, vt_ref, ot_ref, st_ref, ac_ref, s0_ref, m0_ref, gb_ref,
                  y_ref, s_ref, m_ref, hf_ref, hr_ref, *, seq):
    nc = seq // TC
    s_idx = lax.broadcasted_iota(jnp.int32, (TC, TC), 0)
    t_idx = lax.broadcasted_iota(jnp.int32, (TC, TC), 1)
    seen = (s_idx <= t_idx, s_idx >= t_idx)
    ones_rows = (lax.broadcasted_iota(jnp.int32, (VROWS - 64, TC), 0) == 0).astype(F32)
    units = [(d, h) for d in range(2) for h in range(H_B)]

    def chunk(j, carry):
        states, m = carry
        cs = (j, nc - 1 - j)
        stat = lambda r: jnp.concatenate([st_ref[cs[0], r:r + 4, :], st_ref[cs[1], r + 4:r + 8, :]], axis=0)
        a, b, lm = stat(GS_A), stat(GS_B), stat(GS_LM)
        mx = jnp.maximum(m, lm)
        w = jnp.exp(m - mx)
        floor = jnp.exp(-(b + mx))
        mxl = jnp.maximum(m, stat(GS_ALAST)[:, 0:1])
        decay = jnp.exp(m - mxl)
        wk = jnp.exp(a - mxl)
        m_new = stat(GS_BLAST)[:, 0:1] + mxl

        first = []
        for (d, h), st in zip(units, states):
            hs = slice(h * 64, (h + 1) * 64)
            kh = (k_ref[_ds(cs[d], TC), hs] * (DK_B ** -0.5)).astype(BF16)
            qt = qt_ref[cs[d], hs, :].astype(BF16)
            lhs = jnp.concatenate([kh, st.astype(BF16)], axis=0)
            first.append((kh, jnp.dot(lhs, qt, preferred_element_type=F32)))
        second = []
        for u, ((d, h), (kh, both)) in enumerate(zip(units, first)):
            a_col = ac_ref[cs[d], :, u:u + 1]
            dt = jnp.where(seen[d], a_col - mx[u:u + 1, :], -jnp.inf)
            sp = both[0:TC] * jnp.exp(dt)
            den = jnp.sum(sp, axis=0, keepdims=True) + w[u:u + 1, :] * both[TC + 64:TC + 65]
            second.append((sp.astype(BF16), den))
        new_states = []
        for u, ((d, h), st, (kh, both), (spb, den)) in enumerate(zip(units, states, first, second)):
            hs = slice(h * 64, (h + 1) * 64)
            vx = jnp.concatenate([vt_ref[cs[d], hs, :], ones_rows], axis=0)
            num = (jnp.dot(vx.astype(BF16), spb, preferred_element_type=F32)[0:64]
                   + w[u:u + 1, :] * both[TC:TC + 64])
            hv = num / jnp.maximum(jnp.abs(den), floor[u:u + 1, :])
            if d == 0:
                hf_ref[cs[d], hs, :] = hv
            else:
                hr_ref[cs[d], hs, :] = hv
            upd = jnp.dot((vx * wk[u:u + 1, :]).astype(BF16), kh, preferred_element_type=F32)
            new_states.append(decay[u:u + 1, :] * st + upd)
        return tuple(new_states), m_new

    init = (tuple(s0_ref[0, u] for u in range(2 * H_B)), m0_ref[0][:, 0:1])
    if nc == 1:
        states, m = chunk(0, init)
    else:
        states, m = lax.fori_loop(0, nc, chunk, init)
    for u, st in enumerate(states):
        s_ref[0, u] = st
    m_ref[0] = jnp.broadcast_to(m, (2 * H_B, 128))

    def epilogue(c, carry):
        ys = []
        for h in range(H_B):
            hs = slice(h * 64, (h + 1) * 64)
            x = hf_ref[c, hs, :] + hr_ref[c, hs, :]
            ys.append(_rms(x, gb_ref[...], axis=0) * jax.nn.sigmoid(ot_ref[c, hs, :]))
        y_ref[_ds(c, TC), :] = jnp.concatenate(ys, axis=0).T.astype(BF16)
        return carry

    if nc == 1:
        epilogue(0, 0)
    else:
        lax.fori_loop(0, nc, epilogue, 0)


def _mlstm(z, zt, stats, acols, s0, m0, gb, *, l, latent):
    ls = min(l, s0.shape[1] - 1)
    if latent:
        nb, seq, row0 = DEC_BATCH, DEC_SEQ, N_CTX
    else:
        nb, seq, row0 = BATCH, SEQ, 0
    blk0 = row0 // seq
    nc = seq // TC
    ztspec = lambda rb: pl.BlockSpec((nc, 256, TC), lambda b: (blk0 + b, rb, 0))
    return pl.pallas_call(
        functools.partial(_mlstm_kernel, seq=seq),
        grid=(nb,),
        in_specs=[
            pl.BlockSpec((seq, 256), lambda b: (blk0 + b, Z_KB // 256)),
            ztspec(ZT_QB // 256), ztspec(ZT_VB // 256), ztspec(ZT_OB // 256),
            pl.BlockSpec((nc, GS_ROWS, TC), lambda b: (blk0 + b, 0, 0)),
            pl.BlockSpec((nc, TC, 128), lambda b: (blk0 + b, 0, 0)),
            pl.BlockSpec((1, None, 2 * H_B, VROWS, 64), lambda b: (b, ls, 0, 0, 0)),
            pl.BlockSpec((1, None, 2 * H_B, 128), lambda b: (b, ls, 0, 0)),
            _const_spec((64, 1), (l,)),
        ],
        out_specs=[
            pl.BlockSpec((seq, 256), lambda b: (b, 0)),
            pl.BlockSpec((1, 2 * H_B, VROWS, 64), lambda b: (b, 0, 0, 0)),
            pl.BlockSpec((1, 2 * H_B, 128), lambda b: (b, 0, 0)),
        ],
        out_shape=[
            jax.ShapeDtypeStruct((nb * seq, 256), BF16),
            jax.ShapeDtypeStruct((nb, 2 * H_B, VROWS, 64), F32),
            jax.ShapeDtypeStruct((nb, 2 * H_B, 128), F32),
        ],
        scratch_shapes=[pltpu.VMEM((nc, 256, TC), F32), pltpu.VMEM((nc, 256, TC), F32)],
        compiler_params=_cparams(("arbitrary",)),
        name=f"mlstm_{'lat' if latent else 'ctx'}",
    )(z, zt, zt, zt, stats, acols, s0, m0, gb)


def _rope_tables(dim):
    t = jnp.arange(DEC_SEQ)
    row = (t // GRID_W).astype(F32)
    colp = (t % GRID_W).astype(F32)
    axis_dim = dim // 2
    freqs = ROPE_THETA ** (-jnp.arange(0, axis_dim, 2, dtype=F32) / axis_dim)
    ang = jnp.concatenate([row[:, None] * freqs, colp[:, None] * freqs], axis=-1)
    cos, sin = jnp.cos(ang), jnp.sin(ang)
    reps = 64 // dim
    cq = jnp.concatenate([cos, cos] * reps, axis=1)
    sq = jnp.concatenate([-sin, sin] * reps, axis=1)
    to_chunks = lambda x: x.T.reshape(64, DEC_SEQ // TC, TC).transpose(1, 0, 2)
    return cq, sq, to_chunks(cq), to_chunks(sq)


def _split_w_in(w, b_gates):
    a0, b0, c0 = 0, 768, 1808
    qa, ka, va = w[..., a0:a0 + 512], w[..., a0 + 512:a0 + 640], w[..., a0 + 640:a0 + 768]
    qb, kb = w[..., b0:b0 + 256], w[..., b0 + 256:b0 + 512]
    vb, ob = w[..., b0 + 512:b0 + 768], w[..., b0 + 768:b0 + 1024]
    gb = w[..., b0 + 1024:b0 + 1040]
    qc, kc, vc = w[..., c0:c0 + 256], w[..., c0 + 256:c0 + 512], w[..., c0 + 512:c0 + 768]
    wz = jnp.concatenate([kb, kc, ka], axis=-1)
    wt = jnp.swapaxes(jnp.concatenate([qa, qc, vc, qb, vb, ob, va, gb], axis=-1), -1, -2)
    bt = jnp.zeros((DEPTH, ZT_W, 1), F32).at[:, ZT_G:ZT_G + 16, 0].set(b_gates)
    return wz.astype(BF16), wt.astype(BF16), bt


def kernel(x_prompt, x_sample, cache_a_k, cache_a_v, cache_c_k, cache_c_v, state_b_C, state_b_n,
           state_b_m, c, c_ctx, w_ada, b_ada, g_norm, w_ff_in, w_ff_out, w_in, w_out, g_qa, g_ka,
           b_gates, g_b, lam_q1, lam_k1, lam_q2, lam_k2, g_c, g_final):
    h = (x_prompt.reshape(N_CTX, D_MODEL), x_sample.reshape(N_LAT, D_MODEL))
    cc = jnp.concatenate([c_ctx[None, :], c, jnp.zeros((3, D_MODEL), F32)], axis=0)
    mods = _ada(cc, w_ada, b_ada).reshape(DEPTH, 8, 9, D_MODEL)
    tab_a = _rope_tables(HD_A)
    tab_c = _rope_tables(D_C)

    gf = g_final.reshape(1, D_MODEL)
    gn = g_norm.reshape(DEPTH, 3, 1, D_MODEL)
    wi, wo, wmix = w_ff_in.astype(BF16), w_ff_out.astype(BF16), w_out.astype(BF16)
    wz, wt, bt = _split_w_in(w_in, b_gates)
    par_a = (g_qa.reshape(DEPTH, 64, 1), g_ka.reshape(DEPTH, 1, 64))
    par_c = (jnp.stack([lam_q1, lam_k1, lam_q2, lam_k2], axis=1), g_c.reshape(DEPTH, 64, 1))
    gb = g_b.reshape(DEPTH, 64, 1)
    ctx_a = (cache_a_k.reshape(DEC_BATCH, DEPTH, PAST_LEN, KV_A * HD_A), cache_a_v.transpose(0, 1, 3, 4, 2))
    ctx_c = (cache_c_k.reshape(DEC_BATCH, DEPTH, PAST_LEN, H_C * 2 * D_C), cache_c_v.transpose(0, 1, 3, 4, 2))
    s0 = jnp.concatenate([jnp.swapaxes(state_b_C, -1, -2), state_b_n[..., None, :],
                          jnp.zeros((DEC_BATCH, DEPTH, 2, H_B, VROWS - 65, DK_B), F32)], axis=-2)
    s0 = s0.reshape(DEC_BATCH, DEPTH, 2 * H_B, VROWS, DK_B)
    m0 = jnp.broadcast_to(state_b_m.reshape(DEC_BATCH, DEPTH, 2 * H_B, 1), (DEC_BATCH, DEPTH, 2 * H_B, 128))
    zeros_s = jnp.zeros((BATCH, 1, 2 * H_B, VROWS, DK_B), F32)
    zeros_m = jnp.zeros((BATCH, 1, 2 * H_B, 128), F32)

    outs = {k: [] for k in ("ak", "av", "ck", "cv", "bC", "bn", "bm")}
    for l in range(DEPTH):
        lam_init = 0.8 - 0.6 * math.exp(-0.3 * l)
        h = _ffn(h, mods, gn, wi, wo, gf, l=l, half=0, first=(l == 0))
        z, zt = _proj_in(h, mods, gn, wz, wt, bt, l=l)

        ya_c, ak, av = _attn("A", z, zt, ctx=None, tables=None, params=par_a, l=l, lam_init=lam_init,
                             latent=False)
        yc_c, ck, cv = _attn("C", z, zt, ctx=None, tables=None, params=par_c, l=l, lam_init=lam_init,
                             latent=False)
        stats, acols = _gate_stats(zt)
        yb_c, s_fin, m_fin = _mlstm(z, zt, stats, acols, zeros_s, zeros_m, gb, l=l, latent=False)

        (ya_l,) = _attn("A", z, zt, ctx=ctx_a, tables=tab_a, params=par_a, l=l, lam_init=lam_init,
                        latent=True)
        (yc_l,) = _attn("C", z, zt, ctx=ctx_c, tables=tab_c, params=par_c, l=l, lam_init=lam_init,
                        latent=True)
        yb_l, _, _ = _mlstm(z, zt, stats, acols, s0, m0, gb, l=l, latent=True)

        h = _ffn(h, mods, gn, wi, wo, gf, l=l, half=1,
                 mix=((ya_c, ya_l), (yb_c, yb_l), (yc_c, yc_l), wmix), final=(l == DEPTH - 1))

        outs["ak"].append(ak.reshape(BATCH, SEQ, KV_A, HD_A))
        outs["av"].append(av.reshape(BATCH, SEQ, KV_A, HD_A))
        outs["ck"].append(ck.reshape(BATCH, SEQ, H_C, 2, D_C))
        outs["cv"].append(cv.reshape(BATCH, SEQ, H_C, 2 * D_C))
        s_fin = s_fin.reshape(BATCH, 2, H_B, VROWS, DK_B)
        outs["bC"].append(jnp.swapaxes(s_fin[..., 0:64, :], -1, -2))
        outs["bn"].append(s_fin[..., 64, :])
        outs["bm"].append(m_fin[:, :, 0].reshape(BATCH, 2, H_B))

    y_prompt = h[0].reshape(BATCH, SEQ, D_MODEL)
    y_sample = h[1].reshape(DEC_BATCH, DEC_SEQ, D_MODEL)
    st = lambda k: jnp.stack(outs[k], axis=1)
    return (y_prompt, y_sample, st("ak"), st("av"), st("ck"), st("cv"), st("bC"), st("bn"), st("bm"))
```

```python
import functools
import math

import jax
import jax.numpy as jnp
from jax import lax
from jax.experimental import pallas as pl
from jax.experimental.pallas import tpu as pltpu

F32 = jnp.float32
BF16 = jnp.bfloat16

D_MODEL = 1024
BATCH = 16
SEQ = 256
DEPTH = 2
DEC_BATCH = 4
DEC_SEQ = 2048
PAST_LEN = 512
GRID_W = 64
ROPE_THETA = 10000.0
EPS = 1e-6
HD_A = 64
H_A = 8
KV_A = 2
DK_B = 64
H_B = 4
D_C = 32
H_C = 4
D_FF = 2816
LOG2E = 1.4426950408889634

N_CTX = BATCH * SEQ
N_LAT = DEC_BATCH * DEC_SEQ
N_TOK = N_CTX + N_LAT

TM = 512
CTX_TILES = N_CTX // TM
LAT_TILES_PER_BATCH = DEC_SEQ // TM
FF_CHUNKS = ((0, 1536), (1536, D_FF))
FF_ROWS = 256
TC = 256
SB = 256
TQ = 256
VROWS = 80
BOUND_SLACK = 1.0 + 2.0 ** -6
MAX_BOUND_GAP = 60.0

Z_KB, Z_KC, Z_KA = 0, 256, 512
Z_W = 640
ZT_QA, ZT_QC, ZT_VC, ZT_QB, ZT_VB, ZT_OB, ZT_VA, ZT_G = 0, 512, 768, 1024, 1280, 1536, 1792, 1920
ZT_W = 1936

VMEM_LIMIT = 56 * 1024 * 1024


def _cparams(sem):
    return pltpu.CompilerParams(dimension_semantics=sem, vmem_limit_bytes=VMEM_LIMIT)


def _const_spec(shape, lead=()):
    nd = len(shape)
    return pl.BlockSpec((None,) * len(lead) + tuple(shape), lambda *_: tuple(lead) + (0,) * nd,
                        pipeline_mode=pl.Buffered(1))


def _mod_row(i):
    return jnp.maximum(i - CTX_TILES, 0) // LAT_TILES_PER_BATCH + (i >= CTX_TILES).astype(jnp.int32)


def _rms(x, g, axis=-1):
    ms = jnp.mean(x * x, axis=axis, keepdims=True)
    return x * lax.rsqrt(ms + EPS) * g


def _norm_mod(x, g, mod_ref, k):
    shift = mod_ref[0, k:k + 1, :]
    scale = mod_ref[0, k + 1:k + 2, :]
    return _rms(x, g) * (1.0 + scale) + shift


def _ds(c, n):
    if isinstance(c, int):
        return pl.ds(c * n, n)
    return pl.ds(pl.multiple_of(c * n, n), n)


def _log_sigmoid(x):
    return jnp.minimum(x, 0.0) - jnp.log1p(jnp.exp(-jnp.abs(x)))


def _ada_kernel(c_ref, w_ref, b_ref, o_ref):
    c = c_ref[...]
    a = (c * jax.nn.sigmoid(c)).astype(BF16)
    o_ref[0] = jnp.dot(a, w_ref[0].astype(BF16), preferred_element_type=F32) + b_ref[0]


def _ada(cc, w_ada, b_ada):
    tn = 2304
    n = 9 * D_MODEL
    return pl.pallas_call(
        _ada_kernel,
        grid=(DEPTH, n // tn),
        in_specs=[
            pl.BlockSpec((8, D_MODEL), lambda l, j: (0, 0)),
            pl.BlockSpec((1, D_MODEL, tn), lambda l, j: (l, 0, j)),
            pl.BlockSpec((1, 1, tn), lambda l, j: (l, 0, j)),
        ],
        out_specs=pl.BlockSpec((1, 8, tn), lambda l, j: (l, 0, j)),
        out_shape=jax.ShapeDtypeStruct((DEPTH, 8, n), F32),
        compiler_params=_cparams(("arbitrary", "arbitrary")),
        name="ada",
    )(cc, w_ada, b_ada.reshape(DEPTH, 1, n))


def _ffn_kernel(*refs, k, first, mix, final, proj):
    is_ctx = pl.program_id(0) < CTX_TILES
    it = iter(refs)
    x_refs = (next(it), next(it)) if first else (next(it),)
    if mix:
        y_refs = [(next(it), next(it)) for _ in range(3)]
        wmix_ref = next(it)
    mod_ref, g_ref, wi_ref, wo_ref, gf_ref = (next(it) for _ in range(5))
    if proj:
        gp_ref, wz_ref, wt_ref, bt_ref = (next(it) for _ in range(4))
        h_out_ref, z_ref, zt_ref = refs[-3:]
    elif not final:
        h_out_ref = refs[-1]

    nrows = TM if mix else FF_ROWS
    ys = []
    for r in range(TM // nrows):
        rows = slice(r * nrows, (r + 1) * nrows)
        pick = lambda c_ref, l_ref: jnp.where(is_ctx, c_ref[rows, :], l_ref[rows, :])
        x = pick(*x_refs) if first else x_refs[0][rows, :]
        if mix:
            ya, yb, yc = (pick(*pair) for pair in y_refs)
            y = jnp.dot(ya, wmix_ref[0:512, :], preferred_element_type=F32)
            y = y + jnp.dot(yb, wmix_ref[512:768, :], preferred_element_type=F32)
            y = y + jnp.dot(yc, wmix_ref[768:1024, :], preferred_element_type=F32)
            x = x + mod_ref[0, 5:6, :] * y
        xn = _norm_mod(x, g_ref[...], mod_ref, k).astype(BF16)
        acc = jnp.zeros((nrows, D_MODEL), F32)
        for lo, hi in FF_CHUNKS:
            g = jnp.dot(xn, wi_ref[:, lo:hi], preferred_element_type=F32)
            u = jnp.dot(xn, wi_ref[:, D_FF + lo:D_FF + hi], preferred_element_type=F32)
            a = (g * jax.nn.sigmoid(g) * u).astype(BF16)
            acc = acc + jnp.dot(a, wo_ref[lo:hi, :], preferred_element_type=F32)
        y = x + 0.5 * mod_ref[0, k + 2:k + 3, :] * acc
        if final:
            ys.append(_rms(y, gf_ref[...]))
        else:
            h_out_ref[rows, :] = y
        if proj:
            yn = _norm_mod(y, gp_ref[...], mod_ref, 3).astype(BF16)
            z_ref[rows, :] = jnp.dot(yn, wz_ref[...], preferred_element_type=F32)
            zt_ref[r] = lax.dot_general(wt_ref[...], yn, (((1,), (1,)), ((), ())),
                                        preferred_element_type=F32) + bt_ref[...]
    if final:
        yc_ref, yl_ref = refs[-2:]
        y = jnp.concatenate(ys, axis=0)

        @pl.when(is_ctx)
        def _():
            yc_ref[...] = y

        @pl.when(jnp.logical_not(is_ctx))
        def _():
            yl_ref[...] = y


def _ctx_lat_specs(width):
    return [
        pl.BlockSpec((TM, width), lambda i: (jnp.minimum(i, CTX_TILES - 1), 0)),
        pl.BlockSpec((TM, width), lambda i: (jnp.maximum(i - CTX_TILES, 0), 0)),
    ]


def _mod_spec(l):
    return pl.BlockSpec((None, 1, 9, D_MODEL), lambda i: (l, _mod_row(i), 0, 0))


def _ffn(h, mods, g, wi, wo, gf, *, l, half, first=False, mix=None, final=False, proj=None):
    k = 6 * half
    flat_spec = pl.BlockSpec((TM, D_MODEL), lambda i: (i, 0))
    flat_shape = jax.ShapeDtypeStruct((N_TOK, D_MODEL), F32)
    mix_specs, mix_args = [], []
    if mix is not None:
        mix_specs = (_ctx_lat_specs(512) + _ctx_lat_specs(256) + _ctx_lat_specs(256)
                     + [_const_spec((D_MODEL, D_MODEL), (l,))])
        mix_args = [*mix[0], *mix[1], *mix[2], mix[3]]
    proj_specs, proj_args = [], []
    out_specs = _ctx_lat_specs(D_MODEL) if final else flat_spec
    out_shape = [jax.ShapeDtypeStruct((N_CTX, D_MODEL), F32),
                 jax.ShapeDtypeStruct((N_LAT, D_MODEL), F32)] if final else flat_shape
    if proj is not None:
        proj_specs = [_const_spec((1, D_MODEL), (l, 1)), _const_spec((D_MODEL, Z_W), (l,)),
                      _const_spec((ZT_W, D_MODEL), (l,)), _const_spec((ZT_W, 1), (l,))]
        proj_args = [g, *proj]
        out_specs = [flat_spec, pl.BlockSpec((TM, Z_W), lambda i: (i, 0)),
                     pl.BlockSpec((TM // TC, ZT_W, TC), lambda i: (i, 0, 0))]
        out_shape = [flat_shape, jax.ShapeDtypeStruct((N_TOK, Z_W), F32),
                     jax.ShapeDtypeStruct((N_TOK // TC, ZT_W, TC), F32)]
    return pl.pallas_call(
        functools.partial(_ffn_kernel, k=k, first=first, mix=mix is not None, final=final,
                          proj=proj is not None),
        grid=(N_TOK // TM,),
        in_specs=(_ctx_lat_specs(D_MODEL) if first else [flat_spec]) + mix_specs + [
            _mod_spec(l),
            _const_spec((1, D_MODEL), (l, 2 * half)),
            _const_spec((D_MODEL, 2 * D_FF), (l, half)),
            _const_spec((D_FF, D_MODEL), (l, half)),
            _const_spec((1, D_MODEL)),
        ] + proj_specs,
        out_specs=out_specs,
        out_shape=out_shape,
        compiler_params=_cparams(("arbitrary",)),
        name="ffn_final" if final else ("ffn_first" if first else ("ffn_mix" if mix else "ffn")),
    )(*(h if first else (h,)), *mix_args, mods, g, wi, wo, gf, *proj_args)


def _proj_in_kernel(h_ref, mod_ref, g_ref, w_ref, wt_ref, bt_ref, z_ref, zt_ref):
    xn = _norm_mod(h_ref[...], g_ref[...], mod_ref, 3).astype(BF16)
    z_ref[...] = jnp.dot(xn, w_ref[...], preferred_element_type=F32)
    zt = lax.dot_general(wt_ref[...], xn, (((1,), (1,)), ((), ())), preferred_element_type=F32)
    zt = zt + bt_ref[...]
    for c in range(TM // TC):
        zt_ref[c] = zt[:, c * TC:(c + 1) * TC]


def _proj_in(h, mods, g, wz, wt, bt, *, l):
    return pl.pallas_call(
        _proj_in_kernel,
        grid=(N_TOK // TM,),
        in_specs=[
            pl.BlockSpec((TM, D_MODEL), lambda i: (i, 0)),
            _mod_spec(l),
            _const_spec((1, D_MODEL), (l, 1)),
            _const_spec((D_MODEL, Z_W), (l,)),
            _const_spec((ZT_W, D_MODEL), (l,)),
            _const_spec((ZT_W, 1), (l,)),
        ],
        out_specs=[
            pl.BlockSpec((TM, Z_W), lambda i: (i, 0)),
            pl.BlockSpec((TM // TC, ZT_W, TC), lambda i: (i, 0, 0)),
        ],
        out_shape=[
            jax.ShapeDtypeStruct((N_TOK, Z_W), F32),
            jax.ShapeDtypeStruct((N_TOK // TC, ZT_W, TC), F32),
        ],
        compiler_params=_cparams(("arbitrary",)),
        name="proj_in",
    )(h, mods, g, wz, wt, bt)


def _swap_halves(x, width, axis):
    n = x.shape[axis]
    half = width // 2
    parts = []
    for s in range(0, n, width):
        parts.append(lax.slice_in_dim(x, s + half, s + width, axis=axis))
        parts.append(lax.slice_in_dim(x, s, s + half, axis=axis))
    return jnp.concatenate(parts, axis=axis)


def _attn_kernel(*refs, mode, seq, has_ctx, rope, emit_cache, lam_init):
    it = iter(refs)
    qt_ref, k_ref, vt_ref = next(it), next(it), next(it)
    if has_ctx:
        ck_ref, cvt_ref = next(it), next(it)
    if rope:
        cqt_ref, sqt_ref, ck_tab, sk_tab = next(it), next(it), next(it), next(it)
    if mode == "A":
        gq_ref, gk_ref = next(it), next(it)
    else:
        lam_ref, gc_ref = next(it), next(it)
    y_ref = next(it)
    if emit_cache:
        kc_ref, vc_ref = next(it), next(it)
    kbuf, vbuf, p_ref, kmax_ref = next(it), next(it), next(it), next(it)

    n_kv = KV_A if mode == "A" else H_C
    per_kv = 8 // n_kv
    n_ctx = PAST_LEN if has_ctx else 0
    lk = n_ctx + seq
    width = HD_A if mode == "A" else D_C

    @pl.when(pl.program_id(1) == 0)
    def _fill():
        ones_rows = (lax.broadcasted_iota(jnp.int32, (VROWS - 64, lk), 0) == 0).astype(BF16)
        ones_cols = jnp.ones((64, 128), BF16)

        def max_sq_norm(kb):
            if lk == SB:
                return jnp.zeros((1, 1), F32)
            kf = kb.astype(F32)
            sq = jnp.dot((kf * kf).astype(BF16), ones_cols, preferred_element_type=F32)
            return jnp.max(sq, axis=0, keepdims=True)[:, 0:1]

        for h in range(n_kv):
            hs = slice(h * 64, (h + 1) * 64)
            vbuf[h, 64:VROWS, :] = ones_rows
            kmax2 = jnp.zeros((1, 1), F32)
            if has_ctx:
                kb = ck_ref[0, :, hs].astype(BF16)
                kmax2 = jnp.maximum(kmax2, max_sq_norm(kb))
                kbuf[h, 0:n_ctx, :] = kb
                vbuf[h, 0:64, 0:n_ctx] = cvt_ref[0, h].astype(BF16)
            for c in range(seq // TC):
                rows = slice(c * TC, (c + 1) * TC)
                k = k_ref[rows, hs]
                if mode == "A":
                    k = _rms(k, gk_ref[...])
                if emit_cache:
                    kc_ref[0, rows, hs] = k
                if rope:
                    k = k * ck_tab[rows, :] + _swap_halves(k, width, 1) * sk_tab[rows, :]
                kb = k.astype(BF16)
                kmax2 = jnp.maximum(kmax2, max_sq_norm(kb))
                kbuf[h, n_ctx + c * TC:n_ctx + (c + 1) * TC, :] = kb
                vbuf[h, 0:64, n_ctx + c * TC:n_ctx + (c + 1) * TC] = vt_ref[c, hs, :].astype(BF16)
            kmax_ref[h] = jnp.broadcast_to(jnp.sqrt(kmax2), (8, 128))
        if emit_cache:
            for c in range(seq // TC):
                vc_ref[0, c * TC:(c + 1) * TC, :] = vt_ref[c].T

    qts = []
    if mode == "A":
        scale = HD_A ** -0.5 * LOG2E
        for hh in range(H_A):
            x = _rms(qt_ref[0, hh * 64:(hh + 1) * 64, :], gq_ref[...], axis=0)
            if rope:
                x = x * cqt_ref[0] + _swap_halves(x, width, 0) * sqt_ref[0]
            qts.append((x * scale).astype(BF16))
    else:
        scale = D_C ** -0.5 * LOG2E
        first_map = lax.broadcasted_iota(jnp.int32, (64, TQ), 0) < D_C
        for h in range(H_C):
            x = qt_ref[0, h * 64:(h + 1) * 64, :]
            if rope:
                x = x * cqt_ref[0] + _swap_halves(x, width, 0) * sqt_ref[0]
            x = x * scale
            qts.append(jnp.where(first_map, x, 0.0).astype(BF16))
            qts.append(jnp.where(first_map, 0.0, x).astype(BF16))

    def value_matmul(u):
        ox = jnp.dot(vbuf[u // per_kv], p_ref[:, u * TQ:(u + 1) * TQ], preferred_element_type=F32)
        return ox[0:64] / ox[64:65]

    def emit(outs):
        if mode == "A":
            for pair in range(4):
                o = jnp.concatenate(outs[2 * pair:2 * pair + 2], axis=0)
                y_ref[:, pair * 128:(pair + 1) * 128] = o.T.astype(BF16)
        else:
            lp = lam_ref[...]
            lam = (jnp.exp(jnp.sum(lp[0:1] * lp[1:2], axis=1, keepdims=True))
                   - jnp.exp(jnp.sum(lp[2:3] * lp[3:4], axis=1, keepdims=True)) + lam_init)
            for pair in range(2):
                o = []
                for h in (2 * pair, 2 * pair + 1):
                    d = outs[2 * h] - lam * outs[2 * h + 1]
                    o.append(_rms(d, gc_ref[...], axis=0) * (1.0 - lam_init))
                y_ref[:, pair * 128:(pair + 1) * 128] = jnp.concatenate(o, axis=0).T.astype(BF16)

    outs = []
    if lk == SB:
        sts = [jnp.dot(kbuf[u // per_kv], qts[u], preferred_element_type=F32) for u in range(8)]
        ps = [jnp.exp2(st - jnp.max(st, axis=0, keepdims=True)).astype(BF16) for st in sts]
        for u in range(8):
            ox = jnp.dot(vbuf[u // per_kv], ps[u], preferred_element_type=F32)
            outs.append(ox[0:64] / ox[64:65])
        emit(outs)
        return

    mhats, ms = [], []
    for u in range(8):
        g = u // per_kv
        qn = jnp.sqrt(jnp.sum(jnp.square(qts[u].astype(F32)), axis=0, keepdims=True))
        mhat = qn * kmax_ref[g][0:1, 0:1] * BOUND_SLACK
        mrun = jnp.full((8, TQ), -jnp.inf, F32)
        for j in range(lk // SB):
            rows = slice(j * SB, (j + 1) * SB)
            st = jnp.dot(kbuf[g, rows, :], qts[u], preferred_element_type=F32)
            p_ref[rows, u * TQ:(u + 1) * TQ] = jnp.exp2(st - mhat).astype(BF16)
            for r in range(SB // 8):
                mrun = jnp.maximum(mrun, st[8 * r:8 * r + 8])
        mhats.append(mhat)
        ms.append(jnp.max(mrun, axis=0, keepdims=True))
        if u >= 1:
            outs.append(value_matmul(u - 1))
    outs.append(value_matmul(7))
    emit(outs)

    worst_gap = functools.reduce(jnp.maximum, [jnp.max(mh - m) for mh, m in zip(mhats, ms)])

    @pl.when(worst_gap > MAX_BOUND_GAP)
    def _redo():
        redone = []
        for u in range(8):
            def block(j, carry, u=u):
                rows = _ds(j, SB)
                st = jnp.dot(kbuf[u // per_kv, rows, :], qts[u], preferred_element_type=F32)
                p_ref[rows, u * TQ:(u + 1) * TQ] = jnp.exp2(st - ms[u]).astype(BF16)
                return carry

            lax.fori_loop(0, lk // SB, block, 0)
            redone.append(value_matmul(u))
        emit(redone)


def _attn(mode, z, zt, *, ctx, tables, params, l, lam_init, latent):
    if latent:
        nb, seq, row0 = DEC_BATCH, DEC_SEQ, N_CTX
    else:
        nb, seq, row0 = BATCH, SEQ, 0
    nq = seq // TQ
    n_kv = KV_A if mode == "A" else H_C
    n_ctx = PAST_LEN if ctx is not None else 0
    lk = n_ctx + seq
    qw = 512 if mode == "A" else 256
    vw = n_kv * 64
    q_blk = (ZT_QA if mode == "A" else ZT_QC) // qw
    k_blk = (Z_KA if mode == "A" else Z_KC) // vw
    v_blk = (ZT_VA if mode == "A" else ZT_VC) // vw
    emit_cache = not latent
    blk0 = row0 // seq

    in_specs = [
        pl.BlockSpec((1, qw, TC), lambda b, i: (row0 // TC + b * nq + i, q_blk, 0)),
        pl.BlockSpec((seq, vw), lambda b, i: (blk0 + b, k_blk)),
        pl.BlockSpec((seq // TC, vw, TC), lambda b, i: (blk0 + b, v_blk, 0)),
    ]
    args = [zt, z, zt]
    if ctx is not None:
        in_specs += [
            pl.BlockSpec((1, None, PAST_LEN, vw), lambda b, i: (b, l, 0, 0)),
            pl.BlockSpec((1, None, n_kv, 64, PAST_LEN), lambda b, i: (b, l, 0, 0, 0)),
        ]
        args += list(ctx)
    if tables is not None:
        cq, sq, cqt, sqt = tables
        in_specs += [
            pl.BlockSpec((1, 64, TC), lambda b, i: (i, 0, 0)),
            pl.BlockSpec((1, 64, TC), lambda b, i: (i, 0, 0)),
            _const_spec((seq, 64)),
            _const_spec((seq, 64)),
        ]
        args += [cqt, sqt, cq, sq]
    for p in params:
        in_specs.append(_const_spec(p.shape[1:], (l,)))
        args.append(p)

    out_specs = [pl.BlockSpec((TQ, qw), lambda b, i: (b * nq + i, 0))]
    out_shape = [jax.ShapeDtypeStruct((nb * seq, qw), BF16)]
    if emit_cache:
        out_specs += [pl.BlockSpec((1, seq, vw), lambda b, i: (b, 0, 0))] * 2
        out_shape += [jax.ShapeDtypeStruct((nb, seq, vw), F32)] * 2

    return pl.pallas_call(
        functools.partial(_attn_kernel, mode=mode, seq=seq, has_ctx=ctx is not None,
                          rope=tables is not None, emit_cache=emit_cache, lam_init=lam_init),
        grid=(nb, nq),
        in_specs=in_specs,
        out_specs=out_specs,
        out_shape=out_shape,
        scratch_shapes=[
            pltpu.VMEM((n_kv, lk, 64), BF16),
            pltpu.VMEM((n_kv, VROWS, lk), BF16),
            pltpu.VMEM((lk, 8 * TQ), BF16),
            pltpu.VMEM((n_kv, 8, 128), F32),
        ],
        compiler_params=_cparams(("arbitrary", "arbitrary")),
        name=f"attn_{mode}_{'lat' if latent else 'ctx'}",
    )(*args)


N_CHUNKS = N_TOK // TC
GS_STEPS = 4
GS_A, GS_B, GS_LM, GS_BLAST, GS_ALAST, GS_ROWS = 0, 8, 16, 24, 32, 40


def _scan_lanes(x, op, fill, reverse):
    axis = x.ndim - 1
    lane = lax.broadcasted_iota(jnp.int32, x.shape, axis)
    k = 1
    while k < TC:
        if reverse:
            x = op(x, jnp.where(lane < TC - k, pltpu.roll(x, TC - k, axis), fill))
        else:
            x = op(x, jnp.where(lane >= k, pltpu.roll(x, k, axis), fill))
        k *= 2
    return x


def _gate_stats_kernel(g_ref, st_ref, ac_ref):
    g = g_ref[...]
    is_fwd = lax.broadcasted_iota(jnp.int32, (g.shape[0], 8, TC), 1) < H_B
    gi = jnp.concatenate([g[:, 0:4], g[:, 8:12]], axis=1)
    f = _log_sigmoid(jnp.concatenate([g[:, 4:8], g[:, 12:16]], axis=1))
    b = jnp.where(is_fwd, _scan_lanes(f, jnp.add, 0.0, False), _scan_lanes(f, jnp.add, 0.0, True))
    a = gi - b
    lm = jnp.where(is_fwd, _scan_lanes(a, jnp.maximum, -jnp.inf, False),
                   _scan_lanes(a, jnp.maximum, -jnp.inf, True))
    st_ref[:, GS_A:GS_A + 8, :] = a
    st_ref[:, GS_B:GS_B + 8, :] = b
    st_ref[:, GS_LM:GS_LM + 8, :] = lm
    st_ref[:, GS_BLAST:GS_BLAST + 8, :] = jnp.broadcast_to(jnp.sum(f, axis=2, keepdims=True), a.shape)
    st_ref[:, GS_ALAST:GS_ALAST + 8, :] = jnp.broadcast_to(jnp.max(a, axis=2, keepdims=True), a.shape)

    def transpose_chunk(c, carry):
        ac_ref[c] = jnp.concatenate([st_ref[c, GS_A:GS_A + 8, :], jnp.zeros((120, TC), F32)], axis=0).T
        return carry

    lax.fori_loop(0, g.shape[0], transpose_chunk, 0, unroll=4)


def _gate_stats(zt):
    return pl.pallas_call(
        _gate_stats_kernel,
        grid=(GS_STEPS,),
        in_specs=[pl.BlockSpec((N_CHUNKS // GS_STEPS, 16, TC), lambda i: (i, ZT_G // 16, 0))],
        out_specs=[
            pl.BlockSpec((N_CHUNKS // GS_STEPS, GS_ROWS, TC), lambda i: (i, 0, 0)),
            pl.BlockSpec((N_CHUNKS // GS_STEPS, TC, 128), lambda i: (i, 0, 0)),
        ],
        out_shape=[
            jax.ShapeDtypeStruct((N_CHUNKS, GS_ROWS, TC), F32),
            jax.ShapeDtypeStruct((N_CHUNKS, TC, 128), F32),
        ],
        compiler_params=_cparams(("arbitrary",)),
        name="gate_stats",
    )(zt)


def _mlstm_kernel(k_ref, qt_ref, vt_ref, ot_ref, st_ref, ac_ref, s0_ref, m0_ref, gb_ref,
                  y_ref, s_ref, m_ref, hf_ref, hr_ref, *, seq):
    nc = seq // TC
    s_idx = lax.broadcasted_iota(jnp.int32, (TC, TC), 0)
    t_idx = lax.broadcasted_iota(jnp.int32, (TC, TC), 1)
    seen = (s_idx <= t_idx, s_idx >= t_idx)
    ones_rows = (lax.broadcasted_iota(jnp.int32, (VROWS - 64, TC), 0) == 0).astype(F32)
    units = [(d, h) for d in range(2) for h in range(H_B)]

    def chunk(j, carry):
        states, m = carry
        cs = (j, nc - 1 - j)
        stat = lambda r: jnp.concatenate([st_ref[cs[0], r:r + 4, :], st_ref[cs[1], r + 4:r + 8, :]], axis=0)
        a, b, lm = stat(GS_A), stat(GS_B), stat(GS_LM)
        mx = jnp.maximum(m, lm)
        w = jnp.exp(m - mx)
        floor = jnp.exp(-(b + mx))
        mxl = jnp.maximum(m, stat(GS_ALAST)[:, 0:1])
        decay = jnp.exp(m - mxl)
        wk = jnp.exp(a - mxl)
        m_new = stat(GS_BLAST)[:, 0:1] + mxl

        first = []
        for (d, h), st in zip(units, states):
            hs = slice(h * 64, (h + 1) * 64)
            kh = (k_ref[_ds(cs[d], TC), hs] * (DK_B ** -0.5)).astype(BF16)
            qt = qt_ref[cs[d], hs, :].astype(BF16)
            lhs = jnp.concatenate([kh, st.astype(BF16)], axis=0)
            first.append((kh, jnp.dot(lhs, qt, preferred_element_type=F32)))
        second = []
        for u, ((d, h), (kh, both)) in enumerate(zip(units, first)):
            a_col = ac_ref[cs[d], :, u:u + 1]
            dt = jnp.where(seen[d], a_col - mx[u:u + 1, :], -jnp.inf)
            sp = both[0:TC] * jnp.exp(dt)
            den = jnp.sum(sp, axis=0, keepdims=True) + w[u:u + 1, :] * both[TC + 64:TC + 65]
            second.append((sp.astype(BF16), den))
        new_states = []
        for u, ((d, h), st, (kh, both), (spb, den)) in enumerate(zip(units, states, first, second)):
            hs = slice(h * 64, (h + 1) * 64)
            vx = jnp.concatenate([vt_ref[cs[d], hs, :], ones_rows], axis=0)
            num = (jnp.dot(vx.astype(BF16), spb, preferred_element_type=F32)[0:64]
                   + w[u:u + 1, :] * both[TC:TC + 64])
            hv = num / jnp.maximum(jnp.abs(den), floor[u:u + 1, :])
            if d == 0:
                hf_ref[cs[d], hs, :] = hv
            else:
                hr_ref[cs[d], hs, :] = hv
            upd = jnp.dot((vx * wk[u:u + 1, :]).astype(BF16), kh, preferred_element_type=F32)
            new_states.append(decay[u:u + 1, :] * st + upd)
        return tuple(new_states), m_new

    init = (tuple(s0_ref[0, u] for u in range(2 * H_B)), m0_ref[0][:, 0:1])
    if nc == 1:
        states, m = chunk(0, init)
    else:
        states, m = lax.fori_loop(0, nc, chunk, init)
    for u, st in enumerate(states):
        s_ref[0, u] = st
    m_ref[0] = jnp.broadcast_to(m, (2 * H_B, 128))

    def epilogue(c, carry):
        ys = []
        for h in range(H_B):
            hs = slice(h * 64, (h + 1) * 64)
            x = hf_ref[c, hs, :] + hr_ref[c, hs, :]
            ys.append(_rms(x, gb_ref[...], axis=0) * jax.nn.sigmoid(ot_ref[c, hs, :]))
        y_ref[_ds(c, TC), :] = jnp.concatenate(ys, axis=0).T.astype(BF16)
        return carry

    if nc == 1:
        epilogue(0, 0)
    else:
        lax.fori_loop(0, nc, epilogue, 0)


def _mlstm(z, zt, stats, acols, s0, m0, gb, *, l, latent):
    ls = min(l, s0.shape[1] - 1)
    if latent:
        nb, seq, row0 = DEC_BATCH, DEC_SEQ, N_CTX
    else:
        nb, seq, row0 = BATCH, SEQ, 0
    blk0 = row0 // seq
    nc = seq // TC
    ztspec = lambda rb: pl.BlockSpec((nc, 256, TC), lambda b: (blk0 + b, rb, 0))
    return pl.pallas_call(
        functools.partial(_mlstm_kernel, seq=seq),
        grid=(nb,),
        in_specs=[
            pl.BlockSpec((seq, 256), lambda b: (blk0 + b, Z_KB // 256)),
            ztspec(ZT_QB // 256), ztspec(ZT_VB // 256), ztspec(ZT_OB // 256),
            pl.BlockSpec((nc, GS_ROWS, TC), lambda b: (blk0 + b, 0, 0)),
            pl.BlockSpec((nc, TC, 128), lambda b: (blk0 + b, 0, 0)),
            pl.BlockSpec((1, None, 2 * H_B, VROWS, 64), lambda b: (b, ls, 0, 0, 0)),
            pl.BlockSpec((1, None, 2 * H_B, 128), lambda b: (b, ls, 0, 0)),
            _const_spec((64, 1), (l,)),
        ],
        out_specs=[
            pl.BlockSpec((seq, 256), lambda b: (b, 0)),
            pl.BlockSpec((1, 2 * H_B, VROWS, 64), lambda b: (b, 0, 0, 0)),
            pl.BlockSpec((1, 2 * H_B, 128), lambda b: (b, 0, 0)),
        ],
        out_shape=[
            jax.ShapeDtypeStruct((nb * seq, 256), BF16),
            jax.ShapeDtypeStruct((nb, 2 * H_B, VROWS, 64), F32),
            jax.ShapeDtypeStruct((nb, 2 * H_B, 128), F32),
        ],
        scratch_shapes=[pltpu.VMEM((nc, 256, TC), F32), pltpu.VMEM((nc, 256, TC), F32)],
        compiler_params=_cparams(("arbitrary",)),
        name=f"mlstm_{'lat' if latent else 'ctx'}",
    )(z, zt, zt, zt, stats, acols, s0, m0, gb)


def _rope_tables(dim):
    t = jnp.arange(DEC_SEQ)
    row = (t // GRID_W).astype(F32)
    colp = (t % GRID_W).astype(F32)
    axis_dim = dim // 2
    freqs = ROPE_THETA ** (-jnp.arange(0, axis_dim, 2, dtype=F32) / axis_dim)
    ang = jnp.concatenate([row[:, None] * freqs, colp[:, None] * freqs], axis=-1)
    cos, sin = jnp.cos(ang), jnp.sin(ang)
    reps = 64 // dim
    cq = jnp.concatenate([cos, cos] * reps, axis=1)
    sq = jnp.concatenate([-sin, sin] * reps, axis=1)
    to_chunks = lambda x: x.T.reshape(64, DEC_SEQ // TC, TC).transpose(1, 0, 2)
    return cq, sq, to_chunks(cq), to_chunks(sq)


def _split_w_in(w, b_gates):
    a0, b0, c0 = 0, 768, 1808
    qa, ka, va = w[..., a0:a0 + 512], w[..., a0 + 512:a0 + 640], w[..., a0 + 640:a0 + 768]
    qb, kb = w[..., b0:b0 + 256], w[..., b0 + 256:b0 + 512]
    vb, ob = w[..., b0 + 512:b0 + 768], w[..., b0 + 768:b0 + 1024]
    gb = w[..., b0 + 1024:b0 + 1040]
    qc, kc, vc = w[..., c0:c0 + 256], w[..., c0 + 256:c0 + 512], w[..., c0 + 512:c0 + 768]
    wz = jnp.concatenate([kb, kc, ka], axis=-1)
    wt = jnp.swapaxes(jnp.concatenate([qa, qc, vc, qb, vb, ob, va, gb], axis=-1), -1, -2)
    bt = jnp.zeros((DEPTH, ZT_W, 1), F32).at[:, ZT_G:ZT_G + 16, 0].set(b_gates)
    return wz.astype(BF16), wt.astype(BF16), bt


def kernel(x_prompt, x_sample, cache_a_k, cache_a_v, cache_c_k, cache_c_v, state_b_C, state_b_n,
           state_b_m, c, c_ctx, w_ada, b_ada, g_norm, w_ff_in, w_ff_out, w_in, w_out, g_qa, g_ka,
           b_gates, g_b, lam_q1, lam_k1, lam_q2, lam_k2, g_c, g_final):
    h = (x_prompt.reshape(N_CTX, D_MODEL), x_sample.reshape(N_LAT, D_MODEL))
    cc = jnp.concatenate([c_ctx[None, :], c, jnp.zeros((3, D_MODEL), F32)], axis=0)
    mods = _ada(cc, w_ada, b_ada).reshape(DEPTH, 8, 9, D_MODEL)
    tab_a = _rope_tables(HD_A)
    tab_c = _rope_tables(D_C)

    gf = g_final.reshape(1, D_MODEL)
    gn = g_norm.reshape(DEPTH, 3, 1, D_MODEL)
    wi, wo, wmix = w_ff_in.astype(BF16), w_ff_out.astype(BF16), w_out.astype(BF16)
    wz, wt, bt = _split_w_in(w_in, b_gates)
    par_a = (g_qa.reshape(DEPTH, 64, 1), g_ka.reshape(DEPTH, 1, 64))
    par_c = (jnp.stack([lam_q1, lam_k1, lam_q2, lam_k2], axis=1), g_c.reshape(DEPTH, 64, 1))
    gb = g_b.reshape(DEPTH, 64, 1)
    ctx_a = (cache_a_k.reshape(DEC_BATCH, DEPTH, PAST_LEN, KV_A * HD_A), cache_a_v.transpose(0, 1, 3, 4, 2))
    ctx_c = (cache_c_k.reshape(DEC_BATCH, DEPTH, PAST_LEN, H_C * 2 * D_C), cache_c_v.transpose(0, 1, 3, 4, 2))
    s0 = jnp.concatenate([jnp.swapaxes(state_b_C, -1, -2), state_b_n[..., None, :],
                          jnp.zeros((DEC_BATCH, DEPTH, 2, H_B, VROWS - 65, DK_B), F32)], axis=-2)
    s0 = s0.reshape(DEC_BATCH, DEPTH, 2 * H_B, VROWS, DK_B)
    m0 = jnp.broadcast_to(state_b_m.reshape(DEC_BATCH, DEPTH, 2 * H_B, 1), (DEC_BATCH, DEPTH, 2 * H_B, 128))
    zeros_s = jnp.zeros((BATCH, 1, 2 * H_B, VROWS, DK_B), F32)
    zeros_m = jnp.zeros((BATCH, 1, 2 * H_B, 128), F32)

    outs = {k: [] for k in ("ak", "av", "ck", "cv", "bC", "bn", "bm")}
    for l in range(DEPTH):
        lam_init = 0.8 - 0.6 * math.exp(-0.3 * l)
        h, z, zt = _ffn(h, mods, gn, wi, wo, gf, l=l, half=0, first=(l == 0), proj=(wz, wt, bt))

        ya_c, ak, av = _attn("A", z, zt, ctx=None, tables=None, params=par_a, l=l, lam_init=lam_init,
                             latent=False)
        yc_c, ck, cv = _attn("C", z, zt, ctx=None, tables=None, params=par_c, l=l, lam_init=lam_init,
                             latent=False)
        stats, acols = _gate_stats(zt)
        yb_c, s_fin, m_fin = _mlstm(z, zt, stats, acols, zeros_s, zeros_m, gb, l=l, latent=False)

        (ya_l,) = _attn("A", z, zt, ctx=ctx_a, tables=tab_a, params=par_a, l=l, lam_init=lam_init,
                        latent=True)
        (yc_l,) = _attn("C", z, zt, ctx=ctx_c, tables=tab_c, params=par_c, l=l, lam_init=lam_init,
                        latent=True)
        yb_l, _, _ = _mlstm(z, zt, stats, acols, s0, m0, gb, l=l, latent=True)

        h = _ffn(h, mods, gn, wi, wo, gf, l=l, half=1,
                 mix=((ya_c, ya_l), (yb_c, yb_l), (yc_c, yc_l), wmix), final=(l == DEPTH - 1))

        outs["ak"].append(ak.reshape(BATCH, SEQ, KV_A, HD_A))
        outs["av"].append(av.reshape(BATCH, SEQ, KV_A, HD_A))
        outs["ck"].append(ck.reshape(BATCH, SEQ, H_C, 2, D_C))
        outs["cv"].append(cv.reshape(BATCH, SEQ, H_C, 2 * D_C))
        s_fin = s_fin.reshape(BATCH, 2, H_B, VROWS, DK_B)
        outs["bC"].append(jnp.swapaxes(s_fin[..., 0:64, :], -1, -2))
        outs["bn"].append(s_fin[..., 64, :])
        outs["bm"].append(m_fin[:, :, 0].reshape(BATCH, 2, H_B))

    y_prompt = h[0].reshape(BATCH, SEQ, D_MODEL)
    y_sample = h[1].reshape(DEC_BATCH, DEC_SEQ, D_MODEL)
    st = lambda k: jnp.stack(outs[k], axis=1)
    return (y_prompt, y_sample, st("ak"), st("av"), st("ck"), st("cv"), st("bC"), st("bn"), st("bm"))
```
